```python
import numpy as np
import jax
import jax.numpy as jnp
from jax import lax

D_MODEL = 1024
BATCH = 16
SEQ = 2048
DEPTH = 2

GRID_W = 64
CTX_LEN = 256
HEAD_DIM = 64
ATTN_SCALE = HEAD_DIM ** -0.5
NA_HEADS = 4
NA_WIN_H = 8
NA_WIN_W = 16
NA_QCOLS = 16
NA_KCOLS = NA_WIN_W + NA_QCOLS
HG_HEADS = 4
HG_DK = 128
HG_DV = 128
HG_CHUNK = 64
SWA_Q_HEADS = 4
SWA_KV_HEADS = 2
SWA_WINDOW = 128
SWA_BLOCK = 128
ROPE_BASE = 10000.0
N_EXPERTS = 32
TOP_K = 4
D_FF_EXPERT = 1024
SWIGLU_LIMIT = 7.0
SWIGLU_ALPHA = 1.702
MOE_BLOCK = 256
NORM_EPS = 1e-6
NEG_INF = -1e30
NA_W = NA_HEADS * HEAD_DIM
HG_W = HG_HEADS * HG_DV
HG_KW = HG_HEADS * HG_DK
SWA_W = SWA_Q_HEADS * HEAD_DIM
SWA_KV_W = SWA_KV_HEADS * HEAD_DIM
MIX_W = NA_W + HG_W + SWA_W
IN_WIDTHS = (NA_W, NA_W, NA_W, HG_KW, HG_KW, HG_KW, HG_W, HG_W, SWA_W, SWA_KV_W, SWA_KV_W)
IN_COLS = sum(IN_WIDTHS)

kernel_name = 'hybrid_na_hgrn2_swa_moe_dit'


def _rmsnorm(x, g):
    xf = x.astype(jnp.float32)
    y = xf * lax.rsqrt(jnp.mean(xf * xf, axis=-1, keepdims=True) + NORM_EPS)
    return (y * g.astype(jnp.float32)).astype(x.dtype)


def _heads(t, n):
    return t.reshape(t.shape[0], t.shape[1], n, -1)


def _rope_2d(x, row, col):
    half = x.shape[-1] // 2
    quarter = half // 2
    inv = ROPE_BASE ** (-jnp.arange(quarter, dtype=jnp.float32) / quarter)

    def rot(t, pos):
        ang = pos.astype(jnp.float32)[:, None] * inv[None, :]
        cos = jnp.cos(ang)[:, None, :]
        sin = jnp.sin(ang)[:, None, :]
        t = t.astype(jnp.float32)
        t1, t2 = t[..., :quarter], t[..., quarter:]
        return jnp.concatenate([t1 * cos - t2 * sin, t2 * cos + t1 * sin], axis=-1)

    return jnp.concatenate([rot(x[..., :half], row), rot(x[..., half:], col)], axis=-1).astype(x.dtype)


def _context_attention(q, k, v, sink):
    B, Lc, HQ, Dh = q.shape
    HKV = k.shape[2]
    G = HQ // HKV
    qg = q.reshape(B, Lc, HKV, G, Dh)
    s = jnp.einsum('bqkgd,bckd->bkgqc', qg, k, preferred_element_type=jnp.float32) * ATTN_SCALE
    if sink is not None:
        sink_col = jnp.broadcast_to(sink.astype(jnp.float32).reshape(1, HKV, G, 1, 1), s.shape[:-1] + (1,))
        s = jnp.concatenate([s, sink_col], axis=-1)
    p = jax.nn.softmax(s, axis=-1)[..., :Lc].astype(v.dtype)
    return jnp.einsum('bkgqc,bckd->bqkgd', p, v).reshape(B, Lc, HQ, Dh)


def _neighborhood_attention(q, k, v, kc, vc, rpb):
    B, L, H, Dh = q.shape
    R = L // GRID_W
    wr = min(NA_WIN_H, R)
    nbw = GRID_W // NA_QCOLS
    rows = np.arange(R)
    row_idx = np.clip(rows - wr // 2, 0, R - wr)[:, None] + np.arange(wr)[None, :]
    q_cols = np.arange(GRID_W).reshape(nbw, NA_QCOLS)
    q_start = np.clip(q_cols - NA_WIN_W // 2, 0, GRID_W - NA_WIN_W)
    k_start = np.clip(np.arange(nbw) * NA_QCOLS - NA_WIN_W // 2, 0, GRID_W - NA_KCOLS)
    key_cols = k_start[:, None] + np.arange(NA_KCOLS)[None, :]
    col_ok = (key_cols[:, None, :] >= q_start[:, :, None]) & (key_cols[:, None, :] < q_start[:, :, None] + NA_WIN_W)
    dr = row_idx - rows[:, None] + (NA_WIN_H - 1)
    dc = np.clip(key_cols[:, None, :] - q_cols[:, :, None] + (NA_WIN_W - 1), 0, 2 * NA_WIN_W - 2)
    bias = rpb.astype(jnp.float32)[:, dr[:, None, None, :, None], dc[None, :, :, None, :]]
    qg = q.reshape(B, R, nbw, NA_QCOLS, H, Dh)
    kg = k.reshape(B, R, GRID_W, H, Dh)[:, :, key_cols][:, row_idx]
    vg = v.reshape(B, R, GRID_W, H, Dh)[:, :, key_cols][:, row_idx]
    s = jnp.einsum('brjqhd,brwjmhd->bhrjqwm', qg, kg, preferred_element_type=jnp.float32) * ATTN_SCALE + bias
    s = jnp.where(col_ok[:, :, None, :], s, NEG_INF)
    s_ctx = jnp.einsum('brjqhd,bchd->bhrjqc', qg, kc, preferred_element_type=jnp.float32) * ATTN_SCALE
    n_lat = wr * NA_KCOLS
    p = jax.nn.softmax(jnp.concatenate([s.reshape(s.shape[:5] + (n_lat,)), s_ctx], axis=-1), axis=-1).astype(v.dtype)
    p_lat = p[..., :n_lat].reshape(s.shape)
    p_ctx = p[..., n_lat:]
    o = jnp.einsum('bhrjqwm,brwjmhd->brjqhd', p_lat, vg) + jnp.einsum('bhrjqc,bchd->brjqhd', p_ctx, vc)
    return o.reshape(B, L, H, Dh)


def _window_attention(q, k, v, kc, vc, sink):
    B, L, HQ, Dh = q.shape
    HKV = k.shape[2]
    G = HQ // HKV
    nb = L // SWA_BLOCK
    qb = q.reshape(B, nb, SWA_BLOCK, HKV, G, Dh)

    def band(t):
        tp = jnp.pad(t, ((0, 0), (SWA_BLOCK, SWA_BLOCK), (0, 0), (0, 0))).reshape(B, nb + 2, SWA_BLOCK, HKV, Dh)
        return jnp.concatenate([tp[:, :-2], tp[:, 1:-1], tp[:, 2:]], axis=2)

    kb, vb = band(k), band(v)
    blk = np.arange(nb)[:, None, None]
    qpos = blk * SWA_BLOCK + np.arange(SWA_BLOCK)[None, :, None]
    kpos = (blk - 1) * SWA_BLOCK + np.arange(3 * SWA_BLOCK)[None, None, :]
    ok = (np.abs(qpos - kpos) <= SWA_WINDOW) & (kpos >= 0) & (kpos < L)
    s = jnp.einsum('bnqkgd,bnmkd->bkgnqm', qb, kb, preferred_element_type=jnp.float32) * ATTN_SCALE
    s = jnp.where(ok, s, NEG_INF)
    s_ctx = jnp.einsum('bnqkgd,bckd->bkgnqc', qb, kc, preferred_element_type=jnp.float32) * ATTN_SCALE
    sink_col = jnp.broadcast_to(sink.astype(jnp.float32).reshape(1, HKV, G, 1, 1, 1), s_ctx.shape[:-1] + (1,))
    n_lat = 3 * SWA_BLOCK
    n_ctx = kc.shape[1]
    p = jax.nn.softmax(jnp.concatenate([s, s_ctx, sink_col], axis=-1), axis=-1).astype(v.dtype)
    o = (jnp.einsum('bkgnqm,bnmkd->bnqkgd', p[..., :n_lat], vb)
         + jnp.einsum('bkgnqc,bckd->bnqkgd', p[..., n_lat:n_lat + n_ctx], vc))
    return o.reshape(B, L, HQ, Dh)


def _hgrn2_scan(q, k, v, logf, s0):
    B, L, H, DK = q.shape
    DV = v.shape[-1]
    n = L // HG_CHUNK

    def chunks(t):
        return t.reshape(B, n, HG_CHUNK, H, t.shape[-1]).transpose(1, 0, 3, 2, 4)

    tril = np.tril(np.ones((HG_CHUNK, HG_CHUNK), dtype=bool))

    def step(S, xs):
        qc, kc, vc, gc = xs
        b = jnp.cumsum(gc, axis=2)
        diff = b[:, :, :, None, :] - b[:, :, None, :, :]
        decay = jnp.exp(jnp.where(tril[:, :, None], diff, -jnp.inf))
        attn = jnp.einsum('bhtk,bhtsk,bhsk->bhts', qc, decay, kc)
        o = attn @ vc + jnp.einsum('bhtk,bhkv->bhtv', qc * jnp.exp(b), S)
        b_end = b[:, :, -1, :]
        S_new = (jnp.exp(b_end)[..., None] * S
                 + jnp.einsum('bhsk,bhsv->bhkv', kc * jnp.exp(b_end[:, :, None, :] - b), vc))
        return S_new, o

    S_final, o = lax.scan(step, s0, (chunks(q), chunks(k), chunks(v), chunks(logf)))
    return o.transpose(1, 0, 3, 2, 4).reshape(B, L, H, DV), S_final


def _hgrn2_final_state(k, logf, v):
    b = jnp.cumsum(logf, axis=1)
    return jnp.einsum('blhk,blhv->bhkv', k * jnp.exp(b[:, -1:] - b), v)


def _token_mixers(h, hc, w_in, lb, na_q_norm, na_k_norm, na_rpb, hg_norm_g, swa_q_norm, swa_k_norm, swa_sink, need_ctx):
    B, L, _ = h.shape
    splits = np.cumsum(IN_WIDTHS)[:-1].tolist()
    (na_q, na_k, na_v, hg_q, hg_ff, hg_fb, hg_i, hg_g, sw_q, sw_k, sw_v) = jnp.split(h @ w_in, splits, axis=-1)
    (cna_q, cna_k, cna_v, chg_q, chg_ff, chg_fb, chg_i, chg_g, csw_q, csw_k, csw_v) = jnp.split(hc @ w_in, splits, axis=-1)
    t = jnp.arange(L)
    row = t // GRID_W
    col = t % GRID_W

    ck_na = _rmsnorm(_heads(cna_k, NA_HEADS), na_k_norm)
    cv_na = _heads(cna_v, NA_HEADS)
    o_na = _neighborhood_attention(_rmsnorm(_heads(na_q, NA_HEADS), na_q_norm),
                                   _rmsnorm(_heads(na_k, NA_HEADS), na_k_norm),
                                   _heads(na_v, NA_HEADS), ck_na, cv_na, na_rpb)

    lbh = lb.reshape(HG_HEADS, HG_DK).astype(jnp.float32)

    def fgate(z):
        f = lbh + (1.0 - lbh) * jax.nn.sigmoid(_heads(z, HG_HEADS).astype(jnp.float32))
        return jnp.log(f), 1.0 - f

    def qin(z):
        return jax.nn.silu(_heads(z, HG_HEADS).astype(jnp.float32)) * HG_DK ** -0.5

    def readout(o, z):
        return (_rmsnorm(o, hg_norm_g) * jax.nn.silu(_heads(z, HG_HEADS).astype(jnp.float32))).astype(h.dtype)

    def flip(a):
        return a[:, ::-1]

    lf_f, k_f = fgate(hg_ff)
    lf_b, k_b = fgate(hg_fb)
    v_hg = _heads(hg_i, HG_HEADS).astype(jnp.float32)
    clf_f, ck_f = fgate(chg_ff)
    clf_b, ck_b = fgate(chg_fb)
    cv_hg = _heads(chg_i, HG_HEADS).astype(jnp.float32)
    if need_ctx:
        cq_hg = qin(chg_q)
        s0 = jnp.zeros((B, HG_HEADS, HG_DK, HG_DV), jnp.float32)
        oc_f, s_f = _hgrn2_scan(cq_hg, ck_f, cv_hg, clf_f, s0)
        oc_b, s_b = _hgrn2_scan(flip(cq_hg), flip(ck_b), flip(cv_hg), flip(clf_b), s0)
        oc_hg = readout(oc_f + flip(oc_b), chg_g)
    else:
        s_f = _hgrn2_final_state(ck_f, clf_f, cv_hg)
        s_b = _hgrn2_final_state(flip(ck_b), flip(clf_b), flip(cv_hg))
    q_hg = qin(hg_q)
    o_f, _ = _hgrn2_scan(q_hg, k_f, v_hg, lf_f, s_f)
    o_b, _ = _hgrn2_scan(flip(q_hg), flip(k_b), flip(v_hg), flip(lf_b), s_b)
    o_hg = readout(o_f + flip(o_b), hg_g)

    q_sw = _rope_2d(_rmsnorm(_heads(sw_q, SWA_Q_HEADS), swa_q_norm), row, col)
    k_sw = _rope_2d(_rmsnorm(_heads(sw_k, SWA_KV_HEADS), swa_k_norm), row, col)
    ck_sw = _rmsnorm(_heads(csw_k, SWA_KV_HEADS), swa_k_norm)
    cv_sw = _heads(csw_v, SWA_KV_HEADS)
    o_sw = _window_attention(q_sw, k_sw, _heads(sw_v, SWA_KV_HEADS), ck_sw, cv_sw, swa_sink)

    y = jnp.concatenate([o_na.reshape(B, L, NA_W), o_hg.reshape(B, L, HG_W), o_sw.reshape(B, L, SWA_W)], axis=-1)
    if not need_ctx:
        return y, None
    Lc = hc.shape[1]
    oc_na = _context_attention(_rmsnorm(_heads(cna_q, NA_HEADS), na_q_norm), ck_na, cv_na, None)
    oc_sw = _context_attention(_rmsnorm(_heads(csw_q, SWA_Q_HEADS), swa_q_norm), ck_sw, cv_sw, swa_sink)
    yc = jnp.concatenate([oc_na.reshape(B, Lc, NA_W), oc_hg.reshape(B, Lc, HG_W), oc_sw.reshape(B, Lc, SWA_W)], axis=-1)
    return y, yc


def _moe_ffn(h, router_w, router_b, w_gu, b_gu, w_down, b_down):
    T, D = h.shape
    logits = (h @ router_w).astype(jnp.float32) + router_b.astype(jnp.float32)
    top_v, top_e = lax.top_k(logits, TOP_K)
    gate = jax.nn.softmax(top_v, axis=-1)
    A = T * TOP_K
    flat_e = top_e.reshape(-1)
    flat_tok = jnp.repeat(jnp.arange(T, dtype=jnp.int32), TOP_K)
    flat_g = gate.reshape(-1)
    onehot = (flat_e[:, None] == jnp.arange(N_EXPERTS)[None, :]).astype(jnp.int32)
    counts = onehot.sum(axis=0)
    rank = jnp.take_along_axis(jnp.cumsum(onehot, axis=0), flat_e[:, None], axis=1)[:, 0] - 1
    padded = (counts + MOE_BLOCK - 1) // MOE_BLOCK * MOE_BLOCK
    pend = jnp.cumsum(padded)
    dest = (pend - padded)[flat_e] + rank
    n_blocks = (A + N_EXPERTS * (MOE_BLOCK - 1) + MOE_BLOCK - 1) // MOE_BLOCK
    P = n_blocks * MOE_BLOCK
    slot_tok = jnp.zeros((P,), jnp.int32).at[dest].set(flat_tok)
    slot_g = jnp.zeros((P,), jnp.float32).at[dest].set(flat_g)
    block_e = jnp.minimum(jnp.searchsorted(pend, jnp.arange(n_blocks) * MOE_BLOCK, side='right'), N_EXPERTS - 1)

    def step(out, xs):
        tok, g, e = xs
        gu = h[tok] @ w_gu[e] + b_gu[e]
        glu = jnp.minimum(gu[:, :D_FF_EXPERT], SWIGLU_LIMIT)
        lin = jnp.clip(gu[:, D_FF_EXPERT:], -SWIGLU_LIMIT, SWIGLU_LIMIT)
        act = glu * jax.nn.sigmoid(SWIGLU_ALPHA * glu) * (lin + 1.0)
        y = act @ w_down[e] + b_down[e]
        return out.at[tok].add((y * g[:, None]).astype(out.dtype)), None

    out, _ = lax.scan(step, jnp.zeros_like(h),
                      (slot_tok.reshape(n_blocks, MOE_BLOCK), slot_g.reshape(n_blocks, MOE_BLOCK), block_e))
    return out


def setup_inputs(seed: int = 0) -> dict:
    key = jax.random.key(seed)
    keys = iter(jax.random.split(key, 24))
    D = D_MODEL

    def nrm(shape, s):
        return jax.random.normal(next(keys), shape, jnp.float32) * s

    return {
        'x': nrm((BATCH, SEQ, D), 1.0),
        'c': nrm((BATCH, D), 1.0),
        'ctx': nrm((BATCH, CTX_LEN, D), 1.0),
        'c_ctx': nrm((D,), 1.0),
        'hg_lower_bounds': nrm((DEPTH, HG_KW), 1.0),
        'ada_w': nrm((DEPTH, D, 6 * D), 0.5 * D ** -0.5),
        'ada_b': nrm((DEPTH, 6 * D), 0.02),
        'norm1_g': 1.0 + nrm((DEPTH, D), 0.02),
        'norm2_g': 1.0 + nrm((DEPTH, D), 0.02),
        'w_in': nrm((DEPTH, D, IN_COLS), D ** -0.5),
        'na_q_norm': 1.0 + nrm((DEPTH, HEAD_DIM), 0.02),
        'na_k_norm': 1.0 + nrm((DEPTH, HEAD_DIM), 0.02),
        'na_rpb': nrm((DEPTH, NA_HEADS, 2 * NA_WIN_H - 1, 2 * NA_WIN_W - 1), 0.1),
        'hg_norm_g': 1.0 + nrm((DEPTH, HG_DV), 0.02),
        'swa_q_norm': 1.0 + nrm((DEPTH, HEAD_DIM), 0.02),
        'swa_k_norm': 1.0 + nrm((DEPTH, HEAD_DIM), 0.02),
        'swa_sink': nrm((DEPTH, SWA_Q_HEADS), 0.5),
        'w_out': nrm((DEPTH, MIX_W, D), MIX_W ** -0.5),
        'router_w': nrm((DEPTH, D, N_EXPERTS), D ** -0.5),
        'router_b': nrm((DEPTH, N_EXPERTS), 0.01),
        'w_gu': nrm((DEPTH, N_EXPERTS, D, 2 * D_FF_EXPERT), D ** -0.5),
        'b_gu': nrm((DEPTH, N_EXPERTS, 2 * D_FF_EXPERT), 0.01),
        'w_down': nrm((DEPTH, N_EXPERTS, D_FF_EXPERT, D), D_FF_EXPERT ** -0.5),
        'b_down': nrm((DEPTH, N_EXPERTS, D), 0.01),
    }


def reference(x, c, ctx, c_ctx, hg_lower_bounds, ada_w, ada_b, norm1_g, norm2_g, w_in, na_q_norm, na_k_norm,
              na_rpb, hg_norm_g, swa_q_norm, swa_k_norm, swa_sink, w_out, router_w, router_b, w_gu, b_gu,
              w_down, b_down):
    B, L, D = x.shape
    Lc = ctx.shape[1]
    p_lb = jax.nn.softmax(hg_lower_bounds.astype(jnp.float32), axis=0)
    lbs = jnp.cumsum(p_lb, axis=0) - p_lb[0]
    for l in range(DEPTH):
        last = l == DEPTH - 1
        n_mod_c = 2 if last else 6
        mod = jax.nn.silu(c) @ ada_w[l] + ada_b[l]
        sh1, sc1, g1, sh2, sc2, g2 = [m[:, None, :] for m in jnp.split(mod, 6, axis=-1)]
        cmods = jnp.split(jax.nn.silu(c_ctx) @ ada_w[l][:, :n_mod_c * D] + ada_b[l][:n_mod_c * D], n_mod_c)
        h = _rmsnorm(x, norm1_g[l]) * (1.0 + sc1) + sh1
        hc = _rmsnorm(ctx, norm1_g[l]) * (1.0 + cmods[1]) + cmods[0]
        y, yc = _token_mixers(h, hc, w_in[l], lbs[l], na_q_norm[l], na_k_norm[l], na_rpb[l], hg_norm_g[l],
                              swa_q_norm[l], swa_k_norm[l], swa_sink[l], not last)
        x = x + g1 * (y @ w_out[l])
        h2 = _rmsnorm(x, norm2_g[l]) * (1.0 + sc2) + sh2
        if last:
            f = _moe_ffn(h2.reshape(B * L, D), router_w[l], router_b[l], w_gu[l], b_gu[l], w_down[l], b_down[l])
            x = x + g2 * f.reshape(B, L, D)
        else:
            ctx = ctx + cmods[2] * (yc @ w_out[l])
            hc2 = _rmsnorm(ctx, norm2_g[l]) * (1.0 + cmods[4]) + cmods[3]
            f = _moe_ffn(jnp.concatenate([hc2.reshape(B * Lc, D), h2.reshape(B * L, D)], axis=0),
                         router_w[l], router_b[l], w_gu[l], b_gu[l], w_down[l], b_down[l])
            ctx = ctx + cmods[5] * f[:B * Lc].reshape(B, Lc, D)
            x = x + g2 * f[B * Lc:].reshape(B, L, D)
    return x
```

```python
import functools

import numpy as np
import jax
import jax.numpy as jnp
from jax import lax
from jax.experimental import pallas as pl
from jax.experimental.pallas import tpu as pltpu

D_MODEL = 1024
GRID_W = 64
HEAD_DIM = 64
ATTN_SCALE = HEAD_DIM ** -0.5
NA_HEADS = 4
NA_WIN_H = 8
NA_WIN_W = 16
HG_HEADS = 4
HG_DK = 128
SWA_Q_HEADS = 4
SWA_KV_HEADS = 2
SWA_WINDOW = 128
SWA_BLOCK = 128
ROPE_BASE = 10000.0
N_EXPERTS = 32
TOP_K = 4
D_FF = 1024
SWIGLU_LIMIT = 7.0
SWIGLU_ALPHA = 1.702
NORM_EPS = 1e-6
NEG_INF = -1e30

NA_W = NA_HEADS * HEAD_DIM
HG_W = HG_HEADS * HG_DK
SWA_W = SWA_Q_HEADS * HEAD_DIM
SWA_KV_W = SWA_KV_HEADS * HEAD_DIM
HG_COLS = 5 * HG_W
IN_COLS_EXT = 3 * NA_W + HG_COLS + 3 * SWA_W

LANES = 128
ROW_BLOCK = 256
NA_QROWS = 4
NA_KROWS = 12
HG_CHUNK = 64
HG_SUB = 8
HG_STEP_ROWS = 128
MOE_ROWS = 256
GATHER_ROWS = 256
VMEM_LIMIT = 56 * 1024 * 1024

F32 = jnp.float32
BF16 = jnp.bfloat16
HIGHEST = lax.Precision.HIGHEST


def _cparams(*sem):
    return pltpu.CompilerParams(dimension_semantics=sem, vmem_limit_bytes=VMEM_LIMIT)


def _dot(a, b, precision=None):
    return jnp.dot(a, b, preferred_element_type=F32, precision=precision)


def _dot_nt(a, b):
    return lax.dot_general(a, b, (((1,), (1,)), ((), ())), preferred_element_type=F32)


def _dot_tn(a, b):
    return lax.dot_general(a, b, (((0,), (0,)), ((), ())), preferred_element_type=F32)


def _sigmoid(x):
    return 1.0 / (1.0 + jnp.exp(-x))


def _silu(x):
    return x * _sigmoid(x)


def _ada_kernel(c_ref, w_ref, b_ref, o_ref):
    o_ref[...] = _dot(_silu(c_ref[...]), w_ref[...], HIGHEST) + b_ref[...]


def _ada_mod(cc, w, b):
    m, d = cc.shape
    n = w.shape[1]
    tn = 1024
    return pl.pallas_call(
        _ada_kernel,
        grid=(n // tn,),
        in_specs=[pl.BlockSpec((m, d), lambda j: (0, 0)),
                  pl.BlockSpec((d, tn), lambda j: (0, j)),
                  pl.BlockSpec((1, tn), lambda j: (0, j))],
        out_specs=pl.BlockSpec((m, tn), lambda j: (0, j)),
        out_shape=jax.ShapeDtypeStruct((m, n), F32),
        compiler_params=_cparams("parallel"),
        name="ada_mod",
    )(cc, w, b.reshape(1, n))


def _mod_rmsnorm(x, g, mod, shift_row, scale_row):
    y = x * lax.rsqrt(jnp.mean(x * x, axis=-1, keepdims=True) + NORM_EPS) * g
    return y * (1.0 + mod[scale_row:scale_row + 1]) + mod[shift_row:shift_row + 1]


def _head_rmsnorm(x, w, group_ones):
    ss = _dot(x * x, group_ones, HIGHEST)
    return x * lax.rsqrt(ss * (1.0 / HEAD_DIM) + NORM_EPS) * w


def _rope(x, cos, sin_signed):
    n = x.shape[-1]
    lane = lax.broadcasted_iota(jnp.int32, x.shape, 1)
    quarter = HEAD_DIM // 4
    partner = jnp.where(lane % (2 * quarter) < quarter,
                        pltpu.roll(x, n - quarter, 1), pltpu.roll(x, quarter, 1))
    return x * cos + partner * sin_signed


def _inproj_kernel(x_ref, mod_ref, g_ref, w_ref, naq_w, nak_w, swq_w, swk_w, cos_ref, sin_ref, ones_ref,
                   hg_ref, naq_ref, nak_ref, nav_ref, swq_ref, swk_ref, swv_ref):
    h = _mod_rmsnorm(x_ref[...], g_ref[...], mod_ref[...], 0, 1).astype(BF16)
    ones = ones_ref[...]

    def proj(lo, width):
        return _dot(h, w_ref[:, lo:lo + width])

    naq_ref[...] = (_head_rmsnorm(proj(0, NA_W), naq_w[...], ones) * ATTN_SCALE).astype(BF16)
    nak_ref[...] = _head_rmsnorm(proj(NA_W, NA_W), nak_w[...], ones).astype(BF16)
    nav_ref[...] = proj(2 * NA_W, NA_W).astype(BF16)
    base = 3 * NA_W
    for j in range(HG_COLS // HG_W):
        hg_ref[:, j * HG_W:(j + 1) * HG_W] = proj(base + j * HG_W, HG_W)
    base += HG_COLS
    cos, sin = cos_ref[...], sin_ref[...]
    swq = _rope(_head_rmsnorm(proj(base, SWA_W), swq_w[...], ones), cos, sin)
    swq_ref[...] = (swq * ATTN_SCALE).astype(BF16)
    swk_ref[...] = _rope(_head_rmsnorm(proj(base + SWA_W, SWA_W), swk_w[...], ones), cos, sin).astype(BF16)
    swv_ref[...] = proj(base + 2 * SWA_W, SWA_W).astype(BF16)


def _mod_row(i, n_ctx_blocks, blocks_per_batch, n_batch):
    return jnp.where(i < n_ctx_blocks, n_batch, (i - n_ctx_blocks) // blocks_per_batch)


def _inproj(x_all, mods, norm_g, w_ext, naq_w, nak_w, swq_w, swk_w, cos_t, sin_t, ones, geom):
    n_batch, n_ctx_blocks, blocks_per_batch = geom
    t = x_all.shape[0]
    tm = ROW_BLOCK
    row = lambda w: pl.BlockSpec((tm, w), lambda i: (i, 0))
    const = lambda a: pl.BlockSpec(a.shape, lambda i: (0,) * a.ndim)
    rope_blk = lambda i: (jnp.where(i < n_ctx_blocks, 0, 1 + (i - n_ctx_blocks) % blocks_per_batch), 0)
    bf = lambda w: jax.ShapeDtypeStruct((t, w), BF16)
    return pl.pallas_call(
        _inproj_kernel,
        grid=(t // tm,),
        in_specs=[row(D_MODEL),
                  pl.BlockSpec((None, 6, D_MODEL),
                               lambda i: (_mod_row(i, n_ctx_blocks, blocks_per_batch, n_batch), 0, 0)),
                  const(norm_g), const(w_ext), const(naq_w), const(nak_w), const(swq_w), const(swk_w),
                  pl.BlockSpec((tm, SWA_W), rope_blk), pl.BlockSpec((tm, SWA_W), rope_blk), const(ones)],
        out_specs=[row(HG_COLS), row(NA_W), row(NA_W), row(NA_W), row(SWA_W), row(SWA_W), row(SWA_W)],
        out_shape=[jax.ShapeDtypeStruct((t, HG_COLS), F32), bf(NA_W), bf(NA_W), bf(NA_W),
                   bf(SWA_W), bf(SWA_W), bf(SWA_W)],
        compiler_params=_cparams("parallel"),
        name="inproj",
    )(x_all, mods, norm_g, w_ext, naq_w, nak_w, swq_w, swk_w, cos_t, sin_t, ones)


def _attend(q, parts, sink_vals):
    m_rows = q.shape[0]
    lane = lax.broadcasted_iota(jnp.int32, (m_rows, q.shape[1]), 1)
    out = jnp.zeros((m_rows, q.shape[1]), F32)
    for h in range(q.shape[1] // HEAD_DIM):
        in_head = (lane // HEAD_DIM) == h
        qh = jnp.where(in_head, q, jnp.zeros_like(q))
        scores = []
        for k, _, add in parts:
            s = _dot_nt(qh, k)
            if add is not None:
                s = s + (add[h] if add.ndim == 3 else add)
            scores.append(s)
        mx = functools.reduce(jnp.maximum, [jnp.max(s, axis=-1, keepdims=True) for s in scores])
        if sink_vals is not None:
            mx = jnp.maximum(mx, sink_vals[h])
        ps = [jnp.exp(s - mx) for s in scores]
        den = functools.reduce(jnp.add, [jnp.sum(p, axis=-1, keepdims=True) for p in ps])
        if sink_vals is not None:
            den = den + jnp.exp(sink_vals[h] - mx)
        acc = functools.reduce(jnp.add, [_dot(p.astype(BF16), v) for p, (_, v, _) in zip(ps, parts)])
        out = jnp.where(in_head, acc / den, out)
    return out


def _na_kernel(q_ref, k0, k1, k2, kc, v0, v1, v2, vc, bias_ref, o_ref):
    k_lat = jnp.concatenate([k0[...], k1[...], k2[...]], axis=0)
    v_lat = jnp.concatenate([v0[...], v1[...], v2[...]], axis=0)
    o = _attend(q_ref[...], [(k_lat, v_lat, bias_ref[...]), (kc[...], vc[...], None)], None)
    o_ref[...] = o.astype(o_ref.dtype)


def _na_attention(naq, nak, nav, bias, geom):
    n_batch, n_ctx_blocks, blocks_per_batch = geom
    tm = ROW_BLOCK
    n_rb = blocks_per_batch
    kb_max = n_rb - NA_KROWS // NA_QROWS
    lat = lambda rb, b: n_ctx_blocks + b * blocks_per_batch

    def band(j):
        return pl.BlockSpec((tm, NA_W), lambda rb, b: (lat(rb, b) + jnp.clip(rb - 1, 0, kb_max) + j, 0))

    ctx = pl.BlockSpec((tm, NA_W), lambda rb, b: (b, 0))
    return pl.pallas_call(
        _na_kernel,
        grid=(n_rb, n_batch),
        in_specs=[pl.BlockSpec((tm, NA_W), lambda rb, b: (lat(rb, b) + rb, 0)),
                  band(0), band(1), band(2), ctx, band(0), band(1), band(2), ctx,
                  pl.BlockSpec((None, NA_HEADS, tm, NA_KROWS * GRID_W), lambda rb, b: (rb, 0, 0, 0))],
        out_specs=pl.BlockSpec((tm, NA_W), lambda rb, b: (b * blocks_per_batch + rb, 0)),
        out_shape=jax.ShapeDtypeStruct((n_batch * blocks_per_batch * tm, NA_W), BF16),
        compiler_params=_cparams("parallel", "parallel"),
        name="na_attn",
    )(naq, nak, nak, nak, nak, nav, nav, nav, nav, bias)


def _swa_kernel(sink_ref, q_ref, kp, kc_, kn, kx, vp, vc_, vn, vx, o_ref, *, n_blocks):
    n = pl.program_id(1)
    tq = q_ref.shape[0]
    k_lat = jnp.concatenate([kp[...], kc_[...], kn[...]], axis=0)
    v_lat = jnp.concatenate([vp[...], vc_[...], vn[...]], axis=0)
    qi = lax.broadcasted_iota(jnp.int32, (tq, 3 * tq), 0)
    mi = lax.broadcasted_iota(jnp.int32, (tq, 3 * tq), 1)
    kpos = (n - 1) * tq + mi
    ok = (jnp.abs(qi + tq - mi) <= SWA_WINDOW) & (kpos >= 0) & (kpos < n_blocks * tq)
    add = jnp.where(ok, 0.0, NEG_INF).astype(F32)
    sinks = [sink_ref[h] for h in range(SWA_Q_HEADS)]
    o = _attend(q_ref[...], [(k_lat, v_lat, add), (kx[...], vx[...], None)], sinks)
    o_ref[...] = o.astype(o_ref.dtype)


def _swa_attention(swq, swk, swv, sink, geom, seq, ctx_len):
    n_batch, _, _ = geom
    tq = SWA_BLOCK
    nb = seq // tq
    first = n_batch * ctx_len // tq
    blk = lambda f: pl.BlockSpec((tq, SWA_W), lambda b, n: (first + b * nb + f(n), 0))
    prev, cur, nxt = blk(lambda n: jnp.maximum(n - 1, 0)), blk(lambda n: n), blk(lambda n: jnp.minimum(n + 1, nb - 1))
    ctx = pl.BlockSpec((ctx_len, SWA_W), lambda b, n: (b, 0))
    return pl.pallas_call(
        functools.partial(_swa_kernel, n_blocks=nb),
        grid=(n_batch, nb),
        in_specs=[pl.BlockSpec(memory_space=pltpu.SMEM), cur, prev, cur, nxt, ctx, prev, cur, nxt, ctx],
        out_specs=pl.BlockSpec((tq, SWA_W), lambda b, n: (b * nb + n, 0)),
        out_shape=jax.ShapeDtypeStruct((n_batch * seq, SWA_W), BF16),
        compiler_params=_cparams("parallel", "parallel"),
        name="swa_attn",
    )(sink, swq, swk, swk, swk, swk, swv, swv, swv, swv)


def _ctx_attn_kernel(sink_ref, naq, nak, nav, swq, swk, swv, ona_ref, osw_ref):
    ona_ref[...] = _attend(naq[...], [(nak[...], nav[...], None)], None).astype(ona_ref.dtype)
    sinks = [sink_ref[h] for h in range(SWA_Q_HEADS)]
    osw_ref[...] = _attend(swq[...], [(swk[...], swv[...], None)], sinks).astype(osw_ref.dtype)


def _ctx_attention(naq, nak, nav, swq, swk, swv, sink, n_batch, ctx_len):
    blk = pl.BlockSpec((ctx_len, NA_W), lambda b: (b, 0))
    out = jax.ShapeDtypeStruct((n_batch * ctx_len, NA_W), BF16)
    return pl.pallas_call(
        _ctx_attn_kernel,
        grid=(n_batch,),
        in_specs=[pl.BlockSpec(memory_space=pltpu.SMEM)] + [blk] * 6,
        out_specs=[blk, blk],
        out_shape=[out, out],
        compiler_params=_cparams("parallel"),
        name="ctx_attn",
    )(sink, naq, nak, nav, swq, swk, swv)


def _hgrn_chunk(q, k, g, v, st, tri, ones, reverse):
    n_rows = q.shape[0]
    c = HG_SUB
    n_sub = n_rows // c
    cum = _dot(tri, g, HIGHEST)
    tot = cum[0:1] if reverse else cum[n_rows - 1:n_rows]
    o_inter = _dot_nt((q * jnp.exp(cum)).astype(BF16), st.astype(BF16))
    k_end = (k * jnp.exp(tot - cum)).astype(BF16)
    st_new = st * jnp.exp(tot) + _dot_tn(v.astype(BF16), k_end)

    rows = lax.broadcasted_iota(jnp.int32, (c, LANES), 0)
    direct = []
    for i in range(n_sub):
        sl = slice(c * i, c * i + c)
        qi, ki, ci = q[sl], k[sl], cum[sl]
        for s in range(c):
            seen = (rows <= s) if reverse else (rows >= s)
            decay = jnp.exp(jnp.where(seen, ci - ci[s:s + 1], NEG_INF))
            direct.append(qi * ki[s:s + 1] * decay)
    a_rep = _dot(jnp.concatenate(direct, axis=0).astype(BF16), ones)

    v_b = v.astype(BF16)
    outs = []
    for i in range(n_sub):
        sl = slice(c * i, c * i + c)
        vi = v[sl]
        o_i = o_inter[sl]
        for s in range(c):
            r0 = (i * c + s) * c
            o_i = o_i + a_rep[r0:r0 + c] * vi[s:s + 1]
        if reverse:
            src = slice(c * (i + 1), n_rows) if i < n_sub - 1 else None
            ref_row = c * (i + 1)
        else:
            src = slice(0, c * i) if i > 0 else None
            ref_row = c * i - 1
        if src is not None:
            ref = cum[ref_row:ref_row + 1]
            q_t = (q[sl] * jnp.exp(cum[sl] - ref)).astype(BF16)
            k_t = (k[src] * jnp.exp(ref - cum[src])).astype(BF16)
            o_i = o_i + _dot(_dot_nt(q_t, k_t).astype(BF16), v_b[src])
        outs.append(o_i)
    return jnp.concatenate(outs, axis=0), st_new


def _hgrn_gates(zq, zf, lb):
    f = lb + (1.0 - lb) * _sigmoid(zf)
    return _silu(zq) * (HG_DK ** -0.5), 1.0 - f, jnp.log(f)


def _hgrn_kernel(zq_f, zf_f, zi_f, zq_b, zf_b, zi_b, lb_ref, trif_ref, trib_ref, ones_ref,
                 of_ref, ob_ref, st_f, st_b):
    @pl.when(pl.program_id(2) == 0)
    def _():
        st_f[...] = jnp.zeros_like(st_f)
        st_b[...] = jnp.zeros_like(st_b)

    lb = lb_ref[...]
    ones = ones_ref[...]
    n_chunks = zq_f.shape[0] // HG_CHUNK
    q, k, g = _hgrn_gates(zq_f[...], zf_f[...], lb)
    v = zi_f[...]
    st = st_f[...]
    for j in range(n_chunks):
        sl = slice(j * HG_CHUNK, (j + 1) * HG_CHUNK)
        o, st = _hgrn_chunk(q[sl], k[sl], g[sl], v[sl], st, trif_ref[...], ones, False)
        of_ref[sl, :] = o
    st_f[...] = st
    q, k, g = _hgrn_gates(zq_b[...], zf_b[...], lb)
    v = zi_b[...]
    st = st_b[...]
    for j in reversed(range(n_chunks)):
        sl = slice(j * HG_CHUNK, (j + 1) * HG_CHUNK)
        o, st = _hgrn_chunk(q[sl], k[sl], g[sl], v[sl], st, trib_ref[...], ones, True)
        ob_ref[sl, :] = o
    st_b[...] = st


def _hgrn(hg, lb, tri_f, tri_b, ones, n_batch, seq, ctx_len):
    tr = HG_STEP_ROWS
    nc, nl = ctx_len // tr, seq // tr
    first = n_batch * nc

    def fwd(b, j):
        return jnp.where(j < nc, b * nc + j, first + b * nl + (j - nc))

    def bwd(b, j):
        return jnp.where(j < nc, b * nc + (nc - 1 - j), first + b * nl + (nl - 1 - (j - nc)))

    heads = HG_HEADS
    col = lambda row_fn, group: pl.BlockSpec((tr, HG_DK), lambda b, h, j: (row_fn(b, j), group * heads + h))
    const = lambda a: pl.BlockSpec(a.shape, lambda b, h, j: (0, 0))
    out = jax.ShapeDtypeStruct((hg.shape[0], HG_W), F32)
    return pl.pallas_call(
        _hgrn_kernel,
        grid=(n_batch, heads, nc + nl),
        in_specs=[col(fwd, 0), col(fwd, 1), col(fwd, 3), col(bwd, 0), col(bwd, 2), col(bwd, 3),
                  pl.BlockSpec((1, HG_DK), lambda b, h, j: (0, h)), const(tri_f), const(tri_b), const(ones)],
        out_specs=[pl.BlockSpec((tr, HG_DK), lambda b, h, j: (fwd(b, j), h)),
                   pl.BlockSpec((tr, HG_DK), lambda b, h, j: (bwd(b, j), h))],
        out_shape=[out, out],
        scratch_shapes=[pltpu.VMEM((HG_DK, HG_DK), F32), pltpu.VMEM((HG_DK, HG_DK), F32)],
        compiler_params=_cparams("parallel", "parallel", "arbitrary"),
        name="hgrn",
    )(hg, hg, hg, hg, hg, hg, lb, tri_f, tri_b, ones)


def _mixout_kernel(x_ref, yna_ref, of_ref, ob_ref, zg_ref, ysw_ref, hgn_ref, w_ref, mod_ref, g2_ref,
                   rw_ref, rb_ref, xo_ref, h2_ref, te_ref, tg_ref):
    o = of_ref[...] + ob_ref[...]
    zg = zg_ref[...]
    parts = [yna_ref[...]]
    for h in range(HG_HEADS):
        sl = slice(h * HG_DK, (h + 1) * HG_DK)
        oh = o[:, sl]
        yh = oh * lax.rsqrt(jnp.mean(oh * oh, axis=-1, keepdims=True) + NORM_EPS) * hgn_ref[...]
        parts.append((yh * _silu(zg[:, sl])).astype(BF16))
    parts.append(ysw_ref[...])
    y = _dot(jnp.concatenate(parts, axis=1), w_ref[...])
    mod = mod_ref[...]
    x_new = x_ref[...] + mod[2:3] * y
    xo_ref[...] = x_new
    h2 = _mod_rmsnorm(x_new, g2_ref[...], mod, 3, 4)
    h2_ref[...] = h2

    logits = _dot(h2, rw_ref[...], HIGHEST) + rb_ref[...]
    lane = lax.broadcasted_iota(jnp.int32, logits.shape, 1).astype(F32)
    top_e = jnp.zeros(logits.shape, F32)
    top_v = jnp.full(logits.shape, NEG_INF, F32)
    for j in range(TOP_K):
        best = jnp.max(logits, axis=-1, keepdims=True)
        arg = jnp.min(jnp.where(logits == best, lane, float(LANES)), axis=-1, keepdims=True)
        top_e = jnp.where(lane == j, arg, top_e)
        top_v = jnp.where(lane == j, best, top_v)
        logits = jnp.where(lane == arg, -jnp.inf, logits)
    ex = jnp.exp(top_v - jnp.max(top_v, axis=-1, keepdims=True))
    te_ref[...] = top_e.astype(jnp.int32)
    tg_ref[...] = ex / jnp.sum(ex, axis=-1, keepdims=True)


def _mixout(x, yna, o_f, o_b, hg, ysw, hgn, w_out, mods, norm2_g, rw, rb, geom, stream_off):
    n_batch, n_ctx_blocks, blocks_per_batch = geom
    t = x.shape[0]
    tm = ROW_BLOCK
    row = lambda w: pl.BlockSpec((tm, w), lambda i: (i, 0))
    full = lambda w, cb: pl.BlockSpec((tm, w), lambda i: (i + stream_off, cb))
    const = lambda a: pl.BlockSpec(a.shape, lambda i: (0,) * a.ndim)
    return pl.pallas_call(
        _mixout_kernel,
        grid=(t // tm,),
        in_specs=[row(D_MODEL), row(NA_W), full(HG_W, 0), full(HG_W, 0), full(HG_W, 4), row(SWA_W),
                  const(hgn), const(w_out),
                  pl.BlockSpec((None, 6, D_MODEL), lambda i: (
                      _mod_row(i + stream_off, n_ctx_blocks, blocks_per_batch, n_batch), 0, 0)),
                  const(norm2_g), const(rw), const(rb)],
        out_specs=[row(D_MODEL), row(D_MODEL), row(LANES), row(LANES)],
        out_shape=[jax.ShapeDtypeStruct((t, D_MODEL), F32), jax.ShapeDtypeStruct((t, D_MODEL), F32),
                   jax.ShapeDtypeStruct((t, LANES), jnp.int32), jax.ShapeDtypeStruct((t, LANES), F32)],
        compiler_params=_cparams("parallel"),
        name="mixout",
    )(x, yna, o_f, o_b, hg, ysw, hgn, w_out, mods, norm2_g, rw, rb)


def _gather_kernel(idx_ref, src_ref, o_ref, sem):
    n = o_ref.shape[0]

    def issue(r, carry):
        pltpu.make_async_copy(src_ref.at[pl.ds(idx_ref[0, 0, r], 1), :], o_ref.at[pl.ds(r, 1), :], sem).start()
        return carry

    lax.fori_loop(0, n, issue, 0)

    def drain(r, carry):
        pltpu.make_async_copy(src_ref.at[pl.ds(0, 1), :], o_ref.at[pl.ds(r, 1), :], sem).wait()
        return carry

    lax.fori_loop(0, n, drain, 0)


def _gather_rows(src, idx):
    n = idx.shape[0]
    tg = GATHER_ROWS
    d = src.shape[1]
    return pl.pallas_call(
        _gather_kernel,
        grid=(n // tg,),
        in_specs=[pl.BlockSpec((1, 1, tg), lambda i: (i, 0, 0), memory_space=pltpu.SMEM),
                  pl.BlockSpec(memory_space=pl.ANY)],
        out_specs=pl.BlockSpec((tg, d), lambda i: (i, 0)),
        out_shape=jax.ShapeDtypeStruct((n, d), src.dtype),
        scratch_shapes=[pltpu.SemaphoreType.DMA(())],
        compiler_params=_cparams("arbitrary"),
        name="gather_rows",
    )(idx.reshape(n // tg, 1, tg), src)


def _moe_kernel(be_ref, nu_ref, xs_ref, wgu_ref, bgu_ref, wd_ref, bd_ref, y_ref, wgu_bf, wd_bf):
    i = pl.program_id(0)
    changed = (i == 0) | (be_ref[i] != be_ref[jnp.maximum(i - 1, 0)])

    @pl.when(changed & (i < nu_ref[0]))
    def _():
        wgu_bf[...] = wgu_ref[...].astype(BF16)
        wd_bf[...] = wd_ref[...].astype(BF16)

    @pl.when(i < nu_ref[0])
    def _():
        gu = _dot(xs_ref[...].astype(BF16), wgu_bf[...]) + bgu_ref[...]
        glu = jnp.minimum(gu[:, :D_FF], SWIGLU_LIMIT)
        lin = jnp.clip(gu[:, D_FF:], -SWIGLU_LIMIT, SWIGLU_LIMIT)
        act = glu * _sigmoid(SWIGLU_ALPHA * glu) * (lin + 1.0)
        y_ref[...] = _dot(act.astype(BF16), wd_bf[...]) + bd_ref[...]

    @pl.when(i >= nu_ref[0])
    def _():
        y_ref[...] = jnp.zeros_like(y_ref)


def _moe_ffn(xs, block_e, n_used, w_gu, b_gu, w_down, b_down):
    p, d = xs.shape
    tm = MOE_ROWS
    n_e, _, f2 = w_gu.shape
    grid_spec = pltpu.PrefetchScalarGridSpec(
        num_scalar_prefetch=2,
        grid=(p // tm,),
        in_specs=[pl.BlockSpec((tm, d), lambda i, be, nu: (i, 0)),
                  pl.BlockSpec((None, d, f2), lambda i, be, nu: (be[i], 0, 0)),
                  pl.BlockSpec((None, 1, f2), lambda i, be, nu: (be[i], 0, 0)),
                  pl.BlockSpec((None, f2 // 2, d), lambda i, be, nu: (be[i], 0, 0)),
                  pl.BlockSpec((None, 1, d), lambda i, be, nu: (be[i], 0, 0))],
        out_specs=pl.BlockSpec((tm, d), lambda i, be, nu: (i, 0)),
        scratch_shapes=[pltpu.VMEM((d, f2), BF16), pltpu.VMEM((f2 // 2, d), BF16)],
    )
    return pl.pallas_call(
        _moe_kernel,
        grid_spec=grid_spec,
        out_shape=jax.ShapeDtypeStruct((p, d), F32),
        compiler_params=_cparams("arbitrary"),
        name="moe_ffn",
    )(block_e, n_used, xs, w_gu, b_gu.reshape(n_e, 1, f2), w_down, b_down.reshape(n_e, 1, d))


def _combine_kernel(x_ref, y0, y1, y2, y3, tg_ref, mod_ref, o_ref):
    gate = tg_ref[...]
    f = (y0[...] * gate[:, 0:1] + y1[...] * gate[:, 1:2]) + (y2[...] * gate[:, 2:3] + y3[...] * gate[:, 3:4])
    o_ref[...] = x_ref[...] + mod_ref[...][5:6] * f


def _combine(x, yg, gates, mods, geom, stream_off):
    n_batch, n_ctx_blocks, blocks_per_batch = geom
    t = x.shape[0]
    tm = ROW_BLOCK
    nb = t // tm
    row = lambda w: pl.BlockSpec((tm, w), lambda i: (i, 0))
    return pl.pallas_call(
        _combine_kernel,
        grid=(nb,),
        in_specs=[row(D_MODEL)] + [pl.BlockSpec((tm, D_MODEL), functools.partial(lambda i, j: (j * nb + i, 0), j=j))
                                   for j in range(TOP_K)]
        + [row(LANES), pl.BlockSpec((None, 6, D_MODEL), lambda i: (
            _mod_row(i + stream_off, n_ctx_blocks, blocks_per_batch, n_batch), 0, 0))],
        out_specs=row(D_MODEL),
        out_shape=jax.ShapeDtypeStruct((t, D_MODEL), F32),
        compiler_params=_cparams("parallel"),
        name="moe_combine",
    )(x, yg, yg, yg, yg, gates, mods)


def _route(top_e, n_experts, tm):
    t, k = top_e.shape
    a = t * k
    flat_e = top_e.reshape(-1)
    onehot = (flat_e[:, None] == jnp.arange(n_experts, dtype=jnp.int32)[None, :]).astype(jnp.int32)
    counts = onehot.sum(axis=0)
    rank = jnp.take_along_axis(jnp.cumsum(onehot, axis=0), flat_e[:, None], axis=1)[:, 0] - 1
    padded = (counts + tm - 1) // tm * tm
    pend = jnp.cumsum(padded)
    dest = ((pend - padded)[flat_e] + rank).astype(jnp.int32)
    n_blocks = (a + n_experts * (tm - 1) + tm - 1) // tm
    slot_tok = jnp.zeros((n_blocks * tm,), jnp.int32).at[dest].set(jnp.arange(a, dtype=jnp.int32) // k)
    block_e = jnp.minimum(jnp.searchsorted(pend, jnp.arange(n_blocks, dtype=jnp.int32) * tm, side='right'),
                          n_experts - 1).astype(jnp.int32)
    n_used = (pend[-1] // tm).astype(jnp.int32).reshape(1)
    return slot_tok, dest.reshape(t, k), block_e, n_used


def _na_bias_table(rpb, seq):
    n_rows = seq // GRID_W
    wr = min(NA_WIN_H, n_rows)
    n_rb = n_rows // NA_QROWS
    rb = np.arange(n_rb)[:, None, None]
    q = np.arange(NA_QROWS * GRID_W)[None, :, None]
    key = np.arange(NA_KROWS * GRID_W)[None, None, :]
    q_row, q_col = rb * NA_QROWS + q // GRID_W, q % GRID_W
    k_row = np.clip(rb - 1, 0, n_rb - NA_KROWS // NA_QROWS) * NA_QROWS + key // GRID_W
    k_col = key % GRID_W
    row_start = np.clip(q_row - wr // 2, 0, n_rows - wr)
    col_start = np.clip(q_col - NA_WIN_W // 2, 0, GRID_W - NA_WIN_W)
    ok = ((k_row >= row_start) & (k_row < row_start + wr) & (k_col >= col_start) & (k_col < col_start + NA_WIN_W))
    dr = np.clip(k_row - q_row + (NA_WIN_H - 1), 0, 2 * NA_WIN_H - 2)
    dc = np.clip(k_col - q_col + (NA_WIN_W - 1), 0, 2 * NA_WIN_W - 2)
    bias = rpb.astype(F32)[:, dr, dc]
    return jnp.where(ok[None], bias, NEG_INF).transpose(1, 0, 2, 3)


def _rope_tables(seq, ctx_len):
    quarter = HEAD_DIM // 4
    lane = np.arange(SWA_W)
    inv = ROPE_BASE ** (-(lane % quarter).astype(np.float64) / quarter)
    t = np.arange(seq)
    pos = np.where((lane % HEAD_DIM < HEAD_DIM // 2)[None, :], (t // GRID_W)[:, None], (t % GRID_W)[:, None])
    ang = jnp.asarray(pos, F32) * jnp.asarray(inv, F32)[None, :]
    sign = np.where(lane % (2 * quarter) < quarter, -1.0, 1.0).astype(np.float32)
    cos = jnp.concatenate([jnp.ones((ctx_len, SWA_W), F32), jnp.cos(ang)], axis=0)
    sin = jnp.concatenate([jnp.zeros((ctx_len, SWA_W), F32), jnp.sin(ang) * sign[None, :]], axis=0)
    return cos, sin


def kernel(x, c, ctx, c_ctx, hg_lower_bounds, ada_w, ada_b, norm1_g, norm2_g, w_in, na_q_norm, na_k_norm, na_rpb,
           hg_norm_g, swa_q_norm, swa_k_norm, swa_sink, w_out, router_w, router_b, w_gu, b_gu, w_down, b_down):
    n_batch, seq, d = x.shape
    ctx_len = ctx.shape[1]
    depth = ada_w.shape[0]
    assert d == D_MODEL and seq % ROW_BLOCK == 0 and ctx_len == ROW_BLOCK
    n_ctx_rows = n_batch * ctx_len
    geom = (n_batch, n_ctx_rows // ROW_BLOCK, seq // ROW_BLOCK)

    p_lb = jax.nn.softmax(hg_lower_bounds.astype(F32), axis=0)
    lbs = jnp.cumsum(p_lb, axis=0) - p_lb[0]

    cos_t, sin_t = _rope_tables(seq, ctx_len)
    lane = np.arange(NA_W)
    group_ones = jnp.asarray((lane[:, None] // HEAD_DIM == lane[None, :] // HEAD_DIM).astype(np.float32))
    tri = np.tril(np.ones((HG_CHUNK, HG_CHUNK), np.float32))
    tri_f, tri_b = jnp.asarray(tri), jnp.asarray(tri.T)
    ones_bf = jnp.ones((LANES, LANES), BF16)
    n_mod_rows = -(-(n_batch + 1) // 8) * 8
    cc = jnp.zeros((n_mod_rows, d), F32).at[:n_batch].set(c).at[n_batch].set(c_ctx)
    dup = lambda w: jnp.concatenate([w[:, :HEAD_DIM], w[:, :HEAD_DIM], w[:, HEAD_DIM:], w[:, HEAD_DIM:]], axis=1)
    tile4 = lambda g: jnp.tile(g.astype(F32), 4).reshape(1, 4 * HEAD_DIM)
    pad_e = LANES - N_EXPERTS

    x_all = jnp.concatenate([ctx.reshape(n_ctx_rows, d), x.reshape(n_batch * seq, d)], axis=0)
    for l in range(depth):
        last = l == depth - 1
        mods = _ada_mod(cc, ada_w[l], ada_b[l]).reshape(n_mod_rows, 6, d)
        w = w_in[l]
        kv0 = 3 * NA_W + HG_COLS + SWA_W
        w_ext = jnp.concatenate([w[:, :kv0], dup(w[:, kv0:kv0 + SWA_KV_W]), dup(w[:, kv0 + SWA_KV_W:])],
                                axis=1).astype(BF16)
        hg, naq, nak, nav, swq, swk, swv = _inproj(
            x_all, mods, norm1_g[l].reshape(1, d), w_ext, tile4(na_q_norm[l]), tile4(na_k_norm[l]),
            tile4(swa_q_norm[l]), tile4(swa_k_norm[l]), cos_t, sin_t, group_ones, geom)

        y_na = _na_attention(naq, nak, nav, _na_bias_table(na_rpb[l], seq), geom)
        y_sw = _swa_attention(swq, swk, swv, swa_sink[l].astype(F32), geom, seq, ctx_len)
        o_f, o_b = _hgrn(hg, lbs[l].reshape(1, HG_W), tri_f, tri_b, ones_bf, n_batch, seq, ctx_len)

        if last:
            stream_off = geom[1]
            x_cur = x_all[n_ctx_rows:]
        else:
            stream_off = 0
            x_cur = x_all
            yc_na, yc_sw = _ctx_attention(naq, nak, nav, swq, swk, swv, swa_sink[l].astype(F32), n_batch, ctx_len)
            y_na = jnp.concatenate([yc_na, y_na], axis=0)
            y_sw = jnp.concatenate([yc_sw, y_sw], axis=0)

        rw = jnp.pad(router_w[l].astype(F32), ((0, 0), (0, pad_e)))
        rb = jnp.pad(router_b[l].astype(F32), (0, pad_e), constant_values=NEG_INF).reshape(1, LANES)
        x_new, h2, top_e, top_g = _mixout(
            x_cur, y_na, o_f, o_b, hg, y_sw, hg_norm_g[l].reshape(1, HG_DK).astype(F32), w_out[l].astype(BF16),
            mods, norm2_g[l].reshape(1, d), rw, rb, geom, stream_off)

        slot_tok, dest, block_e, n_used = _route(top_e[:, :TOP_K], N_EXPERTS, MOE_ROWS)
        xs = _gather_rows(h2, slot_tok)
        y_slots = _moe_ffn(xs, block_e, n_used, w_gu[l], b_gu[l], w_down[l], b_down[l])
        y_tok = _gather_rows(y_slots, dest.T.reshape(-1))
        x_all = _combine(x_new, y_tok, top_g, mods, geom, stream_off)
    return x_all.reshape(n_batch, seq, d)
```

```python
import functools

import numpy as np
import jax
import jax.numpy as jnp
from jax import lax
from jax.experimental import pallas as pl
from jax.experimental.pallas import tpu as pltpu

D_MODEL = 1024
GRID_W = 64
HEAD_DIM = 64
ATTN_SCALE = HEAD_DIM ** -0.5
NA_HEADS = 4
NA_WIN_H = 8
NA_WIN_W = 16
HG_HEADS = 4
HG_DK = 128
SWA_Q_HEADS = 4
SWA_KV_HEADS = 2
SWA_WINDOW = 128
SWA_BLOCK = 128
ROPE_BASE = 10000.0
N_EXPERTS = 32
TOP_K = 4
D_FF = 1024
SWIGLU_LIMIT = 7.0
SWIGLU_ALPHA = 1.702
NORM_EPS = 1e-6
NEG_INF = -1e30

NA_W = NA_HEADS * HEAD_DIM
HG_W = HG_HEADS * HG_DK
SWA_W = SWA_Q_HEADS * HEAD_DIM
SWA_KV_W = SWA_KV_HEADS * HEAD_DIM
HG_COLS = 5 * HG_W
IN_COLS_EXT = 3 * NA_W + HG_COLS + 3 * SWA_W

LANES = 128
ROW_BLOCK = 256
NA_QROWS = 4
NA_KROWS = 12
HG_SUB = 8
HG_STEP_ROWS = 128
MOE_ROWS = 256
VMEM_LIMIT = 56 * 1024 * 1024

F32 = jnp.float32
BF16 = jnp.bfloat16
HIGHEST = lax.Precision.HIGHEST


def _cparams(*sem):
    return pltpu.CompilerParams(dimension_semantics=sem, vmem_limit_bytes=VMEM_LIMIT)


def _dot(a, b, precision=None):
    return jnp.dot(a, b, preferred_element_type=F32, precision=precision)


def _dot_nt(a, b):
    return lax.dot_general(a, b, (((1,), (1,)), ((), ())), preferred_element_type=F32)


def _dot_tn(a, b):
    return lax.dot_general(a, b, (((0,), (0,)), ((), ())), preferred_element_type=F32)


def _sigmoid(x):
    return 1.0 / (1.0 + jnp.exp(-x))


def _silu(x):
    return x * _sigmoid(x)


def _ada_kernel(c_ref, w_ref, b_ref, o_ref):
    o_ref[...] = _dot(_silu(c_ref[...]), w_ref[...], HIGHEST) + b_ref[...]


def _ada_mod(cc, w, b):
    m, d = cc.shape
    n = w.shape[1]
    tn = 1024
    return pl.pallas_call(
        _ada_kernel,
        grid=(n // tn,),
        in_specs=[pl.BlockSpec((m, d), lambda j: (0, 0)),
                  pl.BlockSpec((d, tn), lambda j: (0, j)),
                  pl.BlockSpec((1, tn), lambda j: (0, j))],
        out_specs=pl.BlockSpec((m, tn), lambda j: (0, j)),
        out_shape=jax.ShapeDtypeStruct((m, n), F32),
        compiler_params=_cparams("parallel"),
        name="ada_mod",
    )(cc, w, b.reshape(1, n))


def _mod_rmsnorm(x, g, mod, shift_row, scale_row):
    y = x * lax.rsqrt(jnp.mean(x * x, axis=-1, keepdims=True) + NORM_EPS) * g
    return y * (1.0 + mod[scale_row:scale_row + 1]) + mod[shift_row:shift_row + 1]


def _head_rmsnorm(x, w, group_ones):
    ss = _dot(x * x, group_ones, HIGHEST)
    return x * lax.rsqrt(ss * (1.0 / HEAD_DIM) + NORM_EPS) * w


def _rope(x, cos, sin_signed):
    n = x.shape[-1]
    lane = lax.broadcasted_iota(jnp.int32, x.shape, 1)
    quarter = HEAD_DIM // 4
    partner = jnp.where(lane % (2 * quarter) < quarter,
                        pltpu.roll(x, n - quarter, 1), pltpu.roll(x, quarter, 1))
    return x * cos + partner * sin_signed


def _inproj_kernel(x_ref, mod_ref, g_ref, w_ref, naq_w, nak_w, swq_w, swk_w, cos_ref, sin_ref, ones_ref,
                   hg_ref, naq_ref, nak_ref, nav_ref, swq_ref, swk_ref, swv_ref):
    h = _mod_rmsnorm(x_ref[...], g_ref[...], mod_ref[...], 0, 1).astype(BF16)
    ones = ones_ref[...]

    def proj(lo, width):
        return _dot(h, w_ref[:, lo:lo + width])

    naq_ref[...] = (_head_rmsnorm(proj(0, NA_W), naq_w[...], ones) * ATTN_SCALE).astype(BF16)
    nak_ref[...] = _head_rmsnorm(proj(NA_W, NA_W), nak_w[...], ones).astype(BF16)
    nav_ref[...] = proj(2 * NA_W, NA_W).astype(BF16)
    base = 3 * NA_W
    for j in range(HG_COLS // HG_W):
        hg_ref[:, j * HG_W:(j + 1) * HG_W] = proj(base + j * HG_W, HG_W)
    base += HG_COLS
    cos, sin = cos_ref[...], sin_ref[...]
    swq = _rope(_head_rmsnorm(proj(base, SWA_W), swq_w[...], ones), cos, sin)
    swq_ref[...] = (swq * ATTN_SCALE).astype(BF16)
    swk_ref[...] = _rope(_head_rmsnorm(proj(base + SWA_W, SWA_W), swk_w[...], ones), cos, sin).astype(BF16)
    swv_ref[...] = proj(base + 2 * SWA_W, SWA_W).astype(BF16)


def _mod_row(i, n_ctx_blocks, blocks_per_batch, n_batch):
    return jnp.where(i < n_ctx_blocks, n_batch, (i - n_ctx_blocks) // blocks_per_batch)


def _inproj(x_all, mods, norm_g, w_ext, naq_w, nak_w, swq_w, swk_w, cos_t, sin_t, ones, geom):
    n_batch, n_ctx_blocks, blocks_per_batch = geom
    t = x_all.shape[0]
    tm = ROW_BLOCK
    row = lambda w: pl.BlockSpec((tm, w), lambda i: (i, 0))
    const = lambda a: pl.BlockSpec(a.shape, lambda i: (0,) * a.ndim)
    rope_blk = lambda i: (jnp.where(i < n_ctx_blocks, 0, 1 + (i - n_ctx_blocks) % blocks_per_batch), 0)
    bf = lambda w: jax.ShapeDtypeStruct((t, w), BF16)
    return pl.pallas_call(
        _inproj_kernel,
        grid=(t // tm,),
        in_specs=[row(D_MODEL),
                  pl.BlockSpec((None, 6, D_MODEL),
                               lambda i: (_mod_row(i, n_ctx_blocks, blocks_per_batch, n_batch), 0, 0)),
                  const(norm_g), const(w_ext), const(naq_w), const(nak_w), const(swq_w), const(swk_w),
                  pl.BlockSpec((tm, SWA_W), rope_blk), pl.BlockSpec((tm, SWA_W), rope_blk), const(ones)],
        out_specs=[row(HG_COLS), row(NA_W), row(NA_W), row(NA_W), row(SWA_W), row(SWA_W), row(SWA_W)],
        out_shape=[jax.ShapeDtypeStruct((t, HG_COLS), F32), bf(NA_W), bf(NA_W), bf(NA_W),
                   bf(SWA_W), bf(SWA_W), bf(SWA_W)],
        compiler_params=_cparams("parallel"),
        name="inproj",
    )(x_all, mods, norm_g, w_ext, naq_w, nak_w, swq_w, swk_w, cos_t, sin_t, ones)


def _attend(q, parts, sink_vals):
    m_rows = q.shape[0]
    lane = lax.broadcasted_iota(jnp.int32, (m_rows, q.shape[1]), 1)
    out = jnp.zeros((m_rows, q.shape[1]), F32)
    for h in range(q.shape[1] // HEAD_DIM):
        in_head = (lane // HEAD_DIM) == h
        qh = jnp.where(in_head, q, jnp.zeros_like(q))
        scores = []
        for k, _, add in parts:
            s = _dot_nt(qh, k)
            if add is not None:
                s = s + (add[h] if add.ndim == 3 else add)
            scores.append(s)
        mx = functools.reduce(jnp.maximum, [jnp.max(s, axis=-1, keepdims=True) for s in scores])
        if sink_vals is not None:
            mx = jnp.maximum(mx, sink_vals[h])
        ps = [jnp.exp(s - mx) for s in scores]
        den = functools.reduce(jnp.add, [jnp.sum(p, axis=-1, keepdims=True) for p in ps])
        if sink_vals is not None:
            den = den + jnp.exp(sink_vals[h] - mx)
        acc = functools.reduce(jnp.add, [_dot(p.astype(BF16), v) for p, (_, v, _) in zip(ps, parts)])
        out = jnp.where(in_head, acc / den, out)
    return out


def _na_kernel(q_ref, k0, k1, k2, kc, v0, v1, v2, vc, bias_ref, o_ref):
    k_lat = jnp.concatenate([k0[...], k1[...], k2[...]], axis=0)
    v_lat = jnp.concatenate([v0[...], v1[...], v2[...]], axis=0)
    o = _attend(q_ref[...], [(k_lat, v_lat, bias_ref[...]), (kc[...], vc[...], None)], None)
    o_ref[...] = o.astype(o_ref.dtype)


def _na_attention(naq, nak, nav, bias, geom):
    n_batch, n_ctx_blocks, blocks_per_batch = geom
    tm = ROW_BLOCK
    n_rb = blocks_per_batch
    kb_max = n_rb - NA_KROWS // NA_QROWS
    lat = lambda rb, b: n_ctx_blocks + b * blocks_per_batch

    def band(j):
        return pl.BlockSpec((tm, NA_W), lambda rb, b: (lat(rb, b) + jnp.clip(rb - 1, 0, kb_max) + j, 0))

    ctx = pl.BlockSpec((tm, NA_W), lambda rb, b: (b, 0))
    return pl.pallas_call(
        _na_kernel,
        grid=(n_rb, n_batch),
        in_specs=[pl.BlockSpec((tm, NA_W), lambda rb, b: (lat(rb, b) + rb, 0)),
                  band(0), band(1), band(2), ctx, band(0), band(1), band(2), ctx,
                  pl.BlockSpec((None, NA_HEADS, tm, NA_KROWS * GRID_W), lambda rb, b: (
                      jnp.where(rb == 0, 0, jnp.where(rb == n_rb - 1, 2, 1)), 0, 0, 0))],
        out_specs=pl.BlockSpec((tm, NA_W), lambda rb, b: (b * blocks_per_batch + rb, 0)),
        out_shape=jax.ShapeDtypeStruct((n_batch * blocks_per_batch * tm, NA_W), BF16),
        compiler_params=_cparams("parallel", "parallel"),
        name="na_attn",
    )(naq, nak, nak, nak, nak, nav, nav, nav, nav, bias)


def _swa_kernel(sink_ref, q_ref, kp, kc_, kn, kx, vp, vc_, vn, vx, o_ref, *, n_blocks):
    n = pl.program_id(1)
    tq = q_ref.shape[0]
    k_lat = jnp.concatenate([kp[...], kc_[...], kn[...]], axis=0)
    v_lat = jnp.concatenate([vp[...], vc_[...], vn[...]], axis=0)
    qi = lax.broadcasted_iota(jnp.int32, (tq, 3 * tq), 0)
    mi = lax.broadcasted_iota(jnp.int32, (tq, 3 * tq), 1)
    kpos = (n - 1) * tq + mi
    ok = (jnp.abs(qi + tq - mi) <= SWA_WINDOW) & (kpos >= 0) & (kpos < n_blocks * tq)
    add = jnp.where(ok, 0.0, NEG_INF).astype(F32)
    sinks = [sink_ref[h] for h in range(SWA_Q_HEADS)]
    o = _attend(q_ref[...], [(k_lat, v_lat, add), (kx[...], vx[...], None)], sinks)
    o_ref[...] = o.astype(o_ref.dtype)


def _swa_attention(swq, swk, swv, sink, geom, seq, ctx_len):
    n_batch, _, _ = geom
    tq = SWA_BLOCK
    nb = seq // tq
    first = n_batch * ctx_len // tq
    blk = lambda f: pl.BlockSpec((tq, SWA_W), lambda b, n: (first + b * nb + f(n), 0))
    prev, cur, nxt = blk(lambda n: jnp.maximum(n - 1, 0)), blk(lambda n: n), blk(lambda n: jnp.minimum(n + 1, nb - 1))
    ctx = pl.BlockSpec((ctx_len, SWA_W), lambda b, n: (b, 0))
    return pl.pallas_call(
        functools.partial(_swa_kernel, n_blocks=nb),
        grid=(n_batch, nb),
        in_specs=[pl.BlockSpec(memory_space=pltpu.SMEM), cur, prev, cur, nxt, ctx, prev, cur, nxt, ctx],
        out_specs=pl.BlockSpec((tq, SWA_W), lambda b, n: (b * nb + n, 0)),
        out_shape=jax.ShapeDtypeStruct((n_batch * seq, SWA_W), BF16),
        compiler_params=_cparams("parallel", "parallel"),
        name="swa_attn",
    )(sink, swq, swk, swk, swk, swk, swv, swv, swv, swv)


def _ctx_attn_kernel(sink_ref, naq, nak, nav, swq, swk, swv, ona_ref, osw_ref):
    ona_ref[...] = _attend(naq[...], [(nak[...], nav[...], None)], None).astype(ona_ref.dtype)
    sinks = [sink_ref[h] for h in range(SWA_Q_HEADS)]
    osw_ref[...] = _attend(swq[...], [(swk[...], swv[...], None)], sinks).astype(osw_ref.dtype)


def _ctx_attention(naq, nak, nav, swq, swk, swv, sink, n_batch, ctx_len):
    blk = pl.BlockSpec((ctx_len, NA_W), lambda b: (b, 0))
    out = jax.ShapeDtypeStruct((n_batch * ctx_len, NA_W), BF16)
    return pl.pallas_call(
        _ctx_attn_kernel,
        grid=(n_batch,),
        in_specs=[pl.BlockSpec(memory_space=pltpu.SMEM)] + [blk] * 6,
        out_specs=[blk, blk],
        out_shape=[out, out],
        compiler_params=_cparams("parallel"),
        name="ctx_attn",
    )(sink, naq, nak, nav, swq, swk, swv)


def _hgrn_block(q, k, g, v, st, tri, ones, reverse):
    n_rows = q.shape[0]
    c = HG_SUB
    cum = _dot(tri, g, HIGHEST)
    tot = cum[0:1] if reverse else cum[n_rows - 1:n_rows]
    o_inter = _dot_nt((q * jnp.exp(cum)).astype(BF16), st.astype(BF16))
    k_end = (k * jnp.exp(tot - cum)).astype(BF16)
    st_new = st * jnp.exp(tot) + _dot_tn(v.astype(BF16), k_end)

    row = lax.broadcasted_iota(jnp.int32, (n_rows, LANES), 0)
    row_a = lax.broadcasted_iota(jnp.int32, (n_rows, n_rows), 0)
    col_a = lax.broadcasted_iota(jnp.int32, (n_rows, n_rows), 1)
    a = jnp.zeros((n_rows, n_rows), F32)
    size = 2 * c
    while size <= n_rows:
        half = size // 2
        ref_row = half if reverse else half - 1
        ref = jnp.concatenate([jnp.broadcast_to(cum[b * size + ref_row:b * size + ref_row + 1], (size, LANES))
                               for b in range(n_rows // size)], axis=0)
        later = (row % size < half) if reverse else (row % size >= half)
        q_t = q * jnp.exp(jnp.where(later, cum - ref, NEG_INF))
        k_t = k * jnp.exp(jnp.where(later, NEG_INF, ref - cum))
        a_l = _dot_nt(q_t.astype(BF16), k_t.astype(BF16))
        a = a + jnp.where(row_a // size == col_a // size, a_l, 0.0)
        size *= 2

    rows = lax.broadcasted_iota(jnp.int32, (c, LANES), 0)
    lanes = lax.broadcasted_iota(jnp.int32, (c, LANES), 1)
    direct = []
    for i in range(n_rows // c):
        sl = slice(c * i, c * i + c)
        qi, ki, ci = q[sl], k[sl], cum[sl]
        for s in range(c):
            seen = (rows <= s) if reverse else (rows >= s)
            direct.append(qi * ki[s:s + 1] * jnp.exp(jnp.where(seen, ci - ci[s:s + 1], NEG_INF)))
    a_rep = _dot(jnp.concatenate(direct, axis=0).astype(BF16), ones)
    tiles = []
    for i in range(n_rows // c):
        tile = jnp.zeros((c, LANES), F32)
        for s in range(c):
            r0 = (i * c + s) * c
            tile = jnp.where(lanes == i * c + s, a_rep[r0:r0 + c], tile)
        tiles.append(tile)
    a = a + jnp.concatenate(tiles, axis=0)
    return o_inter + _dot(a.astype(BF16), v.astype(BF16)), st_new


def _hgrn_gates(zq, zf, lb):
    f = lb + (1.0 - lb) * _sigmoid(zf)
    return _silu(zq) * (HG_DK ** -0.5), 1.0 - f, jnp.log(f)


def _hgrn_kernel(zq_f, zf_f, zi_f, zq_b, zf_b, zi_b, lb_ref, trif_ref, trib_ref, ones_ref,
                 of_ref, ob_ref, st_f, st_b):
    @pl.when(pl.program_id(2) == 0)
    def _():
        st_f[...] = jnp.zeros_like(st_f)
        st_b[...] = jnp.zeros_like(st_b)

    lb = lb_ref[...]
    ones = ones_ref[...]
    q, k, g = _hgrn_gates(zq_f[...], zf_f[...], lb)
    of_ref[...], st_f[...] = _hgrn_block(q, k, g, zi_f[...], st_f[...], trif_ref[...], ones, False)
    q, k, g = _hgrn_gates(zq_b[...], zf_b[...], lb)
    ob_ref[...], st_b[...] = _hgrn_block(q, k, g, zi_b[...], st_b[...], trib_ref[...], ones, True)


def _hgrn(hg, lb, tri_f, tri_b, ones, n_batch, seq, ctx_len):
    tr = HG_STEP_ROWS
    nc, nl = ctx_len // tr, seq // tr
    first = n_batch * nc

    def fwd(b, j):
        return jnp.where(j < nc, b * nc + j, first + b * nl + (j - nc))

    def bwd(b, j):
        return jnp.where(j < nc, b * nc + (nc - 1 - j), first + b * nl + (nl - 1 - (j - nc)))

    heads = HG_HEADS
    col = lambda row_fn, group: pl.BlockSpec((tr, HG_DK), lambda b, h, j: (row_fn(b, j), group * heads + h))
    const = lambda a: pl.BlockSpec(a.shape, lambda b, h, j: (0, 0))
    out = jax.ShapeDtypeStruct((hg.shape[0], HG_W), F32)
    return pl.pallas_call(
        _hgrn_kernel,
        grid=(n_batch, heads, nc + nl),
        in_specs=[col(fwd, 0), col(fwd, 1), col(fwd, 3), col(bwd, 0), col(bwd, 2), col(bwd, 3),
                  pl.BlockSpec((1, HG_DK), lambda b, h, j: (0, h)), const(tri_f), const(tri_b), const(ones)],
        out_specs=[pl.BlockSpec((tr, HG_DK), lambda b, h, j: (fwd(b, j), h)),
                   pl.BlockSpec((tr, HG_DK), lambda b, h, j: (bwd(b, j), h))],
        out_shape=[out, out],
        scratch_shapes=[pltpu.VMEM((HG_DK, HG_DK), F32), pltpu.VMEM((HG_DK, HG_DK), F32)],
        compiler_params=_cparams("parallel", "parallel", "arbitrary"),
        name="hgrn",
    )(hg, hg, hg, hg, hg, hg, lb, tri_f, tri_b, ones)


def _mixout_kernel(x_ref, yna_ref, of_ref, ob_ref, zg_ref, ysw_ref, hgn_ref, w_ref, mod_ref, g2_ref,
                   rw_ref, rb_ref, xo_ref, h2_ref, te_ref, tg_ref):
    o = of_ref[...] + ob_ref[...]
    zg = zg_ref[...]
    parts = [yna_ref[...]]
    for h in range(HG_HEADS):
        sl = slice(h * HG_DK, (h + 1) * HG_DK)
        oh = o[:, sl]
        yh = oh * lax.rsqrt(jnp.mean(oh * oh, axis=-1, keepdims=True) + NORM_EPS) * hgn_ref[...]
        parts.append((yh * _silu(zg[:, sl])).astype(BF16))
    parts.append(ysw_ref[...])
    y = _dot(jnp.concatenate(parts, axis=1), w_ref[...])
    mod = mod_ref[...]
    x_new = x_ref[...] + mod[2:3] * y
    xo_ref[...] = x_new
    h2 = _mod_rmsnorm(x_new, g2_ref[...], mod, 3, 4)
    h2_ref[...] = h2

    logits = _dot(h2, rw_ref[...], HIGHEST) + rb_ref[...]
    lane = lax.broadcasted_iota(jnp.int32, logits.shape, 1).astype(F32)
    top_e = jnp.zeros(logits.shape, F32)
    top_v = jnp.full(logits.shape, NEG_INF, F32)
    for j in range(TOP_K):
        best = jnp.max(logits, axis=-1, keepdims=True)
        arg = jnp.min(jnp.where(logits == best, lane, float(LANES)), axis=-1, keepdims=True)
        top_e = jnp.where(lane == j, arg, top_e)
        top_v = jnp.where(lane == j, best, top_v)
        logits = jnp.where(lane == arg, -jnp.inf, logits)
    ex = jnp.exp(top_v - jnp.max(top_v, axis=-1, keepdims=True))
    te_ref[...] = top_e.astype(jnp.int32)
    tg_ref[...] = ex / jnp.sum(ex, axis=-1, keepdims=True)


def _mixout(x, yna, o_f, o_b, hg, ysw, hgn, w_out, mods, norm2_g, rw, rb, geom, stream_off):
    n_batch, n_ctx_blocks, blocks_per_batch = geom
    tm = ROW_BLOCK
    t = x.shape[0] - stream_off * tm
    row = lambda w: pl.BlockSpec((tm, w), lambda i: (i, 0))
    full = lambda w, cb: pl.BlockSpec((tm, w), lambda i: (i + stream_off, cb))
    const = lambda a: pl.BlockSpec(a.shape, lambda i: (0,) * a.ndim)
    return pl.pallas_call(
        _mixout_kernel,
        grid=(t // tm,),
        in_specs=[full(D_MODEL, 0), row(NA_W), full(HG_W, 0), full(HG_W, 0), full(HG_W, 4), row(SWA_W),
                  const(hgn), const(w_out),
                  pl.BlockSpec((None, 6, D_MODEL), lambda i: (
                      _mod_row(i + stream_off, n_ctx_blocks, blocks_per_batch, n_batch), 0, 0)),
                  const(norm2_g), const(rw), const(rb)],
        out_specs=[row(D_MODEL), row(D_MODEL), row(LANES), row(LANES)],
        out_shape=[jax.ShapeDtypeStruct((t, D_MODEL), F32), jax.ShapeDtypeStruct((t, D_MODEL), F32),
                   jax.ShapeDtypeStruct((t, LANES), jnp.int32), jax.ShapeDtypeStruct((t, LANES), F32)],
        compiler_params=_cparams("parallel"),
        name="mixout",
    )(x, yna, o_f, o_b, hg, ysw, hgn, w_out, mods, norm2_g, rw, rb)


def _gather_copy(src_hbm, idx_ref, buf, sem, slot, n_rows):
    def start():
        for r in range(n_rows):
            pltpu.make_async_copy(src_hbm.at[pl.ds(idx_ref[0, 0, r], 1), :],
                                  buf.at[slot, pl.ds(r, 1), :], sem.at[slot]).start()

    def wait():
        pltpu.make_async_copy(src_hbm.at[pl.ds(0, n_rows), :], buf.at[slot], sem.at[slot]).wait()

    return start, wait


def _moe_kernel(be_ref, nu_ref, idx_ref, h_hbm, wgu_ref, bgu_ref, wd_ref, bd_ref, y_ref,
                xbuf, wgu_bf, wd_bf, sem):
    i = pl.program_id(0)
    n_used = nu_ref[0]
    tm = y_ref.shape[0]
    blk = i - 1

    @pl.when(i < n_used)
    def _():
        _gather_copy(h_hbm, idx_ref, xbuf, sem, i % 2, tm)[0]()

    live = (blk >= 0) & (blk < n_used)
    changed = (blk == 0) | (be_ref[jnp.maximum(blk, 0)] != be_ref[jnp.maximum(blk - 1, 0)])

    @pl.when(live & changed)
    def _():
        wgu_bf[...] = wgu_ref[...].astype(BF16)
        wd_bf[...] = wd_ref[...].astype(BF16)

    @pl.when(live)
    def _():
        slot = blk % 2
        _gather_copy(h_hbm, idx_ref, xbuf, sem, slot, tm)[1]()
        gu = _dot(xbuf[slot].astype(BF16), wgu_bf[...]) + bgu_ref[...]
        glu = jnp.minimum(gu[:, :D_FF], SWIGLU_LIMIT)
        lin = jnp.clip(gu[:, D_FF:], -SWIGLU_LIMIT, SWIGLU_LIMIT)
        act = glu * _sigmoid(SWIGLU_ALPHA * glu) * (lin + 1.0)
        y_ref[...] = _dot(act.astype(BF16), wd_bf[...]) + bd_ref[...]

    @pl.when(blk >= n_used)
    def _():
        y_ref[...] = jnp.zeros_like(y_ref)


def _moe_ffn(h, slot_tok, block_e, n_used, w_gu, b_gu, w_down, b_down):
    d = h.shape[1]
    tm = MOE_ROWS
    nb = slot_tok.shape[0] // tm
    n_e, _, f2 = w_gu.shape
    prev = lambda i: jnp.maximum(i - 1, 0)
    grid_spec = pltpu.PrefetchScalarGridSpec(
        num_scalar_prefetch=2,
        grid=(nb + 1,),
        in_specs=[pl.BlockSpec((1, 1, tm), lambda i, be, nu: (jnp.minimum(i, nb - 1), 0, 0),
                               memory_space=pltpu.SMEM),
                  pl.BlockSpec(memory_space=pl.ANY),
                  pl.BlockSpec((None, d, f2), lambda i, be, nu: (be[prev(i)], 0, 0)),
                  pl.BlockSpec((None, 1, f2), lambda i, be, nu: (be[prev(i)], 0, 0)),
                  pl.BlockSpec((None, f2 // 2, d), lambda i, be, nu: (be[prev(i)], 0, 0)),
                  pl.BlockSpec((None, 1, d), lambda i, be, nu: (be[prev(i)], 0, 0))],
        out_specs=pl.BlockSpec((tm, d), lambda i, be, nu: (prev(i), 0)),
        scratch_shapes=[pltpu.VMEM((2, tm, d), F32), pltpu.VMEM((d, f2), BF16), pltpu.VMEM((f2 // 2, d), BF16),
                        pltpu.SemaphoreType.DMA((2,))],
    )
    return pl.pallas_call(
        _moe_kernel,
        grid_spec=grid_spec,
        out_shape=jax.ShapeDtypeStruct((nb * tm, d), F32),
        compiler_params=_cparams("arbitrary"),
        name="moe_ffn",
    )(block_e, n_used, slot_tok.reshape(nb, 1, tm), h, w_gu, b_gu.reshape(n_e, 1, f2), w_down,
      b_down.reshape(n_e, 1, d))


def _combine_kernel(idx_ref, x_ref, y_hbm, tg_ref, mod_ref, o_ref, ybuf, sem):
    i = pl.program_id(0)
    tm = o_ref.shape[0]
    n_rows = TOP_K * tm

    @pl.when(i < pl.num_programs(0) - 1)
    def _():
        _gather_copy(y_hbm, idx_ref, ybuf, sem, i % 2, n_rows)[0]()

    @pl.when(i > 0)
    def _():
        slot = (i - 1) % 2
        _gather_copy(y_hbm, idx_ref, ybuf, sem, slot, n_rows)[1]()
        gate = tg_ref[...]
        y = [ybuf[slot, j * tm:(j + 1) * tm, :] * gate[:, j:j + 1] for j in range(TOP_K)]
        o_ref[...] = x_ref[...] + mod_ref[...][5:6] * ((y[0] + y[1]) + (y[2] + y[3]))


def _combine(x, y_slots, dest, gates, mods, geom, stream_off):
    n_batch, n_ctx_blocks, blocks_per_batch = geom
    t = x.shape[0]
    tm = ROW_BLOCK
    nb = t // tm
    idx = dest.reshape(nb, tm, TOP_K).transpose(0, 2, 1).reshape(nb, 1, TOP_K * tm)
    prev = lambda i: jnp.maximum(i - 1, 0)
    row = lambda w: pl.BlockSpec((tm, w), lambda i: (prev(i), 0))
    return pl.pallas_call(
        _combine_kernel,
        grid=(nb + 1,),
        in_specs=[pl.BlockSpec((1, 1, TOP_K * tm), lambda i: (jnp.minimum(i, nb - 1), 0, 0),
                               memory_space=pltpu.SMEM),
                  row(D_MODEL), pl.BlockSpec(memory_space=pl.ANY), row(LANES),
                  pl.BlockSpec((None, 6, D_MODEL), lambda i: (
                      _mod_row(prev(i) + stream_off, n_ctx_blocks, blocks_per_batch, n_batch), 0, 0))],
        out_specs=row(D_MODEL),
        out_shape=jax.ShapeDtypeStruct((t, D_MODEL), F32),
        scratch_shapes=[pltpu.VMEM((2, TOP_K * tm, D_MODEL), F32), pltpu.SemaphoreType.DMA((2,))],
        compiler_params=_cparams("arbitrary"),
        name="moe_combine",
    )(idx, x, y_slots, gates, mods)


def _route(top_e, n_experts, tm):
    t, k = top_e.shape
    a = t * k
    flat_e = top_e.reshape(-1)
    onehot = (flat_e[:, None] == jnp.arange(n_experts, dtype=jnp.int32)[None, :]).astype(jnp.int32)
    counts = onehot.sum(axis=0)
    rank = (jnp.cumsum(onehot, axis=0) * onehot).sum(axis=1) - 1
    padded = (counts + tm - 1) // tm * tm
    pend = jnp.cumsum(padded)
    dest = ((onehot * (pend - padded)[None, :]).sum(axis=1) + rank).astype(jnp.int32)
    n_blocks = (a + n_experts * (tm - 1) + tm - 1) // tm
    slot_tok = jnp.zeros((n_blocks * tm,), jnp.int32).at[dest].set(jnp.arange(a, dtype=jnp.int32) // k)
    block_start = jnp.arange(n_blocks, dtype=jnp.int32) * tm
    block_e = jnp.minimum((block_start[:, None] >= pend[None, :]).sum(axis=1), n_experts - 1).astype(jnp.int32)
    n_used = (pend[-1] // tm).astype(jnp.int32).reshape(1)
    return slot_tok, dest.reshape(t, k), block_e, n_used


def _na_row_pattern(rb, n_rows):
    wr = min(NA_WIN_H, n_rows)
    n_rb = n_rows // NA_QROWS
    q_row = rb * NA_QROWS + np.arange(NA_QROWS)[:, None]
    k_row = np.clip(rb - 1, 0, n_rb - NA_KROWS // NA_QROWS) * NA_QROWS + np.arange(NA_KROWS)[None, :]
    row_start = np.clip(q_row - wr // 2, 0, n_rows - wr)
    ok = (k_row >= row_start) & (k_row < row_start + wr)
    return np.where(ok, k_row - q_row + (NA_WIN_H - 1), -1)


def _na_bias_table(rpb, seq):
    n_rows = seq // GRID_W
    n_rb = n_rows // NA_QROWS
    patterns = [_na_row_pattern(rb, n_rows) for rb in range(n_rb)]
    assert all((p == patterns[1]).all() for p in patterns[1:-1])
    q_col = np.arange(GRID_W)[:, None]
    k_col = np.arange(GRID_W)[None, :]
    col_start = np.clip(q_col - NA_WIN_W // 2, 0, GRID_W - NA_WIN_W)
    col_ok = (k_col >= col_start) & (k_col < col_start + NA_WIN_W)
    dc = np.clip(k_col - q_col + (NA_WIN_W - 1), 0, 2 * NA_WIN_W - 2)
    onehot = ((dc[None] == np.arange(2 * NA_WIN_W - 1)[:, None, None]) & col_ok[None]).astype(np.float32)
    by_col = jnp.einsum('hab,bqk->haqk', rpb.astype(F32), jnp.asarray(onehot), precision=HIGHEST)
    by_col = by_col + jnp.asarray(np.where(col_ok, 0.0, NEG_INF).astype(np.float32))
    masked = jnp.full((rpb.shape[0], GRID_W, GRID_W), NEG_INF, F32)
    variants = []
    for pattern in (patterns[0], patterns[1], patterns[-1]):
        rows = [jnp.concatenate([by_col[:, a] if a >= 0 else masked for a in pattern[qr]], axis=2)
                for qr in range(NA_QROWS)]
        variants.append(jnp.concatenate(rows, axis=1))
    return jnp.stack(variants, axis=0)


def _rope_tables(seq, ctx_len):
    quarter = HEAD_DIM // 4
    lane = np.arange(SWA_W)
    inv = ROPE_BASE ** (-(lane % quarter).astype(np.float64) / quarter)
    t = np.arange(seq)
    pos = np.where((lane % HEAD_DIM < HEAD_DIM // 2)[None, :], (t // GRID_W)[:, None], (t % GRID_W)[:, None])
    ang = jnp.asarray(pos, F32) * jnp.asarray(inv, F32)[None, :]
    sign = np.where(lane % (2 * quarter) < quarter, -1.0, 1.0).astype(np.float32)
    cos = jnp.concatenate([jnp.ones((ctx_len, SWA_W), F32), jnp.cos(ang)], axis=0)
    sin = jnp.concatenate([jnp.zeros((ctx_len, SWA_W), F32), jnp.sin(ang) * sign[None, :]], axis=0)
    return cos, sin


def kernel(x, c, ctx, c_ctx, hg_lower_bounds, ada_w, ada_b, norm1_g, norm2_g, w_in, na_q_norm, na_k_norm, na_rpb,
           hg_norm_g, swa_q_norm, swa_k_norm, swa_sink, w_out, router_w, router_b, w_gu, b_gu, w_down, b_down):
    n_batch, seq, d = x.shape
    ctx_len = ctx.shape[1]
    depth = ada_w.shape[0]
    assert d == D_MODEL and seq % ROW_BLOCK == 0 and ctx_len == ROW_BLOCK
    n_ctx_rows = n_batch * ctx_len
    geom = (n_batch, n_ctx_rows // ROW_BLOCK, seq // ROW_BLOCK)

    p_lb = jax.nn.softmax(hg_lower_bounds.astype(F32), axis=0)
    lbs = jnp.cumsum(p_lb, axis=0) - p_lb[0]

    cos_t, sin_t = _rope_tables(seq, ctx_len)
    lane = np.arange(NA_W)
    group_ones = jnp.asarray((lane[:, None] // HEAD_DIM == lane[None, :] // HEAD_DIM).astype(np.float32))
    tri = np.tril(np.ones((HG_STEP_ROWS, HG_STEP_ROWS), np.float32))
    tri_f, tri_b = jnp.asarray(tri), jnp.asarray(tri.T)
    ones_bf = jnp.ones((LANES, LANES), BF16)
    n_mod_rows = -(-(n_batch + 1) // 8) * 8
    cc = jnp.zeros((n_mod_rows, d), F32).at[:n_batch].set(c).at[n_batch].set(c_ctx)
    dup = lambda w: jnp.concatenate([w[:, :HEAD_DIM], w[:, :HEAD_DIM], w[:, HEAD_DIM:], w[:, HEAD_DIM:]], axis=1)
    tile4 = lambda g: jnp.tile(g.astype(F32), 4).reshape(1, 4 * HEAD_DIM)
    pad_e = LANES - N_EXPERTS

    x_all = jnp.concatenate([ctx.reshape(n_ctx_rows, d), x.reshape(n_batch * seq, d)], axis=0)
    for l in range(depth):
        last = l == depth - 1
        mods = _ada_mod(cc, ada_w[l], ada_b[l]).reshape(n_mod_rows, 6, d)
        w = w_in[l]
        kv0 = 3 * NA_W + HG_COLS + SWA_W
        w_ext = jnp.concatenate([w[:, :kv0], dup(w[:, kv0:kv0 + SWA_KV_W]), dup(w[:, kv0 + SWA_KV_W:])],
                                axis=1).astype(BF16)
        hg, naq, nak, nav, swq, swk, swv = _inproj(
            x_all, mods, norm1_g[l].reshape(1, d), w_ext, tile4(na_q_norm[l]), tile4(na_k_norm[l]),
            tile4(swa_q_norm[l]), tile4(swa_k_norm[l]), cos_t, sin_t, group_ones, geom)

        y_na = _na_attention(naq, nak, nav, _na_bias_table(na_rpb[l], seq), geom)
        y_sw = _swa_attention(swq, swk, swv, swa_sink[l].astype(F32), geom, seq, ctx_len)
        o_f, o_b = _hgrn(hg, lbs[l].reshape(1, HG_W), tri_f, tri_b, ones_bf, n_batch, seq, ctx_len)

        if last:
            stream_off = geom[1]
        else:
            stream_off = 0
            yc_na, yc_sw = _ctx_attention(naq, nak, nav, swq, swk, swv, swa_sink[l].astype(F32), n_batch, ctx_len)
            y_na = jnp.concatenate([yc_na, y_na], axis=0)
            y_sw = jnp.concatenate([yc_sw, y_sw], axis=0)

        rw = jnp.pad(router_w[l].astype(F32), ((0, 0), (0, pad_e)))
        rb = jnp.pad(router_b[l].astype(F32), (0, pad_e), constant_values=NEG_INF).reshape(1, LANES)
        x_new, h2, top_e, top_g = _mixout(
            x_all, y_na, o_f, o_b, hg, y_sw, hg_norm_g[l].reshape(1, HG_DK).astype(F32), w_out[l].astype(BF16),
            mods, norm2_g[l].reshape(1, d), rw, rb, geom, stream_off)

        slot_tok, dest, block_e, n_used = _route(top_e[:, :TOP_K], N_EXPERTS, MOE_ROWS)
        y_slots = _moe_ffn(h2, slot_tok, block_e, n_used, w_gu[l], b_gu[l], w_down[l], b_down[l])
        x_all = _combine(x_new, y_slots, dest, top_g, mods, geom, stream_off)
    return x_all.reshape(n_batch, seq, d)
```

```python
import functools

import numpy as np
import jax
import jax.numpy as jnp
from jax import lax
from jax.experimental import pallas as pl
from jax.experimental.pallas import tpu as pltpu

D_MODEL = 1024
GRID_W = 64
HEAD_DIM = 64
ATTN_SCALE = HEAD_DIM ** -0.5
NA_HEADS = 4
NA_WIN_H = 8
NA_WIN_W = 16
HG_HEADS = 4
HG_DK = 128
SWA_Q_HEADS = 4
SWA_KV_HEADS = 2
SWA_WINDOW = 128
SWA_BLOCK = 128
ROPE_BASE = 10000.0
N_EXPERTS = 32
TOP_K = 4
D_FF = 1024
SWIGLU_LIMIT = 7.0
SWIGLU_ALPHA = 1.702
NORM_EPS = 1e-6
NEG_INF = -1e30

NA_W = NA_HEADS * HEAD_DIM
HG_W = HG_HEADS * HG_DK
SWA_W = SWA_Q_HEADS * HEAD_DIM
SWA_KV_W = SWA_KV_HEADS * HEAD_DIM
HG_COLS = 5 * HG_W
IN_COLS_EXT = 3 * NA_W + HG_COLS + 3 * SWA_W

LANES = 128
SUBLANES = 8
ROW_BLOCK = 256
NA_QROWS = 4
NA_KROWS = 12
HG_SUB = 8
HG_STEP_ROWS = 128
HG_HEADS_PER_STEP = 2
MOE_ROWS = 256
VMEM_LIMIT = 56 * 1024 * 1024

F32 = jnp.float32
BF16 = jnp.bfloat16
HIGHEST = lax.Precision.HIGHEST


def _cparams(*sem):
    return pltpu.CompilerParams(dimension_semantics=sem, vmem_limit_bytes=VMEM_LIMIT)


def _dot(a, b, precision=None):
    return jnp.dot(a, b, preferred_element_type=F32, precision=precision)


def _dot_nt(a, b):
    return lax.dot_general(a, b, (((1,), (1,)), ((), ())), preferred_element_type=F32)


def _dot_tn(a, b):
    return lax.dot_general(a, b, (((0,), (0,)), ((), ())), preferred_element_type=F32)


def _store_token_tiles(ref, x):
    n = x.shape[0]
    for c in range(SUBLANES):
        ref[pl.ds(c, n, stride=SUBLANES), :] = x[:, c * LANES:(c + 1) * LANES]


def _load_token_tiles(ref, first_token, n):
    return jnp.concatenate([ref[pl.ds(first_token * SUBLANES + c, n, stride=SUBLANES), :] for c in range(SUBLANES)],
                           axis=1)


def _sigmoid(x):
    return 1.0 / (1.0 + jnp.exp(-x))


def _silu(x):
    return x * _sigmoid(x)


def _ada_kernel(c_ref, w_ref, b_ref, o_ref):
    o_ref[...] = _dot(_silu(c_ref[...]), w_ref[...], HIGHEST) + b_ref[...]


def _ada_mod(cc, w, b):
    m, d = cc.shape
    n = w.shape[1]
    tn = 1024
    return pl.pallas_call(
        _ada_kernel,
        grid=(n // tn,),
        in_specs=[pl.BlockSpec((m, d), lambda j: (0, 0)),
                  pl.BlockSpec((d, tn), lambda j: (0, j)),
                  pl.BlockSpec((1, tn), lambda j: (0, j))],
        out_specs=pl.BlockSpec((m, tn), lambda j: (0, j)),
        out_shape=jax.ShapeDtypeStruct((m, n), F32),
        compiler_params=_cparams("parallel"),
        name="ada_mod",
    )(cc, w, b.reshape(1, n))


def _mod_rmsnorm(x, g, mod, shift_row, scale_row):
    y = x * lax.rsqrt(jnp.mean(x * x, axis=-1, keepdims=True) + NORM_EPS) * g
    return y * (1.0 + mod[scale_row:scale_row + 1]) + mod[shift_row:shift_row + 1]


def _head_rmsnorm(x, w, group_ones):
    ss = _dot(x * x, group_ones, HIGHEST)
    return x * lax.rsqrt(ss * (1.0 / HEAD_DIM) + NORM_EPS) * w


def _rope(x, cos, sin_signed):
    n = x.shape[-1]
    lane = lax.broadcasted_iota(jnp.int32, x.shape, 1)
    quarter = HEAD_DIM // 4
    partner = jnp.where(lane % (2 * quarter) < quarter,
                        pltpu.roll(x, n - quarter, 1), pltpu.roll(x, quarter, 1))
    return x * cos + partner * sin_signed


def _inproj_kernel(x_ref, mod_ref, g_ref, w_ref, naq_w, nak_w, swq_w, swk_w, cos_ref, sin_ref, ones_ref,
                   hg_ref, naq_ref, nak_ref, nav_ref, swq_ref, swk_ref, swv_ref):
    h = _mod_rmsnorm(x_ref[...], g_ref[...], mod_ref[...], 0, 1).astype(BF16)
    ones = ones_ref[...]

    def proj(lo, width):
        return _dot(h, w_ref[:, lo:lo + width])

    naq_ref[...] = (_head_rmsnorm(proj(0, NA_W), naq_w[...], ones) * ATTN_SCALE).astype(BF16)
    nak_ref[...] = _head_rmsnorm(proj(NA_W, NA_W), nak_w[...], ones).astype(BF16)
    nav_ref[...] = proj(2 * NA_W, NA_W).astype(BF16)
    base = 3 * NA_W
    for j in range(HG_COLS // HG_W):
        hg_ref[:, j * HG_W:(j + 1) * HG_W] = proj(base + j * HG_W, HG_W)
    base += HG_COLS
    cos, sin = cos_ref[...], sin_ref[...]
    swq = _rope(_head_rmsnorm(proj(base, SWA_W), swq_w[...], ones), cos, sin)
    swq_ref[...] = (swq * ATTN_SCALE).astype(BF16)
    swk_ref[...] = _rope(_head_rmsnorm(proj(base + SWA_W, SWA_W), swk_w[...], ones), cos, sin).astype(BF16)
    swv_ref[...] = proj(base + 2 * SWA_W, SWA_W).astype(BF16)


def _mod_row(i, n_ctx_blocks, blocks_per_batch, n_batch):
    return jnp.where(i < n_ctx_blocks, n_batch, (i - n_ctx_blocks) // blocks_per_batch)


def _inproj(x_all, mods, norm_g, w_ext, naq_w, nak_w, swq_w, swk_w, cos_t, sin_t, ones, geom):
    n_batch, n_ctx_blocks, blocks_per_batch = geom
    t = x_all.shape[0]
    tm = ROW_BLOCK
    row = lambda w: pl.BlockSpec((tm, w), lambda i: (i, 0))
    const = lambda a: pl.BlockSpec(a.shape, lambda i: (0,) * a.ndim)
    rope_blk = lambda i: (jnp.where(i < n_ctx_blocks, 0, 1 + (i - n_ctx_blocks) % blocks_per_batch), 0)
    bf = lambda w: jax.ShapeDtypeStruct((t, w), BF16)
    return pl.pallas_call(
        _inproj_kernel,
        grid=(t // tm,),
        in_specs=[row(D_MODEL),
                  pl.BlockSpec((None, 6, D_MODEL),
                               lambda i: (_mod_row(i, n_ctx_blocks, blocks_per_batch, n_batch), 0, 0)),
                  const(norm_g), const(w_ext), const(naq_w), const(nak_w), const(swq_w), const(swk_w),
                  pl.BlockSpec((tm, SWA_W), rope_blk), pl.BlockSpec((tm, SWA_W), rope_blk), const(ones)],
        out_specs=[row(HG_COLS), row(NA_W), row(NA_W), row(NA_W), row(SWA_W), row(SWA_W), row(SWA_W)],
        out_shape=[jax.ShapeDtypeStruct((t, HG_COLS), F32), bf(NA_W), bf(NA_W), bf(NA_W),
                   bf(SWA_W), bf(SWA_W), bf(SWA_W)],
        compiler_params=_cparams("parallel"),
        name="inproj",
    )(x_all, mods, norm_g, w_ext, naq_w, nak_w, swq_w, swk_w, cos_t, sin_t, ones)


def _attend(q, parts, sink_vals):
    m_rows = q.shape[0]
    lane = lax.broadcasted_iota(jnp.int32, (m_rows, q.shape[1]), 1)
    out = jnp.zeros((m_rows, q.shape[1]), F32)
    for h in range(q.shape[1] // HEAD_DIM):
        in_head = (lane // HEAD_DIM) == h
        qh = jnp.where(in_head, q, jnp.zeros_like(q))
        scores = []
        for k, _, add in parts:
            s = _dot_nt(qh, k)
            if add is not None:
                s = s + (add[h] if add.ndim == 3 else add)
            scores.append(s)
        mx = functools.reduce(jnp.maximum, [jnp.max(s, axis=-1, keepdims=True) for s in scores])
        if sink_vals is not None:
            mx = jnp.maximum(mx, sink_vals[h])
        ps = [jnp.exp(s - mx) for s in scores]
        den = functools.reduce(jnp.add, [jnp.sum(p, axis=-1, keepdims=True) for p in ps])
        if sink_vals is not None:
            den = den + jnp.exp(sink_vals[h] - mx)
        acc = functools.reduce(jnp.add, [_dot(p.astype(BF16), v) for p, (_, v, _) in zip(ps, parts)])
        out = jnp.where(in_head, acc / den, out)
    return out


def _na_kernel(q_ref, k0, k1, k2, kc, v0, v1, v2, vc, bias_ref, o_ref):
    k_lat = jnp.concatenate([k0[...], k1[...], k2[...]], axis=0)
    v_lat = jnp.concatenate([v0[...], v1[...], v2[...]], axis=0)
    o = _attend(q_ref[...], [(k_lat, v_lat, bias_ref[...]), (kc[...], vc[...], None)], None)
    o_ref[...] = o.astype(o_ref.dtype)


def _na_attention(naq, nak, nav, bias, geom):
    n_batch, n_ctx_blocks, blocks_per_batch = geom
    tm = ROW_BLOCK
    n_rb = blocks_per_batch
    kb_max = n_rb - NA_KROWS // NA_QROWS
    lat = lambda rb, b: n_ctx_blocks + b * blocks_per_batch

    def band(j):
        return pl.BlockSpec((tm, NA_W), lambda rb, b: (lat(rb, b) + jnp.clip(rb - 1, 0, kb_max) + j, 0))

    ctx = pl.BlockSpec((tm, NA_W), lambda rb, b: (b, 0))
    return pl.pallas_call(
        _na_kernel,
        grid=(n_rb, n_batch),
        in_specs=[pl.BlockSpec((tm, NA_W), lambda rb, b: (lat(rb, b) + rb, 0)),
                  band(0), band(1), band(2), ctx, band(0), band(1), band(2), ctx,
                  pl.BlockSpec((None, NA_HEADS, tm, NA_KROWS * GRID_W), lambda rb, b: (
                      jnp.where(rb == 0, 0, jnp.where(rb == n_rb - 1, 2, 1)), 0, 0, 0))],
        out_specs=pl.BlockSpec((tm, NA_W), lambda rb, b: (b * blocks_per_batch + rb, 0)),
        out_shape=jax.ShapeDtypeStruct((n_batch * blocks_per_batch * tm, NA_W), BF16),
        compiler_params=_cparams("parallel", "parallel"),
        name="na_attn",
    )(naq, nak, nak, nak, nak, nav, nav, nav, nav, bias)


def _swa_kernel(sink_ref, q_ref, kp, kc_, kn, kx, vp, vc_, vn, vx, o_ref, *, n_blocks):
    n = pl.program_id(1)
    tq = q_ref.shape[0]
    k_lat = jnp.concatenate([kp[...], kc_[...], kn[...]], axis=0)
    v_lat = jnp.concatenate([vp[...], vc_[...], vn[...]], axis=0)
    qi = lax.broadcasted_iota(jnp.int32, (tq, 3 * tq), 0)
    mi = lax.broadcasted_iota(jnp.int32, (tq, 3 * tq), 1)
    kpos = (n - 1) * tq + mi
    ok = (jnp.abs(qi + tq - mi) <= SWA_WINDOW) & (kpos >= 0) & (kpos < n_blocks * tq)
    add = jnp.where(ok, 0.0, NEG_INF).astype(F32)
    sinks = [sink_ref[h] for h in range(SWA_Q_HEADS)]
    o = _attend(q_ref[...], [(k_lat, v_lat, add), (kx[...], vx[...], None)], sinks)
    o_ref[...] = o.astype(o_ref.dtype)


def _swa_attention(swq, swk, swv, sink, geom, seq, ctx_len):
    n_batch, _, _ = geom
    tq = SWA_BLOCK
    nb = seq // tq
    first = n_batch * ctx_len // tq
    blk = lambda f: pl.BlockSpec((tq, SWA_W), lambda b, n: (first + b * nb + f(n), 0))
    prev, cur, nxt = blk(lambda n: jnp.maximum(n - 1, 0)), blk(lambda n: n), blk(lambda n: jnp.minimum(n + 1, nb - 1))
    ctx = pl.BlockSpec((ctx_len, SWA_W), lambda b, n: (b, 0))
    return pl.pallas_call(
        functools.partial(_swa_kernel, n_blocks=nb),
        grid=(n_batch, nb),
        in_specs=[pl.BlockSpec(memory_space=pltpu.SMEM), cur, prev, cur, nxt, ctx, prev, cur, nxt, ctx],
        out_specs=pl.BlockSpec((tq, SWA_W), lambda b, n: (b * nb + n, 0)),
        out_shape=jax.ShapeDtypeStruct((n_batch * seq, SWA_W), BF16),
        compiler_params=_cparams("parallel", "parallel"),
        name="swa_attn",
    )(sink, swq, swk, swk, swk, swk, swv, swv, swv, swv)


def _ctx_attn_kernel(sink_ref, naq, nak, nav, swq, swk, swv, ona_ref, osw_ref):
    ona_ref[...] = _attend(naq[...], [(nak[...], nav[...], None)], None).astype(ona_ref.dtype)
    sinks = [sink_ref[h] for h in range(SWA_Q_HEADS)]
    osw_ref[...] = _attend(swq[...], [(swk[...], swv[...], None)], sinks).astype(osw_ref.dtype)


def _ctx_attention(naq, nak, nav, swq, swk, swv, sink, n_batch, ctx_len):
    blk = pl.BlockSpec((ctx_len, NA_W), lambda b: (b, 0))
    out = jax.ShapeDtypeStruct((n_batch * ctx_len, NA_W), BF16)
    return pl.pallas_call(
        _ctx_attn_kernel,
        grid=(n_batch,),
        in_specs=[pl.BlockSpec(memory_space=pltpu.SMEM)] + [blk] * 6,
        out_specs=[blk, blk],
        out_shape=[out, out],
        compiler_params=_cparams("parallel"),
        name="ctx_attn",
    )(sink, naq, nak, nav, swq, swk, swv)


def _hgrn_block(q, k, g, v, st, tri, ones, reverse):
    n_rows = q.shape[0]
    c = HG_SUB
    cum = _dot(tri, g, HIGHEST)
    tot = cum[0:1] if reverse else cum[n_rows - 1:n_rows]
    o_inter = _dot_nt((q * jnp.exp(cum)).astype(BF16), st.astype(BF16))
    k_end = (k * jnp.exp(tot - cum)).astype(BF16)
    st_new = st * jnp.exp(tot) + _dot_tn(v.astype(BF16), k_end)

    row = lax.broadcasted_iota(jnp.int32, (n_rows, LANES), 0)
    row_a = lax.broadcasted_iota(jnp.int32, (n_rows, n_rows), 0)
    col_a = lax.broadcasted_iota(jnp.int32, (n_rows, n_rows), 1)
    a = jnp.zeros((n_rows, n_rows), F32)
    size = 2 * c
    while size <= n_rows:
        half = size // 2
        ref_row = half if reverse else half - 1
        ref = jnp.concatenate([jnp.broadcast_to(cum[b * size + ref_row:b * size + ref_row + 1], (size, LANES))
                               for b in range(n_rows // size)], axis=0)
        later = (row % size < half) if reverse else (row % size >= half)
        q_t = q * jnp.exp(jnp.where(later, cum - ref, NEG_INF))
        k_t = k * jnp.exp(jnp.where(later, NEG_INF, ref - cum))
        a_l = _dot_nt(q_t.astype(BF16), k_t.astype(BF16))
        a = a + jnp.where(row_a // size == col_a // size, a_l, 0.0)
        size *= 2

    rows = lax.broadcasted_iota(jnp.int32, (c, LANES), 0)
    lanes = lax.broadcasted_iota(jnp.int32, (c, LANES), 1)
    direct = []
    for i in range(n_rows // c):
        sl = slice(c * i, c * i + c)
        qi, ki, ci = q[sl], k[sl], cum[sl]
        for s in range(c):
            seen = (rows <= s) if reverse else (rows >= s)
            direct.append(qi * ki[s:s + 1] * jnp.exp(jnp.where(seen, ci - ci[s:s + 1], NEG_INF)))
    a_rep = _dot(jnp.concatenate(direct, axis=0).astype(BF16), ones)
    tiles = []
    for i in range(n_rows // c):
        tile = jnp.zeros((c, LANES), F32)
        for s in range(c):
            r0 = (i * c + s) * c
            tile = jnp.where(lanes == i * c + s, a_rep[r0:r0 + c], tile)
        tiles.append(tile)
    a = a + jnp.concatenate(tiles, axis=0)
    return o_inter + _dot(a.astype(BF16), v.astype(BF16)), st_new


def _hgrn_gates(zq, zf, lb):
    f = lb + (1.0 - lb) * _sigmoid(zf)
    return _silu(zq) * (HG_DK ** -0.5), 1.0 - f, jnp.log(f)


def _hgrn_kernel(zq_f, zf_f, zi_f, zq_b, zf_b, zi_b, lb_ref, trif_ref, trib_ref, ones_ref,
                 of_ref, ob_ref, st_f, st_b):
    @pl.when(pl.program_id(2) == 0)
    def _():
        st_f[...] = jnp.zeros_like(st_f)
        st_b[...] = jnp.zeros_like(st_b)

    ones = ones_ref[...]
    for h in range(HG_HEADS_PER_STEP):
        sl = slice(h * HG_DK, (h + 1) * HG_DK)
        lb = lb_ref[:, sl]
        q, k, g = _hgrn_gates(zq_f[:, sl], zf_f[:, sl], lb)
        of_ref[:, sl], st_f[h] = _hgrn_block(q, k, g, zi_f[:, sl], st_f[h], trif_ref[...], ones, False)
        q, k, g = _hgrn_gates(zq_b[:, sl], zf_b[:, sl], lb)
        ob_ref[:, sl], st_b[h] = _hgrn_block(q, k, g, zi_b[:, sl], st_b[h], trib_ref[...], ones, True)


def _hgrn(hg, lb, tri_f, tri_b, ones, n_batch, seq, ctx_len):
    tr = HG_STEP_ROWS
    nc, nl = ctx_len // tr, seq // tr
    first = n_batch * nc

    def fwd(b, j):
        return jnp.where(j < nc, b * nc + j, first + b * nl + (j - nc))

    def bwd(b, j):
        return jnp.where(j < nc, b * nc + (nc - 1 - j), first + b * nl + (nl - 1 - (j - nc)))

    hps = HG_HEADS_PER_STEP
    groups = HG_HEADS // hps
    wide = hps * HG_DK
    col = lambda row_fn, group: pl.BlockSpec((tr, wide), lambda b, h, j: (row_fn(b, j), group * groups + h))
    const = lambda a: pl.BlockSpec(a.shape, lambda b, h, j: (0, 0))
    out = jax.ShapeDtypeStruct((hg.shape[0], HG_W), F32)
    return pl.pallas_call(
        _hgrn_kernel,
        grid=(n_batch, groups, nc + nl),
        in_specs=[col(fwd, 0), col(fwd, 1), col(fwd, 3), col(bwd, 0), col(bwd, 2), col(bwd, 3),
                  pl.BlockSpec((1, wide), lambda b, h, j: (0, h)), const(tri_f), const(tri_b), const(ones)],
        out_specs=[pl.BlockSpec((tr, wide), lambda b, h, j: (fwd(b, j), h)),
                   pl.BlockSpec((tr, wide), lambda b, h, j: (bwd(b, j), h))],
        out_shape=[out, out],
        scratch_shapes=[pltpu.VMEM((hps, HG_DK, HG_DK), F32), pltpu.VMEM((hps, HG_DK, HG_DK), F32)],
        compiler_params=_cparams("parallel", "parallel", "arbitrary"),
        name="hgrn",
    )(hg, hg, hg, hg, hg, hg, lb, tri_f, tri_b, ones)


def _mixout_kernel(x_ref, yna_ref, of_ref, ob_ref, zg_ref, ysw_ref, hgn_ref, w_ref, mod_ref, g2_ref,
                   rw_ref, rb_ref, xo_ref, h2_ref, te_ref, tg_ref):
    o = of_ref[...] + ob_ref[...]
    zg = zg_ref[...]
    parts = [yna_ref[...]]
    for h in range(HG_HEADS):
        sl = slice(h * HG_DK, (h + 1) * HG_DK)
        oh = o[:, sl]
        yh = oh * lax.rsqrt(jnp.mean(oh * oh, axis=-1, keepdims=True) + NORM_EPS) * hgn_ref[...]
        parts.append((yh * _silu(zg[:, sl])).astype(BF16))
    parts.append(ysw_ref[...])
    y = _dot(jnp.concatenate(parts, axis=1), w_ref[...])
    mod = mod_ref[...]
    x_new = x_ref[...] + mod[2:3] * y
    xo_ref[...] = x_new
    h2 = _mod_rmsnorm(x_new, g2_ref[...], mod, 3, 4)
    _store_token_tiles(h2_ref, h2)

    logits = _dot(h2, rw_ref[...], HIGHEST) + rb_ref[...]
    lane = lax.broadcasted_iota(jnp.int32, logits.shape, 1).astype(F32)
    top_e = jnp.zeros(logits.shape, F32)
    top_v = jnp.full(logits.shape, NEG_INF, F32)
    for j in range(TOP_K):
        best = jnp.max(logits, axis=-1, keepdims=True)
        arg = jnp.min(jnp.where(logits == best, lane, float(LANES)), axis=-1, keepdims=True)
        top_e = jnp.where(lane == j, arg, top_e)
        top_v = jnp.where(lane == j, best, top_v)
        logits = jnp.where(lane == arg, -jnp.inf, logits)
    ex = jnp.exp(top_v - jnp.max(top_v, axis=-1, keepdims=True))
    te_ref[...] = top_e.astype(jnp.int32)
    tg_ref[...] = ex / jnp.sum(ex, axis=-1, keepdims=True)


def _mixout(x, yna, o_f, o_b, hg, ysw, hgn, w_out, mods, norm2_g, rw, rb, geom, stream_off):
    n_batch, n_ctx_blocks, blocks_per_batch = geom
    tm = ROW_BLOCK
    t = x.shape[0] - stream_off * tm
    row = lambda w: pl.BlockSpec((tm, w), lambda i: (i, 0))
    full = lambda w, cb: pl.BlockSpec((tm, w), lambda i: (i + stream_off, cb))
    const = lambda a: pl.BlockSpec(a.shape, lambda i: (0,) * a.ndim)
    return pl.pallas_call(
        _mixout_kernel,
        grid=(t // tm,),
        in_specs=[full(D_MODEL, 0), row(NA_W), full(HG_W, 0), full(HG_W, 0), full(HG_W, 4), row(SWA_W),
                  const(hgn), const(w_out),
                  pl.BlockSpec((None, 6, D_MODEL), lambda i: (
                      _mod_row(i + stream_off, n_ctx_blocks, blocks_per_batch, n_batch), 0, 0)),
                  const(norm2_g), const(rw), const(rb)],
        out_specs=[row(D_MODEL), pl.BlockSpec((tm * SUBLANES, LANES), lambda i: (i, 0)), row(LANES), row(LANES)],
        out_shape=[jax.ShapeDtypeStruct((t, D_MODEL), F32), jax.ShapeDtypeStruct((t * SUBLANES, LANES), F32),
                   jax.ShapeDtypeStruct((t, LANES), jnp.int32), jax.ShapeDtypeStruct((t, LANES), F32)],
        compiler_params=_cparams("parallel"),
        name="mixout",
    )(x, yna, o_f, o_b, hg, ysw, hgn, w_out, mods, norm2_g, rw, rb)


def _gather_copy(src_hbm, idx_ref, buf, sem, slot, n_tokens):
    def start():
        for r in range(n_tokens):
            src_row = pl.multiple_of(idx_ref[0, 0, r], SUBLANES)
            pltpu.make_async_copy(src_hbm.at[pl.ds(src_row, SUBLANES), :],
                                  buf.at[slot, pl.ds(r * SUBLANES, SUBLANES), :], sem.at[slot]).start()

    def wait():
        pltpu.make_async_copy(src_hbm.at[pl.ds(0, n_tokens * SUBLANES), :], buf.at[slot], sem.at[slot]).wait()

    return start, wait


def _moe_kernel(be_ref, nu_ref, idx_ref, h_hbm, wgu_ref, bgu_ref, wd_ref, bd_ref, y_ref,
                xbuf, wgu_bf, wd_bf, sem):
    i = pl.program_id(0)
    n_used = nu_ref[0]
    tm = y_ref.shape[0] // SUBLANES
    blk = i - 1

    @pl.when(i < n_used)
    def _():
        _gather_copy(h_hbm, idx_ref, xbuf, sem, i % 2, tm)[0]()

    live = (blk >= 0) & (blk < n_used)
    changed = (blk == 0) | (be_ref[jnp.maximum(blk, 0)] != be_ref[jnp.maximum(blk - 1, 0)])

    @pl.when(live & changed)
    def _():
        wgu_bf[...] = wgu_ref[...].astype(BF16)
        wd_bf[...] = wd_ref[...].astype(BF16)

    @pl.when(live)
    def _():
        slot = blk % 2
        _gather_copy(h_hbm, idx_ref, xbuf, sem, slot, tm)[1]()
        x = _load_token_tiles(xbuf.at[slot], 0, tm)
        gu = _dot(x.astype(BF16), wgu_bf[...]) + bgu_ref[...]
        glu = jnp.minimum(gu[:, :D_FF], SWIGLU_LIMIT)
        lin = jnp.clip(gu[:, D_FF:], -SWIGLU_LIMIT, SWIGLU_LIMIT)
        act = glu * _sigmoid(SWIGLU_ALPHA * glu) * (lin + 1.0)
        _store_token_tiles(y_ref, _dot(act.astype(BF16), wd_bf[...]) + bd_ref[...])

    @pl.when(blk >= n_used)
    def _():
        y_ref[...] = jnp.zeros_like(y_ref)


def _moe_ffn(h_tiles, slot_row, block_e, n_used, w_gu, b_gu, w_down, b_down, layer):
    tm = MOE_ROWS
    nb = slot_row.shape[0] // tm
    _, n_e, d, f2 = w_gu.shape
    prev = lambda i: jnp.maximum(i - 1, 0)
    weight = lambda r, c: pl.BlockSpec((None, None, r, c), lambda i, be, nu: (layer, be[prev(i)], 0, 0))
    grid_spec = pltpu.PrefetchScalarGridSpec(
        num_scalar_prefetch=2,
        grid=(nb + 1,),
        in_specs=[pl.BlockSpec((1, 1, tm), lambda i, be, nu: (jnp.minimum(i, nb - 1), 0, 0),
                               memory_space=pltpu.SMEM),
                  pl.BlockSpec(memory_space=pl.ANY),
                  weight(d, f2), weight(1, f2), weight(f2 // 2, d), weight(1, d)],
        out_specs=pl.BlockSpec((tm * SUBLANES, LANES), lambda i, be, nu: (prev(i), 0)),
        scratch_shapes=[pltpu.VMEM((2, tm * SUBLANES, LANES), F32), pltpu.VMEM((d, f2), BF16),
                        pltpu.VMEM((f2 // 2, d), BF16), pltpu.SemaphoreType.DMA((2,))],
    )
    n_l = w_gu.shape[0]
    return pl.pallas_call(
        _moe_kernel,
        grid_spec=grid_spec,
        out_shape=jax.ShapeDtypeStruct((nb * tm * SUBLANES, LANES), F32),
        compiler_params=_cparams("arbitrary"),
        name="moe_ffn",
    )(block_e, n_used, slot_row.reshape(nb, 1, tm), h_tiles, w_gu, b_gu.reshape(n_l, n_e, 1, f2), w_down,
      b_down.reshape(n_l, n_e, 1, d))


def _combine_kernel(idx_ref, x_ref, y_hbm, tg_ref, mod_ref, o_ref, ybuf, sem):
    i = pl.program_id(0)
    tm = o_ref.shape[0]
    n_rows = TOP_K * tm

    @pl.when(i < pl.num_programs(0) - 1)
    def _():
        _gather_copy(y_hbm, idx_ref, ybuf, sem, i % 2, n_rows)[0]()

    @pl.when(i > 0)
    def _():
        slot = (i - 1) % 2
        _gather_copy(y_hbm, idx_ref, ybuf, sem, slot, n_rows)[1]()
        gate = tg_ref[...]
        y = [_load_token_tiles(ybuf.at[slot], j * tm, tm) * gate[:, j:j + 1] for j in range(TOP_K)]
        o_ref[...] = x_ref[...] + mod_ref[...][5:6] * ((y[0] + y[1]) + (y[2] + y[3]))


def _combine(x, y_tiles, dest_row, gates, mods, geom, stream_off):
    n_batch, n_ctx_blocks, blocks_per_batch = geom
    t = x.shape[0]
    tm = ROW_BLOCK
    nb = t // tm
    idx = dest_row.reshape(nb, tm, TOP_K).transpose(0, 2, 1).reshape(nb, 1, TOP_K * tm)
    prev = lambda i: jnp.maximum(i - 1, 0)
    row = lambda w: pl.BlockSpec((tm, w), lambda i: (prev(i), 0))
    return pl.pallas_call(
        _combine_kernel,
        grid=(nb + 1,),
        in_specs=[pl.BlockSpec((1, 1, TOP_K * tm), lambda i: (jnp.minimum(i, nb - 1), 0, 0),
                               memory_space=pltpu.SMEM),
                  row(D_MODEL), pl.BlockSpec(memory_space=pl.ANY), row(LANES),
                  pl.BlockSpec((None, 6, D_MODEL), lambda i: (
                      _mod_row(prev(i) + stream_off, n_ctx_blocks, blocks_per_batch, n_batch), 0, 0))],
        out_specs=row(D_MODEL),
        out_shape=jax.ShapeDtypeStruct((t, D_MODEL), F32),
        scratch_shapes=[pltpu.VMEM((2, TOP_K * tm * SUBLANES, LANES), F32), pltpu.SemaphoreType.DMA((2,))],
        compiler_params=_cparams("arbitrary"),
        name="moe_combine",
    )(idx, x, y_tiles, gates, mods)


def _route(top_e, n_experts, tm):
    t, k = top_e.shape
    a = t * k
    flat_e = top_e.reshape(-1)
    onehot = (flat_e[:, None] == jnp.arange(n_experts, dtype=jnp.int32)[None, :]).astype(jnp.int32)
    counts = onehot.sum(axis=0)
    rank = (jnp.cumsum(onehot, axis=0) * onehot).sum(axis=1) - 1
    padded = (counts + tm - 1) // tm * tm
    pend = jnp.cumsum(padded)
    dest = ((onehot * (pend - padded)[None, :]).sum(axis=1) + rank).astype(jnp.int32)
    n_blocks = (a + n_experts * (tm - 1) + tm - 1) // tm
    slot_tok = jnp.zeros((n_blocks * tm,), jnp.int32).at[dest].set(jnp.arange(a, dtype=jnp.int32) // k)
    block_start = jnp.arange(n_blocks, dtype=jnp.int32) * tm
    block_e = jnp.minimum((block_start[:, None] >= pend[None, :]).sum(axis=1), n_experts - 1).astype(jnp.int32)
    n_used = (pend[-1] // tm).astype(jnp.int32).reshape(1)
    return slot_tok, dest.reshape(t, k), block_e, n_used


def _na_row_pattern(rb, n_rows):
    wr = min(NA_WIN_H, n_rows)
    n_rb = n_rows // NA_QROWS
    q_row = rb * NA_QROWS + np.arange(NA_QROWS)[:, None]
    k_row = np.clip(rb - 1, 0, n_rb - NA_KROWS // NA_QROWS) * NA_QROWS + np.arange(NA_KROWS)[None, :]
    row_start = np.clip(q_row - wr // 2, 0, n_rows - wr)
    ok = (k_row >= row_start) & (k_row < row_start + wr)
    return np.where(ok, k_row - q_row + (NA_WIN_H - 1), -1)


def _na_bias_table(rpb, seq):
    n_rows = seq // GRID_W
    n_rb = n_rows // NA_QROWS
    patterns = [_na_row_pattern(rb, n_rows) for rb in range(n_rb)]
    assert all((p == patterns[1]).all() for p in patterns[1:-1])
    q_col = np.arange(GRID_W)[:, None]
    k_col = np.arange(GRID_W)[None, :]
    col_start = np.clip(q_col - NA_WIN_W // 2, 0, GRID_W - NA_WIN_W)
    col_ok = (k_col >= col_start) & (k_col < col_start + NA_WIN_W)
    dc = np.clip(k_col - q_col + (NA_WIN_W - 1), 0, 2 * NA_WIN_W - 2)
    onehot = ((dc[None] == np.arange(2 * NA_WIN_W - 1)[:, None, None]) & col_ok[None]).astype(np.float32)
    by_col = jnp.einsum('hab,bqk->haqk', rpb.astype(F32), jnp.asarray(onehot), precision=HIGHEST)
    by_col = by_col + jnp.asarray(np.where(col_ok, 0.0, NEG_INF).astype(np.float32))
    masked = jnp.full((rpb.shape[0], GRID_W, GRID_W), NEG_INF, F32)
    variants = []
    for pattern in (patterns[0], patterns[1], patterns[-1]):
        rows = [jnp.concatenate([by_col[:, a] if a >= 0 else masked for a in pattern[qr]], axis=2)
                for qr in range(NA_QROWS)]
        variants.append(jnp.concatenate(rows, axis=1))
    return jnp.stack(variants, axis=0)


def _rope_tables(seq, ctx_len):
    quarter = HEAD_DIM // 4
    lane = np.arange(SWA_W)
    inv = ROPE_BASE ** (-(lane % quarter).astype(np.float64) / quarter)
    t = np.arange(seq)
    pos = np.where((lane % HEAD_DIM < HEAD_DIM // 2)[None, :], (t // GRID_W)[:, None], (t % GRID_W)[:, None])
    ang = jnp.asarray(pos, F32) * jnp.asarray(inv, F32)[None, :]
    sign = np.where(lane % (2 * quarter) < quarter, -1.0, 1.0).astype(np.float32)
    cos = jnp.concatenate([jnp.ones((ctx_len, SWA_W), F32), jnp.cos(ang)], axis=0)
    sin = jnp.concatenate([jnp.zeros((ctx_len, SWA_W), F32), jnp.sin(ang) * sign[None, :]], axis=0)
    return cos, sin


def kernel(x, c, ctx, c_ctx, hg_lower_bounds, ada_w, ada_b, norm1_g, norm2_g, w_in, na_q_norm, na_k_norm, na_rpb,
           hg_norm_g, swa_q_norm, swa_k_norm, swa_sink, w_out, router_w, router_b, w_gu, b_gu, w_down, b_down):
    n_batch, seq, d = x.shape
    ctx_len = ctx.shape[1]
    depth = ada_w.shape[0]
    assert d == D_MODEL and seq % ROW_BLOCK == 0 and ctx_len == ROW_BLOCK
    n_ctx_rows = n_batch * ctx_len
    geom = (n_batch, n_ctx_rows // ROW_BLOCK, seq // ROW_BLOCK)

    p_lb = jax.nn.softmax(hg_lower_bounds.astype(F32), axis=0)
    lbs = jnp.cumsum(p_lb, axis=0) - p_lb[0]

    cos_t, sin_t = _rope_tables(seq, ctx_len)
    lane = np.arange(NA_W)
    group_ones = jnp.asarray((lane[:, None] // HEAD_DIM == lane[None, :] // HEAD_DIM).astype(np.float32))
    tri = np.tril(np.ones((HG_STEP_ROWS, HG_STEP_ROWS), np.float32))
    tri_f, tri_b = jnp.asarray(tri), jnp.asarray(tri.T)
    ones_bf = jnp.ones((LANES, LANES), BF16)
    n_mod_rows = -(-(n_batch + 1) // 8) * 8
    cc = jnp.zeros((n_mod_rows, d), F32).at[:n_batch].set(c).at[n_batch].set(c_ctx)
    dup = lambda w: jnp.concatenate([w[:, :HEAD_DIM], w[:, :HEAD_DIM], w[:, HEAD_DIM:], w[:, HEAD_DIM:]], axis=1)
    tile4 = lambda g: jnp.tile(g.astype(F32), 4).reshape(1, 4 * HEAD_DIM)
    pad_e = LANES - N_EXPERTS

    x_all = jnp.concatenate([ctx.reshape(n_ctx_rows, d), x.reshape(n_batch * seq, d)], axis=0)
    for l in range(depth):
        last = l == depth - 1
        mods = _ada_mod(cc, ada_w[l], ada_b[l]).reshape(n_mod_rows, 6, d)
        w = w_in[l]
        kv0 = 3 * NA_W + HG_COLS + SWA_W
        w_ext = jnp.concatenate([w[:, :kv0], dup(w[:, kv0:kv0 + SWA_KV_W]), dup(w[:, kv0 + SWA_KV_W:])],
                                axis=1).astype(BF16)
        hg, naq, nak, nav, swq, swk, swv = _inproj(
            x_all, mods, norm1_g[l].reshape(1, d), w_ext, tile4(na_q_norm[l]), tile4(na_k_norm[l]),
            tile4(swa_q_norm[l]), tile4(swa_k_norm[l]), cos_t, sin_t, group_ones, geom)

        y_na = _na_attention(naq, nak, nav, _na_bias_table(na_rpb[l], seq), geom)
        y_sw = _swa_attention(swq, swk, swv, swa_sink[l].astype(F32), geom, seq, ctx_len)
        o_f, o_b = _hgrn(hg, lbs[l].reshape(1, HG_W), tri_f, tri_b, ones_bf, n_batch, seq, ctx_len)

        if last:
            stream_off = geom[1]
        else:
            stream_off = 0
            yc_na, yc_sw = _ctx_attention(naq, nak, nav, swq, swk, swv, swa_sink[l].astype(F32), n_batch, ctx_len)
            y_na = jnp.concatenate([yc_na, y_na], axis=0)
            y_sw = jnp.concatenate([yc_sw, y_sw], axis=0)

        rw = jnp.pad(router_w[l].astype(F32), ((0, 0), (0, pad_e)))
        rb = jnp.pad(router_b[l].astype(F32), (0, pad_e), constant_values=NEG_INF).reshape(1, LANES)
        x_new, h2, top_e, top_g = _mixout(
            x_all, y_na, o_f, o_b, hg, y_sw, hg_norm_g[l].reshape(1, HG_DK).astype(F32), w_out[l].astype(BF16),
            mods, norm2_g[l].reshape(1, d), rw, rb, geom, stream_off)

        slot_tok, dest, block_e, n_used = _route(top_e[:, :TOP_K], N_EXPERTS, MOE_ROWS)
        y_slots = _moe_ffn(h2, slot_tok * SUBLANES, block_e, n_used, w_gu, b_gu, w_down, b_down, l)
        x_all = _combine(x_new, y_slots, dest * SUBLANES, top_g, mods, geom, stream_off)
    return x_all.reshape(n_batch, seq, d)
```

```python
import functools

import numpy as np
import jax
import jax.numpy as jnp
from jax import lax
from jax.experimental import pallas as pl
from jax.experimental.pallas import tpu as pltpu

D_MODEL = 1024
GRID_W = 64
HEAD_DIM = 64
ATTN_SCALE = HEAD_DIM ** -0.5
NA_HEADS = 4
NA_WIN_H = 8
NA_WIN_W = 16
HG_HEADS = 4
HG_DK = 128
SWA_Q_HEADS = 4
SWA_KV_HEADS = 2
SWA_WINDOW = 128
SWA_BLOCK = 128
ROPE_BASE = 10000.0
N_EXPERTS = 32
TOP_K = 4
D_FF = 1024
SWIGLU_LIMIT = 7.0
SWIGLU_ALPHA = 1.702
NORM_EPS = 1e-6
NEG_INF = -1e30

NA_W = NA_HEADS * HEAD_DIM
HG_W = HG_HEADS * HG_DK
SWA_W = SWA_Q_HEADS * HEAD_DIM
SWA_KV_W = SWA_KV_HEADS * HEAD_DIM
HG_COLS = 5 * HG_W
IN_COLS_EXT = 3 * NA_W + HG_COLS + 3 * SWA_W

LANES = 128
SUBLANES = 8
ROW_BLOCK = 256
NA_QROWS = 4
NA_KROWS = 12
HG_SMALL_LEVELS = (2, 4, 8)
HG_STEP_ROWS = 128
HG_HEADS_PER_STEP = 4
MOE_ROWS = 256
VMEM_LIMIT = 56 * 1024 * 1024

F32 = jnp.float32
BF16 = jnp.bfloat16
HIGHEST = lax.Precision.HIGHEST


def _cparams(*sem):
    return pltpu.CompilerParams(dimension_semantics=sem, vmem_limit_bytes=VMEM_LIMIT)


def _dot(a, b, precision=None):
    return jnp.dot(a, b, preferred_element_type=F32, precision=precision)


def _dot_nt(a, b):
    return lax.dot_general(a, b, (((1,), (1,)), ((), ())), preferred_element_type=F32)


def _dot_tn(a, b):
    return lax.dot_general(a, b, (((0,), (0,)), ((), ())), preferred_element_type=F32)


def _store_token_tiles(ref, x):
    n = x.shape[0]
    for c in range(SUBLANES):
        ref[pl.ds(c, n, stride=SUBLANES), :] = x[:, c * LANES:(c + 1) * LANES]


def _load_token_tiles(ref, first_token, n):
    return jnp.concatenate([ref[pl.ds(first_token * SUBLANES + c, n, stride=SUBLANES), :] for c in range(SUBLANES)],
                           axis=1)


def _sigmoid(x):
    return 1.0 / (1.0 + jnp.exp(-x))


def _silu(x):
    return x * _sigmoid(x)


def _ada_kernel(c_ref, w_ref, b_ref, o_ref):
    o_ref[...] = _dot(_silu(c_ref[...]), w_ref[...], HIGHEST) + b_ref[...]


def _ada_mod(cc, w, b):
    m, d = cc.shape
    n = w.shape[1]
    tn = 1024
    return pl.pallas_call(
        _ada_kernel,
        grid=(n // tn,),
        in_specs=[pl.BlockSpec((m, d), lambda j: (0, 0)),
                  pl.BlockSpec((d, tn), lambda j: (0, j)),
                  pl.BlockSpec((1, tn), lambda j: (0, j))],
        out_specs=pl.BlockSpec((m, tn), lambda j: (0, j)),
        out_shape=jax.ShapeDtypeStruct((m, n), F32),
        compiler_params=_cparams("parallel"),
        name="ada_mod",
    )(cc, w, b.reshape(1, n))


def _mod_rmsnorm(x, g, mod, shift_row, scale_row):
    y = x * lax.rsqrt(jnp.mean(x * x, axis=-1, keepdims=True) + NORM_EPS) * g
    return y * (1.0 + mod[scale_row:scale_row + 1]) + mod[shift_row:shift_row + 1]


def _head_rmsnorm(x, w, group_ones):
    ss = _dot(x * x, group_ones, HIGHEST)
    return x * lax.rsqrt(ss * (1.0 / HEAD_DIM) + NORM_EPS) * w


def _rope(x, cos, sin_signed):
    n = x.shape[-1]
    lane = lax.broadcasted_iota(jnp.int32, x.shape, 1)
    quarter = HEAD_DIM // 4
    partner = jnp.where(lane % (2 * quarter) < quarter,
                        pltpu.roll(x, n - quarter, 1), pltpu.roll(x, quarter, 1))
    return x * cos + partner * sin_signed


def _inproj_kernel(x_ref, mod_ref, g_ref, w_ref, naq_w, nak_w, swq_w, swk_w, cos_ref, sin_ref, ones_ref,
                   hg_ref, naq_ref, nak_ref, nav_ref, swq_ref, swk_ref, swv_ref):
    h = _mod_rmsnorm(x_ref[...], g_ref[...], mod_ref[...], 0, 1).astype(BF16)
    ones = ones_ref[...]

    def proj(lo, width):
        return _dot(h, w_ref[:, lo:lo + width])

    naq_ref[...] = (_head_rmsnorm(proj(0, NA_W), naq_w[...], ones) * ATTN_SCALE).astype(BF16)
    nak_ref[...] = _head_rmsnorm(proj(NA_W, NA_W), nak_w[...], ones).astype(BF16)
    nav_ref[...] = proj(2 * NA_W, NA_W).astype(BF16)
    base = 3 * NA_W
    for j in range(HG_COLS // HG_W):
        hg_ref[:, j * HG_W:(j + 1) * HG_W] = proj(base + j * HG_W, HG_W)
    base += HG_COLS
    cos, sin = cos_ref[...], sin_ref[...]
    swq = _rope(_head_rmsnorm(proj(base, SWA_W), swq_w[...], ones), cos, sin)
    swq_ref[...] = (swq * ATTN_SCALE).astype(BF16)
    swk_ref[...] = _rope(_head_rmsnorm(proj(base + SWA_W, SWA_W), swk_w[...], ones), cos, sin).astype(BF16)
    swv_ref[...] = proj(base + 2 * SWA_W, SWA_W).astype(BF16)


def _mod_row(i, n_ctx_blocks, blocks_per_batch, n_batch):
    return jnp.where(i < n_ctx_blocks, n_batch, (i - n_ctx_blocks) // blocks_per_batch)


def _inproj(x_all, mods, norm_g, w_ext, naq_w, nak_w, swq_w, swk_w, cos_t, sin_t, ones, geom):
    n_batch, n_ctx_blocks, blocks_per_batch = geom
    t = x_all.shape[0]
    tm = ROW_BLOCK
    row = lambda w: pl.BlockSpec((tm, w), lambda i: (i, 0))
    const = lambda a: pl.BlockSpec(a.shape, lambda i: (0,) * a.ndim)
    rope_blk = lambda i: (jnp.where(i < n_ctx_blocks, 0, 1 + (i - n_ctx_blocks) % blocks_per_batch), 0)
    bf = lambda w: jax.ShapeDtypeStruct((t, w), BF16)
    return pl.pallas_call(
        _inproj_kernel,
        grid=(t // tm,),
        in_specs=[row(D_MODEL),
                  pl.BlockSpec((None, 6, D_MODEL),
                               lambda i: (_mod_row(i, n_ctx_blocks, blocks_per_batch, n_batch), 0, 0)),
                  const(norm_g), const(w_ext), const(naq_w), const(nak_w), const(swq_w), const(swk_w),
                  pl.BlockSpec((tm, SWA_W), rope_blk), pl.BlockSpec((tm, SWA_W), rope_blk), const(ones)],
        out_specs=[row(HG_COLS), row(NA_W), row(NA_W), row(NA_W), row(SWA_W), row(SWA_W), row(SWA_W)],
        out_shape=[jax.ShapeDtypeStruct((t, HG_COLS), F32), bf(NA_W), bf(NA_W), bf(NA_W),
                   bf(SWA_W), bf(SWA_W), bf(SWA_W)],
        compiler_params=_cparams("parallel"),
        name="inproj",
    )(x_all, mods, norm_g, w_ext, naq_w, nak_w, swq_w, swk_w, cos_t, sin_t, ones)


def _attend(q, parts, sink_vals):
    m_rows = q.shape[0]
    lane = lax.broadcasted_iota(jnp.int32, (m_rows, q.shape[1]), 1)
    out = jnp.zeros((m_rows, q.shape[1]), F32)
    for h in range(q.shape[1] // HEAD_DIM):
        in_head = (lane // HEAD_DIM) == h
        qh = jnp.where(in_head, q, jnp.zeros_like(q))
        scores = []
        for k, _, add in parts:
            s = _dot_nt(qh, k)
            if add is not None:
                s = s + (add[h] if add.ndim == 3 else add)
            scores.append(s)
        mx = functools.reduce(jnp.maximum, [jnp.max(s, axis=-1, keepdims=True) for s in scores])
        if sink_vals is not None:
            mx = jnp.maximum(mx, sink_vals[h])
        ps = [jnp.exp(s - mx) for s in scores]
        den = functools.reduce(jnp.add, [jnp.sum(p, axis=-1, keepdims=True) for p in ps])
        if sink_vals is not None:
            den = den + jnp.exp(sink_vals[h] - mx)
        acc = functools.reduce(jnp.add, [_dot(p.astype(BF16), v) for p, (_, v, _) in zip(ps, parts)])
        out = jnp.where(in_head, acc / den, out)
    return out


def _na_kernel(q_ref, k0, k1, k2, kc, v0, v1, v2, vc, bias_ref, o_ref):
    k_lat = jnp.concatenate([k0[...], k1[...], k2[...]], axis=0)
    v_lat = jnp.concatenate([v0[...], v1[...], v2[...]], axis=0)
    o = _attend(q_ref[...], [(k_lat, v_lat, bias_ref[...]), (kc[...], vc[...], None)], None)
    o_ref[...] = o.astype(o_ref.dtype)


def _na_attention(naq, nak, nav, bias, geom):
    n_batch, n_ctx_blocks, blocks_per_batch = geom
    tm = ROW_BLOCK
    n_rb = blocks_per_batch
    kb_max = n_rb - NA_KROWS // NA_QROWS
    lat = lambda rb, b: n_ctx_blocks + b * blocks_per_batch

    def band(j):
        return pl.BlockSpec((tm, NA_W), lambda rb, b: (lat(rb, b) + jnp.clip(rb - 1, 0, kb_max) + j, 0))

    ctx = pl.BlockSpec((tm, NA_W), lambda rb, b: (b, 0))
    return pl.pallas_call(
        _na_kernel,
        grid=(n_rb, n_batch),
        in_specs=[pl.BlockSpec((tm, NA_W), lambda rb, b: (lat(rb, b) + rb, 0)),
                  band(0), band(1), band(2), ctx, band(0), band(1), band(2), ctx,
                  pl.BlockSpec((None, NA_HEADS, tm, NA_KROWS * GRID_W), lambda rb, b: (
                      jnp.where(rb == 0, 0, jnp.where(rb == n_rb - 1, 2, 1)), 0, 0, 0))],
        out_specs=pl.BlockSpec((tm, NA_W), lambda rb, b: (b * blocks_per_batch + rb, 0)),
        out_shape=jax.ShapeDtypeStruct((n_batch * blocks_per_batch * tm, NA_W), BF16),
        compiler_params=_cparams("parallel", "parallel"),
        name="na_attn",
    )(naq, nak, nak, nak, nak, nav, nav, nav, nav, bias)


def _swa_kernel(sink_ref, q_ref, kp, kc_, kn, kx, vp, vc_, vn, vx, o_ref, *, n_blocks):
    n = pl.program_id(1)
    tq = q_ref.shape[0]
    k_lat = jnp.concatenate([kp[...], kc_[...], kn[...]], axis=0)
    v_lat = jnp.concatenate([vp[...], vc_[...], vn[...]], axis=0)
    qi = lax.broadcasted_iota(jnp.int32, (tq, 3 * tq), 0)
    mi = lax.broadcasted_iota(jnp.int32, (tq, 3 * tq), 1)
    kpos = (n - 1) * tq + mi
    ok = (jnp.abs(qi + tq - mi) <= SWA_WINDOW) & (kpos >= 0) & (kpos < n_blocks * tq)
    add = jnp.where(ok, 0.0, NEG_INF).astype(F32)
    sinks = [sink_ref[h] for h in range(SWA_Q_HEADS)]
    o = _attend(q_ref[...], [(k_lat, v_lat, add), (kx[...], vx[...], None)], sinks)
    o_ref[...] = o.astype(o_ref.dtype)


def _swa_attention(swq, swk, swv, sink, geom, seq, ctx_len):
    n_batch, _, _ = geom
    tq = SWA_BLOCK
    nb = seq // tq
    first = n_batch * ctx_len // tq
    blk = lambda f: pl.BlockSpec((tq, SWA_W), lambda b, n: (first + b * nb + f(n), 0))
    prev, cur, nxt = blk(lambda n: jnp.maximum(n - 1, 0)), blk(lambda n: n), blk(lambda n: jnp.minimum(n + 1, nb - 1))
    ctx = pl.BlockSpec((ctx_len, SWA_W), lambda b, n: (b, 0))
    return pl.pallas_call(
        functools.partial(_swa_kernel, n_blocks=nb),
        grid=(n_batch, nb),
        in_specs=[pl.BlockSpec(memory_space=pltpu.SMEM), cur, prev, cur, nxt, ctx, prev, cur, nxt, ctx],
        out_specs=pl.BlockSpec((tq, SWA_W), lambda b, n: (b * nb + n, 0)),
        out_shape=jax.ShapeDtypeStruct((n_batch * seq, SWA_W), BF16),
        compiler_params=_cparams("parallel", "parallel"),
        name="swa_attn",
    )(sink, swq, swk, swk, swk, swk, swv, swv, swv, swv)


def _ctx_attn_kernel(sink_ref, naq, nak, nav, swq, swk, swv, ona_ref, osw_ref):
    ona_ref[...] = _attend(naq[...], [(nak[...], nav[...], None)], None).astype(ona_ref.dtype)
    sinks = [sink_ref[h] for h in range(SWA_Q_HEADS)]
    osw_ref[...] = _attend(swq[...], [(swk[...], swv[...], None)], sinks).astype(osw_ref.dtype)


def _ctx_attention(naq, nak, nav, swq, swk, swv, sink, n_batch, ctx_len):
    blk = pl.BlockSpec((ctx_len, NA_W), lambda b: (b, 0))
    out = jax.ShapeDtypeStruct((n_batch * ctx_len, NA_W), BF16)
    return pl.pallas_call(
        _ctx_attn_kernel,
        grid=(n_batch,),
        in_specs=[pl.BlockSpec(memory_space=pltpu.SMEM)] + [blk] * 6,
        out_specs=[blk, blk],
        out_shape=[out, out],
        compiler_params=_cparams("parallel"),
        name="ctx_attn",
    )(sink, naq, nak, nav, swq, swk, swv)


def _dot_exact_lhs(m, x):
    hi = x.astype(BF16)
    rest = x - hi.astype(F32)
    mid = rest.astype(BF16)
    lo = (rest - mid.astype(F32)).astype(BF16)
    out = _dot(m, jnp.concatenate([hi, mid, lo], axis=1))
    n = x.shape[1]
    return out[:, :n] + out[:, n:2 * n] + out[:, 2 * n:]


def _hgrn_sum_table(n_rows, reverse):
    tri = np.tril(np.ones((n_rows, n_rows), np.float32))
    if reverse:
        tri = tri.T
    t = np.arange(n_rows)
    groups = [tri] + [tri[t // size * size + (size // 2 if reverse else size // 2 - 1)] for size in HG_SMALL_LEVELS]
    return np.concatenate(groups, axis=0)


def _hgrn_block(q, k, g, v, st, sums, ones, reverse):
    n_rows = q.shape[0]
    all_sums = _dot_exact_lhs(sums, g)
    cum = all_sums[:n_rows]
    tot = cum[0:1] if reverse else cum[n_rows - 1:n_rows]
    o_inter = _dot_nt((q * jnp.exp2(cum)).astype(BF16), st.astype(BF16))
    k_end = (k * jnp.exp2(tot - cum)).astype(BF16)
    st_new = st * jnp.exp2(tot) + _dot_tn(v.astype(BF16), k_end)

    row = lax.broadcasted_iota(jnp.int32, (n_rows, LANES), 0)
    row_a = lax.broadcasted_iota(jnp.int32, (n_rows, n_rows), 0)
    col_a = lax.broadcasted_iota(jnp.int32, (n_rows, n_rows), 1)
    a = jnp.where(row_a == col_a, _dot((q * k).astype(BF16), ones), 0.0)
    zero_tile = jnp.zeros((SUBLANES, LANES), F32)
    size = 2
    while size <= n_rows:
        half = size // 2
        if size in HG_SMALL_LEVELS:
            level = HG_SMALL_LEVELS.index(size) + 1
            ref = all_sums[level * n_rows:(level + 1) * n_rows]
            later = (row % size < half) if reverse else (row % size >= half)
            q_t = (q * jnp.exp2(jnp.where(later, cum - ref, NEG_INF))).astype(BF16)
            k_t = (k * jnp.exp2(jnp.where(later, NEG_INF, ref - cum))).astype(BF16)
        else:
            q_tiles, k_tiles = [], []
            for r0 in range(0, n_rows, SUBLANES):
                sl = slice(r0, r0 + SUBLANES)
                first = r0 // size * size
                ref_row = first + (half if reverse else half - 1)
                ref = cum[ref_row:ref_row + 1]
                if (r0 - first < half) if reverse else (r0 - first >= half):
                    q_tiles.append(q[sl] * jnp.exp2(cum[sl] - ref))
                    k_tiles.append(zero_tile)
                else:
                    q_tiles.append(zero_tile)
                    k_tiles.append(k[sl] * jnp.exp2(ref - cum[sl]))
            q_t = jnp.concatenate(q_tiles, axis=0).astype(BF16)
            k_t = jnp.concatenate(k_tiles, axis=0).astype(BF16)
        a_l = _dot_nt(q_t, k_t)
        a = a + (a_l if size == n_rows else jnp.where(row_a // size == col_a // size, a_l, 0.0))
        size *= 2
    return o_inter + _dot(a.astype(BF16), v.astype(BF16)), st_new


def _hgrn_gates(zq, zf, lb):
    f = lb + (1.0 - lb) * _sigmoid(zf)
    return _silu(zq) * (HG_DK ** -0.5), 1.0 - f, jnp.log2(f)


def _hgrn_kernel(zq_f, zf_f, zi_f, zq_b, zf_b, zi_b, lb_ref, trif_ref, trib_ref, ones_ref,
                 of_ref, ob_ref, st_f, st_b):
    @pl.when(pl.program_id(2) == 0)
    def _():
        st_f[...] = jnp.zeros_like(st_f)
        st_b[...] = jnp.zeros_like(st_b)

    ones = ones_ref[...]
    for h in range(HG_HEADS_PER_STEP):
        sl = slice(h * HG_DK, (h + 1) * HG_DK)
        lb = lb_ref[:, sl]
        q, k, g = _hgrn_gates(zq_f[:, sl], zf_f[:, sl], lb)
        of_ref[:, sl], st_f[h] = _hgrn_block(q, k, g, zi_f[:, sl], st_f[h], trif_ref[...], ones, False)
        q, k, g = _hgrn_gates(zq_b[:, sl], zf_b[:, sl], lb)
        ob_ref[:, sl], st_b[h] = _hgrn_block(q, k, g, zi_b[:, sl], st_b[h], trib_ref[...], ones, True)


def _hgrn(hg, lb, tri_f, tri_b, ones, n_batch, seq, ctx_len):
    tr = HG_STEP_ROWS
    nc, nl = ctx_len // tr, seq // tr
    first = n_batch * nc

    def fwd(b, j):
        return jnp.where(j < nc, b * nc + j, first + b * nl + (j - nc))

    def bwd(b, j):
        return jnp.where(j < nc, b * nc + (nc - 1 - j), first + b * nl + (nl - 1 - (j - nc)))

    hps = HG_HEADS_PER_STEP
    groups = HG_HEADS // hps
    wide = hps * HG_DK
    col = lambda row_fn, group: pl.BlockSpec((tr, wide), lambda b, h, j: (row_fn(b, j), group * groups + h))
    const = lambda a: pl.BlockSpec(a.shape, lambda b, h, j: (0, 0))
    out = jax.ShapeDtypeStruct((hg.shape[0], HG_W), F32)
    return pl.pallas_call(
        _hgrn_kernel,
        grid=(n_batch, groups, nc + nl),
        in_specs=[col(fwd, 0), col(fwd, 1), col(fwd, 3), col(bwd, 0), col(bwd, 2), col(bwd, 3),
                  pl.BlockSpec((1, wide), lambda b, h, j: (0, h)), const(tri_f), const(tri_b), const(ones)],
        out_specs=[pl.BlockSpec((tr, wide), lambda b, h, j: (fwd(b, j), h)),
                   pl.BlockSpec((tr, wide), lambda b, h, j: (bwd(b, j), h))],
        out_shape=[out, out],
        scratch_shapes=[pltpu.VMEM((hps, HG_DK, HG_DK), F32), pltpu.VMEM((hps, HG_DK, HG_DK), F32)],
        compiler_params=_cparams("parallel", "parallel", "arbitrary"),
        name="hgrn",
    )(hg, hg, hg, hg, hg, hg, lb, tri_f, tri_b, ones)


def _mixout_kernel(x_ref, yna_ref, of_ref, ob_ref, zg_ref, ysw_ref, hgn_ref, w_ref, mod_ref, g2_ref,
                   rw_ref, rb_ref, xo_ref, h2_ref, te_ref, tg_ref):
    o = of_ref[...] + ob_ref[...]
    zg = zg_ref[...]
    parts = [yna_ref[...]]
    for h in range(HG_HEADS):
        sl = slice(h * HG_DK, (h + 1) * HG_DK)
        oh = o[:, sl]
        yh = oh * lax.rsqrt(jnp.mean(oh * oh, axis=-1, keepdims=True) + NORM_EPS) * hgn_ref[...]
        parts.append((yh * _silu(zg[:, sl])).astype(BF16))
    parts.append(ysw_ref[...])
    y = _dot(jnp.concatenate(parts, axis=1), w_ref[...])
    mod = mod_ref[...]
    x_new = x_ref[...] + mod[2:3] * y
    xo_ref[...] = x_new
    h2 = _mod_rmsnorm(x_new, g2_ref[...], mod, 3, 4)
    _store_token_tiles(h2_ref, h2)

    logits = _dot(h2, rw_ref[...], HIGHEST) + rb_ref[...]
    lane = lax.broadcasted_iota(jnp.int32, logits.shape, 1).astype(F32)
    top_e = jnp.zeros(logits.shape, F32)
    top_v = jnp.full(logits.shape, NEG_INF, F32)
    for j in range(TOP_K):
        best = jnp.max(logits, axis=-1, keepdims=True)
        arg = jnp.min(jnp.where(logits == best, lane, float(LANES)), axis=-1, keepdims=True)
        top_e = jnp.where(lane == j, arg, top_e)
        top_v = jnp.where(lane == j, best, top_v)
        logits = jnp.where(lane == arg, -jnp.inf, logits)
    ex = jnp.exp(top_v - jnp.max(top_v, axis=-1, keepdims=True))
    te_ref[...] = top_e.astype(jnp.int32)
    tg_ref[...] = ex / jnp.sum(ex, axis=-1, keepdims=True)


def _mixout(x, yna, o_f, o_b, hg, ysw, hgn, w_out, mods, norm2_g, rw, rb, geom, stream_off):
    n_batch, n_ctx_blocks, blocks_per_batch = geom
    tm = ROW_BLOCK
    t = x.shape[0] - stream_off * tm
    row = lambda w: pl.BlockSpec((tm, w), lambda i: (i, 0))
    full = lambda w, cb: pl.BlockSpec((tm, w), lambda i: (i + stream_off, cb))
    const = lambda a: pl.BlockSpec(a.shape, lambda i: (0,) * a.ndim)
    return pl.pallas_call(
        _mixout_kernel,
        grid=(t // tm,),
        in_specs=[full(D_MODEL, 0), row(NA_W), full(HG_W, 0), full(HG_W, 0), full(HG_W, 4), row(SWA_W),
                  const(hgn), const(w_out),
                  pl.BlockSpec((None, 6, D_MODEL), lambda i: (
                      _mod_row(i + stream_off, n_ctx_blocks, blocks_per_batch, n_batch), 0, 0)),
                  const(norm2_g), const(rw), const(rb)],
        out_specs=[row(D_MODEL), pl.BlockSpec((tm * SUBLANES, LANES), lambda i: (i, 0)), row(LANES), row(LANES)],
        out_shape=[jax.ShapeDtypeStruct((t, D_MODEL), F32), jax.ShapeDtypeStruct((t * SUBLANES, LANES), F32),
                   jax.ShapeDtypeStruct((t, LANES), jnp.int32), jax.ShapeDtypeStruct((t, LANES), F32)],
        compiler_params=_cparams("parallel"),
        name="mixout",
    )(x, yna, o_f, o_b, hg, ysw, hgn, w_out, mods, norm2_g, rw, rb)


def _gather_copy(src_hbm, idx_ref, buf, sem, slot, n_tokens):
    def start():
        for r in range(n_tokens):
            src_row = pl.multiple_of(idx_ref[0, 0, r], SUBLANES)
            pltpu.make_async_copy(src_hbm.at[pl.ds(src_row, SUBLANES), :],
                                  buf.at[slot, pl.ds(r * SUBLANES, SUBLANES), :], sem.at[slot]).start()

    def wait():
        pltpu.make_async_copy(src_hbm.at[pl.ds(0, n_tokens * SUBLANES), :], buf.at[slot], sem.at[slot]).wait()

    return start, wait


def _dispatch_kernel(pad_ref, nu_ref, idx_ref, h_ref, xs_hbm, zeros, sem, zero_sem):
    n_tok = h_ref.shape[0] // SUBLANES
    block_rows = zeros.shape[0]
    n_blocks = xs_hbm.shape[0] // block_rows

    def zero_slot(e, r):
        row = pl.multiple_of((pad_ref[0, e] + r) * SUBLANES, SUBLANES)
        return pltpu.make_async_copy(zeros.at[pl.ds(0, SUBLANES), :], xs_hbm.at[pl.ds(row, SUBLANES), :], zero_sem)

    def zero_block(b):
        row = pl.multiple_of(b * block_rows, block_rows)
        return pltpu.make_async_copy(zeros, xs_hbm.at[pl.ds(row, block_rows), :], zero_sem)

    def for_each_unused(slot_fn, block_fn):
        def per_expert(e, carry):
            lax.fori_loop(0, pad_ref[1, e], lambda r, c: (slot_fn(e, r), c)[1], 0)
            return carry

        lax.fori_loop(0, pad_ref.shape[1], per_expert, 0)
        lax.fori_loop(nu_ref[0], n_blocks, lambda b, c: (block_fn(b), c)[1], 0)

    @pl.when(pl.program_id(0) == 0)
    def _():
        zeros[...] = jnp.zeros_like(zeros)
        for_each_unused(lambda e, r: zero_slot(e, r).start(), lambda b: zero_block(b).start())
        for_each_unused(lambda e, r: zero_slot(e, r).wait(), lambda b: zero_block(b).wait())

    for t in range(n_tok):
        for j in range(TOP_K):
            dst = pl.multiple_of(idx_ref[0, 0, t * TOP_K + j], SUBLANES)
            pltpu.make_async_copy(h_ref.at[pl.ds(t * SUBLANES, SUBLANES), :],
                                  xs_hbm.at[pl.ds(dst, SUBLANES), :], sem).start()
    for j in range(TOP_K):
        pltpu.make_async_copy(h_ref, xs_hbm.at[pl.ds(0, n_tok * SUBLANES), :], sem).wait()


def _dispatch(h_tiles, dest_row, pad_slots, n_used, n_slots):
    tm = ROW_BLOCK
    nb = dest_row.shape[0] // tm
    assert n_slots % MOE_ROWS == 0
    return pl.pallas_call(
        _dispatch_kernel,
        grid=(nb,),
        in_specs=[pl.BlockSpec(memory_space=pltpu.SMEM), pl.BlockSpec(memory_space=pltpu.SMEM),
                  pl.BlockSpec((1, 1, TOP_K * tm), lambda i: (i, 0, 0), memory_space=pltpu.SMEM),
                  pl.BlockSpec((tm * SUBLANES, LANES), lambda i: (i, 0))],
        out_specs=pl.BlockSpec(memory_space=pl.ANY),
        out_shape=jax.ShapeDtypeStruct((n_slots * SUBLANES, LANES), F32),
        scratch_shapes=[pltpu.VMEM((MOE_ROWS * SUBLANES, LANES), F32), pltpu.SemaphoreType.DMA(()),
                        pltpu.SemaphoreType.DMA(())],
        compiler_params=_cparams("arbitrary"),
        name="moe_dispatch",
    )(pad_slots, n_used, dest_row.reshape(nb, 1, TOP_K * tm), h_tiles)


def _moe_kernel(be_ref, nu_ref, xs_ref, wgu_ref, bgu_ref, wd_ref, bd_ref, y_ref, wgu_bf, wd_bf):
    i = pl.program_id(0)
    live = i < nu_ref[0]
    tm = y_ref.shape[0] // SUBLANES
    changed = (i == 0) | (be_ref[i] != be_ref[jnp.maximum(i - 1, 0)])

    @pl.when(live & changed)
    def _():
        wgu_bf[...] = wgu_ref[...].astype(BF16)
        wd_bf[...] = wd_ref[...].astype(BF16)

    @pl.when(live)
    def _():
        x = _load_token_tiles(xs_ref, 0, tm)
        gu = _dot(x.astype(BF16), wgu_bf[...]) + bgu_ref[...]
        glu = jnp.minimum(gu[:, :D_FF], SWIGLU_LIMIT)
        lin = jnp.clip(gu[:, D_FF:], -SWIGLU_LIMIT, SWIGLU_LIMIT)
        act = glu * _sigmoid(SWIGLU_ALPHA * glu) * (lin + 1.0)
        _store_token_tiles(y_ref, _dot(act.astype(BF16), wd_bf[...]) + bd_ref[...])

    @pl.when(jnp.logical_not(live))
    def _():
        y_ref[...] = jnp.zeros_like(y_ref)


def _moe_ffn(xs_tiles, block_e, n_used, w_gu, b_gu, w_down, b_down, layer):
    tm = MOE_ROWS
    nb = xs_tiles.shape[0] // (tm * SUBLANES)
    _, n_e, d, f2 = w_gu.shape
    weight = lambda r, c: pl.BlockSpec((None, None, r, c), lambda i, be, nu: (layer, be[i], 0, 0))
    grid_spec = pltpu.PrefetchScalarGridSpec(
        num_scalar_prefetch=2,
        grid=(nb,),
        in_specs=[pl.BlockSpec((tm * SUBLANES, LANES), lambda i, be, nu: (jnp.minimum(i, nu[0] - 1), 0)),
                  weight(d, f2), weight(1, f2), weight(f2 // 2, d), weight(1, d)],
        out_specs=pl.BlockSpec((tm * SUBLANES, LANES), lambda i, be, nu: (i, 0)),
        scratch_shapes=[pltpu.VMEM((d, f2), BF16), pltpu.VMEM((f2 // 2, d), BF16)],
    )
    n_l = w_gu.shape[0]
    return pl.pallas_call(
        _moe_kernel,
        grid_spec=grid_spec,
        out_shape=jax.ShapeDtypeStruct((nb * tm * SUBLANES, LANES), F32),
        compiler_params=_cparams("arbitrary"),
        name="moe_ffn",
    )(block_e, n_used, xs_tiles, w_gu, b_gu.reshape(n_l, n_e, 1, f2), w_down,
      b_down.reshape(n_l, n_e, 1, d))


def _combine_kernel(idx_ref, x_ref, y_hbm, tg_ref, mod_ref, o_ref, ybuf, sem):
    i = pl.program_id(0)
    tm = o_ref.shape[0]
    n_rows = TOP_K * tm

    @pl.when(i < pl.num_programs(0) - 1)
    def _():
        _gather_copy(y_hbm, idx_ref, ybuf, sem, i % 2, n_rows)[0]()

    @pl.when(i > 0)
    def _():
        slot = (i - 1) % 2
        _gather_copy(y_hbm, idx_ref, ybuf, sem, slot, n_rows)[1]()
        gate = tg_ref[...]
        y = [_load_token_tiles(ybuf.at[slot], j * tm, tm) * gate[:, j:j + 1] for j in range(TOP_K)]
        o_ref[...] = x_ref[...] + mod_ref[...][5:6] * ((y[0] + y[1]) + (y[2] + y[3]))


def _combine(x, y_tiles, dest_row, gates, mods, geom, stream_off):
    n_batch, n_ctx_blocks, blocks_per_batch = geom
    t = x.shape[0]
    tm = ROW_BLOCK
    nb = t // tm
    idx = dest_row.reshape(nb, tm, TOP_K).transpose(0, 2, 1).reshape(nb, 1, TOP_K * tm)
    prev = lambda i: jnp.maximum(i - 1, 0)
    row = lambda w: pl.BlockSpec((tm, w), lambda i: (prev(i), 0))
    return pl.pallas_call(
        _combine_kernel,
        grid=(nb + 1,),
        in_specs=[pl.BlockSpec((1, 1, TOP_K * tm), lambda i: (jnp.minimum(i, nb - 1), 0, 0),
                               memory_space=pltpu.SMEM),
                  row(D_MODEL), pl.BlockSpec(memory_space=pl.ANY), row(LANES),
                  pl.BlockSpec((None, 6, D_MODEL), lambda i: (
                      _mod_row(prev(i) + stream_off, n_ctx_blocks, blocks_per_batch, n_batch), 0, 0))],
        out_specs=row(D_MODEL),
        out_shape=jax.ShapeDtypeStruct((t, D_MODEL), F32),
        scratch_shapes=[pltpu.VMEM((2, TOP_K * tm * SUBLANES, LANES), F32), pltpu.SemaphoreType.DMA((2,))],
        compiler_params=_cparams("arbitrary"),
        name="moe_combine",
    )(idx, x, y_tiles, gates, mods)


def _route(top_e, n_experts, tm):
    t, k = top_e.shape
    a = t * k
    flat_e = top_e.reshape(-1)
    onehot = (flat_e[:, None] == jnp.arange(n_experts, dtype=jnp.int32)[None, :]).astype(jnp.int32)
    counts = onehot.sum(axis=0)
    rank = (jnp.cumsum(onehot, axis=0) * onehot).sum(axis=1) - 1
    padded = (counts + tm - 1) // tm * tm
    pend = jnp.cumsum(padded)
    start = pend - padded
    dest = ((onehot * start[None, :]).sum(axis=1) + rank).astype(jnp.int32)
    n_blocks = (a + n_experts * (tm - 1) + tm - 1) // tm
    block_start = jnp.arange(n_blocks, dtype=jnp.int32) * tm
    block_e = jnp.minimum((block_start[:, None] >= pend[None, :]).sum(axis=1), n_experts - 1).astype(jnp.int32)
    n_used = (pend[-1] // tm).astype(jnp.int32).reshape(1)
    pad_slots = jnp.stack([start + counts, padded - counts], axis=0).astype(jnp.int32)
    return dest.reshape(t, k), block_e, n_used, pad_slots, n_blocks * tm


def _na_row_pattern(rb, n_rows):
    wr = min(NA_WIN_H, n_rows)
    n_rb = n_rows // NA_QROWS
    q_row = rb * NA_QROWS + np.arange(NA_QROWS)[:, None]
    k_row = np.clip(rb - 1, 0, n_rb - NA_KROWS // NA_QROWS) * NA_QROWS + np.arange(NA_KROWS)[None, :]
    row_start = np.clip(q_row - wr // 2, 0, n_rows - wr)
    ok = (k_row >= row_start) & (k_row < row_start + wr)
    return np.where(ok, k_row - q_row + (NA_WIN_H - 1), -1)


def _na_bias_table(rpb, seq):
    n_rows = seq // GRID_W
    n_rb = n_rows // NA_QROWS
    patterns = [_na_row_pattern(rb, n_rows) for rb in range(n_rb)]
    assert all((p == patterns[1]).all() for p in patterns[1:-1])
    q_col = np.arange(GRID_W)[:, None]
    k_col = np.arange(GRID_W)[None, :]
    col_start = np.clip(q_col - NA_WIN_W // 2, 0, GRID_W - NA_WIN_W)
    col_ok = (k_col >= col_start) & (k_col < col_start + NA_WIN_W)
    dc = np.clip(k_col - q_col + (NA_WIN_W - 1), 0, 2 * NA_WIN_W - 2)
    onehot = ((dc[None] == np.arange(2 * NA_WIN_W - 1)[:, None, None]) & col_ok[None]).astype(np.float32)
    by_col = jnp.einsum('hab,bqk->haqk', rpb.astype(F32), jnp.asarray(onehot), precision=HIGHEST)
    by_col = by_col + jnp.asarray(np.where(col_ok, 0.0, NEG_INF).astype(np.float32))
    masked = jnp.full((rpb.shape[0], GRID_W, GRID_W), NEG_INF, F32)
    variants = []
    for pattern in (patterns[0], patterns[1], patterns[-1]):
        rows = [jnp.concatenate([by_col[:, a] if a >= 0 else masked for a in pattern[qr]], axis=2)
                for qr in range(NA_QROWS)]
        variants.append(jnp.concatenate(rows, axis=1))
    return jnp.stack(variants, axis=0)


def _rope_tables(seq, ctx_len):
    quarter = HEAD_DIM // 4
    lane = np.arange(SWA_W)
    inv = ROPE_BASE ** (-(lane % quarter).astype(np.float64) / quarter)
    t = np.arange(seq)
    pos = np.where((lane % HEAD_DIM < HEAD_DIM // 2)[None, :], (t // GRID_W)[:, None], (t % GRID_W)[:, None])
    ang = jnp.asarray(pos, F32) * jnp.asarray(inv, F32)[None, :]
    sign = np.where(lane % (2 * quarter) < quarter, -1.0, 1.0).astype(np.float32)
    cos = jnp.concatenate([jnp.ones((ctx_len, SWA_W), F32), jnp.cos(ang)], axis=0)
    sin = jnp.concatenate([jnp.zeros((ctx_len, SWA_W), F32), jnp.sin(ang) * sign[None, :]], axis=0)
    return cos, sin


def kernel(x, c, ctx, c_ctx, hg_lower_bounds, ada_w, ada_b, norm1_g, norm2_g, w_in, na_q_norm, na_k_norm, na_rpb,
           hg_norm_g, swa_q_norm, swa_k_norm, swa_sink, w_out, router_w, router_b, w_gu, b_gu, w_down, b_down):
    n_batch, seq, d = x.shape
    ctx_len = ctx.shape[1]
    depth = ada_w.shape[0]
    assert d == D_MODEL and seq % ROW_BLOCK == 0 and ctx_len == ROW_BLOCK
    n_ctx_rows = n_batch * ctx_len
    geom = (n_batch, n_ctx_rows // ROW_BLOCK, seq // ROW_BLOCK)

    p_lb = jax.nn.softmax(hg_lower_bounds.astype(F32), axis=0)
    lbs = jnp.cumsum(p_lb, axis=0) - p_lb[0]

    cos_t, sin_t = _rope_tables(seq, ctx_len)
    lane = np.arange(NA_W)
    group_ones = jnp.asarray((lane[:, None] // HEAD_DIM == lane[None, :] // HEAD_DIM).astype(np.float32))
    tri_f = jnp.asarray(_hgrn_sum_table(HG_STEP_ROWS, False), BF16)
    tri_b = jnp.asarray(_hgrn_sum_table(HG_STEP_ROWS, True), BF16)
    ones_bf = jnp.ones((LANES, LANES), BF16)
    n_mod_rows = -(-(n_batch + 1) // 8) * 8
    cc = jnp.zeros((n_mod_rows, d), F32).at[:n_batch].set(c).at[n_batch].set(c_ctx)
    dup = lambda w: jnp.concatenate([w[:, :HEAD_DIM], w[:, :HEAD_DIM], w[:, HEAD_DIM:], w[:, HEAD_DIM:]], axis=1)
    tile4 = lambda g: jnp.tile(g.astype(F32), 4).reshape(1, 4 * HEAD_DIM)
    pad_e = LANES - N_EXPERTS

    x_all = jnp.concatenate([ctx.reshape(n_ctx_rows, d), x.reshape(n_batch * seq, d)], axis=0)
    for l in range(depth):
        last = l == depth - 1
        mods = _ada_mod(cc, ada_w[l], ada_b[l]).reshape(n_mod_rows, 6, d)
        w = w_in[l]
        kv0 = 3 * NA_W + HG_COLS + SWA_W
        w_ext = jnp.concatenate([w[:, :kv0], dup(w[:, kv0:kv0 + SWA_KV_W]), dup(w[:, kv0 + SWA_KV_W:])],
                                axis=1).astype(BF16)
        hg, naq, nak, nav, swq, swk, swv = _inproj(
            x_all, mods, norm1_g[l].reshape(1, d), w_ext, tile4(na_q_norm[l]), tile4(na_k_norm[l]),
            tile4(swa_q_norm[l]), tile4(swa_k_norm[l]), cos_t, sin_t, group_ones, geom)

        y_na = _na_attention(naq, nak, nav, _na_bias_table(na_rpb[l], seq), geom)
        y_sw = _swa_attention(swq, swk, swv, swa_sink[l].astype(F32), geom, seq, ctx_len)
        o_f, o_b = _hgrn(hg, lbs[l].reshape(1, HG_W), tri_f, tri_b, ones_bf, n_batch, seq, ctx_len)

        if last:
            stream_off = geom[1]
        else:
            stream_off = 0
            yc_na, yc_sw = _ctx_attention(naq, nak, nav, swq, swk, swv, swa_sink[l].astype(F32), n_batch, ctx_len)
            y_na = jnp.concatenate([yc_na, y_na], axis=0)
            y_sw = jnp.concatenate([yc_sw, y_sw], axis=0)

        rw = jnp.pad(router_w[l].astype(F32), ((0, 0), (0, pad_e)))
        rb = jnp.pad(router_b[l].astype(F32), (0, pad_e), constant_values=NEG_INF).reshape(1, LANES)
        x_new, h2, top_e, top_g = _mixout(
            x_all, y_na, o_f, o_b, hg, y_sw, hg_norm_g[l].reshape(1, HG_DK).astype(F32), w_out[l].astype(BF16),
            mods, norm2_g[l].reshape(1, d), rw, rb, geom, stream_off)

        dest, block_e, n_used, pad_slots, n_slots = _route(top_e[:, :TOP_K], N_EXPERTS, MOE_ROWS)
        dest_row = dest * SUBLANES
        xs = _dispatch(h2, dest_row, pad_slots, n_used, n_slots)
        y_slots = _moe_ffn(xs, block_e, n_used, w_gu, b_gu, w_down, b_down, l)
        x_all = _combine(x_new, y_slots, dest_row, top_g, mods, geom, stream_off)
    return x_all.reshape(n_batch, seq, d)
```

```python
import functools

import numpy as np
import jax
import jax.numpy as jnp
from jax import lax
from jax.experimental import pallas as pl
from jax.experimental.pallas import tpu as pltpu

D_MODEL = 1024
GRID_W = 64
HEAD_DIM = 64
ATTN_SCALE = HEAD_DIM ** -0.5
NA_HEADS = 4
NA_WIN_H = 8
NA_WIN_W = 16
HG_HEADS = 4
HG_DK = 128
SWA_Q_HEADS = 4
SWA_KV_HEADS = 2
SWA_WINDOW = 128
SWA_BLOCK = 128
ROPE_BASE = 10000.0
N_EXPERTS = 32
TOP_K = 4
D_FF = 1024
SWIGLU_LIMIT = 7.0
SWIGLU_ALPHA = 1.702
NORM_EPS = 1e-6
NEG_INF = -1e30

NA_W = NA_HEADS * HEAD_DIM
HG_W = HG_HEADS * HG_DK
SWA_W = SWA_Q_HEADS * HEAD_DIM
SWA_KV_W = SWA_KV_HEADS * HEAD_DIM
HG_COLS = 5 * HG_W
IN_COLS_EXT = 3 * NA_W + HG_COLS + 3 * SWA_W

LANES = 128
SUBLANES = 8
ROW_BLOCK = 256
NA_QROWS = 4
NA_KROWS = 12
HG_SMALL_LEVELS = (2, 4, 8)
HG_STEP_ROWS = 128
HG_HEADS_PER_STEP = 4
MOE_ROWS = 256
VMEM_LIMIT = 56 * 1024 * 1024

F32 = jnp.float32
BF16 = jnp.bfloat16
HIGHEST = lax.Precision.HIGHEST


def _cparams(*sem):
    return pltpu.CompilerParams(dimension_semantics=sem, vmem_limit_bytes=VMEM_LIMIT)


def _dot(a, b, precision=None):
    return jnp.dot(a, b, preferred_element_type=F32, precision=precision)


def _dot_nt(a, b):
    return lax.dot_general(a, b, (((1,), (1,)), ((), ())), preferred_element_type=F32)


def _dot_tn(a, b):
    return lax.dot_general(a, b, (((0,), (0,)), ((), ())), preferred_element_type=F32)


def _store_token_tiles(ref, x):
    n = x.shape[0]
    for c in range(SUBLANES):
        ref[pl.ds(c, n, stride=SUBLANES), :] = x[:, c * LANES:(c + 1) * LANES]


def _load_token_tiles(ref, first_token, n):
    return jnp.concatenate([ref[pl.ds(first_token * SUBLANES + c, n, stride=SUBLANES), :] for c in range(SUBLANES)],
                           axis=1)


def _sigmoid(x):
    return 1.0 / (1.0 + jnp.exp(-x))


def _silu(x):
    return x * _sigmoid(x)


def _ada_kernel(c_ref, w_ref, b_ref, o_ref):
    o_ref[...] = _dot(_silu(c_ref[...]), w_ref[...], HIGHEST) + b_ref[...]


def _ada_mod(cc, w, b):
    m, d = cc.shape
    n = w.shape[1]
    tn = 1024
    return pl.pallas_call(
        _ada_kernel,
        grid=(n // tn,),
        in_specs=[pl.BlockSpec((m, d), lambda j: (0, 0)),
                  pl.BlockSpec((d, tn), lambda j: (0, j)),
                  pl.BlockSpec((1, tn), lambda j: (0, j))],
        out_specs=pl.BlockSpec((m, tn), lambda j: (0, j)),
        out_shape=jax.ShapeDtypeStruct((m, n), F32),
        compiler_params=_cparams("parallel"),
        name="ada_mod",
    )(cc, w, b.reshape(1, n))


def _mod_rmsnorm(x, g, mod, shift_row, scale_row):
    y = x * lax.rsqrt(jnp.mean(x * x, axis=-1, keepdims=True) + NORM_EPS) * g
    return y * (1.0 + mod[scale_row:scale_row + 1]) + mod[shift_row:shift_row + 1]


def _split_bf16(x, terms):
    parts = []
    for _ in range(terms - 1):
        parts.append(x.astype(BF16))
        x = x - parts[-1].astype(F32)
    return parts + [x.astype(BF16)]


def _head_rmsnorm(x, w, group_ones):
    hi, lo = _split_bf16(x * x, 2)
    n = x.shape[0]
    ss = _dot(jnp.concatenate([hi, lo], axis=0), group_ones)
    return x * lax.rsqrt((ss[:n] + ss[n:]) * (1.0 / HEAD_DIM) + NORM_EPS) * w


def _rope(x, cos, sin_signed):
    n = x.shape[-1]
    lane = lax.broadcasted_iota(jnp.int32, x.shape, 1)
    quarter = HEAD_DIM // 4
    partner = jnp.where(lane % (2 * quarter) < quarter,
                        pltpu.roll(x, n - quarter, 1), pltpu.roll(x, quarter, 1))
    return x * cos + partner * sin_signed


def _inproj_kernel(x_ref, mod_ref, g_ref, w_ref, naq_w, nak_w, swq_w, swk_w, cos_ref, sin_ref, ones_ref,
                   hg_ref, naq_ref, nak_ref, nav_ref, swq_ref, swk_ref, swv_ref):
    h = _mod_rmsnorm(x_ref[...], g_ref[...], mod_ref[...], 0, 1).astype(BF16)
    ones = ones_ref[...]

    def proj(lo, width):
        return _dot(h, w_ref[:, lo:lo + width])

    naq_ref[...] = (_head_rmsnorm(proj(0, NA_W), naq_w[...], ones) * ATTN_SCALE).astype(BF16)
    nak_ref[...] = _head_rmsnorm(proj(NA_W, NA_W), nak_w[...], ones).astype(BF16)
    nav_ref[...] = proj(2 * NA_W, NA_W).astype(BF16)
    base = 3 * NA_W
    for j in range(HG_COLS // HG_W):
        hg_ref[:, j * HG_W:(j + 1) * HG_W] = proj(base + j * HG_W, HG_W)
    base += HG_COLS
    cos, sin = cos_ref[...], sin_ref[...]
    swq = _rope(_head_rmsnorm(proj(base, SWA_W), swq_w[...], ones), cos, sin)
    swq_ref[...] = (swq * ATTN_SCALE).astype(BF16)
    swk_ref[...] = _rope(_head_rmsnorm(proj(base + SWA_W, SWA_W), swk_w[...], ones), cos, sin).astype(BF16)
    swv_ref[...] = proj(base + 2 * SWA_W, SWA_W).astype(BF16)


def _mod_row(i, n_ctx_blocks, blocks_per_batch, n_batch):
    return jnp.where(i < n_ctx_blocks, n_batch, (i - n_ctx_blocks) // blocks_per_batch)


def _inproj(x_all, mods, norm_g, w_ext, naq_w, nak_w, swq_w, swk_w, cos_t, sin_t, ones, geom):
    n_batch, n_ctx_blocks, blocks_per_batch = geom
    t = x_all.shape[0]
    tm = ROW_BLOCK
    row = lambda w: pl.BlockSpec((tm, w), lambda i: (i, 0))
    const = lambda a: pl.BlockSpec(a.shape, lambda i: (0,) * a.ndim)
    rope_blk = lambda i: (jnp.where(i < n_ctx_blocks, 0, 1 + (i - n_ctx_blocks) % blocks_per_batch), 0)
    bf = lambda w: jax.ShapeDtypeStruct((t, w), BF16)
    return pl.pallas_call(
        _inproj_kernel,
        grid=(t // tm,),
        in_specs=[row(D_MODEL),
                  pl.BlockSpec((None, 6, D_MODEL),
                               lambda i: (_mod_row(i, n_ctx_blocks, blocks_per_batch, n_batch), 0, 0)),
                  const(norm_g), const(w_ext), const(naq_w), const(nak_w), const(swq_w), const(swk_w),
                  pl.BlockSpec((tm, SWA_W), rope_blk), pl.BlockSpec((tm, SWA_W), rope_blk), const(ones)],
        out_specs=[row(HG_COLS), row(NA_W), row(NA_W), row(NA_W), row(SWA_W), row(SWA_W), row(SWA_W)],
        out_shape=[jax.ShapeDtypeStruct((t, HG_COLS), F32), bf(NA_W), bf(NA_W), bf(NA_W),
                   bf(SWA_W), bf(SWA_W), bf(SWA_W)],
        compiler_params=_cparams("parallel"),
        name="inproj",
    )(x_all, mods, norm_g, w_ext, naq_w, nak_w, swq_w, swk_w, cos_t, sin_t, ones)


def _attend(q, parts, sink_vals):
    m_rows = q.shape[0]
    lane = lax.broadcasted_iota(jnp.int32, (m_rows, q.shape[1]), 1)
    out = jnp.zeros((m_rows, q.shape[1]), F32)
    for h in range(q.shape[1] // HEAD_DIM):
        in_head = (lane // HEAD_DIM) == h
        qh = jnp.where(in_head, q, jnp.zeros_like(q))
        scores = []
        for k, _, add in parts:
            s = _dot_nt(qh, k)
            if add is not None:
                s = s + (add[h] if add.ndim == 3 else add)
            scores.append(s)
        mx = functools.reduce(jnp.maximum, [jnp.max(s, axis=-1, keepdims=True) for s in scores])
        if sink_vals is not None:
            mx = jnp.maximum(mx, sink_vals[h])
        ps = [jnp.exp(s - mx) for s in scores]
        den = functools.reduce(jnp.add, [jnp.sum(p, axis=-1, keepdims=True) for p in ps])
        if sink_vals is not None:
            den = den + jnp.exp(sink_vals[h] - mx)
        acc = functools.reduce(jnp.add, [_dot(p.astype(BF16), v) for p, (_, v, _) in zip(ps, parts)])
        out = jnp.where(in_head, acc / den, out)
    return out


def _na_kernel(q_ref, k0, k1, k2, kc, v0, v1, v2, vc, bias_ref, o_ref):
    k_lat = jnp.concatenate([k0[...], k1[...], k2[...]], axis=0)
    v_lat = jnp.concatenate([v0[...], v1[...], v2[...]], axis=0)
    o = _attend(q_ref[...], [(k_lat, v_lat, bias_ref[...]), (kc[...], vc[...], None)], None)
    o_ref[...] = o.astype(o_ref.dtype)


def _na_attention(naq, nak, nav, bias, geom):
    n_batch, n_ctx_blocks, blocks_per_batch = geom
    tm = ROW_BLOCK
    n_rb = blocks_per_batch
    kb_max = n_rb - NA_KROWS // NA_QROWS
    lat = lambda rb, b: n_ctx_blocks + b * blocks_per_batch

    def band(j):
        return pl.BlockSpec((tm, NA_W), lambda rb, b: (lat(rb, b) + jnp.clip(rb - 1, 0, kb_max) + j, 0))

    ctx = pl.BlockSpec((tm, NA_W), lambda rb, b: (b, 0))
    return pl.pallas_call(
        _na_kernel,
        grid=(n_rb, n_batch),
        in_specs=[pl.BlockSpec((tm, NA_W), lambda rb, b: (lat(rb, b) + rb, 0)),
                  band(0), band(1), band(2), ctx, band(0), band(1), band(2), ctx,
                  pl.BlockSpec((None, NA_HEADS, tm, NA_KROWS * GRID_W), lambda rb, b: (
                      jnp.where(rb == 0, 0, jnp.where(rb == n_rb - 1, 2, 1)), 0, 0, 0))],
        out_specs=pl.BlockSpec((tm, NA_W), lambda rb, b: (b * blocks_per_batch + rb, 0)),
        out_shape=jax.ShapeDtypeStruct((n_batch * blocks_per_batch * tm, NA_W), BF16),
        compiler_params=_cparams("parallel", "parallel"),
        name="na_attn",
    )(naq, nak, nak, nak, nak, nav, nav, nav, nav, bias)


def _swa_kernel(sink_ref, q_ref, kp, kc_, kn, kx, vp, vc_, vn, vx, o_ref, *, n_blocks):
    n = pl.program_id(1)
    tq = q_ref.shape[0]
    k_lat = jnp.concatenate([kp[...], kc_[...], kn[...]], axis=0)
    v_lat = jnp.concatenate([vp[...], vc_[...], vn[...]], axis=0)
    qi = lax.broadcasted_iota(jnp.int32, (tq, 3 * tq), 0)
    mi = lax.broadcasted_iota(jnp.int32, (tq, 3 * tq), 1)
    kpos = (n - 1) * tq + mi
    ok = (jnp.abs(qi + tq - mi) <= SWA_WINDOW) & (kpos >= 0) & (kpos < n_blocks * tq)
    add = jnp.where(ok, 0.0, NEG_INF).astype(F32)
    sinks = [sink_ref[h] for h in range(SWA_Q_HEADS)]
    o = _attend(q_ref[...], [(k_lat, v_lat, add), (kx[...], vx[...], None)], sinks)
    o_ref[...] = o.astype(o_ref.dtype)


def _swa_attention(swq, swk, swv, sink, geom, seq, ctx_len):
    n_batch, _, _ = geom
    tq = SWA_BLOCK
    nb = seq // tq
    first = n_batch * ctx_len // tq
    blk = lambda f: pl.BlockSpec((tq, SWA_W), lambda b, n: (first + b * nb + f(n), 0))
    prev, cur, nxt = blk(lambda n: jnp.maximum(n - 1, 0)), blk(lambda n: n), blk(lambda n: jnp.minimum(n + 1, nb - 1))
    ctx = pl.BlockSpec((ctx_len, SWA_W), lambda b, n: (b, 0))
    return pl.pallas_call(
        functools.partial(_swa_kernel, n_blocks=nb),
        grid=(n_batch, nb),
        in_specs=[pl.BlockSpec(memory_space=pltpu.SMEM), cur, prev, cur, nxt, ctx, prev, cur, nxt, ctx],
        out_specs=pl.BlockSpec((tq, SWA_W), lambda b, n: (b * nb + n, 0)),
        out_shape=jax.ShapeDtypeStruct((n_batch * seq, SWA_W), BF16),
        compiler_params=_cparams("parallel", "parallel"),
        name="swa_attn",
    )(sink, swq, swk, swk, swk, swk, swv, swv, swv, swv)


def _ctx_attn_kernel(sink_ref, naq, nak, nav, swq, swk, swv, ona_ref, osw_ref):
    ona_ref[...] = _attend(naq[...], [(nak[...], nav[...], None)], None).astype(ona_ref.dtype)
    sinks = [sink_ref[h] for h in range(SWA_Q_HEADS)]
    osw_ref[...] = _attend(swq[...], [(swk[...], swv[...], None)], sinks).astype(osw_ref.dtype)


def _ctx_attention(naq, nak, nav, swq, swk, swv, sink, n_batch, ctx_len):
    blk = pl.BlockSpec((ctx_len, NA_W), lambda b: (b, 0))
    out = jax.ShapeDtypeStruct((n_batch * ctx_len, NA_W), BF16)
    return pl.pallas_call(
        _ctx_attn_kernel,
        grid=(n_batch,),
        in_specs=[pl.BlockSpec(memory_space=pltpu.SMEM)] + [blk] * 6,
        out_specs=[blk, blk],
        out_shape=[out, out],
        compiler_params=_cparams("parallel"),
        name="ctx_attn",
    )(sink, naq, nak, nav, swq, swk, swv)


def _dot_exact_lhs(m, x):
    out = _dot(m, jnp.concatenate(_split_bf16(x, 3), axis=1))
    n = x.shape[1]
    return out[:, :n] + out[:, n:2 * n] + out[:, 2 * n:]


def _hgrn_sum_table(n_rows, reverse):
    tri = np.tril(np.ones((n_rows, n_rows), np.float32))
    if reverse:
        tri = tri.T
    t = np.arange(n_rows)
    groups = [tri] + [tri[t // size * size + (size // 2 if reverse else size // 2 - 1)] for size in HG_SMALL_LEVELS]
    return np.concatenate(groups, axis=0)


def _hgrn_block(q, k, g, v, st, sums, ones, reverse):
    n_rows = q.shape[0]
    all_sums = _dot_exact_lhs(sums, g)
    cum = all_sums[:n_rows]
    tot = cum[0:1] if reverse else cum[n_rows - 1:n_rows]
    o_inter = _dot_nt((q * jnp.exp2(cum)).astype(BF16), st.astype(BF16))
    k_end = (k * jnp.exp2(tot - cum)).astype(BF16)
    st_new = st * jnp.exp2(tot) + _dot_tn(v.astype(BF16), k_end)

    row = lax.broadcasted_iota(jnp.int32, (n_rows, LANES), 0)
    row_a = lax.broadcasted_iota(jnp.int32, (n_rows, n_rows), 0)
    col_a = lax.broadcasted_iota(jnp.int32, (n_rows, n_rows), 1)
    a = jnp.where(row_a == col_a, _dot((q * k).astype(BF16), ones), 0.0)
    zero_tile = jnp.zeros((SUBLANES, LANES), F32)
    size = 2
    while size <= n_rows:
        half = size // 2
        if size in HG_SMALL_LEVELS:
            level = HG_SMALL_LEVELS.index(size) + 1
            ref = all_sums[level * n_rows:(level + 1) * n_rows]
            later = (row % size < half) if reverse else (row % size >= half)
            q_t = (q * jnp.exp2(jnp.where(later, cum - ref, NEG_INF))).astype(BF16)
            k_t = (k * jnp.exp2(jnp.where(later, NEG_INF, ref - cum))).astype(BF16)
        else:
            q_tiles, k_tiles = [], []
            for r0 in range(0, n_rows, SUBLANES):
                sl = slice(r0, r0 + SUBLANES)
                first = r0 // size * size
                ref_row = first + (half if reverse else half - 1)
                ref = cum[ref_row:ref_row + 1]
                if (r0 - first < half) if reverse else (r0 - first >= half):
                    q_tiles.append(q[sl] * jnp.exp2(cum[sl] - ref))
                    k_tiles.append(zero_tile)
                else:
                    q_tiles.append(zero_tile)
                    k_tiles.append(k[sl] * jnp.exp2(ref - cum[sl]))
            q_t = jnp.concatenate(q_tiles, axis=0).astype(BF16)
            k_t = jnp.concatenate(k_tiles, axis=0).astype(BF16)
        a_l = _dot_nt(q_t, k_t)
        a = a + (a_l if size == n_rows else jnp.where(row_a // size == col_a // size, a_l, 0.0))
        size *= 2
    return o_inter + _dot(a.astype(BF16), v.astype(BF16)), st_new


def _hgrn_gates(zq, zf, lb):
    f = lb + (1.0 - lb) * _sigmoid(zf)
    return _silu(zq) * (HG_DK ** -0.5), 1.0 - f, jnp.log2(f)


def _hgrn_kernel(zq_f, zf_f, zi_f, zq_b, zf_b, zi_b, lb_ref, trif_ref, trib_ref, ones_ref,
                 of_ref, ob_ref, st_f, st_b):
    @pl.when(pl.program_id(2) == 0)
    def _():
        st_f[...] = jnp.zeros_like(st_f)
        st_b[...] = jnp.zeros_like(st_b)

    ones = ones_ref[...]
    for h in range(HG_HEADS_PER_STEP):
        sl = slice(h * HG_DK, (h + 1) * HG_DK)
        lb = lb_ref[:, sl]
        q, k, g = _hgrn_gates(zq_f[:, sl], zf_f[:, sl], lb)
        of_ref[:, sl], st_f[h] = _hgrn_block(q, k, g, zi_f[:, sl], st_f[h], trif_ref[...], ones, False)
        q, k, g = _hgrn_gates(zq_b[:, sl], zf_b[:, sl], lb)
        ob_ref[:, sl], st_b[h] = _hgrn_block(q, k, g, zi_b[:, sl], st_b[h], trib_ref[...], ones, True)


def _hgrn(hg, lb, tri_f, tri_b, ones, n_batch, seq, ctx_len):
    tr = HG_STEP_ROWS
    nc, nl = ctx_len // tr, seq // tr
    first = n_batch * nc

    def fwd(b, j):
        return jnp.where(j < nc, b * nc + j, first + b * nl + (j - nc))

    def bwd(b, j):
        return jnp.where(j < nc, b * nc + (nc - 1 - j), first + b * nl + (nl - 1 - (j - nc)))

    hps = HG_HEADS_PER_STEP
    groups = HG_HEADS // hps
    wide = hps * HG_DK
    col = lambda row_fn, group: pl.BlockSpec((tr, wide), lambda b, h, j: (row_fn(b, j), group * groups + h))
    const = lambda a: pl.BlockSpec(a.shape, lambda b, h, j: (0, 0))
    out = jax.ShapeDtypeStruct((hg.shape[0], HG_W), F32)
    return pl.pallas_call(
        _hgrn_kernel,
        grid=(n_batch, groups, nc + nl),
        in_specs=[col(fwd, 0), col(fwd, 1), col(fwd, 3), col(bwd, 0), col(bwd, 2), col(bwd, 3),
                  pl.BlockSpec((1, wide), lambda b, h, j: (0, h)), const(tri_f), const(tri_b), const(ones)],
        out_specs=[pl.BlockSpec((tr, wide), lambda b, h, j: (fwd(b, j), h)),
                   pl.BlockSpec((tr, wide), lambda b, h, j: (bwd(b, j), h))],
        out_shape=[out, out],
        scratch_shapes=[pltpu.VMEM((hps, HG_DK, HG_DK), F32), pltpu.VMEM((hps, HG_DK, HG_DK), F32)],
        compiler_params=_cparams("parallel", "parallel", "arbitrary"),
        name="hgrn",
    )(hg, hg, hg, hg, hg, hg, lb, tri_f, tri_b, ones)


def _mixout_kernel(x_ref, yna_ref, of_ref, ob_ref, zg_ref, ysw_ref, hgn_ref, w_ref, mod_ref, g2_ref,
                   rw_ref, rb_ref, xo_ref, h2_ref, te_ref, tg_ref):
    o = of_ref[...] + ob_ref[...]
    zg = zg_ref[...]
    parts = [yna_ref[...]]
    for h in range(HG_HEADS):
        sl = slice(h * HG_DK, (h + 1) * HG_DK)
        oh = o[:, sl]
        yh = oh * lax.rsqrt(jnp.mean(oh * oh, axis=-1, keepdims=True) + NORM_EPS) * hgn_ref[...]
        parts.append((yh * _silu(zg[:, sl])).astype(BF16))
    parts.append(ysw_ref[...])
    y = _dot(jnp.concatenate(parts, axis=1), w_ref[...])
    mod = mod_ref[...]
    x_new = x_ref[...] + mod[2:3] * y
    xo_ref[...] = x_new
    h2 = _mod_rmsnorm(x_new, g2_ref[...], mod, 3, 4)
    _store_token_tiles(h2_ref, h2)

    h_hi, h_lo = _split_bf16(h2, 2)
    logits = _dot(jnp.concatenate([h_hi, h_lo, h_hi], axis=1), rw_ref[...]) + rb_ref[...]
    lane = lax.broadcasted_iota(jnp.int32, logits.shape, 1).astype(F32)
    top_e = jnp.zeros(logits.shape, F32)
    top_v = jnp.full(logits.shape, NEG_INF, F32)
    for j in range(TOP_K):
        best = jnp.max(logits, axis=-1, keepdims=True)
        arg = jnp.min(jnp.where(logits == best, lane, float(LANES)), axis=-1, keepdims=True)
        top_e = jnp.where(lane == j, arg, top_e)
        top_v = jnp.where(lane == j, best, top_v)
        logits = jnp.where(lane == arg, -jnp.inf, logits)
    ex = jnp.exp(top_v - jnp.max(top_v, axis=-1, keepdims=True))
    te_ref[...] = top_e.astype(jnp.int32)
    tg_ref[...] = ex / jnp.sum(ex, axis=-1, keepdims=True)


def _mixout(x, yna, o_f, o_b, hg, ysw, hgn, w_out, mods, norm2_g, rw, rb, geom, stream_off):
    n_batch, n_ctx_blocks, blocks_per_batch = geom
    tm = ROW_BLOCK
    t = x.shape[0] - stream_off * tm
    row = lambda w: pl.BlockSpec((tm, w), lambda i: (i, 0))
    full = lambda w, cb: pl.BlockSpec((tm, w), lambda i: (i + stream_off, cb))
    const = lambda a: pl.BlockSpec(a.shape, lambda i: (0,) * a.ndim)
    return pl.pallas_call(
        _mixout_kernel,
        grid=(t // tm,),
        in_specs=[full(D_MODEL, 0), row(NA_W), full(HG_W, 0), full(HG_W, 0), full(HG_W, 4), row(SWA_W),
                  const(hgn), const(w_out),
                  pl.BlockSpec((None, 6, D_MODEL), lambda i: (
                      _mod_row(i + stream_off, n_ctx_blocks, blocks_per_batch, n_batch), 0, 0)),
                  const(norm2_g), const(rw), const(rb)],
        out_specs=[row(D_MODEL), pl.BlockSpec((tm * SUBLANES, LANES), lambda i: (i, 0)), row(LANES), row(LANES)],
        out_shape=[jax.ShapeDtypeStruct((t, D_MODEL), F32), jax.ShapeDtypeStruct((t * SUBLANES, LANES), F32),
                   jax.ShapeDtypeStruct((t, LANES), jnp.int32), jax.ShapeDtypeStruct((t, LANES), F32)],
        compiler_params=_cparams("parallel"),
        name="mixout",
    )(x, yna, o_f, o_b, hg, ysw, hgn, w_out, mods, norm2_g, rw, rb)


def _gather_copy(src_hbm, idx_ref, buf, sem, slot, n_tokens):
    def start():
        for r in range(n_tokens):
            src_row = pl.multiple_of(idx_ref[0, 0, r], SUBLANES)
            pltpu.make_async_copy(src_hbm.at[pl.ds(src_row, SUBLANES), :],
                                  buf.at[slot, pl.ds(r * SUBLANES, SUBLANES), :], sem.at[slot]).start()

    def wait():
        pltpu.make_async_copy(src_hbm.at[pl.ds(0, n_tokens * SUBLANES), :], buf.at[slot], sem.at[slot]).wait()

    return start, wait


def _dispatch_kernel(pad_ref, nu_ref, idx_ref, h_ref, xs_hbm, zeros, sem, zero_sem):
    n_tok = h_ref.shape[0] // SUBLANES
    block_rows = zeros.shape[0]
    n_blocks = xs_hbm.shape[0] // block_rows

    def zero_slot(e, r):
        row = pl.multiple_of((pad_ref[0, e] + r) * SUBLANES, SUBLANES)
        return pltpu.make_async_copy(zeros.at[pl.ds(0, SUBLANES), :], xs_hbm.at[pl.ds(row, SUBLANES), :], zero_sem)

    def zero_block(b):
        row = pl.multiple_of(b * block_rows, block_rows)
        return pltpu.make_async_copy(zeros, xs_hbm.at[pl.ds(row, block_rows), :], zero_sem)

    def for_each_unused(slot_fn, block_fn):
        def per_expert(e, carry):
            lax.fori_loop(0, pad_ref[1, e], lambda r, c: (slot_fn(e, r), c)[1], 0)
            return carry

        lax.fori_loop(0, pad_ref.shape[1], per_expert, 0)
        lax.fori_loop(nu_ref[0], n_blocks, lambda b, c: (block_fn(b), c)[1], 0)

    @pl.when(pl.program_id(0) == 0)
    def _():
        zeros[...] = jnp.zeros_like(zeros)
        for_each_unused(lambda e, r: zero_slot(e, r).start(), lambda b: zero_block(b).start())
        for_each_unused(lambda e, r: zero_slot(e, r).wait(), lambda b: zero_block(b).wait())

    for t in range(n_tok):
        for j in range(TOP_K):
            dst = pl.multiple_of(idx_ref[0, 0, t * TOP_K + j], SUBLANES)
            pltpu.make_async_copy(h_ref.at[pl.ds(t * SUBLANES, SUBLANES), :],
                                  xs_hbm.at[pl.ds(dst, SUBLANES), :], sem).start()
    for j in range(TOP_K):
        pltpu.make_async_copy(h_ref, xs_hbm.at[pl.ds(0, n_tok * SUBLANES), :], sem).wait()


def _dispatch(h_tiles, dest_row, pad_slots, n_used, n_slots):
    tm = ROW_BLOCK
    nb = dest_row.shape[0] // tm
    assert n_slots % MOE_ROWS == 0
    return pl.pallas_call(
        _dispatch_kernel,
        grid=(nb,),
        in_specs=[pl.BlockSpec(memory_space=pltpu.SMEM), pl.BlockSpec(memory_space=pltpu.SMEM),
                  pl.BlockSpec((1, 1, TOP_K * tm), lambda i: (i, 0, 0), memory_space=pltpu.SMEM),
                  pl.BlockSpec((tm * SUBLANES, LANES), lambda i: (i, 0))],
        out_specs=pl.BlockSpec(memory_space=pl.ANY),
        out_shape=jax.ShapeDtypeStruct((n_slots * SUBLANES, LANES), F32),
        scratch_shapes=[pltpu.VMEM((MOE_ROWS * SUBLANES, LANES), F32), pltpu.SemaphoreType.DMA(()),
                        pltpu.SemaphoreType.DMA(())],
        compiler_params=_cparams("arbitrary"),
        name="moe_dispatch",
    )(pad_slots, n_used, dest_row.reshape(nb, 1, TOP_K * tm), h_tiles)


def _moe_kernel(be_ref, nu_ref, xs_ref, wgu_ref, bgu_ref, wd_ref, bd_ref, y_ref, wgu_bf, wd_bf):
    i = pl.program_id(0)
    live = i < nu_ref[0]
    tm = y_ref.shape[0] // SUBLANES
    changed = (i == 0) | (be_ref[i] != be_ref[jnp.maximum(i - 1, 0)])

    @pl.when(live & changed)
    def _():
        wgu_bf[...] = wgu_ref[...].astype(BF16)
        wd_bf[...] = wd_ref[...].astype(BF16)

    @pl.when(live)
    def _():
        x = _load_token_tiles(xs_ref, 0, tm)
        gu = _dot(x.astype(BF16), wgu_bf[...]) + bgu_ref[...]
        glu = jnp.minimum(gu[:, :D_FF], SWIGLU_LIMIT)
        lin = jnp.clip(gu[:, D_FF:], -SWIGLU_LIMIT, SWIGLU_LIMIT)
        act = glu * _sigmoid(SWIGLU_ALPHA * glu) * (lin + 1.0)
        _store_token_tiles(y_ref, _dot(act.astype(BF16), wd_bf[...]) + bd_ref[...])

    @pl.when(jnp.logical_not(live))
    def _():
        y_ref[...] = jnp.zeros_like(y_ref)


def _moe_ffn(xs_tiles, block_e, n_used, w_gu, b_gu, w_down, b_down, layer):
    tm = MOE_ROWS
    nb = xs_tiles.shape[0] // (tm * SUBLANES)
    _, n_e, d, f2 = w_gu.shape
    weight = lambda r, c: pl.BlockSpec((None, None, r, c), lambda i, be, nu: (layer, be[i], 0, 0))
    grid_spec = pltpu.PrefetchScalarGridSpec(
        num_scalar_prefetch=2,
        grid=(nb,),
        in_specs=[pl.BlockSpec((tm * SUBLANES, LANES), lambda i, be, nu: (jnp.minimum(i, nu[0] - 1), 0)),
                  weight(d, f2), weight(1, f2), weight(f2 // 2, d), weight(1, d)],
        out_specs=pl.BlockSpec((tm * SUBLANES, LANES), lambda i, be, nu: (i, 0)),
        scratch_shapes=[pltpu.VMEM((d, f2), BF16), pltpu.VMEM((f2 // 2, d), BF16)],
    )
    n_l = w_gu.shape[0]
    return pl.pallas_call(
        _moe_kernel,
        grid_spec=grid_spec,
        out_shape=jax.ShapeDtypeStruct((nb * tm * SUBLANES, LANES), F32),
        compiler_params=_cparams("arbitrary"),
        name="moe_ffn",
    )(block_e, n_used, xs_tiles, w_gu, b_gu.reshape(n_l, n_e, 1, f2), w_down,
      b_down.reshape(n_l, n_e, 1, d))


def _combine_kernel(idx_ref, x_ref, y_hbm, tg_ref, mod_ref, o_ref, ybuf, sem):
    i = pl.program_id(0)
    tm = o_ref.shape[0]
    n_rows = TOP_K * tm

    @pl.when(i < pl.num_programs(0) - 1)
    def _():
        _gather_copy(y_hbm, idx_ref, ybuf, sem, i % 2, n_rows)[0]()

    @pl.when(i > 0)
    def _():
        slot = (i - 1) % 2
        _gather_copy(y_hbm, idx_ref, ybuf, sem, slot, n_rows)[1]()
        gate = tg_ref[...]
        y = [_load_token_tiles(ybuf.at[slot], j * tm, tm) * gate[:, j:j + 1] for j in range(TOP_K)]
        o_ref[...] = x_ref[...] + mod_ref[...][5:6] * ((y[0] + y[1]) + (y[2] + y[3]))


def _combine(x, y_tiles, dest_row, gates, mods, geom, stream_off):
    n_batch, n_ctx_blocks, blocks_per_batch = geom
    t = x.shape[0]
    tm = ROW_BLOCK
    nb = t // tm
    idx = dest_row.reshape(nb, tm, TOP_K).transpose(0, 2, 1).reshape(nb, 1, TOP_K * tm)
    prev = lambda i: jnp.maximum(i - 1, 0)
    row = lambda w: pl.BlockSpec((tm, w), lambda i: (prev(i), 0))
    return pl.pallas_call(
        _combine_kernel,
        grid=(nb + 1,),
        in_specs=[pl.BlockSpec((1, 1, TOP_K * tm), lambda i: (jnp.minimum(i, nb - 1), 0, 0),
                               memory_space=pltpu.SMEM),
                  row(D_MODEL), pl.BlockSpec(memory_space=pl.ANY), row(LANES),
                  pl.BlockSpec((None, 6, D_MODEL), lambda i: (
                      _mod_row(prev(i) + stream_off, n_ctx_blocks, blocks_per_batch, n_batch), 0, 0))],
        out_specs=row(D_MODEL),
        out_shape=jax.ShapeDtypeStruct((t, D_MODEL), F32),
        scratch_shapes=[pltpu.VMEM((2, TOP_K * tm * SUBLANES, LANES), F32), pltpu.SemaphoreType.DMA((2,))],
        compiler_params=_cparams("arbitrary"),
        name="moe_combine",
    )(idx, x, y_tiles, gates, mods)


def _route(top_e, n_experts, tm):
    t, k = top_e.shape
    a = t * k
    flat_e = top_e.reshape(-1)
    onehot = (flat_e[:, None] == jnp.arange(n_experts, dtype=jnp.int32)[None, :]).astype(jnp.int32)
    counts = onehot.sum(axis=0)
    rank = (jnp.cumsum(onehot, axis=0) * onehot).sum(axis=1) - 1
    padded = (counts + tm - 1) // tm * tm
    pend = jnp.cumsum(padded)
    start = pend - padded
    dest = ((onehot * start[None, :]).sum(axis=1) + rank).astype(jnp.int32)
    n_blocks = (a + n_experts * (tm - 1) + tm - 1) // tm
    block_start = jnp.arange(n_blocks, dtype=jnp.int32) * tm
    block_e = jnp.minimum((block_start[:, None] >= pend[None, :]).sum(axis=1), n_experts - 1).astype(jnp.int32)
    n_used = (pend[-1] // tm).astype(jnp.int32).reshape(1)
    pad_slots = jnp.stack([start + counts, padded - counts], axis=0).astype(jnp.int32)
    return dest.reshape(t, k), block_e, n_used, pad_slots, n_blocks * tm


def _na_row_pattern(rb, n_rows):
    wr = min(NA_WIN_H, n_rows)
    n_rb = n_rows // NA_QROWS
    q_row = rb * NA_QROWS + np.arange(NA_QROWS)[:, None]
    k_row = np.clip(rb - 1, 0, n_rb - NA_KROWS // NA_QROWS) * NA_QROWS + np.arange(NA_KROWS)[None, :]
    row_start = np.clip(q_row - wr // 2, 0, n_rows - wr)
    ok = (k_row >= row_start) & (k_row < row_start + wr)
    return np.where(ok, k_row - q_row + (NA_WIN_H - 1), -1)


def _na_bias_table(rpb, seq):
    n_rows = seq // GRID_W
    n_rb = n_rows // NA_QROWS
    patterns = [_na_row_pattern(rb, n_rows) for rb in range(n_rb)]
    assert all((p == patterns[1]).all() for p in patterns[1:-1])
    q_col = np.arange(GRID_W)[:, None]
    k_col = np.arange(GRID_W)[None, :]
    col_start = np.clip(q_col - NA_WIN_W // 2, 0, GRID_W - NA_WIN_W)
    col_ok = (k_col >= col_start) & (k_col < col_start + NA_WIN_W)
    dc = np.clip(k_col - q_col + (NA_WIN_W - 1), 0, 2 * NA_WIN_W - 2)
    onehot = ((dc[None] == np.arange(2 * NA_WIN_W - 1)[:, None, None]) & col_ok[None]).astype(np.float32)
    by_col = jnp.einsum('hab,bqk->haqk', rpb.astype(F32), jnp.asarray(onehot), precision=HIGHEST)
    by_col = by_col + jnp.asarray(np.where(col_ok, 0.0, NEG_INF).astype(np.float32))
    masked = jnp.full((rpb.shape[0], GRID_W, GRID_W), NEG_INF, F32)
    variants = []
    for pattern in (patterns[0], patterns[1], patterns[-1]):
        rows = [jnp.concatenate([by_col[:, a] if a >= 0 else masked for a in pattern[qr]], axis=2)
                for qr in range(NA_QROWS)]
        variants.append(jnp.concatenate(rows, axis=1))
    return jnp.stack(variants, axis=0)


def _rope_tables(seq, ctx_len):
    quarter = HEAD_DIM // 4
    lane = np.arange(SWA_W)
    inv = ROPE_BASE ** (-(lane % quarter).astype(np.float64) / quarter)
    t = np.arange(seq)
    pos = np.where((lane % HEAD_DIM < HEAD_DIM // 2)[None, :], (t // GRID_W)[:, None], (t % GRID_W)[:, None])
    ang = jnp.asarray(pos, F32) * jnp.asarray(inv, F32)[None, :]
    sign = np.where(lane % (2 * quarter) < quarter, -1.0, 1.0).astype(np.float32)
    cos = jnp.concatenate([jnp.ones((ctx_len, SWA_W), F32), jnp.cos(ang)], axis=0)
    sin = jnp.concatenate([jnp.zeros((ctx_len, SWA_W), F32), jnp.sin(ang) * sign[None, :]], axis=0)
    return cos, sin


def kernel(x, c, ctx, c_ctx, hg_lower_bounds, ada_w, ada_b, norm1_g, norm2_g, w_in, na_q_norm, na_k_norm, na_rpb,
           hg_norm_g, swa_q_norm, swa_k_norm, swa_sink, w_out, router_w, router_b, w_gu, b_gu, w_down, b_down):
    n_batch, seq, d = x.shape
    ctx_len = ctx.shape[1]
    depth = ada_w.shape[0]
    assert d == D_MODEL and seq % ROW_BLOCK == 0 and ctx_len == ROW_BLOCK
    n_ctx_rows = n_batch * ctx_len
    geom = (n_batch, n_ctx_rows // ROW_BLOCK, seq // ROW_BLOCK)

    p_lb = jax.nn.softmax(hg_lower_bounds.astype(F32), axis=0)
    lbs = jnp.cumsum(p_lb, axis=0) - p_lb[0]

    cos_t, sin_t = _rope_tables(seq, ctx_len)
    lane = np.arange(NA_W)
    group_ones = jnp.asarray((lane[:, None] // HEAD_DIM == lane[None, :] // HEAD_DIM).astype(np.float32), BF16)
    tri_f = jnp.asarray(_hgrn_sum_table(HG_STEP_ROWS, False), BF16)
    tri_b = jnp.asarray(_hgrn_sum_table(HG_STEP_ROWS, True), BF16)
    ones_bf = jnp.ones((LANES, LANES), BF16)
    n_mod_rows = -(-(n_batch + 1) // 8) * 8
    cc = jnp.zeros((n_mod_rows, d), F32).at[:n_batch].set(c).at[n_batch].set(c_ctx)
    dup = lambda w: jnp.concatenate([w[:, :HEAD_DIM], w[:, :HEAD_DIM], w[:, HEAD_DIM:], w[:, HEAD_DIM:]], axis=1)
    tile4 = lambda g: jnp.tile(g.astype(F32), 4).reshape(1, 4 * HEAD_DIM)
    pad_e = LANES - N_EXPERTS

    x_all = jnp.concatenate([ctx.reshape(n_ctx_rows, d), x.reshape(n_batch * seq, d)], axis=0)
    for l in range(depth):
        last = l == depth - 1
        mods = _ada_mod(cc, ada_w[l], ada_b[l]).reshape(n_mod_rows, 6, d)
        w = w_in[l]
        kv0 = 3 * NA_W + HG_COLS + SWA_W
        w_ext = jnp.concatenate([w[:, :kv0], dup(w[:, kv0:kv0 + SWA_KV_W]), dup(w[:, kv0 + SWA_KV_W:])],
                                axis=1).astype(BF16)
        hg, naq, nak, nav, swq, swk, swv = _inproj(
            x_all, mods, norm1_g[l].reshape(1, d), w_ext, tile4(na_q_norm[l]), tile4(na_k_norm[l]),
            tile4(swa_q_norm[l]), tile4(swa_k_norm[l]), cos_t, sin_t, group_ones, geom)

        y_na = _na_attention(naq, nak, nav, _na_bias_table(na_rpb[l], seq), geom)
        y_sw = _swa_attention(swq, swk, swv, swa_sink[l].astype(F32), geom, seq, ctx_len)
        o_f, o_b = _hgrn(hg, lbs[l].reshape(1, HG_W), tri_f, tri_b, ones_bf, n_batch, seq, ctx_len)

        if last:
            stream_off = geom[1]
        else:
            stream_off = 0
            yc_na, yc_sw = _ctx_attention(naq, nak, nav, swq, swk, swv, swa_sink[l].astype(F32), n_batch, ctx_len)
            y_na = jnp.concatenate([yc_na, y_na], axis=0)
            y_sw = jnp.concatenate([yc_sw, y_sw], axis=0)

        rw_hi, rw_lo = _split_bf16(jnp.pad(router_w[l].astype(F32), ((0, 0), (0, pad_e))), 2)
        rw = jnp.concatenate([rw_hi, rw_hi, rw_lo], axis=0)
        rb = jnp.pad(router_b[l].astype(F32), (0, pad_e), constant_values=NEG_INF).reshape(1, LANES)
        x_new, h2, top_e, top_g = _mixout(
            x_all, y_na, o_f, o_b, hg, y_sw, hg_norm_g[l].reshape(1, HG_DK).astype(F32), w_out[l].astype(BF16),
            mods, norm2_g[l].reshape(1, d), rw, rb, geom, stream_off)

        dest, block_e, n_used, pad_slots, n_slots = _route(top_e[:, :TOP_K], N_EXPERTS, MOE_ROWS)
        dest_row = dest * SUBLANES
        xs = _dispatch(h2, dest_row, pad_slots, n_used, n_slots)
        y_slots = _moe_ffn(xs, block_e, n_used, w_gu, b_gu, w_down, b_down, l)
        x_all = _combine(x_new, y_slots, dest_row, top_g, mods, geom, stream_off)
    return x_all.reshape(n_batch, seq, d)
```

```python
import functools

import numpy as np
import jax
import jax.numpy as jnp
from jax import lax
from jax.experimental import pallas as pl
from jax.experimental.pallas import tpu as pltpu

D_MODEL = 1024
GRID_W = 64
HEAD_DIM = 64
ATTN_SCALE = HEAD_DIM ** -0.5
NA_HEADS = 4
NA_WIN_H = 8
NA_WIN_W = 16
HG_HEADS = 4
HG_DK = 128
SWA_Q_HEADS = 4
SWA_KV_HEADS = 2
SWA_WINDOW = 128
SWA_BLOCK = 128
ROPE_BASE = 10000.0
N_EXPERTS = 32
TOP_K = 4
D_FF = 1024
SWIGLU_LIMIT = 7.0
SWIGLU_ALPHA = 1.702
NORM_EPS = 1e-6
NEG_INF = -1e30

NA_W = NA_HEADS * HEAD_DIM
HG_W = HG_HEADS * HG_DK
SWA_W = SWA_Q_HEADS * HEAD_DIM
SWA_KV_W = SWA_KV_HEADS * HEAD_DIM
HG_COLS = 5 * HG_W
IN_COLS_EXT = 3 * NA_W + HG_COLS + 3 * SWA_W

LANES = 128
SUBLANES = 8
ROW_BLOCK = 256
NA_QROWS = 4
NA_KROWS = 12
HG_SMALL_LEVELS = (2, 4, 8)
HG_STEP_ROWS = 128
HG_HEADS_PER_STEP = 4
MOE_ROWS = 256
RUN_BITS = ROW_BLOCK.bit_length()
VMEM_LIMIT = 56 * 1024 * 1024

F32 = jnp.float32
BF16 = jnp.bfloat16
HIGHEST = lax.Precision.HIGHEST


def _cparams(*sem):
    return pltpu.CompilerParams(dimension_semantics=sem, vmem_limit_bytes=VMEM_LIMIT)


def _dot(a, b, precision=None):
    return jnp.dot(a, b, preferred_element_type=F32, precision=precision)


def _dot_nt(a, b):
    return lax.dot_general(a, b, (((1,), (1,)), ((), ())), preferred_element_type=F32)


def _dot_tn(a, b):
    return lax.dot_general(a, b, (((0,), (0,)), ((), ())), preferred_element_type=F32)


def _store_token_tiles(ref, x):
    n = x.shape[0]
    for c in range(SUBLANES):
        ref[pl.ds(c, n, stride=SUBLANES), :] = x[:, c * LANES:(c + 1) * LANES]


def _load_token_tiles(ref, first_token, n):
    return jnp.concatenate([ref[pl.ds(first_token * SUBLANES + c, n, stride=SUBLANES), :] for c in range(SUBLANES)],
                           axis=1)


def _sigmoid(x):
    return 1.0 / (1.0 + jnp.exp(-x))


def _silu(x):
    return x * _sigmoid(x)


def _ada_kernel(c_ref, w_ref, b_ref, o_ref):
    o_ref[...] = _dot(_silu(c_ref[...]), w_ref[...], HIGHEST) + b_ref[...]


def _ada_mod(cc, w, b):
    m, d = cc.shape
    n = w.shape[1]
    tn = 1024
    return pl.pallas_call(
        _ada_kernel,
        grid=(n // tn,),
        in_specs=[pl.BlockSpec((m, d), lambda j: (0, 0)),
                  pl.BlockSpec((d, tn), lambda j: (0, j)),
                  pl.BlockSpec((1, tn), lambda j: (0, j))],
        out_specs=pl.BlockSpec((m, tn), lambda j: (0, j)),
        out_shape=jax.ShapeDtypeStruct((m, n), F32),
        compiler_params=_cparams("parallel"),
        name="ada_mod",
    )(cc, w, b.reshape(1, n))


def _mod_rmsnorm(x, g, mod, shift_row, scale_row):
    y = x * lax.rsqrt(jnp.mean(x * x, axis=-1, keepdims=True) + NORM_EPS) * g
    return y * (1.0 + mod[scale_row:scale_row + 1]) + mod[shift_row:shift_row + 1]


def _split_bf16(x, terms):
    parts = []
    for _ in range(terms - 1):
        parts.append(x.astype(BF16))
        x = x - parts[-1].astype(F32)
    return parts + [x.astype(BF16)]


def _head_rmsnorm(x, w, group_ones):
    hi, lo = _split_bf16(x * x, 2)
    n = x.shape[0]
    ss = _dot(jnp.concatenate([hi, lo], axis=0), group_ones)
    return x * lax.rsqrt((ss[:n] + ss[n:]) * (1.0 / HEAD_DIM) + NORM_EPS) * w


def _rope(x, cos, sin_signed):
    n = x.shape[-1]
    lane = lax.broadcasted_iota(jnp.int32, x.shape, 1)
    quarter = HEAD_DIM // 4
    partner = jnp.where(lane % (2 * quarter) < quarter,
                        pltpu.roll(x, n - quarter, 1), pltpu.roll(x, quarter, 1))
    return x * cos + partner * sin_signed


def _inproj_kernel(x_ref, mod_ref, g_ref, w_ref, naq_w, nak_w, swq_w, swk_w, cos_ref, sin_ref, ones_ref,
                   hg_ref, naq_ref, nak_ref, nav_ref, swq_ref, swk_ref, swv_ref):
    h = _mod_rmsnorm(x_ref[...], g_ref[...], mod_ref[...], 0, 1).astype(BF16)
    ones = ones_ref[...]

    def proj(lo, width):
        return _dot(h, w_ref[:, lo:lo + width])

    naq_ref[...] = (_head_rmsnorm(proj(0, NA_W), naq_w[...], ones) * ATTN_SCALE).astype(BF16)
    nak_ref[...] = _head_rmsnorm(proj(NA_W, NA_W), nak_w[...], ones).astype(BF16)
    nav_ref[...] = proj(2 * NA_W, NA_W).astype(BF16)
    base = 3 * NA_W
    for j in range(HG_COLS // HG_W):
        hg_ref[:, j * HG_W:(j + 1) * HG_W] = proj(base + j * HG_W, HG_W)
    base += HG_COLS
    cos, sin = cos_ref[...], sin_ref[...]
    swq = _rope(_head_rmsnorm(proj(base, SWA_W), swq_w[...], ones), cos, sin)
    swq_ref[...] = (swq * ATTN_SCALE).astype(BF16)
    swk_ref[...] = _rope(_head_rmsnorm(proj(base + SWA_W, SWA_W), swk_w[...], ones), cos, sin).astype(BF16)
    swv_ref[...] = proj(base + 2 * SWA_W, SWA_W).astype(BF16)


def _mod_row(i, n_ctx_blocks, blocks_per_batch, n_batch):
    return jnp.where(i < n_ctx_blocks, n_batch, (i - n_ctx_blocks) // blocks_per_batch)


def _inproj(x_all, mods, norm_g, w_ext, naq_w, nak_w, swq_w, swk_w, cos_t, sin_t, ones, geom):
    n_batch, n_ctx_blocks, blocks_per_batch = geom
    t = x_all.shape[0]
    tm = ROW_BLOCK
    row = lambda w: pl.BlockSpec((tm, w), lambda i: (i, 0))
    const = lambda a: pl.BlockSpec(a.shape, lambda i: (0,) * a.ndim)
    rope_blk = lambda i: (jnp.where(i < n_ctx_blocks, 0, 1 + (i - n_ctx_blocks) % blocks_per_batch), 0)
    bf = lambda w: jax.ShapeDtypeStruct((t, w), BF16)
    return pl.pallas_call(
        _inproj_kernel,
        grid=(t // tm,),
        in_specs=[row(D_MODEL),
                  pl.BlockSpec((None, 6, D_MODEL),
                               lambda i: (_mod_row(i, n_ctx_blocks, blocks_per_batch, n_batch), 0, 0)),
                  const(norm_g), const(w_ext), const(naq_w), const(nak_w), const(swq_w), const(swk_w),
                  pl.BlockSpec((tm, SWA_W), rope_blk), pl.BlockSpec((tm, SWA_W), rope_blk), const(ones)],
        out_specs=[row(HG_COLS), row(NA_W), row(NA_W), row(NA_W), row(SWA_W), row(SWA_W), row(SWA_W)],
        out_shape=[jax.ShapeDtypeStruct((t, HG_COLS), F32), bf(NA_W), bf(NA_W), bf(NA_W),
                   bf(SWA_W), bf(SWA_W), bf(SWA_W)],
        compiler_params=_cparams("parallel"),
        name="inproj",
    )(x_all, mods, norm_g, w_ext, naq_w, nak_w, swq_w, swk_w, cos_t, sin_t, ones)


def _attend(q, parts, sink_vals):
    m_rows = q.shape[0]
    lane = lax.broadcasted_iota(jnp.int32, (m_rows, q.shape[1]), 1)
    out = jnp.zeros((m_rows, q.shape[1]), F32)
    for h in range(q.shape[1] // HEAD_DIM):
        in_head = (lane // HEAD_DIM) == h
        qh = jnp.where(in_head, q, jnp.zeros_like(q))
        scores = []
        for k, _, add in parts:
            s = _dot_nt(qh, k)
            if add is not None:
                s = s + (add[h] if add.ndim == 3 else add)
            scores.append(s)
        mx = functools.reduce(jnp.maximum, [jnp.max(s, axis=-1, keepdims=True) for s in scores])
        if sink_vals is not None:
            mx = jnp.maximum(mx, sink_vals[h])
        ps = [jnp.exp(s - mx) for s in scores]
        den = functools.reduce(jnp.add, [jnp.sum(p, axis=-1, keepdims=True) for p in ps])
        if sink_vals is not None:
            den = den + jnp.exp(sink_vals[h] - mx)
        acc = functools.reduce(jnp.add, [_dot(p.astype(BF16), v) for p, (_, v, _) in zip(ps, parts)])
        out = jnp.where(in_head, acc / den, out)
    return out


def _na_kernel(q_ref, k0, k1, k2, kc, v0, v1, v2, vc, bias_ref, o_ref):
    k_lat = jnp.concatenate([k0[...], k1[...], k2[...]], axis=0)
    v_lat = jnp.concatenate([v0[...], v1[...], v2[...]], axis=0)
    o = _attend(q_ref[...], [(k_lat, v_lat, bias_ref[...]), (kc[...], vc[...], None)], None)
    o_ref[...] = o.astype(o_ref.dtype)


def _na_attention(naq, nak, nav, bias, geom):
    n_batch, n_ctx_blocks, blocks_per_batch = geom
    tm = ROW_BLOCK
    n_rb = blocks_per_batch
    kb_max = n_rb - NA_KROWS // NA_QROWS
    lat = lambda rb, b: n_ctx_blocks + b * blocks_per_batch

    def band(j):
        return pl.BlockSpec((tm, NA_W), lambda rb, b: (lat(rb, b) + jnp.clip(rb - 1, 0, kb_max) + j, 0))

    ctx = pl.BlockSpec((tm, NA_W), lambda rb, b: (b, 0))
    return pl.pallas_call(
        _na_kernel,
        grid=(n_rb, n_batch),
        in_specs=[pl.BlockSpec((tm, NA_W), lambda rb, b: (lat(rb, b) + rb, 0)),
                  band(0), band(1), band(2), ctx, band(0), band(1), band(2), ctx,
                  pl.BlockSpec((None, NA_HEADS, tm, NA_KROWS * GRID_W), lambda rb, b: (
                      jnp.where(rb == 0, 0, jnp.where(rb == n_rb - 1, 2, 1)), 0, 0, 0))],
        out_specs=pl.BlockSpec((tm, NA_W), lambda rb, b: (b * blocks_per_batch + rb, 0)),
        out_shape=jax.ShapeDtypeStruct((n_batch * blocks_per_batch * tm, NA_W), BF16),
        compiler_params=_cparams("parallel", "parallel"),
        name="na_attn",
    )(naq, nak, nak, nak, nak, nav, nav, nav, nav, bias)


def _swa_kernel(sink_ref, q_ref, kp, kc_, kn, kx, vp, vc_, vn, vx, o_ref, *, n_blocks):
    n = pl.program_id(1)
    tq = q_ref.shape[0]
    k_lat = jnp.concatenate([kp[...], kc_[...], kn[...]], axis=0)
    v_lat = jnp.concatenate([vp[...], vc_[...], vn[...]], axis=0)
    qi = lax.broadcasted_iota(jnp.int32, (tq, 3 * tq), 0)
    mi = lax.broadcasted_iota(jnp.int32, (tq, 3 * tq), 1)
    kpos = (n - 1) * tq + mi
    ok = (jnp.abs(qi + tq - mi) <= SWA_WINDOW) & (kpos >= 0) & (kpos < n_blocks * tq)
    add = jnp.where(ok, 0.0, NEG_INF).astype(F32)
    sinks = [sink_ref[h] for h in range(SWA_Q_HEADS)]
    o = _attend(q_ref[...], [(k_lat, v_lat, add), (kx[...], vx[...], None)], sinks)
    o_ref[...] = o.astype(o_ref.dtype)


def _swa_attention(swq, swk, swv, sink, geom, seq, ctx_len):
    n_batch, _, _ = geom
    tq = SWA_BLOCK
    nb = seq // tq
    first = n_batch * ctx_len // tq
    blk = lambda f: pl.BlockSpec((tq, SWA_W), lambda b, n: (first + b * nb + f(n), 0))
    prev, cur, nxt = blk(lambda n: jnp.maximum(n - 1, 0)), blk(lambda n: n), blk(lambda n: jnp.minimum(n + 1, nb - 1))
    ctx = pl.BlockSpec((ctx_len, SWA_W), lambda b, n: (b, 0))
    return pl.pallas_call(
        functools.partial(_swa_kernel, n_blocks=nb),
        grid=(n_batch, nb),
        in_specs=[pl.BlockSpec(memory_space=pltpu.SMEM), cur, prev, cur, nxt, ctx, prev, cur, nxt, ctx],
        out_specs=pl.BlockSpec((tq, SWA_W), lambda b, n: (b * nb + n, 0)),
        out_shape=jax.ShapeDtypeStruct((n_batch * seq, SWA_W), BF16),
        compiler_params=_cparams("parallel", "parallel"),
        name="swa_attn",
    )(sink, swq, swk, swk, swk, swk, swv, swv, swv, swv)


def _ctx_attn_kernel(sink_ref, naq, nak, nav, swq, swk, swv, ona_ref, osw_ref):
    ona_ref[...] = _attend(naq[...], [(nak[...], nav[...], None)], None).astype(ona_ref.dtype)
    sinks = [sink_ref[h] for h in range(SWA_Q_HEADS)]
    osw_ref[...] = _attend(swq[...], [(swk[...], swv[...], None)], sinks).astype(osw_ref.dtype)


def _ctx_attention(naq, nak, nav, swq, swk, swv, sink, n_batch, ctx_len):
    blk = pl.BlockSpec((ctx_len, NA_W), lambda b: (b, 0))
    out = jax.ShapeDtypeStruct((n_batch * ctx_len, NA_W), BF16)
    return pl.pallas_call(
        _ctx_attn_kernel,
        grid=(n_batch,),
        in_specs=[pl.BlockSpec(memory_space=pltpu.SMEM)] + [blk] * 6,
        out_specs=[blk, blk],
        out_shape=[out, out],
        compiler_params=_cparams("parallel"),
        name="ctx_attn",
    )(sink, naq, nak, nav, swq, swk, swv)


def _dot_exact_lhs(m, x):
    out = _dot(m, jnp.concatenate(_split_bf16(x, 3), axis=1))
    n = x.shape[1]
    return out[:, :n] + out[:, n:2 * n] + out[:, 2 * n:]


def _hgrn_sum_table(n_rows, reverse):
    tri = np.tril(np.ones((n_rows, n_rows), np.float32))
    if reverse:
        tri = tri.T
    t = np.arange(n_rows)
    groups = [tri] + [tri[t // size * size + (size // 2 if reverse else size // 2 - 1)] for size in HG_SMALL_LEVELS]
    return np.concatenate(groups, axis=0)


def _hgrn_block(q, k, g, v, st, sums, ones, reverse):
    n_rows = q.shape[0]
    all_sums = _dot_exact_lhs(sums, g)
    cum = all_sums[:n_rows]
    tot = cum[0:1] if reverse else cum[n_rows - 1:n_rows]
    o_inter = _dot_nt((q * jnp.exp2(cum)).astype(BF16), st.astype(BF16))
    k_end = (k * jnp.exp2(tot - cum)).astype(BF16)
    st_new = st * jnp.exp2(tot) + _dot_tn(v.astype(BF16), k_end)

    row = lax.broadcasted_iota(jnp.int32, (n_rows, LANES), 0)
    row_a = lax.broadcasted_iota(jnp.int32, (n_rows, n_rows), 0)
    col_a = lax.broadcasted_iota(jnp.int32, (n_rows, n_rows), 1)
    a = jnp.where(row_a == col_a, _dot((q * k).astype(BF16), ones), 0.0)
    zero_tile = jnp.zeros((SUBLANES, LANES), F32)
    size = 2
    while size <= n_rows:
        half = size // 2
        if size in HG_SMALL_LEVELS:
            level = HG_SMALL_LEVELS.index(size) + 1
            ref = all_sums[level * n_rows:(level + 1) * n_rows]
            later = (row % size < half) if reverse else (row % size >= half)
            q_t = (q * jnp.exp2(jnp.where(later, cum - ref, NEG_INF))).astype(BF16)
            k_t = (k * jnp.exp2(jnp.where(later, NEG_INF, ref - cum))).astype(BF16)
        else:
            q_tiles, k_tiles = [], []
            for r0 in range(0, n_rows, SUBLANES):
                sl = slice(r0, r0 + SUBLANES)
                first = r0 // size * size
                ref_row = first + (half if reverse else half - 1)
                ref = cum[ref_row:ref_row + 1]
                if (r0 - first < half) if reverse else (r0 - first >= half):
                    q_tiles.append(q[sl] * jnp.exp2(cum[sl] - ref))
                    k_tiles.append(zero_tile)
                else:
                    q_tiles.append(zero_tile)
                    k_tiles.append(k[sl] * jnp.exp2(ref - cum[sl]))
            q_t = jnp.concatenate(q_tiles, axis=0).astype(BF16)
            k_t = jnp.concatenate(k_tiles, axis=0).astype(BF16)
        a_l = _dot_nt(q_t, k_t)
        a = a + (a_l if size == n_rows else jnp.where(row_a // size == col_a // size, a_l, 0.0))
        size *= 2
    return o_inter + _dot(a.astype(BF16), v.astype(BF16)), st_new


def _hgrn_gates(zq, zf, lb):
    f = lb + (1.0 - lb) * _sigmoid(zf)
    return _silu(zq) * (HG_DK ** -0.5), 1.0 - f, jnp.log2(f)


def _hgrn_kernel(zq_f, zf_f, zi_f, zq_b, zf_b, zi_b, lb_ref, trif_ref, trib_ref, ones_ref,
                 of_ref, ob_ref, st_f, st_b):
    @pl.when(pl.program_id(2) == 0)
    def _():
        st_f[...] = jnp.zeros_like(st_f)
        st_b[...] = jnp.zeros_like(st_b)

    ones = ones_ref[...]
    for h in range(HG_HEADS_PER_STEP):
        sl = slice(h * HG_DK, (h + 1) * HG_DK)
        lb = lb_ref[:, sl]
        q, k, g = _hgrn_gates(zq_f[:, sl], zf_f[:, sl], lb)
        of_ref[:, sl], st_f[h] = _hgrn_block(q, k, g, zi_f[:, sl], st_f[h], trif_ref[...], ones, False)
        q, k, g = _hgrn_gates(zq_b[:, sl], zf_b[:, sl], lb)
        ob_ref[:, sl], st_b[h] = _hgrn_block(q, k, g, zi_b[:, sl], st_b[h], trib_ref[...], ones, True)


def _hgrn(hg, lb, tri_f, tri_b, ones, n_batch, seq, ctx_len):
    tr = HG_STEP_ROWS
    nc, nl = ctx_len // tr, seq // tr
    first = n_batch * nc

    def fwd(b, j):
        return jnp.where(j < nc, b * nc + j, first + b * nl + (j - nc))

    def bwd(b, j):
        return jnp.where(j < nc, b * nc + (nc - 1 - j), first + b * nl + (nl - 1 - (j - nc)))

    hps = HG_HEADS_PER_STEP
    groups = HG_HEADS // hps
    wide = hps * HG_DK
    col = lambda row_fn, group: pl.BlockSpec((tr, wide), lambda b, h, j: (row_fn(b, j), group * groups + h))
    const = lambda a: pl.BlockSpec(a.shape, lambda b, h, j: (0, 0))
    out = jax.ShapeDtypeStruct((hg.shape[0], HG_W), F32)
    return pl.pallas_call(
        _hgrn_kernel,
        grid=(n_batch, groups, nc + nl),
        in_specs=[col(fwd, 0), col(fwd, 1), col(fwd, 3), col(bwd, 0), col(bwd, 2), col(bwd, 3),
                  pl.BlockSpec((1, wide), lambda b, h, j: (0, h)), const(tri_f), const(tri_b), const(ones)],
        out_specs=[pl.BlockSpec((tr, wide), lambda b, h, j: (fwd(b, j), h)),
                   pl.BlockSpec((tr, wide), lambda b, h, j: (bwd(b, j), h))],
        out_shape=[out, out],
        scratch_shapes=[pltpu.VMEM((hps, HG_DK, HG_DK), F32), pltpu.VMEM((hps, HG_DK, HG_DK), F32)],
        compiler_params=_cparams("parallel", "parallel", "arbitrary"),
        name="hgrn",
    )(hg, hg, hg, hg, hg, hg, lb, tri_f, tri_b, ones)


def _mixout_kernel(x_ref, yna_ref, of_ref, ob_ref, zg_ref, ysw_ref, hgn_ref, w_ref, mod_ref, g2_ref,
                   rw_ref, rb_ref, xo_ref, h2_ref, te_ref, tg_ref):
    o = of_ref[...] + ob_ref[...]
    zg = zg_ref[...]
    parts = [yna_ref[...]]
    for h in range(HG_HEADS):
        sl = slice(h * HG_DK, (h + 1) * HG_DK)
        oh = o[:, sl]
        yh = oh * lax.rsqrt(jnp.mean(oh * oh, axis=-1, keepdims=True) + NORM_EPS) * hgn_ref[...]
        parts.append((yh * _silu(zg[:, sl])).astype(BF16))
    parts.append(ysw_ref[...])
    y = _dot(jnp.concatenate(parts, axis=1), w_ref[...])
    mod = mod_ref[...]
    x_new = x_ref[...] + mod[2:3] * y
    xo_ref[...] = x_new
    h2 = _mod_rmsnorm(x_new, g2_ref[...], mod, 3, 4)
    h2_ref[...] = h2.astype(BF16)

    h_hi, h_lo = _split_bf16(h2, 2)
    logits = _dot(jnp.concatenate([h_hi, h_lo, h_hi], axis=1), rw_ref[...]) + rb_ref[...]
    lane = lax.broadcasted_iota(jnp.int32, logits.shape, 1).astype(F32)
    top_e = jnp.zeros(logits.shape, F32)
    top_v = jnp.full(logits.shape, NEG_INF, F32)
    for j in range(TOP_K):
        best = jnp.max(logits, axis=-1, keepdims=True)
        arg = jnp.min(jnp.where(logits == best, lane, float(LANES)), axis=-1, keepdims=True)
        top_e = jnp.where(lane == j, arg, top_e)
        top_v = jnp.where(lane == j, best, top_v)
        logits = jnp.where(lane == arg, -jnp.inf, logits)
    ex = jnp.exp(top_v - jnp.max(top_v, axis=-1, keepdims=True))
    te_ref[...] = top_e.astype(jnp.int32)
    tg_ref[...] = ex / jnp.sum(ex, axis=-1, keepdims=True)


def _mixout(x, yna, o_f, o_b, hg, ysw, hgn, w_out, mods, norm2_g, rw, rb, geom, stream_off):
    n_batch, n_ctx_blocks, blocks_per_batch = geom
    tm = ROW_BLOCK
    t = x.shape[0] - stream_off * tm
    row = lambda w: pl.BlockSpec((tm, w), lambda i: (i, 0))
    full = lambda w, cb: pl.BlockSpec((tm, w), lambda i: (i + stream_off, cb))
    const = lambda a: pl.BlockSpec(a.shape, lambda i: (0,) * a.ndim)
    return pl.pallas_call(
        _mixout_kernel,
        grid=(t // tm,),
        in_specs=[full(D_MODEL, 0), row(NA_W), full(HG_W, 0), full(HG_W, 0), full(HG_W, 4), row(SWA_W),
                  const(hgn), const(w_out),
                  pl.BlockSpec((None, 6, D_MODEL), lambda i: (
                      _mod_row(i + stream_off, n_ctx_blocks, blocks_per_batch, n_batch), 0, 0)),
                  const(norm2_g), const(rw), const(rb)],
        out_specs=[row(D_MODEL), row(D_MODEL), row(LANES), row(LANES)],
        out_shape=[jax.ShapeDtypeStruct((t, D_MODEL), F32), jax.ShapeDtypeStruct((t, D_MODEL), BF16),
                   jax.ShapeDtypeStruct((t, LANES), jnp.int32), jax.ShapeDtypeStruct((t, LANES), F32)],
        compiler_params=_cparams("parallel"),
        name="mixout",
    )(x, yna, o_f, o_b, hg, ysw, hgn, w_out, mods, norm2_g, rw, rb)


def _start_run_copies(run_ref, n_experts, make_copy):
    for e in range(n_experts):
        n = run_ref[0, 0, e]
        local0 = run_ref[0, 0, n_experts + e]
        slot0 = run_ref[0, 0, 2 * n_experts + e]
        for bit in reversed(range(RUN_BITS)):
            @pl.when((n & (1 << bit)) != 0)
            def _():
                done = (n >> (bit + 1)) << (bit + 1)
                make_copy(local0 + done, slot0 + done, 1 << bit).start()


def _tile_rows(first_token, n_tokens):
    return pl.ds(pl.multiple_of(first_token * SUBLANES, SUBLANES), n_tokens * SUBLANES)


def _local_positions(te_ref, off_ref, stri_ref):
    e_tok = te_ref[...]
    lane = lax.broadcasted_iota(jnp.int32, e_tok.shape, 1)
    member = [lane == e_tok[:, j:j + 1] for j in range(TOP_K)]
    count = functools.reduce(jnp.add, [jnp.where(m, 1.0, 0.0) for m in member])
    before = _dot(stri_ref[...], count.astype(BF16)) + off_ref[...]
    return [jnp.sum(jnp.where(m, before, 0.0), axis=1, keepdims=True) for m in member]


def _dispatch_kernel(pad_ref, nu_ref, run_ref, te_ref, off_ref, h_ref, stri_ref, xs_hbm, pos_ref,
                     zeros, xloc, sem, zero_sem):
    n_tok = h_ref.shape[0]
    block_rows = zeros.shape[0]
    n_blocks = xs_hbm.shape[0] // block_rows

    def zero_slot(e, r):
        row = pl.multiple_of((pad_ref[0, e] + r) * SUBLANES, SUBLANES)
        return pltpu.make_async_copy(zeros.at[pl.ds(0, SUBLANES), :], xs_hbm.at[pl.ds(row, SUBLANES), :], zero_sem)

    def zero_block(b):
        row = pl.multiple_of(b * block_rows, block_rows)
        return pltpu.make_async_copy(zeros, xs_hbm.at[pl.ds(row, block_rows), :], zero_sem)

    def for_each_unused(slot_fn, block_fn):
        def per_expert(e, carry):
            lax.fori_loop(0, pad_ref[1, e], lambda r, c: (slot_fn(e, r), c)[1], 0)
            return carry

        lax.fori_loop(0, pad_ref.shape[1], per_expert, 0)
        lax.fori_loop(nu_ref[0], n_blocks, lambda b, c: (block_fn(b), c)[1], 0)

    @pl.when(pl.program_id(0) == 0)
    def _():
        zeros[...] = jnp.zeros_like(zeros)
        for_each_unused(lambda e, r: zero_slot(e, r).start(), lambda b: zero_block(b).start())
        for_each_unused(lambda e, r: zero_slot(e, r).wait(), lambda b: zero_block(b).wait())

    pos = _local_positions(te_ref, off_ref, stri_ref)
    lane = lax.broadcasted_iota(jnp.int32, (n_tok, LANES), 1)
    pos_ref[...] = functools.reduce(jnp.add, [jnp.where(lane == j, p, 0.0) for j, p in enumerate(pos)])
    col = lax.broadcasted_iota(jnp.int32, (n_tok, TOP_K * n_tok), 1).astype(F32)
    chosen = functools.reduce(jnp.logical_or, [col == p for p in pos])
    sorted_rows = _dot_tn(jnp.where(chosen, 1.0, 0.0).astype(BF16), h_ref[...])
    _store_token_tiles(xloc, sorted_rows)
    _start_run_copies(run_ref, pad_ref.shape[1], lambda local, slot, size: pltpu.make_async_copy(
        xloc.at[_tile_rows(local, size), :], xs_hbm.at[_tile_rows(slot, size), :], sem))
    pltpu.make_async_copy(xloc, xs_hbm.at[pl.ds(0, xloc.shape[0]), :], sem).wait()


def _dispatch(h, top_e, runs, run_first, pad_slots, n_used, n_slots):
    tm = ROW_BLOCK
    t = h.shape[0]
    nb = t // tm
    assert n_slots % MOE_ROWS == 0
    stri = jnp.asarray(np.tril(np.ones((tm, tm), np.float32), -1), BF16)
    row = lambda w: pl.BlockSpec((tm, w), lambda i: (i, 0))
    return pl.pallas_call(
        _dispatch_kernel,
        grid=(nb,),
        in_specs=[pl.BlockSpec(memory_space=pltpu.SMEM), pl.BlockSpec(memory_space=pltpu.SMEM),
                  pl.BlockSpec((1, 1, runs.shape[2]), lambda i: (i, 0, 0), memory_space=pltpu.SMEM),
                  row(LANES), pl.BlockSpec((None, 1, LANES), lambda i: (i, 0, 0)), row(D_MODEL),
                  pl.BlockSpec((tm, tm), lambda i: (0, 0))],
        out_specs=[pl.BlockSpec(memory_space=pl.ANY), row(LANES)],
        out_shape=[jax.ShapeDtypeStruct((n_slots * SUBLANES, LANES), F32), jax.ShapeDtypeStruct((t, LANES), F32)],
        scratch_shapes=[pltpu.VMEM((MOE_ROWS * SUBLANES, LANES), F32),
                        pltpu.VMEM((TOP_K * tm * SUBLANES, LANES), F32), pltpu.SemaphoreType.DMA(()),
                        pltpu.SemaphoreType.DMA(())],
        compiler_params=_cparams("arbitrary"),
        name="moe_dispatch",
    )(pad_slots, n_used, runs, top_e, run_first, h, stri)


def _moe_kernel(be_ref, nu_ref, xs_ref, wgu_ref, bgu_ref, wd_ref, bd_ref, y_ref, wgu_bf, wd_bf):
    i = pl.program_id(0)
    live = i < nu_ref[0]
    tm = y_ref.shape[0] // SUBLANES
    changed = (i == 0) | (be_ref[i] != be_ref[jnp.maximum(i - 1, 0)])

    @pl.when(live & changed)
    def _():
        wgu_bf[...] = wgu_ref[...].astype(BF16)
        wd_bf[...] = wd_ref[...].astype(BF16)

    @pl.when(live)
    def _():
        x = _load_token_tiles(xs_ref, 0, tm)
        gu = _dot(x.astype(BF16), wgu_bf[...]) + bgu_ref[...]
        glu = jnp.minimum(gu[:, :D_FF], SWIGLU_LIMIT)
        lin = jnp.clip(gu[:, D_FF:], -SWIGLU_LIMIT, SWIGLU_LIMIT)
        act = glu * _sigmoid(SWIGLU_ALPHA * glu) * (lin + 1.0)
        _store_token_tiles(y_ref, _dot(act.astype(BF16), wd_bf[...]) + bd_ref[...])

    @pl.when(jnp.logical_not(live))
    def _():
        y_ref[...] = jnp.zeros_like(y_ref)


def _moe_ffn(xs_tiles, block_e, n_used, w_gu, b_gu, w_down, b_down, layer):
    tm = MOE_ROWS
    nb = xs_tiles.shape[0] // (tm * SUBLANES)
    _, n_e, d, f2 = w_gu.shape
    weight = lambda r, c: pl.BlockSpec((None, None, r, c), lambda i, be, nu: (layer, be[i], 0, 0))
    grid_spec = pltpu.PrefetchScalarGridSpec(
        num_scalar_prefetch=2,
        grid=(nb,),
        in_specs=[pl.BlockSpec((tm * SUBLANES, LANES), lambda i, be, nu: (jnp.minimum(i, nu[0] - 1), 0)),
                  weight(d, f2), weight(1, f2), weight(f2 // 2, d), weight(1, d)],
        out_specs=pl.BlockSpec((tm * SUBLANES, LANES), lambda i, be, nu: (i, 0)),
        scratch_shapes=[pltpu.VMEM((d, f2), BF16), pltpu.VMEM((f2 // 2, d), BF16)],
    )
    n_l = w_gu.shape[0]
    return pl.pallas_call(
        _moe_kernel,
        grid_spec=grid_spec,
        out_shape=jax.ShapeDtypeStruct((nb * tm * SUBLANES, LANES), F32),
        compiler_params=_cparams("arbitrary"),
        name="moe_ffn",
    )(block_e, n_used, xs_tiles, w_gu, b_gu.reshape(n_l, n_e, 1, f2), w_down,
      b_down.reshape(n_l, n_e, 1, d))


def _combine_kernel(run_ref, pos_ref, tg_ref, x_ref, y_hbm, mod_ref, o_ref, ybuf, sem):
    i = pl.program_id(0)
    tm = o_ref.shape[0]
    n_rows = TOP_K * tm

    @pl.when(i < pl.num_programs(0) - 1)
    def _():
        _start_run_copies(run_ref, run_ref.shape[2] // 3, lambda local, slot, size: pltpu.make_async_copy(
            y_hbm.at[_tile_rows(slot, size), :], ybuf.at[i % 2, _tile_rows(local, size), :], sem.at[i % 2]))

    @pl.when(i > 0)
    def _():
        slot = (i - 1) % 2
        pltpu.make_async_copy(y_hbm.at[pl.ds(0, n_rows * SUBLANES), :], ybuf.at[slot], sem.at[slot]).wait()
        y = _load_token_tiles(ybuf.at[slot], 0, n_rows).astype(BF16)
        pos, gate = pos_ref[...], tg_ref[...]
        col = lax.broadcasted_iota(jnp.int32, (tm, n_rows), 1).astype(F32)
        g = functools.reduce(jnp.add, [jnp.where(col == pos[:, j:j + 1], gate[:, j:j + 1], 0.0)
                                       for j in range(TOP_K)])
        g_hi, g_lo = _split_bf16(g, 2)
        o_ref[...] = x_ref[...] + mod_ref[...][5:6] * (_dot(g_hi, y) + _dot(g_lo, y))


def _combine(x, y_tiles, runs, pos, gates, mods, geom, stream_off):
    n_batch, n_ctx_blocks, blocks_per_batch = geom
    t = x.shape[0]
    tm = ROW_BLOCK
    nb = t // tm
    prev = lambda i: jnp.maximum(i - 1, 0)
    row = lambda w: pl.BlockSpec((tm, w), lambda i: (prev(i), 0))
    return pl.pallas_call(
        _combine_kernel,
        grid=(nb + 1,),
        in_specs=[pl.BlockSpec((1, 1, runs.shape[2]), lambda i: (jnp.minimum(i, nb - 1), 0, 0),
                               memory_space=pltpu.SMEM),
                  row(LANES), row(LANES), row(D_MODEL), pl.BlockSpec(memory_space=pl.ANY),
                  pl.BlockSpec((None, 6, D_MODEL), lambda i: (
                      _mod_row(prev(i) + stream_off, n_ctx_blocks, blocks_per_batch, n_batch), 0, 0))],
        out_specs=row(D_MODEL),
        out_shape=jax.ShapeDtypeStruct((t, D_MODEL), F32),
        scratch_shapes=[pltpu.VMEM((2, TOP_K * tm * SUBLANES, LANES), F32), pltpu.SemaphoreType.DMA((2,))],
        compiler_params=_cparams("arbitrary"),
        name="moe_combine",
    )(runs, pos, gates, x, y_tiles, mods)


def _route(top_e, n_experts, tok_block, slot_block):
    t, k = top_e.shape
    nb = t // tok_block
    onehot = (top_e[:, :, None] == jnp.arange(n_experts, dtype=jnp.int32)).astype(jnp.int32)
    n = onehot.reshape(nb, tok_block * k, n_experts).sum(axis=1)
    counts = n.sum(axis=0)
    padded = (counts + slot_block - 1) // slot_block * slot_block
    pend = jnp.cumsum(padded)
    start = pend - padded
    local0 = jnp.cumsum(n, axis=1) - n
    slot0 = start[None, :] + jnp.cumsum(n, axis=0) - n
    runs = jnp.concatenate([n, local0, slot0], axis=1).astype(jnp.int32).reshape(nb, 1, 3 * n_experts)
    run_first = jnp.pad(local0.astype(F32), ((0, 0), (0, LANES - n_experts))).reshape(nb, 1, LANES)
    n_blocks = (t * k + n_experts * (slot_block - 1) + slot_block - 1) // slot_block
    block_start = jnp.arange(n_blocks, dtype=jnp.int32) * slot_block
    block_e = jnp.minimum((block_start[:, None] >= pend[None, :]).sum(axis=1), n_experts - 1).astype(jnp.int32)
    n_used = (pend[-1] // slot_block).astype(jnp.int32).reshape(1)
    pad_slots = jnp.stack([start + counts, padded - counts], axis=0).astype(jnp.int32)
    return runs, run_first, block_e, n_used, pad_slots, n_blocks * slot_block


def _na_row_pattern(rb, n_rows):
    wr = min(NA_WIN_H, n_rows)
    n_rb = n_rows // NA_QROWS
    q_row = rb * NA_QROWS + np.arange(NA_QROWS)[:, None]
    k_row = np.clip(rb - 1, 0, n_rb - NA_KROWS // NA_QROWS) * NA_QROWS + np.arange(NA_KROWS)[None, :]
    row_start = np.clip(q_row - wr // 2, 0, n_rows - wr)
    ok = (k_row >= row_start) & (k_row < row_start + wr)
    return np.where(ok, k_row - q_row + (NA_WIN_H - 1), -1)


def _na_bias_table(rpb, seq):
    n_rows = seq // GRID_W
    n_rb = n_rows // NA_QROWS
    patterns = [_na_row_pattern(rb, n_rows) for rb in range(n_rb)]
    assert all((p == patterns[1]).all() for p in patterns[1:-1])
    q_col = np.arange(GRID_W)[:, None]
    k_col = np.arange(GRID_W)[None, :]
    col_start = np.clip(q_col - NA_WIN_W // 2, 0, GRID_W - NA_WIN_W)
    col_ok = (k_col >= col_start) & (k_col < col_start + NA_WIN_W)
    dc = np.clip(k_col - q_col + (NA_WIN_W - 1), 0, 2 * NA_WIN_W - 2)
    onehot = ((dc[None] == np.arange(2 * NA_WIN_W - 1)[:, None, None]) & col_ok[None]).astype(np.float32)
    by_col = jnp.einsum('hab,bqk->haqk', rpb.astype(F32), jnp.asarray(onehot), precision=HIGHEST)
    by_col = by_col + jnp.asarray(np.where(col_ok, 0.0, NEG_INF).astype(np.float32))
    masked = jnp.full((rpb.shape[0], GRID_W, GRID_W), NEG_INF, F32)
    variants = []
    for pattern in (patterns[0], patterns[1], patterns[-1]):
        rows = [jnp.concatenate([by_col[:, a] if a >= 0 else masked for a in pattern[qr]], axis=2)
                for qr in range(NA_QROWS)]
        variants.append(jnp.concatenate(rows, axis=1))
    return jnp.stack(variants, axis=0)


def _rope_tables(seq, ctx_len):
    quarter = HEAD_DIM // 4
    lane = np.arange(SWA_W)
    inv = ROPE_BASE ** (-(lane % quarter).astype(np.float64) / quarter)
    t = np.arange(seq)
    pos = np.where((lane % HEAD_DIM < HEAD_DIM // 2)[None, :], (t // GRID_W)[:, None], (t % GRID_W)[:, None])
    ang = jnp.asarray(pos, F32) * jnp.asarray(inv, F32)[None, :]
    sign = np.where(lane % (2 * quarter) < quarter, -1.0, 1.0).astype(np.float32)
    cos = jnp.concatenate([jnp.ones((ctx_len, SWA_W), F32), jnp.cos(ang)], axis=0)
    sin = jnp.concatenate([jnp.zeros((ctx_len, SWA_W), F32), jnp.sin(ang) * sign[None, :]], axis=0)
    return cos, sin


def kernel(x, c, ctx, c_ctx, hg_lower_bounds, ada_w, ada_b, norm1_g, norm2_g, w_in, na_q_norm, na_k_norm, na_rpb,
           hg_norm_g, swa_q_norm, swa_k_norm, swa_sink, w_out, router_w, router_b, w_gu, b_gu, w_down, b_down):
    n_batch, seq, d = x.shape
    ctx_len = ctx.shape[1]
    depth = ada_w.shape[0]
    assert d == D_MODEL and seq % ROW_BLOCK == 0 and ctx_len == ROW_BLOCK
    n_ctx_rows = n_batch * ctx_len
    geom = (n_batch, n_ctx_rows // ROW_BLOCK, seq // ROW_BLOCK)

    p_lb = jax.nn.softmax(hg_lower_bounds.astype(F32), axis=0)
    lbs = jnp.cumsum(p_lb, axis=0) - p_lb[0]

    cos_t, sin_t = _rope_tables(seq, ctx_len)
    lane = np.arange(NA_W)
    group_ones = jnp.asarray((lane[:, None] // HEAD_DIM == lane[None, :] // HEAD_DIM).astype(np.float32), BF16)
    tri_f = jnp.asarray(_hgrn_sum_table(HG_STEP_ROWS, False), BF16)
    tri_b = jnp.asarray(_hgrn_sum_table(HG_STEP_ROWS, True), BF16)
    ones_bf = jnp.ones((LANES, LANES), BF16)
    n_mod_rows = -(-(n_batch + 1) // 8) * 8
    cc = jnp.zeros((n_mod_rows, d), F32).at[:n_batch].set(c).at[n_batch].set(c_ctx)
    dup = lambda w: jnp.concatenate([w[:, :HEAD_DIM], w[:, :HEAD_DIM], w[:, HEAD_DIM:], w[:, HEAD_DIM:]], axis=1)
    tile4 = lambda g: jnp.tile(g.astype(F32), 4).reshape(1, 4 * HEAD_DIM)
    pad_e = LANES - N_EXPERTS

    x_all = jnp.concatenate([ctx.reshape(n_ctx_rows, d), x.reshape(n_batch * seq, d)], axis=0)
    for l in range(depth):
        last = l == depth - 1
        mods = _ada_mod(cc, ada_w[l], ada_b[l]).reshape(n_mod_rows, 6, d)
        w = w_in[l]
        kv0 = 3 * NA_W + HG_COLS + SWA_W
        w_ext = jnp.concatenate([w[:, :kv0], dup(w[:, kv0:kv0 + SWA_KV_W]), dup(w[:, kv0 + SWA_KV_W:])],
                                axis=1).astype(BF16)
        hg, naq, nak, nav, swq, swk, swv = _inproj(
            x_all, mods, norm1_g[l].reshape(1, d), w_ext, tile4(na_q_norm[l]), tile4(na_k_norm[l]),
            tile4(swa_q_norm[l]), tile4(swa_k_norm[l]), cos_t, sin_t, group_ones, geom)

        y_na = _na_attention(naq, nak, nav, _na_bias_table(na_rpb[l], seq), geom)
        y_sw = _swa_attention(swq, swk, swv, swa_sink[l].astype(F32), geom, seq, ctx_len)
        o_f, o_b = _hgrn(hg, lbs[l].reshape(1, HG_W), tri_f, tri_b, ones_bf, n_batch, seq, ctx_len)

        if last:
            stream_off = geom[1]
        else:
            stream_off = 0
            yc_na, yc_sw = _ctx_attention(naq, nak, nav, swq, swk, swv, swa_sink[l].astype(F32), n_batch, ctx_len)
            y_na = jnp.concatenate([yc_na, y_na], axis=0)
            y_sw = jnp.concatenate([yc_sw, y_sw], axis=0)

        rw_hi, rw_lo = _split_bf16(jnp.pad(router_w[l].astype(F32), ((0, 0), (0, pad_e))), 2)
        rw = jnp.concatenate([rw_hi, rw_hi, rw_lo], axis=0)
        rb = jnp.pad(router_b[l].astype(F32), (0, pad_e), constant_values=NEG_INF).reshape(1, LANES)
        x_new, h2, top_e, top_g = _mixout(
            x_all, y_na, o_f, o_b, hg, y_sw, hg_norm_g[l].reshape(1, HG_DK).astype(F32), w_out[l].astype(BF16),
            mods, norm2_g[l].reshape(1, d), rw, rb, geom, stream_off)

        runs, run_first, block_e, n_used, pad_slots, n_slots = _route(top_e[:, :TOP_K], N_EXPERTS, ROW_BLOCK, MOE_ROWS)
        xs, pos = _dispatch(h2, top_e, runs, run_first, pad_slots, n_used, n_slots)
        y_slots = _moe_ffn(xs, block_e, n_used, w_gu, b_gu, w_down, b_down, l)
        x_all = _combine(x_new, y_slots, runs, pos, top_g, mods, geom, stream_off)
    return x_all.reshape(n_batch, seq, d)
```

```python
import functools

import numpy as np
import jax
import jax.numpy as jnp
from jax import lax
from jax.experimental import pallas as pl
from jax.experimental.pallas import tpu as pltpu

D_MODEL = 1024
GRID_W = 64
HEAD_DIM = 64
ATTN_SCALE = HEAD_DIM ** -0.5
NA_HEADS = 4
NA_WIN_H = 8
NA_WIN_W = 16
HG_HEADS = 4
HG_DK = 128
SWA_Q_HEADS = 4
SWA_KV_HEADS = 2
SWA_WINDOW = 128
SWA_BLOCK = 128
ROPE_BASE = 10000.0
N_EXPERTS = 32
TOP_K = 4
D_FF = 1024
SWIGLU_LIMIT = 7.0
SWIGLU_ALPHA = 1.702
NORM_EPS = 1e-6
NEG_INF = -1e30

NA_W = NA_HEADS * HEAD_DIM
HG_W = HG_HEADS * HG_DK
SWA_W = SWA_Q_HEADS * HEAD_DIM
SWA_KV_W = SWA_KV_HEADS * HEAD_DIM
HG_COLS = 5 * HG_W
IN_COLS_EXT = 3 * NA_W + HG_COLS + 3 * SWA_W

LANES = 128
SUBLANES = 8
ROW_BLOCK = 256
NA_QROWS = 4
NA_KROWS = 12
HG_SMALL_LEVELS = (2, 4, 8)
HG_STEP_ROWS = 128
HG_HEADS_PER_STEP = 4
MOE_ROWS = 256
RUN_BITS = ROW_BLOCK.bit_length()
VMEM_LIMIT = 56 * 1024 * 1024

F32 = jnp.float32
BF16 = jnp.bfloat16
HIGHEST = lax.Precision.HIGHEST


def _cparams(*sem):
    return pltpu.CompilerParams(dimension_semantics=sem, vmem_limit_bytes=VMEM_LIMIT)


def _dot(a, b, precision=None):
    return jnp.dot(a, b, preferred_element_type=F32, precision=precision)


def _dot_nt(a, b):
    return lax.dot_general(a, b, (((1,), (1,)), ((), ())), preferred_element_type=F32)


def _dot_tn(a, b):
    return lax.dot_general(a, b, (((0,), (0,)), ((), ())), preferred_element_type=F32)


def _store_token_tiles(ref, x):
    n = x.shape[0]
    for c in range(SUBLANES):
        ref[pl.ds(c, n, stride=SUBLANES), :] = x[:, c * LANES:(c + 1) * LANES]


def _load_token_tiles(ref, first_token, n):
    return jnp.concatenate([ref[pl.ds(first_token * SUBLANES + c, n, stride=SUBLANES), :] for c in range(SUBLANES)],
                           axis=1)


def _sigmoid(x):
    return 1.0 / (1.0 + jnp.exp(-x))


def _silu(x):
    return x * _sigmoid(x)


def _ada_kernel(c_ref, w_ref, b_ref, o_ref):
    o_ref[...] = _dot(_silu(c_ref[...]), w_ref[...], HIGHEST) + b_ref[...]


def _ada_mod(cc, w, b):
    m, d = cc.shape
    n = w.shape[1]
    tn = 1024
    return pl.pallas_call(
        _ada_kernel,
        grid=(n // tn,),
        in_specs=[pl.BlockSpec((m, d), lambda j: (0, 0)),
                  pl.BlockSpec((d, tn), lambda j: (0, j)),
                  pl.BlockSpec((1, tn), lambda j: (0, j))],
        out_specs=pl.BlockSpec((m, tn), lambda j: (0, j)),
        out_shape=jax.ShapeDtypeStruct((m, n), F32),
        compiler_params=_cparams("parallel"),
        name="ada_mod",
    )(cc, w, b.reshape(1, n))


def _mod_rmsnorm(x, g, mod, shift_row, scale_row):
    y = x * lax.rsqrt(jnp.mean(x * x, axis=-1, keepdims=True) + NORM_EPS) * g
    return y * (1.0 + mod[scale_row:scale_row + 1]) + mod[shift_row:shift_row + 1]


def _split_bf16(x, terms):
    parts = []
    for _ in range(terms - 1):
        parts.append(x.astype(BF16))
        x = x - parts[-1].astype(F32)
    return parts + [x.astype(BF16)]


def _head_rmsnorm(x, w, group_ones):
    hi, lo = _split_bf16(x * x, 2)
    n = x.shape[0]
    ss = _dot(jnp.concatenate([hi, lo], axis=0), group_ones)
    return x * lax.rsqrt((ss[:n] + ss[n:]) * (1.0 / HEAD_DIM) + NORM_EPS) * w


def _rope(x, cos, sin_signed):
    n = x.shape[-1]
    lane = lax.broadcasted_iota(jnp.int32, x.shape, 1)
    quarter = HEAD_DIM // 4
    partner = jnp.where(lane % (2 * quarter) < quarter,
                        pltpu.roll(x, n - quarter, 1), pltpu.roll(x, quarter, 1))
    return x * cos + partner * sin_signed


def _inproj_kernel(x_ref, mod_ref, g_ref, w_ref, naq_w, nak_w, swq_w, swk_w, cos_ref, sin_ref, ones_ref, lb_ref,
                   hg_ref, naq_ref, nak_ref, nav_ref, swq_ref, swk_ref, swv_ref):
    h = _mod_rmsnorm(x_ref[...], g_ref[...], mod_ref[...], 0, 1).astype(BF16)
    ones = ones_ref[...]

    def proj(lo, width):
        return _dot(h, w_ref[:, lo:lo + width])

    naq_ref[...] = (_head_rmsnorm(proj(0, NA_W), naq_w[...], ones) * ATTN_SCALE).astype(BF16)
    nak_ref[...] = _head_rmsnorm(proj(NA_W, NA_W), nak_w[...], ones).astype(BF16)
    nav_ref[...] = proj(2 * NA_W, NA_W).astype(BF16)
    base = 3 * NA_W
    lb = lb_ref[...]
    hg_ref[:, :HG_W] = _silu(proj(base, HG_W)) * (HG_DK ** -0.5)
    for j in (1, 2):
        hg_ref[:, j * HG_W:(j + 1) * HG_W] = jnp.log2(lb + (1.0 - lb) * _sigmoid(proj(base + j * HG_W, HG_W)))
    for j in (3, 4):
        hg_ref[:, j * HG_W:(j + 1) * HG_W] = proj(base + j * HG_W, HG_W)
    base += HG_COLS
    cos, sin = cos_ref[...], sin_ref[...]
    swq = _rope(_head_rmsnorm(proj(base, SWA_W), swq_w[...], ones), cos, sin)
    swq_ref[...] = (swq * ATTN_SCALE).astype(BF16)
    swk_ref[...] = _rope(_head_rmsnorm(proj(base + SWA_W, SWA_W), swk_w[...], ones), cos, sin).astype(BF16)
    swv_ref[...] = proj(base + 2 * SWA_W, SWA_W).astype(BF16)


def _mod_row(i, n_ctx_blocks, blocks_per_batch, n_batch):
    return jnp.where(i < n_ctx_blocks, n_batch, (i - n_ctx_blocks) // blocks_per_batch)


def _inproj(x_all, mods, norm_g, w_ext, naq_w, nak_w, swq_w, swk_w, cos_t, sin_t, ones, lb, geom):
    n_batch, n_ctx_blocks, blocks_per_batch = geom
    t = x_all.shape[0]
    tm = ROW_BLOCK
    row = lambda w: pl.BlockSpec((tm, w), lambda i: (i, 0))
    const = lambda a: pl.BlockSpec(a.shape, lambda i: (0,) * a.ndim)
    rope_blk = lambda i: (jnp.where(i < n_ctx_blocks, 0, 1 + (i - n_ctx_blocks) % blocks_per_batch), 0)
    bf = lambda w: jax.ShapeDtypeStruct((t, w), BF16)
    return pl.pallas_call(
        _inproj_kernel,
        grid=(t // tm,),
        in_specs=[row(D_MODEL),
                  pl.BlockSpec((None, 6, D_MODEL),
                               lambda i: (_mod_row(i, n_ctx_blocks, blocks_per_batch, n_batch), 0, 0)),
                  const(norm_g), const(w_ext), const(naq_w), const(nak_w), const(swq_w), const(swk_w),
                  pl.BlockSpec((tm, SWA_W), rope_blk), pl.BlockSpec((tm, SWA_W), rope_blk), const(ones),
                  const(lb)],
        out_specs=[row(HG_COLS), row(NA_W), row(NA_W), row(NA_W), row(SWA_W), row(SWA_W), row(SWA_W)],
        out_shape=[jax.ShapeDtypeStruct((t, HG_COLS), F32), bf(NA_W), bf(NA_W), bf(NA_W),
                   bf(SWA_W), bf(SWA_W), bf(SWA_W)],
        compiler_params=_cparams("parallel"),
        name="inproj",
    )(x_all, mods, norm_g, w_ext, naq_w, nak_w, swq_w, swk_w, cos_t, sin_t, ones, lb)


def _attend(q, parts, sink_vals):
    m_rows = q.shape[0]
    lane = lax.broadcasted_iota(jnp.int32, (m_rows, q.shape[1]), 1)
    out = jnp.zeros((m_rows, q.shape[1]), F32)
    for h in range(q.shape[1] // HEAD_DIM):
        in_head = (lane // HEAD_DIM) == h
        qh = jnp.where(in_head, q, jnp.zeros_like(q))
        scores = []
        for k, _, add in parts:
            s = _dot_nt(qh, k)
            if add is not None:
                s = s + (add[h] if add.ndim == 3 else add)
            scores.append(s)
        mx = functools.reduce(jnp.maximum, [jnp.max(s, axis=-1, keepdims=True) for s in scores])
        if sink_vals is not None:
            mx = jnp.maximum(mx, sink_vals[h])
        ps = [jnp.exp(s - mx) for s in scores]
        den = functools.reduce(jnp.add, [jnp.sum(p, axis=-1, keepdims=True) for p in ps])
        if sink_vals is not None:
            den = den + jnp.exp(sink_vals[h] - mx)
        acc = functools.reduce(jnp.add, [_dot(p.astype(BF16), v) for p, (_, v, _) in zip(ps, parts)])
        out = jnp.where(in_head, acc / den, out)
    return out


def _na_kernel(q_ref, k0, k1, k2, kc, v0, v1, v2, vc, bias_ref, o_ref):
    k_lat = jnp.concatenate([k0[...], k1[...], k2[...]], axis=0)
    v_lat = jnp.concatenate([v0[...], v1[...], v2[...]], axis=0)
    o = _attend(q_ref[...], [(k_lat, v_lat, bias_ref[...]), (kc[...], vc[...], None)], None)
    o_ref[...] = o.astype(o_ref.dtype)


def _na_attention(naq, nak, nav, bias, geom):
    n_batch, n_ctx_blocks, blocks_per_batch = geom
    tm = ROW_BLOCK
    n_rb = blocks_per_batch
    kb_max = n_rb - NA_KROWS // NA_QROWS
    lat = lambda rb, b: n_ctx_blocks + b * blocks_per_batch

    def band(j):
        return pl.BlockSpec((tm, NA_W), lambda rb, b: (lat(rb, b) + jnp.clip(rb - 1, 0, kb_max) + j, 0))

    ctx = pl.BlockSpec((tm, NA_W), lambda rb, b: (b, 0))
    return pl.pallas_call(
        _na_kernel,
        grid=(n_rb, n_batch),
        in_specs=[pl.BlockSpec((tm, NA_W), lambda rb, b: (lat(rb, b) + rb, 0)),
                  band(0), band(1), band(2), ctx, band(0), band(1), band(2), ctx,
                  pl.BlockSpec((None, NA_HEADS, tm, NA_KROWS * GRID_W), lambda rb, b: (
                      jnp.where(rb == 0, 0, jnp.where(rb == n_rb - 1, 2, 1)), 0, 0, 0))],
        out_specs=pl.BlockSpec((tm, NA_W), lambda rb, b: (b * blocks_per_batch + rb, 0)),
        out_shape=jax.ShapeDtypeStruct((n_batch * blocks_per_batch * tm, NA_W), BF16),
        compiler_params=_cparams("parallel", "parallel"),
        name="na_attn",
    )(naq, nak, nak, nak, nak, nav, nav, nav, nav, bias)


def _swa_kernel(sink_ref, q_ref, kp, kc_, kn, kx, vp, vc_, vn, vx, o_ref, *, n_blocks):
    n = pl.program_id(1)
    tq = q_ref.shape[0]
    k_lat = jnp.concatenate([kp[...], kc_[...], kn[...]], axis=0)
    v_lat = jnp.concatenate([vp[...], vc_[...], vn[...]], axis=0)
    qi = lax.broadcasted_iota(jnp.int32, (tq, 3 * tq), 0)
    mi = lax.broadcasted_iota(jnp.int32, (tq, 3 * tq), 1)
    kpos = (n - 1) * tq + mi
    ok = (jnp.abs(qi + tq - mi) <= SWA_WINDOW) & (kpos >= 0) & (kpos < n_blocks * tq)
    add = jnp.where(ok, 0.0, NEG_INF).astype(F32)
    sinks = [sink_ref[h] for h in range(SWA_Q_HEADS)]
    o = _attend(q_ref[...], [(k_lat, v_lat, add), (kx[...], vx[...], None)], sinks)
    o_ref[...] = o.astype(o_ref.dtype)


def _swa_attention(swq, swk, swv, sink, geom, seq, ctx_len):
    n_batch, _, _ = geom
    tq = SWA_BLOCK
    nb = seq // tq
    first = n_batch * ctx_len // tq
    blk = lambda f: pl.BlockSpec((tq, SWA_W), lambda b, n: (first + b * nb + f(n), 0))
    prev, cur, nxt = blk(lambda n: jnp.maximum(n - 1, 0)), blk(lambda n: n), blk(lambda n: jnp.minimum(n + 1, nb - 1))
    ctx = pl.BlockSpec((ctx_len, SWA_W), lambda b, n: (b, 0))
    return pl.pallas_call(
        functools.partial(_swa_kernel, n_blocks=nb),
        grid=(n_batch, nb),
        in_specs=[pl.BlockSpec(memory_space=pltpu.SMEM), cur, prev, cur, nxt, ctx, prev, cur, nxt, ctx],
        out_specs=pl.BlockSpec((tq, SWA_W), lambda b, n: (b * nb + n, 0)),
        out_shape=jax.ShapeDtypeStruct((n_batch * seq, SWA_W), BF16),
        compiler_params=_cparams("parallel", "parallel"),
        name="swa_attn",
    )(sink, swq, swk, swk, swk, swk, swv, swv, swv, swv)


def _ctx_attn_kernel(sink_ref, naq, nak, nav, swq, swk, swv, ona_ref, osw_ref):
    ona_ref[...] = _attend(naq[...], [(nak[...], nav[...], None)], None).astype(ona_ref.dtype)
    sinks = [sink_ref[h] for h in range(SWA_Q_HEADS)]
    osw_ref[...] = _attend(swq[...], [(swk[...], swv[...], None)], sinks).astype(osw_ref.dtype)


def _ctx_attention(naq, nak, nav, swq, swk, swv, sink, n_batch, ctx_len):
    blk = pl.BlockSpec((ctx_len, NA_W), lambda b: (b, 0))
    out = jax.ShapeDtypeStruct((n_batch * ctx_len, NA_W), BF16)
    return pl.pallas_call(
        _ctx_attn_kernel,
        grid=(n_batch,),
        in_specs=[pl.BlockSpec(memory_space=pltpu.SMEM)] + [blk] * 6,
        out_specs=[blk, blk],
        out_shape=[out, out],
        compiler_params=_cparams("parallel"),
        name="ctx_attn",
    )(sink, naq, nak, nav, swq, swk, swv)


def _dot_exact_lhs(m, x):
    out = _dot(m, jnp.concatenate(_split_bf16(x, 3), axis=1))
    n = x.shape[1]
    return out[:, :n] + out[:, n:2 * n] + out[:, 2 * n:]


def _hgrn_sum_table(n_rows, reverse):
    tri = np.tril(np.ones((n_rows, n_rows), np.float32))
    if reverse:
        tri = tri.T
    t = np.arange(n_rows)
    groups = [tri] + [tri[t // size * size + (size // 2 if reverse else size // 2 - 1)] for size in HG_SMALL_LEVELS]
    return np.concatenate(groups, axis=0)


def _hgrn_block(q, k, g, v, st, sums, ones, reverse):
    n_rows = q.shape[0]
    all_sums = _dot_exact_lhs(sums, g)
    cum = all_sums[:n_rows]
    tot = cum[0:1] if reverse else cum[n_rows - 1:n_rows]
    o_inter = _dot_nt((q * jnp.exp2(cum)).astype(BF16), st.astype(BF16))
    k_end = (k * jnp.exp2(tot - cum)).astype(BF16)
    st_new = st * jnp.exp2(tot) + _dot_tn(v.astype(BF16), k_end)

    row = lax.broadcasted_iota(jnp.int32, (n_rows, LANES), 0)
    row_a = lax.broadcasted_iota(jnp.int32, (n_rows, n_rows), 0)
    col_a = lax.broadcasted_iota(jnp.int32, (n_rows, n_rows), 1)
    a = jnp.where(row_a == col_a, _dot((q * k).astype(BF16), ones), 0.0)
    zero_tile = jnp.zeros((SUBLANES, LANES), F32)
    size = 2
    while size <= n_rows:
        half = size // 2
        if size in HG_SMALL_LEVELS:
            level = HG_SMALL_LEVELS.index(size) + 1
            ref = all_sums[level * n_rows:(level + 1) * n_rows]
            later = (row % size < half) if reverse else (row % size >= half)
            q_t = (q * jnp.exp2(jnp.where(later, cum - ref, NEG_INF))).astype(BF16)
            k_t = (k * jnp.exp2(jnp.where(later, NEG_INF, ref - cum))).astype(BF16)
        else:
            q_tiles, k_tiles = [], []
            for r0 in range(0, n_rows, SUBLANES):
                sl = slice(r0, r0 + SUBLANES)
                first = r0 // size * size
                ref_row = first + (half if reverse else half - 1)
                ref = cum[ref_row:ref_row + 1]
                if (r0 - first < half) if reverse else (r0 - first >= half):
                    q_tiles.append(q[sl] * jnp.exp2(cum[sl] - ref))
                    k_tiles.append(zero_tile)
                else:
                    q_tiles.append(zero_tile)
                    k_tiles.append(k[sl] * jnp.exp2(ref - cum[sl]))
            q_t = jnp.concatenate(q_tiles, axis=0).astype(BF16)
            k_t = jnp.concatenate(k_tiles, axis=0).astype(BF16)
        a_l = _dot_nt(q_t, k_t)
        a = a + (a_l if size == n_rows else jnp.where(row_a // size == col_a // size, a_l, 0.0))
        size *= 2
    return o_inter + _dot(a.astype(BF16), v.astype(BF16)), st_new


def _hgrn_kernel(q_f, g_f, v_f, q_b, g_b, v_b, trif_ref, trib_ref, ones_ref, of_ref, ob_ref, st_f, st_b):
    @pl.when(pl.program_id(2) == 0)
    def _():
        st_f[...] = jnp.zeros_like(st_f)
        st_b[...] = jnp.zeros_like(st_b)

    ones = ones_ref[...]
    for h in range(HG_HEADS_PER_STEP):
        sl = slice(h * HG_DK, (h + 1) * HG_DK)
        g = g_f[:, sl]
        of_ref[:, sl], st_f[h] = _hgrn_block(q_f[:, sl], 1.0 - jnp.exp2(g), g, v_f[:, sl], st_f[h],
                                             trif_ref[...], ones, False)
        g = g_b[:, sl]
        ob_ref[:, sl], st_b[h] = _hgrn_block(q_b[:, sl], 1.0 - jnp.exp2(g), g, v_b[:, sl], st_b[h],
                                             trib_ref[...], ones, True)


def _hgrn(hg, tri_f, tri_b, ones, n_batch, seq, ctx_len):
    tr = HG_STEP_ROWS
    nc, nl = ctx_len // tr, seq // tr
    first = n_batch * nc

    def fwd(b, j):
        return jnp.where(j < nc, b * nc + j, first + b * nl + (j - nc))

    def bwd(b, j):
        return jnp.where(j < nc, b * nc + (nc - 1 - j), first + b * nl + (nl - 1 - (j - nc)))

    hps = HG_HEADS_PER_STEP
    groups = HG_HEADS // hps
    wide = hps * HG_DK
    col = lambda row_fn, group: pl.BlockSpec((tr, wide), lambda b, h, j: (row_fn(b, j), group * groups + h))
    const = lambda a: pl.BlockSpec(a.shape, lambda b, h, j: (0, 0))
    out = jax.ShapeDtypeStruct((hg.shape[0], HG_W), F32)
    return pl.pallas_call(
        _hgrn_kernel,
        grid=(n_batch, groups, nc + nl),
        in_specs=[col(fwd, 0), col(fwd, 1), col(fwd, 3), col(bwd, 0), col(bwd, 2), col(bwd, 3),
                  const(tri_f), const(tri_b), const(ones)],
        out_specs=[pl.BlockSpec((tr, wide), lambda b, h, j: (fwd(b, j), h)),
                   pl.BlockSpec((tr, wide), lambda b, h, j: (bwd(b, j), h))],
        out_shape=[out, out],
        scratch_shapes=[pltpu.VMEM((hps, HG_DK, HG_DK), F32), pltpu.VMEM((hps, HG_DK, HG_DK), F32)],
        compiler_params=_cparams("parallel", "parallel", "arbitrary"),
        name="hgrn",
    )(hg, hg, hg, hg, hg, hg, tri_f, tri_b, ones)


def _mixout_kernel(x_ref, yna_ref, of_ref, ob_ref, zg_ref, ysw_ref, hgn_ref, w_ref, mod_ref, g2_ref,
                   rw_ref, rb_ref, xo_ref, h2_ref, te_ref, tg_ref):
    o = of_ref[...] + ob_ref[...]
    zg = zg_ref[...]
    parts = [yna_ref[...]]
    for h in range(HG_HEADS):
        sl = slice(h * HG_DK, (h + 1) * HG_DK)
        oh = o[:, sl]
        yh = oh * lax.rsqrt(jnp.mean(oh * oh, axis=-1, keepdims=True) + NORM_EPS) * hgn_ref[...]
        parts.append((yh * _silu(zg[:, sl])).astype(BF16))
    parts.append(ysw_ref[...])
    y = _dot(jnp.concatenate(parts, axis=1), w_ref[...])
    mod = mod_ref[...]
    x_new = x_ref[...] + mod[2:3] * y
    xo_ref[...] = x_new
    h2 = _mod_rmsnorm(x_new, g2_ref[...], mod, 3, 4)
    h2_ref[...] = h2.astype(BF16)

    h_hi, h_lo = _split_bf16(h2, 2)
    logits = _dot(jnp.concatenate([h_hi, h_lo, h_hi], axis=1), rw_ref[...]) + rb_ref[...]
    lane = lax.broadcasted_iota(jnp.int32, logits.shape, 1).astype(F32)
    top_e = jnp.zeros(logits.shape, F32)
    top_v = jnp.full(logits.shape, NEG_INF, F32)
    for j in range(TOP_K):
        best = jnp.max(logits, axis=-1, keepdims=True)
        arg = jnp.min(jnp.where(logits == best, lane, float(LANES)), axis=-1, keepdims=True)
        top_e = jnp.where(lane == j, arg, top_e)
        top_v = jnp.where(lane == j, best, top_v)
        logits = jnp.where(lane == arg, -jnp.inf, logits)
    ex = jnp.exp(top_v - jnp.max(top_v, axis=-1, keepdims=True))
    te_ref[...] = top_e.astype(jnp.int32)
    tg_ref[...] = ex / jnp.sum(ex, axis=-1, keepdims=True)


def _mixout(x, yna, o_f, o_b, hg, ysw, hgn, w_out, mods, norm2_g, rw, rb, geom, stream_off):
    n_batch, n_ctx_blocks, blocks_per_batch = geom
    tm = ROW_BLOCK
    t = x.shape[0] - stream_off * tm
    row = lambda w: pl.BlockSpec((tm, w), lambda i: (i, 0))
    full = lambda w, cb: pl.BlockSpec((tm, w), lambda i: (i + stream_off, cb))
    const = lambda a: pl.BlockSpec(a.shape, lambda i: (0,) * a.ndim)
    return pl.pallas_call(
        _mixout_kernel,
        grid=(t // tm,),
        in_specs=[full(D_MODEL, 0), row(NA_W), full(HG_W, 0), full(HG_W, 0), full(HG_W, 4), row(SWA_W),
                  const(hgn), const(w_out),
                  pl.BlockSpec((None, 6, D_MODEL), lambda i: (
                      _mod_row(i + stream_off, n_ctx_blocks, blocks_per_batch, n_batch), 0, 0)),
                  const(norm2_g), const(rw), const(rb)],
        out_specs=[row(D_MODEL), row(D_MODEL), row(LANES), row(LANES)],
        out_shape=[jax.ShapeDtypeStruct((t, D_MODEL), F32), jax.ShapeDtypeStruct((t, D_MODEL), BF16),
                   jax.ShapeDtypeStruct((t, LANES), jnp.int32), jax.ShapeDtypeStruct((t, LANES), F32)],
        compiler_params=_cparams("parallel"),
        name="mixout",
    )(x, yna, o_f, o_b, hg, ysw, hgn, w_out, mods, norm2_g, rw, rb)


def _start_run_copies(run_ref, n_experts, make_copy):
    for e in range(n_experts):
        n = run_ref[0, 0, e]
        local0 = run_ref[0, 0, n_experts + e]
        slot0 = run_ref[0, 0, 2 * n_experts + e]
        for bit in reversed(range(RUN_BITS)):
            @pl.when((n & (1 << bit)) != 0)
            def _():
                done = (n >> (bit + 1)) << (bit + 1)
                make_copy(local0 + done, slot0 + done, 1 << bit).start()


def _tile_rows(first_token, n_tokens):
    return pl.ds(pl.multiple_of(first_token * SUBLANES, SUBLANES), n_tokens * SUBLANES)


def _local_positions(te_ref, off_ref, stri_ref):
    e_tok = te_ref[...]
    lane = lax.broadcasted_iota(jnp.int32, e_tok.shape, 1)
    member = [lane == e_tok[:, j:j + 1] for j in range(TOP_K)]
    count = functools.reduce(jnp.add, [jnp.where(m, 1.0, 0.0) for m in member])
    before = _dot(stri_ref[...], count.astype(BF16)) + off_ref[...]
    return [jnp.sum(jnp.where(m, before, 0.0), axis=1, keepdims=True) for m in member]


def _dispatch_kernel(pad_ref, nu_ref, run_ref, te_ref, off_ref, h_ref, stri_ref, xs_hbm, pos_ref,
                     zeros, xloc, sem, zero_sem):
    n_tok = h_ref.shape[0]
    block_rows = zeros.shape[0]
    n_blocks = xs_hbm.shape[0] // block_rows

    def zero_slot(e, r):
        row = pl.multiple_of((pad_ref[0, e] + r) * SUBLANES, SUBLANES)
        return pltpu.make_async_copy(zeros.at[pl.ds(0, SUBLANES), :], xs_hbm.at[pl.ds(row, SUBLANES), :], zero_sem)

    def zero_block(b):
        row = pl.multiple_of(b * block_rows, block_rows)
        return pltpu.make_async_copy(zeros, xs_hbm.at[pl.ds(row, block_rows), :], zero_sem)

    def for_each_unused(slot_fn, block_fn):
        def per_expert(e, carry):
            lax.fori_loop(0, pad_ref[1, e], lambda r, c: (slot_fn(e, r), c)[1], 0)
            return carry

        lax.fori_loop(0, pad_ref.shape[1], per_expert, 0)
        lax.fori_loop(nu_ref[0], n_blocks, lambda b, c: (block_fn(b), c)[1], 0)

    @pl.when(pl.program_id(0) == 0)
    def _():
        zeros[...] = jnp.zeros_like(zeros)
        for_each_unused(lambda e, r: zero_slot(e, r).start(), lambda b: zero_block(b).start())
        for_each_unused(lambda e, r: zero_slot(e, r).wait(), lambda b: zero_block(b).wait())

    pos = _local_positions(te_ref, off_ref, stri_ref)
    lane = lax.broadcasted_iota(jnp.int32, (n_tok, LANES), 1)
    pos_ref[...] = functools.reduce(jnp.add, [jnp.where(lane == j, p, 0.0) for j, p in enumerate(pos)])
    col = lax.broadcasted_iota(jnp.int32, (n_tok, TOP_K * n_tok), 1).astype(F32)
    chosen = functools.reduce(jnp.logical_or, [col == p for p in pos])
    sorted_rows = _dot_tn(jnp.where(chosen, 1.0, 0.0).astype(BF16), h_ref[...])
    i = pl.program_id(0)
    buf = i % 2
    _store_token_tiles(xloc.at[buf], sorted_rows)

    def wait_copies(b):
        pltpu.make_async_copy(xloc.at[b], xs_hbm.at[pl.ds(0, xloc.shape[1]), :], sem.at[b]).wait()

    @pl.when(i > 0)
    def _():
        wait_copies(1 - buf)

    _start_run_copies(run_ref, pad_ref.shape[1], lambda local, slot, size: pltpu.make_async_copy(
        xloc.at[buf, _tile_rows(local, size), :], xs_hbm.at[_tile_rows(slot, size), :], sem.at[buf]))

    @pl.when(i == pl.num_programs(0) - 1)
    def _():
        wait_copies(buf)


def _dispatch(h, top_e, runs, run_first, pad_slots, n_used, n_slots):
    tm = ROW_BLOCK
    t = h.shape[0]
    nb = t // tm
    assert n_slots % MOE_ROWS == 0
    stri = jnp.asarray(np.tril(np.ones((tm, tm), np.float32), -1), BF16)
    row = lambda w: pl.BlockSpec((tm, w), lambda i: (i, 0))
    return pl.pallas_call(
        _dispatch_kernel,
        grid=(nb,),
        in_specs=[pl.BlockSpec(memory_space=pltpu.SMEM), pl.BlockSpec(memory_space=pltpu.SMEM),
                  pl.BlockSpec((1, 1, runs.shape[2]), lambda i: (i, 0, 0), memory_space=pltpu.SMEM),
                  row(LANES), pl.BlockSpec((None, 1, LANES), lambda i: (i, 0, 0)), row(D_MODEL),
                  pl.BlockSpec((tm, tm), lambda i: (0, 0))],
        out_specs=[pl.BlockSpec(memory_space=pl.ANY), row(LANES)],
        out_shape=[jax.ShapeDtypeStruct((n_slots * SUBLANES, LANES), F32), jax.ShapeDtypeStruct((t, LANES), F32)],
        scratch_shapes=[pltpu.VMEM((MOE_ROWS * SUBLANES, LANES), F32),
                        pltpu.VMEM((2, TOP_K * tm * SUBLANES, LANES), F32), pltpu.SemaphoreType.DMA((2,)),
                        pltpu.SemaphoreType.DMA(())],
        compiler_params=_cparams("arbitrary"),
        name="moe_dispatch",
    )(pad_slots, n_used, runs, top_e, run_first, h, stri)


def _moe_kernel(be_ref, nu_ref, xs_ref, wgu_ref, bgu_ref, wd_ref, bd_ref, y_ref, wgu_bf, wd_bf):
    i = pl.program_id(0)
    live = i < nu_ref[0]
    tm = y_ref.shape[0] // SUBLANES
    changed = (i == 0) | (be_ref[i] != be_ref[jnp.maximum(i - 1, 0)])

    @pl.when(live & changed)
    def _():
        wgu_bf[...] = wgu_ref[...].astype(BF16)
        wd_bf[...] = wd_ref[...].astype(BF16)

    @pl.when(live)
    def _():
        x = _load_token_tiles(xs_ref, 0, tm)
        gu = _dot(x.astype(BF16), wgu_bf[...]) + bgu_ref[...]
        glu = jnp.minimum(gu[:, :D_FF], SWIGLU_LIMIT)
        lin = jnp.clip(gu[:, D_FF:], -SWIGLU_LIMIT, SWIGLU_LIMIT)
        act = glu * _sigmoid(SWIGLU_ALPHA * glu) * (lin + 1.0)
        _store_token_tiles(y_ref, _dot(act.astype(BF16), wd_bf[...]) + bd_ref[...])

    @pl.when(jnp.logical_not(live))
    def _():
        y_ref[...] = jnp.zeros_like(y_ref)


def _moe_ffn(xs_tiles, block_e, n_used, w_gu, b_gu, w_down, b_down, layer):
    tm = MOE_ROWS
    nb = xs_tiles.shape[0] // (tm * SUBLANES)
    _, n_e, d, f2 = w_gu.shape
    weight = lambda r, c: pl.BlockSpec((None, None, r, c), lambda i, be, nu: (layer, be[i], 0, 0))
    grid_spec = pltpu.PrefetchScalarGridSpec(
        num_scalar_prefetch=2,
        grid=(nb,),
        in_specs=[pl.BlockSpec((tm * SUBLANES, LANES), lambda i, be, nu: (jnp.minimum(i, nu[0] - 1), 0)),
                  weight(d, f2), weight(1, f2), weight(f2 // 2, d), weight(1, d)],
        out_specs=pl.BlockSpec((tm * SUBLANES, LANES), lambda i, be, nu: (i, 0)),
        scratch_shapes=[pltpu.VMEM((d, f2), BF16), pltpu.VMEM((f2 // 2, d), BF16)],
    )
    n_l = w_gu.shape[0]
    return pl.pallas_call(
        _moe_kernel,
        grid_spec=grid_spec,
        out_shape=jax.ShapeDtypeStruct((nb * tm * SUBLANES, LANES), F32),
        compiler_params=_cparams("arbitrary"),
        name="moe_ffn",
    )(block_e, n_used, xs_tiles, w_gu, b_gu.reshape(n_l, n_e, 1, f2), w_down,
      b_down.reshape(n_l, n_e, 1, d))


def _combine_kernel(run_ref, pos_ref, tg_ref, x_ref, y_hbm, mod_ref, o_ref, ybuf, sem):
    i = pl.program_id(0)
    tm = o_ref.shape[0]
    n_rows = TOP_K * tm

    @pl.when(i < pl.num_programs(0) - 1)
    def _():
        _start_run_copies(run_ref, run_ref.shape[2] // 3, lambda local, slot, size: pltpu.make_async_copy(
            y_hbm.at[_tile_rows(slot, size), :], ybuf.at[i % 2, _tile_rows(local, size), :], sem.at[i % 2]))

    @pl.when(i > 0)
    def _():
        slot = (i - 1) % 2
        pltpu.make_async_copy(y_hbm.at[pl.ds(0, n_rows * SUBLANES), :], ybuf.at[slot], sem.at[slot]).wait()
        y = _load_token_tiles(ybuf.at[slot], 0, n_rows).astype(BF16)
        pos, gate = pos_ref[...], tg_ref[...]
        col = lax.broadcasted_iota(jnp.int32, (tm, n_rows), 1).astype(F32)
        g = functools.reduce(jnp.add, [jnp.where(col == pos[:, j:j + 1], gate[:, j:j + 1], 0.0)
                                       for j in range(TOP_K)])
        g_hi, g_lo = _split_bf16(g, 2)
        o_ref[...] = x_ref[...] + mod_ref[...][5:6] * (_dot(g_hi, y) + _dot(g_lo, y))


def _combine(x, y_tiles, runs, pos, gates, mods, geom, stream_off):
    n_batch, n_ctx_blocks, blocks_per_batch = geom
    t = x.shape[0]
    tm = ROW_BLOCK
    nb = t // tm
    prev = lambda i: jnp.maximum(i - 1, 0)
    row = lambda w: pl.BlockSpec((tm, w), lambda i: (prev(i), 0))
    return pl.pallas_call(
        _combine_kernel,
        grid=(nb + 1,),
        in_specs=[pl.BlockSpec((1, 1, runs.shape[2]), lambda i: (jnp.minimum(i, nb - 1), 0, 0),
                               memory_space=pltpu.SMEM),
                  row(LANES), row(LANES), row(D_MODEL), pl.BlockSpec(memory_space=pl.ANY),
                  pl.BlockSpec((None, 6, D_MODEL), lambda i: (
                      _mod_row(prev(i) + stream_off, n_ctx_blocks, blocks_per_batch, n_batch), 0, 0))],
        out_specs=row(D_MODEL),
        out_shape=jax.ShapeDtypeStruct((t, D_MODEL), F32),
        scratch_shapes=[pltpu.VMEM((2, TOP_K * tm * SUBLANES, LANES), F32), pltpu.SemaphoreType.DMA((2,))],
        compiler_params=_cparams("arbitrary"),
        name="moe_combine",
    )(runs, pos, gates, x, y_tiles, mods)


def _route(top_e, n_experts, tok_block, slot_block):
    t, k = top_e.shape
    nb = t // tok_block
    onehot = (top_e[:, :, None] == jnp.arange(n_experts, dtype=jnp.int32)).astype(jnp.int32)
    n = onehot.reshape(nb, tok_block * k, n_experts).sum(axis=1)
    counts = n.sum(axis=0)
    padded = (counts + slot_block - 1) // slot_block * slot_block
    pend = jnp.cumsum(padded)
    start = pend - padded
    local0 = jnp.cumsum(n, axis=1) - n
    slot0 = start[None, :] + jnp.cumsum(n, axis=0) - n
    runs = jnp.concatenate([n, local0, slot0], axis=1).astype(jnp.int32).reshape(nb, 1, 3 * n_experts)
    run_first = jnp.pad(local0.astype(F32), ((0, 0), (0, LANES - n_experts))).reshape(nb, 1, LANES)
    n_blocks = (t * k + n_experts * (slot_block - 1) + slot_block - 1) // slot_block
    block_start = jnp.arange(n_blocks, dtype=jnp.int32) * slot_block
    block_e = jnp.minimum((block_start[:, None] >= pend[None, :]).sum(axis=1), n_experts - 1).astype(jnp.int32)
    n_used = (pend[-1] // slot_block).astype(jnp.int32).reshape(1)
    pad_slots = jnp.stack([start + counts, padded - counts], axis=0).astype(jnp.int32)
    return runs, run_first, block_e, n_used, pad_slots, n_blocks * slot_block


def _na_row_pattern(rb, n_rows):
    wr = min(NA_WIN_H, n_rows)
    n_rb = n_rows // NA_QROWS
    q_row = rb * NA_QROWS + np.arange(NA_QROWS)[:, None]
    k_row = np.clip(rb - 1, 0, n_rb - NA_KROWS // NA_QROWS) * NA_QROWS + np.arange(NA_KROWS)[None, :]
    row_start = np.clip(q_row - wr // 2, 0, n_rows - wr)
    ok = (k_row >= row_start) & (k_row < row_start + wr)
    return np.where(ok, k_row - q_row + (NA_WIN_H - 1), -1)


def _na_bias_table(rpb, seq):
    n_rows = seq // GRID_W
    n_rb = n_rows // NA_QROWS
    patterns = [_na_row_pattern(rb, n_rows) for rb in range(n_rb)]
    assert all((p == patterns[1]).all() for p in patterns[1:-1])
    q_col = np.arange(GRID_W)[:, None]
    k_col = np.arange(GRID_W)[None, :]
    col_start = np.clip(q_col - NA_WIN_W // 2, 0, GRID_W - NA_WIN_W)
    col_ok = (k_col >= col_start) & (k_col < col_start + NA_WIN_W)
    dc = np.clip(k_col - q_col + (NA_WIN_W - 1), 0, 2 * NA_WIN_W - 2)
    onehot = ((dc[None] == np.arange(2 * NA_WIN_W - 1)[:, None, None]) & col_ok[None]).astype(np.float32)
    by_col = jnp.einsum('hab,bqk->haqk', rpb.astype(F32), jnp.asarray(onehot), precision=HIGHEST)
    by_col = by_col + jnp.asarray(np.where(col_ok, 0.0, NEG_INF).astype(np.float32))
    masked = jnp.full((rpb.shape[0], GRID_W, GRID_W), NEG_INF, F32)
    variants = []
    for pattern in (patterns[0], patterns[1], patterns[-1]):
        rows = [jnp.concatenate([by_col[:, a] if a >= 0 else masked for a in pattern[qr]], axis=2)
                for qr in range(NA_QROWS)]
        variants.append(jnp.concatenate(rows, axis=1))
    return jnp.stack(variants, axis=0)


def _rope_tables(seq, ctx_len):
    quarter = HEAD_DIM // 4
    lane = np.arange(SWA_W)
    inv = ROPE_BASE ** (-(lane % quarter).astype(np.float64) / quarter)
    t = np.arange(seq)
    pos = np.where((lane % HEAD_DIM < HEAD_DIM // 2)[None, :], (t // GRID_W)[:, None], (t % GRID_W)[:, None])
    ang = jnp.asarray(pos, F32) * jnp.asarray(inv, F32)[None, :]
    sign = np.where(lane % (2 * quarter) < quarter, -1.0, 1.0).astype(np.float32)
    cos = jnp.concatenate([jnp.ones((ctx_len, SWA_W), F32), jnp.cos(ang)], axis=0)
    sin = jnp.concatenate([jnp.zeros((ctx_len, SWA_W), F32), jnp.sin(ang) * sign[None, :]], axis=0)
    return cos, sin


def kernel(x, c, ctx, c_ctx, hg_lower_bounds, ada_w, ada_b, norm1_g, norm2_g, w_in, na_q_norm, na_k_norm, na_rpb,
           hg_norm_g, swa_q_norm, swa_k_norm, swa_sink, w_out, router_w, router_b, w_gu, b_gu, w_down, b_down):
    n_batch, seq, d = x.shape
    ctx_len = ctx.shape[1]
    depth = ada_w.shape[0]
    assert d == D_MODEL and seq % ROW_BLOCK == 0 and ctx_len == ROW_BLOCK
    n_ctx_rows = n_batch * ctx_len
    geom = (n_batch, n_ctx_rows // ROW_BLOCK, seq // ROW_BLOCK)

    p_lb = jax.nn.softmax(hg_lower_bounds.astype(F32), axis=0)
    lbs = jnp.cumsum(p_lb, axis=0) - p_lb[0]

    cos_t, sin_t = _rope_tables(seq, ctx_len)
    lane = np.arange(NA_W)
    group_ones = jnp.asarray((lane[:, None] // HEAD_DIM == lane[None, :] // HEAD_DIM).astype(np.float32), BF16)
    tri_f = jnp.asarray(_hgrn_sum_table(HG_STEP_ROWS, False), BF16)
    tri_b = jnp.asarray(_hgrn_sum_table(HG_STEP_ROWS, True), BF16)
    ones_bf = jnp.ones((LANES, LANES), BF16)
    n_mod_rows = -(-(n_batch + 1) // 8) * 8
    cc = jnp.zeros((n_mod_rows, d), F32).at[:n_batch].set(c).at[n_batch].set(c_ctx)
    dup = lambda w: jnp.concatenate([w[:, :HEAD_DIM], w[:, :HEAD_DIM], w[:, HEAD_DIM:], w[:, HEAD_DIM:]], axis=1)
    tile4 = lambda g: jnp.tile(g.astype(F32), 4).reshape(1, 4 * HEAD_DIM)
    pad_e = LANES - N_EXPERTS

    x_all = jnp.concatenate([ctx.reshape(n_ctx_rows, d), x.reshape(n_batch * seq, d)], axis=0)
    for l in range(depth):
        last = l == depth - 1
        mods = _ada_mod(cc, ada_w[l], ada_b[l]).reshape(n_mod_rows, 6, d)
        w = w_in[l]
        kv0 = 3 * NA_W + HG_COLS + SWA_W
        w_ext = jnp.concatenate([w[:, :kv0], dup(w[:, kv0:kv0 + SWA_KV_W]), dup(w[:, kv0 + SWA_KV_W:])],
                                axis=1).astype(BF16)
        hg, naq, nak, nav, swq, swk, swv = _inproj(
            x_all, mods, norm1_g[l].reshape(1, d), w_ext, tile4(na_q_norm[l]), tile4(na_k_norm[l]),
            tile4(swa_q_norm[l]), tile4(swa_k_norm[l]), cos_t, sin_t, group_ones, lbs[l].reshape(1, HG_W), geom)

        y_na = _na_attention(naq, nak, nav, _na_bias_table(na_rpb[l], seq), geom)
        y_sw = _swa_attention(swq, swk, swv, swa_sink[l].astype(F32), geom, seq, ctx_len)
        o_f, o_b = _hgrn(hg, tri_f, tri_b, ones_bf, n_batch, seq, ctx_len)

        if last:
            stream_off = geom[1]
        else:
            stream_off = 0
            yc_na, yc_sw = _ctx_attention(naq, nak, nav, swq, swk, swv, swa_sink[l].astype(F32), n_batch, ctx_len)
            y_na = jnp.concatenate([yc_na, y_na], axis=0)
            y_sw = jnp.concatenate([yc_sw, y_sw], axis=0)

        rw_hi, rw_lo = _split_bf16(jnp.pad(router_w[l].astype(F32), ((0, 0), (0, pad_e))), 2)
        rw = jnp.concatenate([rw_hi, rw_hi, rw_lo], axis=0)
        rb = jnp.pad(router_b[l].astype(F32), (0, pad_e), constant_values=NEG_INF).reshape(1, LANES)
        x_new, h2, top_e, top_g = _mixout(
            x_all, y_na, o_f, o_b, hg, y_sw, hg_norm_g[l].reshape(1, HG_DK).astype(F32), w_out[l].astype(BF16),
            mods, norm2_g[l].reshape(1, d), rw, rb, geom, stream_off)

        runs, run_first, block_e, n_used, pad_slots, n_slots = _route(top_e[:, :TOP_K], N_EXPERTS, ROW_BLOCK, MOE_ROWS)
        xs, pos = _dispatch(h2, top_e, runs, run_first, pad_slots, n_used, n_slots)
        y_slots = _moe_ffn(xs, block_e, n_used, w_gu, b_gu, w_down, b_down, l)
        x_all = _combine(x_new, y_slots, runs, pos, top_g, mods, geom, stream_off)
    return x_all.reshape(n_batch, seq, d)
```

```python
import functools

import numpy as np
import jax
import jax.numpy as jnp
from jax import lax
from jax.experimental import pallas as pl
from jax.experimental.pallas import tpu as pltpu

D_MODEL = 1024
GRID_W = 64
HEAD_DIM = 64
ATTN_SCALE = HEAD_DIM ** -0.5
NA_HEADS = 4
NA_WIN_H = 8
NA_WIN_W = 16
HG_HEADS = 4
HG_DK = 128
SWA_Q_HEADS = 4
SWA_KV_HEADS = 2
SWA_WINDOW = 128
SWA_BLOCK = 128
ROPE_BASE = 10000.0
N_EXPERTS = 32
TOP_K = 4
D_FF = 1024
SWIGLU_LIMIT = 7.0
SWIGLU_ALPHA = 1.702
NORM_EPS = 1e-6
NEG_INF = -1e30

NA_W = NA_HEADS * HEAD_DIM
HG_W = HG_HEADS * HG_DK
SWA_W = SWA_Q_HEADS * HEAD_DIM
SWA_KV_W = SWA_KV_HEADS * HEAD_DIM
HG_COLS = 5 * HG_W
IN_COLS_EXT = 3 * NA_W + HG_COLS + 3 * SWA_W

LANES = 128
SUBLANES = 8
ROW_BLOCK = 256
NA_QROWS = 4
NA_KROWS = 12
NA_HEADS_PER_PASS = 1
SWA_HEADS_PER_PASS = 4
HG_SMALL_LEVELS = (2, 4, 8)
HG_STEP_ROWS = 128
HG_HEADS_PER_STEP = 4
MOE_ROWS = 256
RUN_BITS = ROW_BLOCK.bit_length()
RUN_COMMON_BITS = 6
VMEM_LIMIT = 56 * 1024 * 1024

F32 = jnp.float32
BF16 = jnp.bfloat16
HIGHEST = lax.Precision.HIGHEST


def _cparams(*sem):
    return pltpu.CompilerParams(dimension_semantics=sem, vmem_limit_bytes=VMEM_LIMIT)


def _dot(a, b, precision=None):
    return jnp.dot(a, b, preferred_element_type=F32, precision=precision)


def _dot_nt(a, b):
    return lax.dot_general(a, b, (((1,), (1,)), ((), ())), preferred_element_type=F32)


def _dot_tn(a, b):
    return lax.dot_general(a, b, (((0,), (0,)), ((), ())), preferred_element_type=F32)


def _store_token_tiles(ref, x):
    n = x.shape[0]
    for c in range(SUBLANES):
        ref[pl.ds(c, n, stride=SUBLANES), :] = x[:, c * LANES:(c + 1) * LANES]


def _load_token_tiles(ref, first_token, n):
    return jnp.concatenate([ref[pl.ds(first_token * SUBLANES + c, n, stride=SUBLANES), :] for c in range(SUBLANES)],
                           axis=1)


def _sigmoid(x):
    return 1.0 / (1.0 + jnp.exp(-x))


def _silu(x):
    return x * _sigmoid(x)


def _ada_kernel(c_ref, w_ref, b_ref, o_ref):
    o_ref[...] = _dot(_silu(c_ref[...]), w_ref[...], HIGHEST) + b_ref[...]


def _ada_mod(cc, w, b):
    m, d = cc.shape
    n = w.shape[1]
    tn = 1024
    return pl.pallas_call(
        _ada_kernel,
        grid=(n // tn,),
        in_specs=[pl.BlockSpec((m, d), lambda j: (0, 0)),
                  pl.BlockSpec((d, tn), lambda j: (0, j)),
                  pl.BlockSpec((1, tn), lambda j: (0, j))],
        out_specs=pl.BlockSpec((m, tn), lambda j: (0, j)),
        out_shape=jax.ShapeDtypeStruct((m, n), F32),
        compiler_params=_cparams("parallel"),
        name="ada_mod",
    )(cc, w, b.reshape(1, n))


def _mod_rmsnorm(x, g, mod, shift_row, scale_row):
    y = x * lax.rsqrt(jnp.mean(x * x, axis=-1, keepdims=True) + NORM_EPS) * g
    return y * (1.0 + mod[scale_row:scale_row + 1]) + mod[shift_row:shift_row + 1]


def _split_bf16(x, terms):
    parts = []
    for _ in range(terms - 1):
        parts.append(x.astype(BF16))
        x = x - parts[-1].astype(F32)
    return parts + [x.astype(BF16)]


def _head_rmsnorm(x, w, group_ones):
    hi, lo = _split_bf16(x * x, 2)
    n = x.shape[0]
    ss = _dot(jnp.concatenate([hi, lo], axis=0), group_ones)
    return x * lax.rsqrt((ss[:n] + ss[n:]) * (1.0 / HEAD_DIM) + NORM_EPS) * w


def _rope(x, cos, sin_signed):
    n = x.shape[-1]
    lane = lax.broadcasted_iota(jnp.int32, x.shape, 1)
    quarter = HEAD_DIM // 4
    partner = jnp.where(lane % (2 * quarter) < quarter,
                        pltpu.roll(x, n - quarter, 1), pltpu.roll(x, quarter, 1))
    return x * cos + partner * sin_signed


def _inproj_kernel(x_ref, mod_ref, g_ref, w_ref, naq_w, nak_w, swq_w, swk_w, cos_ref, sin_ref, ones_ref, lb_ref,
                   hg_ref, naq_ref, nak_ref, nav_ref, swq_ref, swk_ref, swv_ref):
    h = _mod_rmsnorm(x_ref[...], g_ref[...], mod_ref[...], 0, 1).astype(BF16)
    ones = ones_ref[...]

    def proj(lo, width):
        return _dot(h, w_ref[:, lo:lo + width])

    naq_ref[...] = (_head_rmsnorm(proj(0, NA_W), naq_w[...], ones) * ATTN_SCALE).astype(BF16)
    nak_ref[...] = _head_rmsnorm(proj(NA_W, NA_W), nak_w[...], ones).astype(BF16)
    nav_ref[...] = proj(2 * NA_W, NA_W).astype(BF16)
    base = 3 * NA_W
    lb = lb_ref[...]
    hg_ref[:, :HG_W] = _silu(proj(base, HG_W)) * (HG_DK ** -0.5)
    for j in (1, 2):
        hg_ref[:, j * HG_W:(j + 1) * HG_W] = jnp.log2(lb + (1.0 - lb) * _sigmoid(proj(base + j * HG_W, HG_W)))
    for j in (3, 4):
        hg_ref[:, j * HG_W:(j + 1) * HG_W] = proj(base + j * HG_W, HG_W)
    base += HG_COLS
    cos, sin = cos_ref[...], sin_ref[...]
    swq = _rope(_head_rmsnorm(proj(base, SWA_W), swq_w[...], ones), cos, sin)
    swq_ref[...] = (swq * ATTN_SCALE).astype(BF16)
    swk_ref[...] = _rope(_head_rmsnorm(proj(base + SWA_W, SWA_W), swk_w[...], ones), cos, sin).astype(BF16)
    swv_ref[...] = proj(base + 2 * SWA_W, SWA_W).astype(BF16)


def _mod_row(i, n_ctx_blocks, blocks_per_batch, n_batch):
    return jnp.where(i < n_ctx_blocks, n_batch, (i - n_ctx_blocks) // blocks_per_batch)


def _inproj(x_all, mods, norm_g, w_ext, naq_w, nak_w, swq_w, swk_w, cos_t, sin_t, ones, lb, geom):
    n_batch, n_ctx_blocks, blocks_per_batch = geom
    t = x_all.shape[0]
    tm = ROW_BLOCK
    row = lambda w: pl.BlockSpec((tm, w), lambda i: (i, 0))
    const = lambda a: pl.BlockSpec(a.shape, lambda i: (0,) * a.ndim)
    rope_blk = lambda i: (jnp.where(i < n_ctx_blocks, 0, 1 + (i - n_ctx_blocks) % blocks_per_batch), 0)
    bf = lambda w: jax.ShapeDtypeStruct((t, w), BF16)
    return pl.pallas_call(
        _inproj_kernel,
        grid=(t // tm,),
        in_specs=[row(D_MODEL),
                  pl.BlockSpec((None, 6, D_MODEL),
                               lambda i: (_mod_row(i, n_ctx_blocks, blocks_per_batch, n_batch), 0, 0)),
                  const(norm_g), const(w_ext), const(naq_w), const(nak_w), const(swq_w), const(swk_w),
                  pl.BlockSpec((tm, SWA_W), rope_blk), pl.BlockSpec((tm, SWA_W), rope_blk), const(ones),
                  const(lb)],
        out_specs=[row(HG_COLS), row(NA_W), row(NA_W), row(NA_W), row(SWA_W), row(SWA_W), row(SWA_W)],
        out_shape=[jax.ShapeDtypeStruct((t, HG_COLS), F32), bf(NA_W), bf(NA_W), bf(NA_W),
                   bf(SWA_W), bf(SWA_W), bf(SWA_W)],
        compiler_params=_cparams("parallel"),
        name="inproj",
    )(x_all, mods, norm_g, w_ext, naq_w, nak_w, swq_w, swk_w, cos_t, sin_t, ones, lb)


def _attend(q, parts, sink_vals, heads_per_pass):
    m_rows, width = q.shape
    n_heads = width // HEAD_DIM
    lane = lax.broadcasted_iota(jnp.int32, (m_rows, width), 1)
    out = jnp.zeros((m_rows, width), F32)
    for h0 in range(0, n_heads, heads_per_pass):
        heads = range(h0, h0 + heads_per_pass)
        rows = slice(h0 * m_rows, (h0 + heads_per_pass) * m_rows)
        in_head = [(lane // HEAD_DIM) == h for h in heads]
        q_stack = jnp.concatenate([jnp.where(m, q, jnp.zeros_like(q)) for m in in_head], axis=0)
        scores = []
        for k, _, add in parts:
            s = _dot_nt(q_stack, k)
            scores.append(s if add is None else s + add[rows])
        mx = functools.reduce(jnp.maximum, [jnp.max(s, axis=-1, keepdims=True) for s in scores])
        if sink_vals is not None:
            head_row = lax.broadcasted_iota(jnp.int32, (heads_per_pass * m_rows, 1), 0) // m_rows + h0
            sink = functools.reduce(jnp.add, [jnp.where(head_row == h, sink_vals[h], 0.0) for h in heads])
            mx = jnp.maximum(mx, sink)
        ps = [jnp.exp(s - mx) for s in scores]
        den = functools.reduce(jnp.add, [jnp.sum(p, axis=-1, keepdims=True) for p in ps])
        if sink_vals is not None:
            den = den + jnp.exp(sink - mx)
        acc = functools.reduce(jnp.add, [_dot(p.astype(BF16), v) for p, (_, v, _) in zip(ps, parts)]) / den
        for i, m in enumerate(in_head):
            out = jnp.where(m, acc[i * m_rows:(i + 1) * m_rows], out)
    return out


def _na_kernel(q_ref, k0, k1, k2, kc, v0, v1, v2, vc, bias_ref, o_ref):
    k_lat = jnp.concatenate([k0[...], k1[...], k2[...]], axis=0)
    v_lat = jnp.concatenate([v0[...], v1[...], v2[...]], axis=0)
    bias = bias_ref[...]
    bias = bias.reshape(bias.shape[0] * bias.shape[1], bias.shape[2])
    o = _attend(q_ref[...], [(k_lat, v_lat, bias), (kc[...], vc[...], None)], None, NA_HEADS_PER_PASS)
    o_ref[...] = o.astype(o_ref.dtype)


def _na_attention(naq, nak, nav, bias, geom):
    n_batch, n_ctx_blocks, blocks_per_batch = geom
    tm = ROW_BLOCK
    n_rb = blocks_per_batch
    kb_max = n_rb - NA_KROWS // NA_QROWS
    lat = lambda rb, b: n_ctx_blocks + b * blocks_per_batch

    def band(j):
        return pl.BlockSpec((tm, NA_W), lambda rb, b: (lat(rb, b) + jnp.clip(rb - 1, 0, kb_max) + j, 0))

    ctx = pl.BlockSpec((tm, NA_W), lambda rb, b: (b, 0))
    return pl.pallas_call(
        _na_kernel,
        grid=(n_rb, n_batch),
        in_specs=[pl.BlockSpec((tm, NA_W), lambda rb, b: (lat(rb, b) + rb, 0)),
                  band(0), band(1), band(2), ctx, band(0), band(1), band(2), ctx,
                  pl.BlockSpec((None, NA_HEADS, tm, NA_KROWS * GRID_W), lambda rb, b: (
                      jnp.where(rb == 0, 0, jnp.where(rb == n_rb - 1, 2, 1)), 0, 0, 0))],
        out_specs=pl.BlockSpec((tm, NA_W), lambda rb, b: (b * blocks_per_batch + rb, 0)),
        out_shape=jax.ShapeDtypeStruct((n_batch * blocks_per_batch * tm, NA_W), BF16),
        compiler_params=_cparams("parallel", "parallel"),
        name="na_attn",
    )(naq, nak, nak, nak, nak, nav, nav, nav, nav, bias)


def _swa_kernel(sink_ref, q_ref, kp, kc_, kn, kx, vp, vc_, vn, vx, o_ref, *, n_blocks):
    n = pl.program_id(1)
    tq = q_ref.shape[0]
    k_lat = jnp.concatenate([kp[...], kc_[...], kn[...]], axis=0)
    v_lat = jnp.concatenate([vp[...], vc_[...], vn[...]], axis=0)
    qi = lax.broadcasted_iota(jnp.int32, (SWA_Q_HEADS * tq, 3 * tq), 0) % tq
    mi = lax.broadcasted_iota(jnp.int32, (SWA_Q_HEADS * tq, 3 * tq), 1)
    kpos = (n - 1) * tq + mi
    ok = (jnp.abs(qi + tq - mi) <= SWA_WINDOW) & (kpos >= 0) & (kpos < n_blocks * tq)
    add = jnp.where(ok, 0.0, NEG_INF).astype(F32)
    sinks = [sink_ref[h] for h in range(SWA_Q_HEADS)]
    o = _attend(q_ref[...], [(k_lat, v_lat, add), (kx[...], vx[...], None)], sinks, SWA_HEADS_PER_PASS)
    o_ref[...] = o.astype(o_ref.dtype)


def _swa_attention(swq, swk, swv, sink, geom, seq, ctx_len):
    n_batch, _, _ = geom
    tq = SWA_BLOCK
    nb = seq // tq
    first = n_batch * ctx_len // tq
    blk = lambda f: pl.BlockSpec((tq, SWA_W), lambda b, n: (first + b * nb + f(n), 0))
    prev, cur, nxt = blk(lambda n: jnp.maximum(n - 1, 0)), blk(lambda n: n), blk(lambda n: jnp.minimum(n + 1, nb - 1))
    ctx = pl.BlockSpec((ctx_len, SWA_W), lambda b, n: (b, 0))
    return pl.pallas_call(
        functools.partial(_swa_kernel, n_blocks=nb),
        grid=(n_batch, nb),
        in_specs=[pl.BlockSpec(memory_space=pltpu.SMEM), cur, prev, cur, nxt, ctx, prev, cur, nxt, ctx],
        out_specs=pl.BlockSpec((tq, SWA_W), lambda b, n: (b * nb + n, 0)),
        out_shape=jax.ShapeDtypeStruct((n_batch * seq, SWA_W), BF16),
        compiler_params=_cparams("parallel", "parallel"),
        name="swa_attn",
    )(sink, swq, swk, swk, swk, swk, swv, swv, swv, swv)


def _ctx_attn_kernel(sink_ref, naq, nak, nav, swq, swk, swv, ona_ref, osw_ref):
    ona_ref[...] = _attend(naq[...], [(nak[...], nav[...], None)], None, NA_HEADS_PER_PASS).astype(ona_ref.dtype)
    sinks = [sink_ref[h] for h in range(SWA_Q_HEADS)]
    osw_ref[...] = _attend(swq[...], [(swk[...], swv[...], None)], sinks, NA_HEADS_PER_PASS).astype(osw_ref.dtype)


def _ctx_attention(naq, nak, nav, swq, swk, swv, sink, n_batch, ctx_len):
    blk = pl.BlockSpec((ctx_len, NA_W), lambda b: (b, 0))
    out = jax.ShapeDtypeStruct((n_batch * ctx_len, NA_W), BF16)
    return pl.pallas_call(
        _ctx_attn_kernel,
        grid=(n_batch,),
        in_specs=[pl.BlockSpec(memory_space=pltpu.SMEM)] + [blk] * 6,
        out_specs=[blk, blk],
        out_shape=[out, out],
        compiler_params=_cparams("parallel"),
        name="ctx_attn",
    )(sink, naq, nak, nav, swq, swk, swv)


def _dot_exact_lhs(m, x):
    out = _dot(m, jnp.concatenate(_split_bf16(x, 3), axis=1))
    n = x.shape[1]
    return out[:, :n] + out[:, n:2 * n] + out[:, 2 * n:]


def _hgrn_sum_table(n_rows, reverse):
    tri = np.tril(np.ones((n_rows, n_rows), np.float32))
    if reverse:
        tri = tri.T
    t = np.arange(n_rows)
    groups = [tri] + [tri[t // size * size + (size // 2 if reverse else size // 2 - 1)] for size in HG_SMALL_LEVELS]
    return np.concatenate(groups, axis=0)


def _hgrn_block(q, k, g, v, st, sums, ones, reverse):
    n_rows = q.shape[0]
    all_sums = _dot_exact_lhs(sums, g)
    cum = all_sums[:n_rows]
    tot = cum[0:1] if reverse else cum[n_rows - 1:n_rows]
    o_inter = _dot_nt((q * jnp.exp2(cum)).astype(BF16), st.astype(BF16))
    k_end = (k * jnp.exp2(tot - cum)).astype(BF16)
    st_new = st * jnp.exp2(tot) + _dot_tn(v.astype(BF16), k_end)

    row = lax.broadcasted_iota(jnp.int32, (n_rows, LANES), 0)
    row_a = lax.broadcasted_iota(jnp.int32, (n_rows, n_rows), 0)
    col_a = lax.broadcasted_iota(jnp.int32, (n_rows, n_rows), 1)
    a = jnp.where(row_a == col_a, _dot((q * k).astype(BF16), ones), 0.0)
    zero_tile = jnp.zeros((SUBLANES, LANES), F32)
    size = 2
    while size <= n_rows:
        half = size // 2
        if size in HG_SMALL_LEVELS:
            level = HG_SMALL_LEVELS.index(size) + 1
            ref = all_sums[level * n_rows:(level + 1) * n_rows]
            later = (row % size < half) if reverse else (row % size >= half)
            q_t = (q * jnp.exp2(jnp.where(later, cum - ref, NEG_INF))).astype(BF16)
            k_t = (k * jnp.exp2(jnp.where(later, NEG_INF, ref - cum))).astype(BF16)
        else:
            q_tiles, k_tiles = [], []
            for r0 in range(0, n_rows, SUBLANES):
                sl = slice(r0, r0 + SUBLANES)
                first = r0 // size * size
                ref_row = first + (half if reverse else half - 1)
                ref = cum[ref_row:ref_row + 1]
                if (r0 - first < half) if reverse else (r0 - first >= half):
                    q_tiles.append(q[sl] * jnp.exp2(cum[sl] - ref))
                    k_tiles.append(zero_tile)
                else:
                    q_tiles.append(zero_tile)
                    k_tiles.append(k[sl] * jnp.exp2(ref - cum[sl]))
            q_t = jnp.concatenate(q_tiles, axis=0).astype(BF16)
            k_t = jnp.concatenate(k_tiles, axis=0).astype(BF16)
        a_l = _dot_nt(q_t, k_t)
        a = a + (a_l if size == n_rows else jnp.where(row_a // size == col_a // size, a_l, 0.0))
        size *= 2
    return o_inter + _dot(a.astype(BF16), v.astype(BF16)), st_new


def _hgrn_kernel(q_f, g_f, v_f, q_b, g_b, v_b, trif_ref, trib_ref, ones_ref, of_ref, ob_ref, st_f, st_b):
    @pl.when(pl.program_id(2) == 0)
    def _():
        st_f[...] = jnp.zeros_like(st_f)
        st_b[...] = jnp.zeros_like(st_b)

    ones = ones_ref[...]
    for h in range(HG_HEADS_PER_STEP):
        sl = slice(h * HG_DK, (h + 1) * HG_DK)
        g = g_f[:, sl]
        of_ref[:, sl], st_f[h] = _hgrn_block(q_f[:, sl], 1.0 - jnp.exp2(g), g, v_f[:, sl], st_f[h],
                                             trif_ref[...], ones, False)
        g = g_b[:, sl]
        ob_ref[:, sl], st_b[h] = _hgrn_block(q_b[:, sl], 1.0 - jnp.exp2(g), g, v_b[:, sl], st_b[h],
                                             trib_ref[...], ones, True)


def _hgrn(hg, tri_f, tri_b, ones, n_batch, seq, ctx_len):
    tr = HG_STEP_ROWS
    nc, nl = ctx_len // tr, seq // tr
    first = n_batch * nc

    def fwd(b, j):
        return jnp.where(j < nc, b * nc + j, first + b * nl + (j - nc))

    def bwd(b, j):
        return jnp.where(j < nc, b * nc + (nc - 1 - j), first + b * nl + (nl - 1 - (j - nc)))

    hps = HG_HEADS_PER_STEP
    groups = HG_HEADS // hps
    wide = hps * HG_DK
    col = lambda row_fn, group: pl.BlockSpec((tr, wide), lambda b, h, j: (row_fn(b, j), group * groups + h))
    const = lambda a: pl.BlockSpec(a.shape, lambda b, h, j: (0, 0))
    out = jax.ShapeDtypeStruct((hg.shape[0], HG_W), F32)
    return pl.pallas_call(
        _hgrn_kernel,
        grid=(n_batch, groups, nc + nl),
        in_specs=[col(fwd, 0), col(fwd, 1), col(fwd, 3), col(bwd, 0), col(bwd, 2), col(bwd, 3),
                  const(tri_f), const(tri_b), const(ones)],
        out_specs=[pl.BlockSpec((tr, wide), lambda b, h, j: (fwd(b, j), h)),
                   pl.BlockSpec((tr, wide), lambda b, h, j: (bwd(b, j), h))],
        out_shape=[out, out],
        scratch_shapes=[pltpu.VMEM((hps, HG_DK, HG_DK), F32), pltpu.VMEM((hps, HG_DK, HG_DK), F32)],
        compiler_params=_cparams("parallel", "parallel", "arbitrary"),
        name="hgrn",
    )(hg, hg, hg, hg, hg, hg, tri_f, tri_b, ones)


def _mixout_kernel(x_ref, yna_ref, of_ref, ob_ref, zg_ref, ysw_ref, hgn_ref, w_ref, mod_ref, g2_ref,
                   rw_ref, rb_ref, xo_ref, h2_ref, te_ref, tg_ref):
    o = of_ref[...] + ob_ref[...]
    zg = zg_ref[...]
    parts = [yna_ref[...]]
    for h in range(HG_HEADS):
        sl = slice(h * HG_DK, (h + 1) * HG_DK)
        oh = o[:, sl]
        yh = oh * lax.rsqrt(jnp.mean(oh * oh, axis=-1, keepdims=True) + NORM_EPS) * hgn_ref[...]
        parts.append((yh * _silu(zg[:, sl])).astype(BF16))
    parts.append(ysw_ref[...])
    y = _dot(jnp.concatenate(parts, axis=1), w_ref[...])
    mod = mod_ref[...]
    x_new = x_ref[...] + mod[2:3] * y
    xo_ref[...] = x_new
    h2 = _mod_rmsnorm(x_new, g2_ref[...], mod, 3, 4)
    h2_ref[...] = h2.astype(BF16)

    h_hi, h_lo = _split_bf16(h2, 2)
    logits = _dot(jnp.concatenate([h_hi, h_lo, h_hi], axis=1), rw_ref[...]) + rb_ref[...]
    lane = lax.broadcasted_iota(jnp.int32, logits.shape, 1).astype(F32)
    top_e = jnp.zeros(logits.shape, F32)
    top_v = jnp.full(logits.shape, NEG_INF, F32)
    for j in range(TOP_K):
        best = jnp.max(logits, axis=-1, keepdims=True)
        arg = jnp.min(jnp.where(logits == best, lane, float(LANES)), axis=-1, keepdims=True)
        top_e = jnp.where(lane == j, arg, top_e)
        top_v = jnp.where(lane == j, best, top_v)
        logits = jnp.where(lane == arg, -jnp.inf, logits)
    ex = jnp.exp(top_v - jnp.max(top_v, axis=-1, keepdims=True))
    te_ref[...] = top_e.astype(jnp.int32)
    tg_ref[...] = ex / jnp.sum(ex, axis=-1, keepdims=True)


def _mixout(x, yna, o_f, o_b, hg, ysw, hgn, w_out, mods, norm2_g, rw, rb, geom, stream_off):
    n_batch, n_ctx_blocks, blocks_per_batch = geom
    tm = ROW_BLOCK
    t = x.shape[0] - stream_off * tm
    row = lambda w: pl.BlockSpec((tm, w), lambda i: (i, 0))
    full = lambda w, cb: pl.BlockSpec((tm, w), lambda i: (i + stream_off, cb))
    const = lambda a: pl.BlockSpec(a.shape, lambda i: (0,) * a.ndim)
    return pl.pallas_call(
        _mixout_kernel,
        grid=(t // tm,),
        in_specs=[full(D_MODEL, 0), row(NA_W), full(HG_W, 0), full(HG_W, 0), full(HG_W, 4), row(SWA_W),
                  const(hgn), const(w_out),
                  pl.BlockSpec((None, 6, D_MODEL), lambda i: (
                      _mod_row(i + stream_off, n_ctx_blocks, blocks_per_batch, n_batch), 0, 0)),
                  const(norm2_g), const(rw), const(rb)],
        out_specs=[row(D_MODEL), row(D_MODEL), row(LANES), row(LANES)],
        out_shape=[jax.ShapeDtypeStruct((t, D_MODEL), F32), jax.ShapeDtypeStruct((t, D_MODEL), BF16),
                   jax.ShapeDtypeStruct((t, LANES), jnp.int32), jax.ShapeDtypeStruct((t, LANES), F32)],
        compiler_params=_cparams("parallel"),
        name="mixout",
    )(x, yna, o_f, o_b, hg, ysw, hgn, w_out, mods, norm2_g, rw, rb)


def _start_run_copies(run_ref, n_experts, make_copy):
    for e in range(n_experts):
        n = run_ref[0, 0, e]
        local0 = run_ref[0, 0, n_experts + e]
        slot0 = run_ref[0, 0, 2 * n_experts + e]

        def pieces(bits):
            for bit in bits:
                @pl.when((n & (1 << bit)) != 0)
                def _():
                    done = (n >> (bit + 1)) << (bit + 1)
                    make_copy(local0 + done, slot0 + done, 1 << bit).start()

        @pl.when(n >= (1 << RUN_COMMON_BITS))
        def _():
            pieces(reversed(range(RUN_COMMON_BITS, RUN_BITS)))

        pieces(reversed(range(RUN_COMMON_BITS)))


def _tile_rows(first_token, n_tokens):
    return pl.ds(pl.multiple_of(first_token * SUBLANES, SUBLANES), n_tokens * SUBLANES)


def _local_positions(te_ref, off_ref, stri_ref):
    e_tok = te_ref[...]
    lane = lax.broadcasted_iota(jnp.int32, e_tok.shape, 1)
    member = [lane == e_tok[:, j:j + 1] for j in range(TOP_K)]
    count = functools.reduce(jnp.add, [jnp.where(m, 1.0, 0.0) for m in member])
    before = _dot(stri_ref[...], count.astype(BF16)) + off_ref[...]
    return [jnp.sum(jnp.where(m, before, 0.0), axis=1, keepdims=True) for m in member]


def _dispatch_kernel(pad_ref, nu_ref, run_ref, te_ref, off_ref, h_ref, stri_ref, xs_hbm, pos_ref,
                     zeros, xloc, sem, zero_sem):
    n_tok = h_ref.shape[0]
    block_rows = zeros.shape[0]
    n_blocks = xs_hbm.shape[0] // block_rows

    def zero_slot(e, r):
        row = pl.multiple_of((pad_ref[0, e] + r) * SUBLANES, SUBLANES)
        return pltpu.make_async_copy(zeros.at[pl.ds(0, SUBLANES), :], xs_hbm.at[pl.ds(row, SUBLANES), :], zero_sem)

    def zero_block(b):
        row = pl.multiple_of(b * block_rows, block_rows)
        return pltpu.make_async_copy(zeros, xs_hbm.at[pl.ds(row, block_rows), :], zero_sem)

    def for_each_unused(slot_fn, block_fn):
        def per_expert(e, carry):
            lax.fori_loop(0, pad_ref[1, e], lambda r, c: (slot_fn(e, r), c)[1], 0)
            return carry

        lax.fori_loop(0, pad_ref.shape[1], per_expert, 0)
        lax.fori_loop(nu_ref[0], n_blocks, lambda b, c: (block_fn(b), c)[1], 0)

    @pl.when(pl.program_id(0) == 0)
    def _():
        zeros[...] = jnp.zeros_like(zeros)
        for_each_unused(lambda e, r: zero_slot(e, r).start(), lambda b: zero_block(b).start())
        for_each_unused(lambda e, r: zero_slot(e, r).wait(), lambda b: zero_block(b).wait())

    pos = _local_positions(te_ref, off_ref, stri_ref)
    lane = lax.broadcasted_iota(jnp.int32, (n_tok, LANES), 1)
    pos_ref[...] = functools.reduce(jnp.add, [jnp.where(lane == j, p, 0.0) for j, p in enumerate(pos)])
    col = lax.broadcasted_iota(jnp.int32, (n_tok, TOP_K * n_tok), 1).astype(F32)
    chosen = functools.reduce(jnp.logical_or, [col == p for p in pos])
    sorted_rows = _dot_tn(jnp.where(chosen, 1.0, 0.0).astype(BF16), h_ref[...])
    i = pl.program_id(0)
    buf = i % 2
    _store_token_tiles(xloc.at[buf], sorted_rows)

    def wait_copies(b):
        pltpu.make_async_copy(xloc.at[b], xs_hbm.at[pl.ds(0, xloc.shape[1]), :], sem.at[b]).wait()

    @pl.when(i > 0)
    def _():
        wait_copies(1 - buf)

    _start_run_copies(run_ref, pad_ref.shape[1], lambda local, slot, size: pltpu.make_async_copy(
        xloc.at[buf, _tile_rows(local, size), :], xs_hbm.at[_tile_rows(slot, size), :], sem.at[buf]))

    @pl.when(i == pl.num_programs(0) - 1)
    def _():
        wait_copies(buf)


def _dispatch(h, top_e, runs, run_first, pad_slots, n_used, n_slots):
    tm = ROW_BLOCK
    t = h.shape[0]
    nb = t // tm
    assert n_slots % MOE_ROWS == 0
    stri = jnp.asarray(np.tril(np.ones((tm, tm), np.float32), -1), BF16)
    row = lambda w: pl.BlockSpec((tm, w), lambda i: (i, 0))
    return pl.pallas_call(
        _dispatch_kernel,
        grid=(nb,),
        in_specs=[pl.BlockSpec(memory_space=pltpu.SMEM), pl.BlockSpec(memory_space=pltpu.SMEM),
                  pl.BlockSpec((1, 1, runs.shape[2]), lambda i: (i, 0, 0), memory_space=pltpu.SMEM),
                  row(LANES), pl.BlockSpec((None, 1, LANES), lambda i: (i, 0, 0)), row(D_MODEL),
                  pl.BlockSpec((tm, tm), lambda i: (0, 0))],
        out_specs=[pl.BlockSpec(memory_space=pl.ANY), row(LANES)],
        out_shape=[jax.ShapeDtypeStruct((n_slots * SUBLANES, LANES), F32), jax.ShapeDtypeStruct((t, LANES), F32)],
        scratch_shapes=[pltpu.VMEM((MOE_ROWS * SUBLANES, LANES), F32),
                        pltpu.VMEM((2, TOP_K * tm * SUBLANES, LANES), F32), pltpu.SemaphoreType.DMA((2,)),
                        pltpu.SemaphoreType.DMA(())],
        compiler_params=_cparams("arbitrary"),
        name="moe_dispatch",
    )(pad_slots, n_used, runs, top_e, run_first, h, stri)


def _moe_kernel(be_ref, nu_ref, xs_ref, wgu_ref, bgu_ref, wd_ref, bd_ref, y_ref, wgu_bf, wd_bf):
    i = pl.program_id(0)
    live = i < nu_ref[0]
    tm = y_ref.shape[0] // SUBLANES
    changed = (i == 0) | (be_ref[i] != be_ref[jnp.maximum(i - 1, 0)])

    @pl.when(live & changed)
    def _():
        wgu_bf[...] = wgu_ref[...].astype(BF16)
        wd_bf[...] = wd_ref[...].astype(BF16)

    @pl.when(live)
    def _():
        x = _load_token_tiles(xs_ref, 0, tm)
        gu = _dot(x.astype(BF16), wgu_bf[...]) + bgu_ref[...]
        glu = jnp.minimum(gu[:, :D_FF], SWIGLU_LIMIT)
        lin = jnp.clip(gu[:, D_FF:], -SWIGLU_LIMIT, SWIGLU_LIMIT)
        act = glu * _sigmoid(SWIGLU_ALPHA * glu) * (lin + 1.0)
        _store_token_tiles(y_ref, _dot(act.astype(BF16), wd_bf[...]) + bd_ref[...])

    @pl.when(jnp.logical_not(live))
    def _():
        y_ref[...] = jnp.zeros_like(y_ref)


def _moe_ffn(xs_tiles, block_e, n_used, w_gu, b_gu, w_down, b_down, layer):
    tm = MOE_ROWS
    nb = xs_tiles.shape[0] // (tm * SUBLANES)
    _, n_e, d, f2 = w_gu.shape
    weight = lambda r, c: pl.BlockSpec((None, None, r, c), lambda i, be, nu: (layer, be[i], 0, 0))
    grid_spec = pltpu.PrefetchScalarGridSpec(
        num_scalar_prefetch=2,
        grid=(nb,),
        in_specs=[pl.BlockSpec((tm * SUBLANES, LANES), lambda i, be, nu: (jnp.minimum(i, nu[0] - 1), 0)),
                  weight(d, f2), weight(1, f2), weight(f2 // 2, d), weight(1, d)],
        out_specs=pl.BlockSpec((tm * SUBLANES, LANES), lambda i, be, nu: (i, 0)),
        scratch_shapes=[pltpu.VMEM((d, f2), BF16), pltpu.VMEM((f2 // 2, d), BF16)],
    )
    n_l = w_gu.shape[0]
    return pl.pallas_call(
        _moe_kernel,
        grid_spec=grid_spec,
        out_shape=jax.ShapeDtypeStruct((nb * tm * SUBLANES, LANES), F32),
        compiler_params=_cparams("arbitrary"),
        name="moe_ffn",
    )(block_e, n_used, xs_tiles, w_gu, b_gu.reshape(n_l, n_e, 1, f2), w_down,
      b_down.reshape(n_l, n_e, 1, d))


def _combine_kernel(run_ref, pos_ref, tg_ref, x_ref, y_hbm, mod_ref, o_ref, ybuf, sem):
    i = pl.program_id(0)
    tm = o_ref.shape[0]
    n_rows = TOP_K * tm

    @pl.when(i < pl.num_programs(0) - 1)
    def _():
        _start_run_copies(run_ref, run_ref.shape[2] // 3, lambda local, slot, size: pltpu.make_async_copy(
            y_hbm.at[_tile_rows(slot, size), :], ybuf.at[i % 2, _tile_rows(local, size), :], sem.at[i % 2]))

    @pl.when(i > 0)
    def _():
        slot = (i - 1) % 2
        pltpu.make_async_copy(y_hbm.at[pl.ds(0, n_rows * SUBLANES), :], ybuf.at[slot], sem.at[slot]).wait()
        y = _load_token_tiles(ybuf.at[slot], 0, n_rows).astype(BF16)
        pos, gate = pos_ref[...], tg_ref[...]
        col = lax.broadcasted_iota(jnp.int32, (tm, n_rows), 1).astype(F32)
        g = functools.reduce(jnp.add, [jnp.where(col == pos[:, j:j + 1], gate[:, j:j + 1], 0.0)
                                       for j in range(TOP_K)])
        g_hi, g_lo = _split_bf16(g, 2)
        o_ref[...] = x_ref[...] + mod_ref[...][5:6] * (_dot(g_hi, y) + _dot(g_lo, y))


def _combine(x, y_tiles, runs, pos, gates, mods, geom, stream_off):
    n_batch, n_ctx_blocks, blocks_per_batch = geom
    t = x.shape[0]
    tm = ROW_BLOCK
    nb = t // tm
    prev = lambda i: jnp.maximum(i - 1, 0)
    row = lambda w: pl.BlockSpec((tm, w), lambda i: (prev(i), 0))
    return pl.pallas_call(
        _combine_kernel,
        grid=(nb + 1,),
        in_specs=[pl.BlockSpec((1, 1, runs.shape[2]), lambda i: (jnp.minimum(i, nb - 1), 0, 0),
                               memory_space=pltpu.SMEM),
                  row(LANES), row(LANES), row(D_MODEL), pl.BlockSpec(memory_space=pl.ANY),
                  pl.BlockSpec((None, 6, D_MODEL), lambda i: (
                      _mod_row(prev(i) + stream_off, n_ctx_blocks, blocks_per_batch, n_batch), 0, 0))],
        out_specs=row(D_MODEL),
        out_shape=jax.ShapeDtypeStruct((t, D_MODEL), F32),
        scratch_shapes=[pltpu.VMEM((2, TOP_K * tm * SUBLANES, LANES), F32), pltpu.SemaphoreType.DMA((2,))],
        compiler_params=_cparams("arbitrary"),
        name="moe_combine",
    )(runs, pos, gates, x, y_tiles, mods)


def _route(top_e, n_experts, tok_block, slot_block):
    t, k = top_e.shape
    nb = t // tok_block
    onehot = (top_e[:, :, None] == jnp.arange(n_experts, dtype=jnp.int32)).astype(jnp.int32)
    n = onehot.reshape(nb, tok_block * k, n_experts).sum(axis=1)
    counts = n.sum(axis=0)
    padded = (counts + slot_block - 1) // slot_block * slot_block
    pend = jnp.cumsum(padded)
    start = pend - padded
    local0 = jnp.cumsum(n, axis=1) - n
    slot0 = start[None, :] + jnp.cumsum(n, axis=0) - n
    runs = jnp.concatenate([n, local0, slot0], axis=1).astype(jnp.int32).reshape(nb, 1, 3 * n_experts)
    run_first = jnp.pad(local0.astype(F32), ((0, 0), (0, LANES - n_experts))).reshape(nb, 1, LANES)
    n_blocks = (t * k + n_experts * (slot_block - 1) + slot_block - 1) // slot_block
    block_start = jnp.arange(n_blocks, dtype=jnp.int32) * slot_block
    block_e = jnp.minimum((block_start[:, None] >= pend[None, :]).sum(axis=1), n_experts - 1).astype(jnp.int32)
    n_used = (pend[-1] // slot_block).astype(jnp.int32).reshape(1)
    pad_slots = jnp.stack([start + counts, padded - counts], axis=0).astype(jnp.int32)
    return runs, run_first, block_e, n_used, pad_slots, n_blocks * slot_block


def _na_row_pattern(rb, n_rows):
    wr = min(NA_WIN_H, n_rows)
    n_rb = n_rows // NA_QROWS
    q_row = rb * NA_QROWS + np.arange(NA_QROWS)[:, None]
    k_row = np.clip(rb - 1, 0, n_rb - NA_KROWS // NA_QROWS) * NA_QROWS + np.arange(NA_KROWS)[None, :]
    row_start = np.clip(q_row - wr // 2, 0, n_rows - wr)
    ok = (k_row >= row_start) & (k_row < row_start + wr)
    return np.where(ok, k_row - q_row + (NA_WIN_H - 1), -1)


def _na_bias_table(rpb, seq):
    n_rows = seq // GRID_W
    n_rb = n_rows // NA_QROWS
    patterns = [_na_row_pattern(rb, n_rows) for rb in range(n_rb)]
    assert all((p == patterns[1]).all() for p in patterns[1:-1])
    q_col = np.arange(GRID_W)[:, None]
    k_col = np.arange(GRID_W)[None, :]
    col_start = np.clip(q_col - NA_WIN_W // 2, 0, GRID_W - NA_WIN_W)
    col_ok = (k_col >= col_start) & (k_col < col_start + NA_WIN_W)
    dc = np.clip(k_col - q_col + (NA_WIN_W - 1), 0, 2 * NA_WIN_W - 2)
    onehot = ((dc[None] == np.arange(2 * NA_WIN_W - 1)[:, None, None]) & col_ok[None]).astype(np.float32)
    by_col = jnp.einsum('hab,bqk->haqk', rpb.astype(F32), jnp.asarray(onehot), precision=HIGHEST)
    by_col = by_col + jnp.asarray(np.where(col_ok, 0.0, NEG_INF).astype(np.float32))
    masked = jnp.full((rpb.shape[0], GRID_W, GRID_W), NEG_INF, F32)
    variants = []
    for pattern in (patterns[0], patterns[1], patterns[-1]):
        rows = [jnp.concatenate([by_col[:, a] if a >= 0 else masked for a in pattern[qr]], axis=2)
                for qr in range(NA_QROWS)]
        variants.append(jnp.concatenate(rows, axis=1))
    return jnp.stack(variants, axis=0)


def _rope_tables(seq, ctx_len):
    quarter = HEAD_DIM // 4
    lane = np.arange(SWA_W)
    inv = ROPE_BASE ** (-(lane % quarter).astype(np.float64) / quarter)
    t = np.arange(seq)
    pos = np.where((lane % HEAD_DIM < HEAD_DIM // 2)[None, :], (t // GRID_W)[:, None], (t % GRID_W)[:, None])
    ang = jnp.asarray(pos, F32) * jnp.asarray(inv, F32)[None, :]
    sign = np.where(lane % (2 * quarter) < quarter, -1.0, 1.0).astype(np.float32)
    cos = jnp.concatenate([jnp.ones((ctx_len, SWA_W), F32), jnp.cos(ang)], axis=0)
    sin = jnp.concatenate([jnp.zeros((ctx_len, SWA_W), F32), jnp.sin(ang) * sign[None, :]], axis=0)
    return cos, sin


def kernel(x, c, ctx, c_ctx, hg_lower_bounds, ada_w, ada_b, norm1_g, norm2_g, w_in, na_q_norm, na_k_norm, na_rpb,
           hg_norm_g, swa_q_norm, swa_k_norm, swa_sink, w_out, router_w, router_b, w_gu, b_gu, w_down, b_down):
    n_batch, seq, d = x.shape
    ctx_len = ctx.shape[1]
    depth = ada_w.shape[0]
    assert d == D_MODEL and seq % ROW_BLOCK == 0 and ctx_len == ROW_BLOCK
    n_ctx_rows = n_batch * ctx_len
    geom = (n_batch, n_ctx_rows // ROW_BLOCK, seq // ROW_BLOCK)

    p_lb = jax.nn.softmax(hg_lower_bounds.astype(F32), axis=0)
    lbs = jnp.cumsum(p_lb, axis=0) - p_lb[0]

    cos_t, sin_t = _rope_tables(seq, ctx_len)
    lane = np.arange(NA_W)
    group_ones = jnp.asarray((lane[:, None] // HEAD_DIM == lane[None, :] // HEAD_DIM).astype(np.float32), BF16)
    tri_f = jnp.asarray(_hgrn_sum_table(HG_STEP_ROWS, False), BF16)
    tri_b = jnp.asarray(_hgrn_sum_table(HG_STEP_ROWS, True), BF16)
    ones_bf = jnp.ones((LANES, LANES), BF16)
    n_mod_rows = -(-(n_batch + 1) // 8) * 8
    cc = jnp.zeros((n_mod_rows, d), F32).at[:n_batch].set(c).at[n_batch].set(c_ctx)
    dup = lambda w: jnp.concatenate([w[:, :HEAD_DIM], w[:, :HEAD_DIM], w[:, HEAD_DIM:], w[:, HEAD_DIM:]], axis=1)
    tile4 = lambda g: jnp.tile(g.astype(F32), 4).reshape(1, 4 * HEAD_DIM)
    pad_e = LANES - N_EXPERTS

    x_all = jnp.concatenate([ctx.reshape(n_ctx_rows, d), x.reshape(n_batch * seq, d)], axis=0)
    for l in range(depth):
        last = l == depth - 1
        mods = _ada_mod(cc, ada_w[l], ada_b[l]).reshape(n_mod_rows, 6, d)
        w = w_in[l]
        kv0 = 3 * NA_W + HG_COLS + SWA_W
        w_ext = jnp.concatenate([w[:, :kv0], dup(w[:, kv0:kv0 + SWA_KV_W]), dup(w[:, kv0 + SWA_KV_W:])],
                                axis=1).astype(BF16)
        hg, naq, nak, nav, swq, swk, swv = _inproj(
            x_all, mods, norm1_g[l].reshape(1, d), w_ext, tile4(na_q_norm[l]), tile4(na_k_norm[l]),
            tile4(swa_q_norm[l]), tile4(swa_k_norm[l]), cos_t, sin_t, group_ones, lbs[l].reshape(1, HG_W), geom)

        y_na = _na_attention(naq, nak, nav, _na_bias_table(na_rpb[l], seq), geom)
        y_sw = _swa_attention(swq, swk, swv, swa_sink[l].astype(F32), geom, seq, ctx_len)
        o_f, o_b = _hgrn(hg, tri_f, tri_b, ones_bf, n_batch, seq, ctx_len)

        if last:
            stream_off = geom[1]
        else:
            stream_off = 0
            yc_na, yc_sw = _ctx_attention(naq, nak, nav, swq, swk, swv, swa_sink[l].astype(F32), n_batch, ctx_len)
            y_na = jnp.concatenate([yc_na, y_na], axis=0)
            y_sw = jnp.concatenate([yc_sw, y_sw], axis=0)

        rw_hi, rw_lo = _split_bf16(jnp.pad(router_w[l].astype(F32), ((0, 0), (0, pad_e))), 2)
        rw = jnp.concatenate([rw_hi, rw_hi, rw_lo], axis=0)
        rb = jnp.pad(router_b[l].astype(F32), (0, pad_e), constant_values=NEG_INF).reshape(1, LANES)
        x_new, h2, top_e, top_g = _mixout(
            x_all, y_na, o_f, o_b, hg, y_sw, hg_norm_g[l].reshape(1, HG_DK).astype(F32), w_out[l].astype(BF16),
            mods, norm2_g[l].reshape(1, d), rw, rb, geom, stream_off)

        runs, run_first, block_e, n_used, pad_slots, n_slots = _route(top_e[:, :TOP_K], N_EXPERTS, ROW_BLOCK, MOE_ROWS)
        xs, pos = _dispatch(h2, top_e, runs, run_first, pad_slots, n_used, n_slots)
        y_slots = _moe_ffn(xs, block_e, n_used, w_gu, b_gu, w_down, b_down, l)
        x_all = _combine(x_new, y_slots, runs, pos, top_g, mods, geom, stream_off)
    return x_all.reshape(n_batch, seq, d)
```

```python
import functools

import numpy as np
import jax
import jax.numpy as jnp
from jax import lax
from jax.experimental import pallas as pl
from jax.experimental.pallas import tpu as pltpu

D_MODEL = 1024
GRID_W = 64
HEAD_DIM = 64
ATTN_SCALE = HEAD_DIM ** -0.5
NA_HEADS = 4
NA_WIN_H = 8
NA_WIN_W = 16
HG_HEADS = 4
HG_DK = 128
SWA_Q_HEADS = 4
SWA_KV_HEADS = 2
SWA_WINDOW = 128
SWA_BLOCK = 128
ROPE_BASE = 10000.0
N_EXPERTS = 32
TOP_K = 4
D_FF = 1024
SWIGLU_LIMIT = 7.0
SWIGLU_ALPHA = 1.702
NORM_EPS = 1e-6
NEG_INF = -1e30

NA_W = NA_HEADS * HEAD_DIM
HG_W = HG_HEADS * HG_DK
SWA_W = SWA_Q_HEADS * HEAD_DIM
SWA_KV_W = SWA_KV_HEADS * HEAD_DIM
HG_COLS = 5 * HG_W
IN_COLS_EXT = 3 * NA_W + HG_COLS + 3 * SWA_W

LANES = 128
SUBLANES = 8
ROW_BLOCK = 256
NA_QROWS = 4
NA_KROWS = 12
NA_HEADS_PER_PASS = 1
SWA_HEADS_PER_PASS = 4
HG_STEP_ROWS = 256
HG_HEADS_PER_STEP = 4
MOE_ROWS = 256
RUN_BITS = ROW_BLOCK.bit_length()
VMEM_LIMIT = 56 * 1024 * 1024

F32 = jnp.float32
BF16 = jnp.bfloat16
HIGHEST = lax.Precision.HIGHEST


def _cparams(*sem):
    return pltpu.CompilerParams(dimension_semantics=sem, vmem_limit_bytes=VMEM_LIMIT)


def _dot(a, b, precision=None):
    return jnp.dot(a, b, preferred_element_type=F32, precision=precision)


def _dot_nt(a, b):
    return lax.dot_general(a, b, (((1,), (1,)), ((), ())), preferred_element_type=F32)


def _dot_tn(a, b):
    return lax.dot_general(a, b, (((0,), (0,)), ((), ())), preferred_element_type=F32)


def _store_token_tiles(ref, x):
    n = x.shape[0]
    for c in range(SUBLANES):
        ref[pl.ds(c, n, stride=SUBLANES), :] = x[:, c * LANES:(c + 1) * LANES]


def _load_token_tiles(ref, first_token, n):
    return jnp.concatenate([ref[pl.ds(first_token * SUBLANES + c, n, stride=SUBLANES), :] for c in range(SUBLANES)],
                           axis=1)


def _sigmoid(x):
    return 1.0 / (1.0 + jnp.exp(-x))


def _silu(x):
    return x * _sigmoid(x)


def _ada_kernel(c_ref, w_ref, b_ref, o_ref):
    o_ref[...] = _dot(_silu(c_ref[...]), w_ref[...], HIGHEST) + b_ref[...]


def _ada_mod(cc, w, b):
    m, d = cc.shape
    n = w.shape[1]
    tn = 1024
    return pl.pallas_call(
        _ada_kernel,
        grid=(n // tn,),
        in_specs=[pl.BlockSpec((m, d), lambda j: (0, 0)),
                  pl.BlockSpec((d, tn), lambda j: (0, j)),
                  pl.BlockSpec((1, tn), lambda j: (0, j))],
        out_specs=pl.BlockSpec((m, tn), lambda j: (0, j)),
        out_shape=jax.ShapeDtypeStruct((m, n), F32),
        compiler_params=_cparams("parallel"),
        name="ada_mod",
    )(cc, w, b.reshape(1, n))


def _mod_rmsnorm(x, g, mod, shift_row, scale_row):
    y = x * lax.rsqrt(jnp.mean(x * x, axis=-1, keepdims=True) + NORM_EPS) * g
    return y * (1.0 + mod[scale_row:scale_row + 1]) + mod[shift_row:shift_row + 1]


def _split_bf16(x, terms):
    parts = []
    for _ in range(terms - 1):
        parts.append(x.astype(BF16))
        x = x - parts[-1].astype(F32)
    return parts + [x.astype(BF16)]


def _head_rmsnorm(x, w, group_ones):
    hi, lo = _split_bf16(x * x, 2)
    n = x.shape[0]
    ss = _dot(jnp.concatenate([hi, lo], axis=0), group_ones)
    return x * lax.rsqrt((ss[:n] + ss[n:]) * (1.0 / HEAD_DIM) + NORM_EPS) * w


def _rope(x, cos, sin_signed):
    n = x.shape[-1]
    lane = lax.broadcasted_iota(jnp.int32, x.shape, 1)
    quarter = HEAD_DIM // 4
    partner = jnp.where(lane % (2 * quarter) < quarter,
                        pltpu.roll(x, n - quarter, 1), pltpu.roll(x, quarter, 1))
    return x * cos + partner * sin_signed


def _inproj_kernel(x_ref, mod_ref, g_ref, w_ref, naq_w, nak_w, swq_w, swk_w, cos_ref, sin_ref, ones_ref, lb_ref,
                   hg_ref, naq_ref, nak_ref, nav_ref, swq_ref, swk_ref, swv_ref):
    h = _mod_rmsnorm(x_ref[...], g_ref[...], mod_ref[...], 0, 1).astype(BF16)
    ones = ones_ref[...]

    def proj(lo, width):
        return _dot(h, w_ref[:, lo:lo + width])

    naq_ref[...] = (_head_rmsnorm(proj(0, NA_W), naq_w[...], ones) * ATTN_SCALE).astype(BF16)
    nak_ref[...] = _head_rmsnorm(proj(NA_W, NA_W), nak_w[...], ones).astype(BF16)
    nav_ref[...] = proj(2 * NA_W, NA_W).astype(BF16)
    base = 3 * NA_W
    lb = lb_ref[...]
    hg_ref[:, :HG_W] = _silu(proj(base, HG_W)) * (HG_DK ** -0.5)
    for j in (1, 2):
        hg_ref[:, j * HG_W:(j + 1) * HG_W] = jnp.log2(lb + (1.0 - lb) * _sigmoid(proj(base + j * HG_W, HG_W)))
    for j in (3, 4):
        hg_ref[:, j * HG_W:(j + 1) * HG_W] = proj(base + j * HG_W, HG_W)
    base += HG_COLS
    cos, sin = cos_ref[...], sin_ref[...]
    swq = _rope(_head_rmsnorm(proj(base, SWA_W), swq_w[...], ones), cos, sin)
    swq_ref[...] = (swq * ATTN_SCALE).astype(BF16)
    swk_ref[...] = _rope(_head_rmsnorm(proj(base + SWA_W, SWA_W), swk_w[...], ones), cos, sin).astype(BF16)
    swv_ref[...] = proj(base + 2 * SWA_W, SWA_W).astype(BF16)


def _mod_row(i, n_ctx_blocks, blocks_per_batch, n_batch):
    return jnp.where(i < n_ctx_blocks, n_batch, (i - n_ctx_blocks) // blocks_per_batch)


def _inproj(x_all, mods, norm_g, w_ext, naq_w, nak_w, swq_w, swk_w, cos_t, sin_t, ones, lb, geom):
    n_batch, n_ctx_blocks, blocks_per_batch = geom
    t = x_all.shape[0]
    tm = ROW_BLOCK
    row = lambda w: pl.BlockSpec((tm, w), lambda i: (i, 0))
    const = lambda a: pl.BlockSpec(a.shape, lambda i: (0,) * a.ndim)
    rope_blk = lambda i: (jnp.where(i < n_ctx_blocks, 0, 1 + (i - n_ctx_blocks) % blocks_per_batch), 0)
    bf = lambda w: jax.ShapeDtypeStruct((t, w), BF16)
    return pl.pallas_call(
        _inproj_kernel,
        grid=(t // tm,),
        in_specs=[row(D_MODEL),
                  pl.BlockSpec((None, 6, D_MODEL),
                               lambda i: (_mod_row(i, n_ctx_blocks, blocks_per_batch, n_batch), 0, 0)),
                  const(norm_g), const(w_ext), const(naq_w), const(nak_w), const(swq_w), const(swk_w),
                  pl.BlockSpec((tm, SWA_W), rope_blk), pl.BlockSpec((tm, SWA_W), rope_blk), const(ones),
                  const(lb)],
        out_specs=[row(HG_COLS), row(NA_W), row(NA_W), row(NA_W), row(SWA_W), row(SWA_W), row(SWA_W)],
        out_shape=[jax.ShapeDtypeStruct((t, HG_COLS), F32), bf(NA_W), bf(NA_W), bf(NA_W),
                   bf(SWA_W), bf(SWA_W), bf(SWA_W)],
        compiler_params=_cparams("parallel"),
        name="inproj",
    )(x_all, mods, norm_g, w_ext, naq_w, nak_w, swq_w, swk_w, cos_t, sin_t, ones, lb)


def _attend(q, parts, sink_vals, heads_per_pass):
    m_rows, width = q.shape
    n_heads = width // HEAD_DIM
    lane = lax.broadcasted_iota(jnp.int32, (m_rows, width), 1)
    out = jnp.zeros((m_rows, width), F32)
    for h0 in range(0, n_heads, heads_per_pass):
        heads = range(h0, h0 + heads_per_pass)
        rows = slice(h0 * m_rows, (h0 + heads_per_pass) * m_rows)
        in_head = [(lane // HEAD_DIM) == h for h in heads]
        q_stack = jnp.concatenate([jnp.where(m, q, jnp.zeros_like(q)) for m in in_head], axis=0)
        scores = []
        for k, _, add in parts:
            s = _dot_nt(q_stack, k)
            scores.append(s if add is None else s + add[rows])
        mx = functools.reduce(jnp.maximum, [jnp.max(s, axis=-1, keepdims=True) for s in scores])
        if sink_vals is not None:
            head_row = lax.broadcasted_iota(jnp.int32, (heads_per_pass * m_rows, 1), 0) // m_rows + h0
            sink = functools.reduce(jnp.add, [jnp.where(head_row == h, sink_vals[h], 0.0) for h in heads])
            mx = jnp.maximum(mx, sink)
        ps = [jnp.exp(s - mx) for s in scores]
        den = functools.reduce(jnp.add, [jnp.sum(p, axis=-1, keepdims=True) for p in ps])
        if sink_vals is not None:
            den = den + jnp.exp(sink - mx)
        acc = functools.reduce(jnp.add, [_dot(p.astype(BF16), v) for p, (_, v, _) in zip(ps, parts)]) / den
        for i, m in enumerate(in_head):
            out = jnp.where(m, acc[i * m_rows:(i + 1) * m_rows], out)
    return out


def _na_kernel(q_ref, k0, k1, k2, kc, v0, v1, v2, vc, bias_ref, o_ref):
    k_lat = jnp.concatenate([k0[...], k1[...], k2[...]], axis=0)
    v_lat = jnp.concatenate([v0[...], v1[...], v2[...]], axis=0)
    bias = bias_ref[...]
    bias = bias.reshape(bias.shape[0] * bias.shape[1], bias.shape[2])
    o = _attend(q_ref[...], [(k_lat, v_lat, bias), (kc[...], vc[...], None)], None, NA_HEADS_PER_PASS)
    o_ref[...] = o.astype(o_ref.dtype)


def _na_attention(naq, nak, nav, bias, geom):
    n_batch, n_ctx_blocks, blocks_per_batch = geom
    tm = ROW_BLOCK
    n_rb = blocks_per_batch
    kb_max = n_rb - NA_KROWS // NA_QROWS
    lat = lambda rb, b: n_ctx_blocks + b * blocks_per_batch

    def band(j):
        return pl.BlockSpec((tm, NA_W), lambda rb, b: (lat(rb, b) + jnp.clip(rb - 1, 0, kb_max) + j, 0))

    ctx = pl.BlockSpec((tm, NA_W), lambda rb, b: (b, 0))
    return pl.pallas_call(
        _na_kernel,
        grid=(n_rb, n_batch),
        in_specs=[pl.BlockSpec((tm, NA_W), lambda rb, b: (lat(rb, b) + rb, 0)),
                  band(0), band(1), band(2), ctx, band(0), band(1), band(2), ctx,
                  pl.BlockSpec((None, NA_HEADS, tm, NA_KROWS * GRID_W), lambda rb, b: (
                      jnp.where(rb == 0, 0, jnp.where(rb == n_rb - 1, 2, 1)), 0, 0, 0))],
        out_specs=pl.BlockSpec((tm, NA_W), lambda rb, b: (b * blocks_per_batch + rb, 0)),
        out_shape=jax.ShapeDtypeStruct((n_batch * blocks_per_batch * tm, NA_W), BF16),
        compiler_params=_cparams("parallel", "parallel"),
        name="na_attn",
    )(naq, nak, nak, nak, nak, nav, nav, nav, nav, bias)


def _swa_kernel(sink_ref, q_ref, kp, kc_, kn, kx, vp, vc_, vn, vx, o_ref, *, n_blocks):
    n = pl.program_id(1)
    tq = q_ref.shape[0]
    k_lat = jnp.concatenate([kp[...], kc_[...], kn[...]], axis=0)
    v_lat = jnp.concatenate([vp[...], vc_[...], vn[...]], axis=0)
    qi = lax.broadcasted_iota(jnp.int32, (SWA_Q_HEADS * tq, 3 * tq), 0) % tq
    mi = lax.broadcasted_iota(jnp.int32, (SWA_Q_HEADS * tq, 3 * tq), 1)
    kpos = (n - 1) * tq + mi
    ok = (jnp.abs(qi + tq - mi) <= SWA_WINDOW) & (kpos >= 0) & (kpos < n_blocks * tq)
    add = jnp.where(ok, 0.0, NEG_INF).astype(F32)
    sinks = [sink_ref[h] for h in range(SWA_Q_HEADS)]
    o = _attend(q_ref[...], [(k_lat, v_lat, add), (kx[...], vx[...], None)], sinks, SWA_HEADS_PER_PASS)
    o_ref[...] = o.astype(o_ref.dtype)


def _swa_attention(swq, swk, swv, sink, geom, seq, ctx_len):
    n_batch, _, _ = geom
    tq = SWA_BLOCK
    nb = seq // tq
    first = n_batch * ctx_len // tq
    blk = lambda f: pl.BlockSpec((tq, SWA_W), lambda b, n: (first + b * nb + f(n), 0))
    prev, cur, nxt = blk(lambda n: jnp.maximum(n - 1, 0)), blk(lambda n: n), blk(lambda n: jnp.minimum(n + 1, nb - 1))
    ctx = pl.BlockSpec((ctx_len, SWA_W), lambda b, n: (b, 0))
    return pl.pallas_call(
        functools.partial(_swa_kernel, n_blocks=nb),
        grid=(n_batch, nb),
        in_specs=[pl.BlockSpec(memory_space=pltpu.SMEM), cur, prev, cur, nxt, ctx, prev, cur, nxt, ctx],
        out_specs=pl.BlockSpec((tq, SWA_W), lambda b, n: (b * nb + n, 0)),
        out_shape=jax.ShapeDtypeStruct((n_batch * seq, SWA_W), BF16),
        compiler_params=_cparams("parallel", "parallel"),
        name="swa_attn",
    )(sink, swq, swk, swk, swk, swk, swv, swv, swv, swv)


def _ctx_attn_kernel(sink_ref, naq, nak, nav, swq, swk, swv, ona_ref, osw_ref):
    ona_ref[...] = _attend(naq[...], [(nak[...], nav[...], None)], None, NA_HEADS_PER_PASS).astype(ona_ref.dtype)
    sinks = [sink_ref[h] for h in range(SWA_Q_HEADS)]
    osw_ref[...] = _attend(swq[...], [(swk[...], swv[...], None)], sinks, NA_HEADS_PER_PASS).astype(osw_ref.dtype)


def _ctx_attention(naq, nak, nav, swq, swk, swv, sink, n_batch, ctx_len):
    blk = pl.BlockSpec((ctx_len, NA_W), lambda b: (b, 0))
    out = jax.ShapeDtypeStruct((n_batch * ctx_len, NA_W), BF16)
    return pl.pallas_call(
        _ctx_attn_kernel,
        grid=(n_batch,),
        in_specs=[pl.BlockSpec(memory_space=pltpu.SMEM)] + [blk] * 6,
        out_specs=[blk, blk],
        out_shape=[out, out],
        compiler_params=_cparams("parallel"),
        name="ctx_attn",
    )(sink, naq, nak, nav, swq, swk, swv)


def _dot_exact_lhs(m, x):
    out = _dot(m, jnp.concatenate(_split_bf16(x, 3), axis=1))
    n = x.shape[1]
    return out[:, :n] + out[:, n:2 * n] + out[:, 2 * n:]


def _hgrn_sum_table(n_rows, reverse):
    tri = np.tril(np.ones((n_rows, n_rows), np.float32))
    return tri.T if reverse else tri


def _tile_row(x, r):
    tiles = x.reshape(x.shape[0] // SUBLANES, SUBLANES, x.shape[1])
    return jnp.broadcast_to(tiles[:, r:r + 1, :], tiles.shape).reshape(x.shape)


def _hgrn_masks(n_rows):
    row = lax.broadcasted_iota(jnp.int32, (n_rows, LANES), 0)
    row_a = lax.broadcasted_iota(jnp.int32, (n_rows, n_rows), 0)
    col_a = lax.broadcasted_iota(jnp.int32, (n_rows, n_rows), 1)
    sizes = [1 << b for b in range(1, n_rows.bit_length())]
    return dict(
        diag=row_a == col_a,
        same={s: row_a // s == col_a // s for s in sizes if s < n_rows},
        upper_half={s: row % s >= s // 2 for s in sizes if s < 2 * SUBLANES},
        from_row={r: row % SUBLANES >= r for r in range(2, SUBLANES, 2)})


def _hgrn_block(q, k, g, v, st, sums, ones, masks, reverse):
    n_rows = q.shape[0]
    cum = _dot_exact_lhs(sums, g)
    tot = cum[0:1] if reverse else cum[n_rows - 1:n_rows]
    o_inter = _dot_nt((q * jnp.exp2(cum)).astype(BF16), st.astype(BF16))
    k_end = (k * jnp.exp2(tot - cum)).astype(BF16)
    st_new = st * jnp.exp2(tot) + _dot_tn(v.astype(BF16), k_end)

    a = None
    zero_tile = jnp.zeros((SUBLANES, LANES), F32)
    size = n_rows
    while size >= 2:
        half = size // 2
        if size < 2 * SUBLANES:
            ref_off = half if reverse else half - 1
            ref = _tile_row(cum, ref_off)
            for first in range(size, SUBLANES, size):
                ref = jnp.where(masks["from_row"][first], _tile_row(cum, first + ref_off), ref)
            upper = masks["upper_half"][size]
            d = cum - ref
            q_exp = jnp.where(upper, NEG_INF, d) if reverse else jnp.where(upper, d, NEG_INF)
            k_exp = jnp.where(upper, -d, NEG_INF) if reverse else jnp.where(upper, NEG_INF, -d)
            q_t = (q * jnp.exp2(q_exp)).astype(BF16)
            k_t = (k * jnp.exp2(k_exp)).astype(BF16)
        else:
            q_tiles, k_tiles = [], []
            for r0 in range(0, n_rows, SUBLANES):
                sl = slice(r0, r0 + SUBLANES)
                first = r0 // size * size
                ref_row = first + (half if reverse else half - 1)
                ref = cum[ref_row:ref_row + 1]
                if (r0 - first < half) if reverse else (r0 - first >= half):
                    q_tiles.append(q[sl] * jnp.exp2(cum[sl] - ref))
                    k_tiles.append(zero_tile)
                else:
                    q_tiles.append(zero_tile)
                    k_tiles.append(k[sl] * jnp.exp2(ref - cum[sl]))
            q_t = jnp.concatenate(q_tiles, axis=0).astype(BF16)
            k_t = jnp.concatenate(k_tiles, axis=0).astype(BF16)
        a_l = _dot_nt(q_t, k_t)
        a = a_l if a is None else jnp.where(masks["same"][size], a_l, a)
        size //= 2
    a = jnp.where(masks["diag"], _dot((q * k).astype(BF16), ones), a)
    return o_inter + _dot(a.astype(BF16), v.astype(BF16)), st_new


def _hgrn_kernel(q_f, g_f, v_f, q_b, g_b, v_b, trif_ref, trib_ref, ones_ref, of_ref, ob_ref, st_f, st_b):
    @pl.when(pl.program_id(2) == 0)
    def _():
        st_f[...] = jnp.zeros_like(st_f)
        st_b[...] = jnp.zeros_like(st_b)

    ones = ones_ref[...]
    masks = _hgrn_masks(q_f.shape[0])
    for h in range(HG_HEADS_PER_STEP):
        sl = slice(h * HG_DK, (h + 1) * HG_DK)
        g = g_f[:, sl]
        of_ref[:, sl], st_f[h] = _hgrn_block(q_f[:, sl], 1.0 - jnp.exp2(g), g, v_f[:, sl], st_f[h],
                                             trif_ref[...], ones, masks, False)
        g = g_b[:, sl]
        ob_ref[:, sl], st_b[h] = _hgrn_block(q_b[:, sl], 1.0 - jnp.exp2(g), g, v_b[:, sl], st_b[h],
                                             trib_ref[...], ones, masks, True)


def _hgrn(hg, tri_f, tri_b, ones, n_batch, seq, ctx_len):
    tr = HG_STEP_ROWS
    nc, nl = ctx_len // tr, seq // tr
    first = n_batch * nc

    def fwd(b, j):
        return jnp.where(j < nc, b * nc + j, first + b * nl + (j - nc))

    def bwd(b, j):
        return jnp.where(j < nc, b * nc + (nc - 1 - j), first + b * nl + (nl - 1 - (j - nc)))

    hps = HG_HEADS_PER_STEP
    groups = HG_HEADS // hps
    wide = hps * HG_DK
    col = lambda row_fn, group: pl.BlockSpec((tr, wide), lambda b, h, j: (row_fn(b, j), group * groups + h))
    const = lambda a: pl.BlockSpec(a.shape, lambda b, h, j: (0, 0))
    out = jax.ShapeDtypeStruct((hg.shape[0], HG_W), F32)
    return pl.pallas_call(
        _hgrn_kernel,
        grid=(n_batch, groups, nc + nl),
        in_specs=[col(fwd, 0), col(fwd, 1), col(fwd, 3), col(bwd, 0), col(bwd, 2), col(bwd, 3),
                  const(tri_f), const(tri_b), const(ones)],
        out_specs=[pl.BlockSpec((tr, wide), lambda b, h, j: (fwd(b, j), h)),
                   pl.BlockSpec((tr, wide), lambda b, h, j: (bwd(b, j), h))],
        out_shape=[out, out],
        scratch_shapes=[pltpu.VMEM((hps, HG_DK, HG_DK), F32), pltpu.VMEM((hps, HG_DK, HG_DK), F32)],
        compiler_params=_cparams("parallel", "parallel", "arbitrary"),
        name="hgrn",
    )(hg, hg, hg, hg, hg, hg, tri_f, tri_b, ones)


def _mixout_kernel(x_ref, yna_ref, of_ref, ob_ref, zg_ref, ysw_ref, hgn_ref, w_ref, mod_ref, g2_ref,
                   rw_ref, rb_ref, xo_ref, h2_ref, te_ref, tg_ref):
    o = of_ref[...] + ob_ref[...]
    zg = zg_ref[...]
    parts = [yna_ref[...]]
    for h in range(HG_HEADS):
        sl = slice(h * HG_DK, (h + 1) * HG_DK)
        oh = o[:, sl]
        yh = oh * lax.rsqrt(jnp.mean(oh * oh, axis=-1, keepdims=True) + NORM_EPS) * hgn_ref[...]
        parts.append((yh * _silu(zg[:, sl])).astype(BF16))
    parts.append(ysw_ref[...])
    y = _dot(jnp.concatenate(parts, axis=1), w_ref[...])
    mod = mod_ref[...]
    x_new = x_ref[...] + mod[2:3] * y
    xo_ref[...] = x_new
    h2 = _mod_rmsnorm(x_new, g2_ref[...], mod, 3, 4)
    h2_ref[...] = h2.astype(BF16)

    h_hi, h_lo = _split_bf16(h2, 2)
    logits = _dot(jnp.concatenate([h_hi, h_lo, h_hi], axis=1), rw_ref[...]) + rb_ref[...]
    lane = lax.broadcasted_iota(jnp.int32, logits.shape, 1).astype(F32)
    top_e = jnp.zeros(logits.shape, F32)
    top_v = jnp.full(logits.shape, NEG_INF, F32)
    for j in range(TOP_K):
        best = jnp.max(logits, axis=-1, keepdims=True)
        arg = jnp.min(jnp.where(logits == best, lane, float(LANES)), axis=-1, keepdims=True)
        top_e = jnp.where(lane == j, arg, top_e)
        top_v = jnp.where(lane == j, best, top_v)
        logits = jnp.where(lane == arg, -jnp.inf, logits)
    ex = jnp.exp(top_v - jnp.max(top_v, axis=-1, keepdims=True))
    te_ref[...] = top_e.astype(jnp.int32)
    tg_ref[...] = ex / jnp.sum(ex, axis=-1, keepdims=True)


def _mixout(x, yna, o_f, o_b, hg, ysw, hgn, w_out, mods, norm2_g, rw, rb, geom, stream_off):
    n_batch, n_ctx_blocks, blocks_per_batch = geom
    tm = ROW_BLOCK
    t = x.shape[0] - stream_off * tm
    row = lambda w: pl.BlockSpec((tm, w), lambda i: (i, 0))
    full = lambda w, cb: pl.BlockSpec((tm, w), lambda i: (i + stream_off, cb))
    const = lambda a: pl.BlockSpec(a.shape, lambda i: (0,) * a.ndim)
    return pl.pallas_call(
        _mixout_kernel,
        grid=(t // tm,),
        in_specs=[full(D_MODEL, 0), row(NA_W), full(HG_W, 0), full(HG_W, 0), full(HG_W, 4), row(SWA_W),
                  const(hgn), const(w_out),
                  pl.BlockSpec((None, 6, D_MODEL), lambda i: (
                      _mod_row(i + stream_off, n_ctx_blocks, blocks_per_batch, n_batch), 0, 0)),
                  const(norm2_g), const(rw), const(rb)],
        out_specs=[row(D_MODEL), row(D_MODEL), row(LANES), row(LANES)],
        out_shape=[jax.ShapeDtypeStruct((t, D_MODEL), F32), jax.ShapeDtypeStruct((t, D_MODEL), BF16),
                   jax.ShapeDtypeStruct((t, LANES), jnp.int32), jax.ShapeDtypeStruct((t, LANES), F32)],
        compiler_params=_cparams("parallel"),
        name="mixout",
    )(x, yna, o_f, o_b, hg, ysw, hgn, w_out, mods, norm2_g, rw, rb)


def _start_run_copies(run_ref, n_experts, make_copy):
    for e in range(n_experts):
        n = run_ref[0, 0, e]
        local0 = run_ref[0, 0, n_experts + e]
        slot0 = run_ref[0, 0, 2 * n_experts + e]
        for bit in reversed(range(RUN_BITS)):
            @pl.when((n & (1 << bit)) != 0)
            def _():
                done = (n >> (bit + 1)) << (bit + 1)
                make_copy(local0 + done, slot0 + done, 1 << bit).start()


def _tile_rows(first_token, n_tokens):
    return pl.ds(pl.multiple_of(first_token * SUBLANES, SUBLANES), n_tokens * SUBLANES)


def _local_positions(te_ref, off_ref, stri_ref):
    e_tok = te_ref[...]
    lane = lax.broadcasted_iota(jnp.int32, e_tok.shape, 1)
    member = [lane == e_tok[:, j:j + 1] for j in range(TOP_K)]
    count = functools.reduce(jnp.add, [jnp.where(m, 1.0, 0.0) for m in member])
    before = _dot(stri_ref[...], count.astype(BF16)) + off_ref[...]
    return [jnp.sum(jnp.where(m, before, 0.0), axis=1, keepdims=True) for m in member]


def _dispatch_kernel(pad_ref, nu_ref, run_ref, te_ref, off_ref, h_ref, stri_ref, xs_hbm, pos_ref,
                     zeros, xloc, sem, zero_sem):
    n_tok = h_ref.shape[0]
    block_rows = zeros.shape[0]
    n_blocks = xs_hbm.shape[0] // block_rows

    def zero_slot(e, r):
        row = pl.multiple_of((pad_ref[0, e] + r) * SUBLANES, SUBLANES)
        return pltpu.make_async_copy(zeros.at[pl.ds(0, SUBLANES), :], xs_hbm.at[pl.ds(row, SUBLANES), :], zero_sem)

    def zero_block(b):
        row = pl.multiple_of(b * block_rows, block_rows)
        return pltpu.make_async_copy(zeros, xs_hbm.at[pl.ds(row, block_rows), :], zero_sem)

    def for_each_unused(slot_fn, block_fn):
        def per_expert(e, carry):
            lax.fori_loop(0, pad_ref[1, e], lambda r, c: (slot_fn(e, r), c)[1], 0)
            return carry

        lax.fori_loop(0, pad_ref.shape[1], per_expert, 0)
        lax.fori_loop(nu_ref[0], n_blocks, lambda b, c: (block_fn(b), c)[1], 0)

    @pl.when(pl.program_id(0) == 0)
    def _():
        zeros[...] = jnp.zeros_like(zeros)
        for_each_unused(lambda e, r: zero_slot(e, r).start(), lambda b: zero_block(b).start())
        for_each_unused(lambda e, r: zero_slot(e, r).wait(), lambda b: zero_block(b).wait())

    pos = _local_positions(te_ref, off_ref, stri_ref)
    lane = lax.broadcasted_iota(jnp.int32, (n_tok, LANES), 1)
    pos_ref[...] = functools.reduce(jnp.add, [jnp.where(lane == j, p, 0.0) for j, p in enumerate(pos)])
    col = lax.broadcasted_iota(jnp.int32, (n_tok, TOP_K * n_tok), 1).astype(F32)
    chosen = functools.reduce(jnp.logical_or, [col == p for p in pos])
    sorted_rows = _dot_tn(jnp.where(chosen, 1.0, 0.0).astype(BF16), h_ref[...])
    i = pl.program_id(0)
    buf = i % 2
    _store_token_tiles(xloc.at[buf], sorted_rows)

    def wait_copies(b):
        pltpu.make_async_copy(xloc.at[b], xs_hbm.at[pl.ds(0, xloc.shape[1]), :], sem.at[b]).wait()

    @pl.when(i > 0)
    def _():
        wait_copies(1 - buf)

    _start_run_copies(run_ref, pad_ref.shape[1], lambda local, slot, size: pltpu.make_async_copy(
        xloc.at[buf, _tile_rows(local, size), :], xs_hbm.at[_tile_rows(slot, size), :], sem.at[buf]))

    @pl.when(i == pl.num_programs(0) - 1)
    def _():
        wait_copies(buf)


def _dispatch(h, top_e, runs, run_first, pad_slots, n_used, n_slots):
    tm = ROW_BLOCK
    t = h.shape[0]
    nb = t // tm
    assert n_slots % MOE_ROWS == 0
    stri = jnp.asarray(np.tril(np.ones((tm, tm), np.float32), -1), BF16)
    row = lambda w: pl.BlockSpec((tm, w), lambda i: (i, 0))
    return pl.pallas_call(
        _dispatch_kernel,
        grid=(nb,),
        in_specs=[pl.BlockSpec(memory_space=pltpu.SMEM), pl.BlockSpec(memory_space=pltpu.SMEM),
                  pl.BlockSpec((1, 1, runs.shape[2]), lambda i: (i, 0, 0), memory_space=pltpu.SMEM),
                  row(LANES), pl.BlockSpec((None, 1, LANES), lambda i: (i, 0, 0)), row(D_MODEL),
                  pl.BlockSpec((tm, tm), lambda i: (0, 0))],
        out_specs=[pl.BlockSpec(memory_space=pl.ANY), row(LANES)],
        out_shape=[jax.ShapeDtypeStruct((n_slots * SUBLANES, LANES), F32), jax.ShapeDtypeStruct((t, LANES), F32)],
        scratch_shapes=[pltpu.VMEM((MOE_ROWS * SUBLANES, LANES), F32),
                        pltpu.VMEM((2, TOP_K * tm * SUBLANES, LANES), F32), pltpu.SemaphoreType.DMA((2,)),
                        pltpu.SemaphoreType.DMA(())],
        compiler_params=_cparams("arbitrary"),
        name="moe_dispatch",
    )(pad_slots, n_used, runs, top_e, run_first, h, stri)


def _moe_kernel(be_ref, nu_ref, xs_ref, wgu_ref, bgu_ref, wd_ref, bd_ref, y_ref, wgu_bf, wd_bf):
    i = pl.program_id(0)
    live = i < nu_ref[0]
    tm = y_ref.shape[0] // SUBLANES
    changed = (i == 0) | (be_ref[i] != be_ref[jnp.maximum(i - 1, 0)])

    @pl.when(live & changed)
    def _():
        wgu_bf[...] = wgu_ref[...].astype(BF16)
        wd_bf[...] = wd_ref[...].astype(BF16)

    @pl.when(live)
    def _():
        x = _load_token_tiles(xs_ref, 0, tm)
        gu = _dot(x.astype(BF16), wgu_bf[...]) + bgu_ref[...]
        glu = jnp.minimum(gu[:, :D_FF], SWIGLU_LIMIT)
        lin = jnp.clip(gu[:, D_FF:], -SWIGLU_LIMIT, SWIGLU_LIMIT)
        act = glu * _sigmoid(SWIGLU_ALPHA * glu) * (lin + 1.0)
        _store_token_tiles(y_ref, _dot(act.astype(BF16), wd_bf[...]) + bd_ref[...])

    @pl.when(jnp.logical_not(live))
    def _():
        y_ref[...] = jnp.zeros_like(y_ref)


def _moe_ffn(xs_tiles, block_e, n_used, w_gu, b_gu, w_down, b_down, layer):
    tm = MOE_ROWS
    nb = xs_tiles.shape[0] // (tm * SUBLANES)
    _, n_e, d, f2 = w_gu.shape
    weight = lambda r, c: pl.BlockSpec((None, None, r, c), lambda i, be, nu: (layer, be[i], 0, 0))
    grid_spec = pltpu.PrefetchScalarGridSpec(
        num_scalar_prefetch=2,
        grid=(nb,),
        in_specs=[pl.BlockSpec((tm * SUBLANES, LANES), lambda i, be, nu: (jnp.minimum(i, nu[0] - 1), 0)),
                  weight(d, f2), weight(1, f2), weight(f2 // 2, d), weight(1, d)],
        out_specs=pl.BlockSpec((tm * SUBLANES, LANES), lambda i, be, nu: (i, 0)),
        scratch_shapes=[pltpu.VMEM((d, f2), BF16), pltpu.VMEM((f2 // 2, d), BF16)],
    )
    n_l = w_gu.shape[0]
    return pl.pallas_call(
        _moe_kernel,
        grid_spec=grid_spec,
        out_shape=jax.ShapeDtypeStruct((nb * tm * SUBLANES, LANES), F32),
        compiler_params=_cparams("arbitrary"),
        name="moe_ffn",
    )(block_e, n_used, xs_tiles, w_gu, b_gu.reshape(n_l, n_e, 1, f2), w_down,
      b_down.reshape(n_l, n_e, 1, d))


def _combine_kernel(run_ref, pos_ref, tg_ref, x_ref, y_hbm, mod_ref, o_ref, ybuf, sem):
    i = pl.program_id(0)
    tm = o_ref.shape[0]
    n_rows = TOP_K * tm

    @pl.when(i < pl.num_programs(0) - 1)
    def _():
        _start_run_copies(run_ref, run_ref.shape[2] // 3, lambda local, slot, size: pltpu.make_async_copy(
            y_hbm.at[_tile_rows(slot, size), :], ybuf.at[i % 2, _tile_rows(local, size), :], sem.at[i % 2]))

    @pl.when(i > 0)
    def _():
        slot = (i - 1) % 2
        pltpu.make_async_copy(y_hbm.at[pl.ds(0, n_rows * SUBLANES), :], ybuf.at[slot], sem.at[slot]).wait()
        y = _load_token_tiles(ybuf.at[slot], 0, n_rows).astype(BF16)
        pos, gate = pos_ref[...], tg_ref[...]
        col = lax.broadcasted_iota(jnp.int32, (tm, n_rows), 1).astype(F32)
        g = functools.reduce(jnp.add, [jnp.where(col == pos[:, j:j + 1], gate[:, j:j + 1], 0.0)
                                       for j in range(TOP_K)])
        g_hi, g_lo = _split_bf16(g, 2)
        o_ref[...] = x_ref[...] + mod_ref[...][5:6] * (_dot(g_hi, y) + _dot(g_lo, y))


def _combine(x, y_tiles, runs, pos, gates, mods, geom, stream_off):
    n_batch, n_ctx_blocks, blocks_per_batch = geom
    t = x.shape[0]
    tm = ROW_BLOCK
    nb = t // tm
    prev = lambda i: jnp.maximum(i - 1, 0)
    row = lambda w: pl.BlockSpec((tm, w), lambda i: (prev(i), 0))
    return pl.pallas_call(
        _combine_kernel,
        grid=(nb + 1,),
        in_specs=[pl.BlockSpec((1, 1, runs.shape[2]), lambda i: (jnp.minimum(i, nb - 1), 0, 0),
                               memory_space=pltpu.SMEM),
                  row(LANES), row(LANES), row(D_MODEL), pl.BlockSpec(memory_space=pl.ANY),
                  pl.BlockSpec((None, 6, D_MODEL), lambda i: (
                      _mod_row(prev(i) + stream_off, n_ctx_blocks, blocks_per_batch, n_batch), 0, 0))],
        out_specs=row(D_MODEL),
        out_shape=jax.ShapeDtypeStruct((t, D_MODEL), F32),
        scratch_shapes=[pltpu.VMEM((2, TOP_K * tm * SUBLANES, LANES), F32), pltpu.SemaphoreType.DMA((2,))],
        compiler_params=_cparams("arbitrary"),
        name="moe_combine",
    )(runs, pos, gates, x, y_tiles, mods)


def _route(top_e, n_experts, tok_block, slot_block):
    t, k = top_e.shape
    nb = t // tok_block
    onehot = (top_e[:, :, None] == jnp.arange(n_experts, dtype=jnp.int32)).astype(jnp.int32)
    n = onehot.reshape(nb, tok_block * k, n_experts).sum(axis=1)
    counts = n.sum(axis=0)
    padded = (counts + slot_block - 1) // slot_block * slot_block
    pend = jnp.cumsum(padded)
    start = pend - padded
    local0 = jnp.cumsum(n, axis=1) - n
    slot0 = start[None, :] + jnp.cumsum(n, axis=0) - n
    runs = jnp.concatenate([n, local0, slot0], axis=1).astype(jnp.int32).reshape(nb, 1, 3 * n_experts)
    run_first = jnp.pad(local0.astype(F32), ((0, 0), (0, LANES - n_experts))).reshape(nb, 1, LANES)
    n_blocks = (t * k + n_experts * (slot_block - 1) + slot_block - 1) // slot_block
    block_start = jnp.arange(n_blocks, dtype=jnp.int32) * slot_block
    block_e = jnp.minimum((block_start[:, None] >= pend[None, :]).sum(axis=1), n_experts - 1).astype(jnp.int32)
    n_used = (pend[-1] // slot_block).astype(jnp.int32).reshape(1)
    pad_slots = jnp.stack([start + counts, padded - counts], axis=0).astype(jnp.int32)
    return runs, run_first, block_e, n_used, pad_slots, n_blocks * slot_block


def _na_row_pattern(rb, n_rows):
    wr = min(NA_WIN_H, n_rows)
    n_rb = n_rows // NA_QROWS
    q_row = rb * NA_QROWS + np.arange(NA_QROWS)[:, None]
    k_row = np.clip(rb - 1, 0, n_rb - NA_KROWS // NA_QROWS) * NA_QROWS + np.arange(NA_KROWS)[None, :]
    row_start = np.clip(q_row - wr // 2, 0, n_rows - wr)
    ok = (k_row >= row_start) & (k_row < row_start + wr)
    return np.where(ok, k_row - q_row + (NA_WIN_H - 1), -1)


def _na_bias_table(rpb, seq):
    n_rows = seq // GRID_W
    n_rb = n_rows // NA_QROWS
    patterns = [_na_row_pattern(rb, n_rows) for rb in range(n_rb)]
    assert all((p == patterns[1]).all() for p in patterns[1:-1])
    q_col = np.arange(GRID_W)[:, None]
    k_col = np.arange(GRID_W)[None, :]
    col_start = np.clip(q_col - NA_WIN_W // 2, 0, GRID_W - NA_WIN_W)
    col_ok = (k_col >= col_start) & (k_col < col_start + NA_WIN_W)
    dc = np.clip(k_col - q_col + (NA_WIN_W - 1), 0, 2 * NA_WIN_W - 2)
    onehot = ((dc[None] == np.arange(2 * NA_WIN_W - 1)[:, None, None]) & col_ok[None]).astype(np.float32)
    by_col = jnp.einsum('hab,bqk->haqk', rpb.astype(F32), jnp.asarray(onehot), precision=HIGHEST)
    by_col = by_col + jnp.asarray(np.where(col_ok, 0.0, NEG_INF).astype(np.float32))
    masked = jnp.full((rpb.shape[0], GRID_W, GRID_W), NEG_INF, F32)
    variants = []
    for pattern in (patterns[0], patterns[1], patterns[-1]):
        rows = [jnp.concatenate([by_col[:, a] if a >= 0 else masked for a in pattern[qr]], axis=2)
                for qr in range(NA_QROWS)]
        variants.append(jnp.concatenate(rows, axis=1))
    return jnp.stack(variants, axis=0)


def _rope_tables(seq, ctx_len):
    quarter = HEAD_DIM // 4
    lane = np.arange(SWA_W)
    inv = ROPE_BASE ** (-(lane % quarter).astype(np.float64) / quarter)
    t = np.arange(seq)
    pos = np.where((lane % HEAD_DIM < HEAD_DIM // 2)[None, :], (t // GRID_W)[:, None], (t % GRID_W)[:, None])
    ang = jnp.asarray(pos, F32) * jnp.asarray(inv, F32)[None, :]
    sign = np.where(lane % (2 * quarter) < quarter, -1.0, 1.0).astype(np.float32)
    cos = jnp.concatenate([jnp.ones((ctx_len, SWA_W), F32), jnp.cos(ang)], axis=0)
    sin = jnp.concatenate([jnp.zeros((ctx_len, SWA_W), F32), jnp.sin(ang) * sign[None, :]], axis=0)
    return cos, sin


def kernel(x, c, ctx, c_ctx, hg_lower_bounds, ada_w, ada_b, norm1_g, norm2_g, w_in, na_q_norm, na_k_norm, na_rpb,
           hg_norm_g, swa_q_norm, swa_k_norm, swa_sink, w_out, router_w, router_b, w_gu, b_gu, w_down, b_down):
    n_batch, seq, d = x.shape
    ctx_len = ctx.shape[1]
    depth = ada_w.shape[0]
    assert d == D_MODEL and seq % ROW_BLOCK == 0 and ctx_len == ROW_BLOCK
    n_ctx_rows = n_batch * ctx_len
    geom = (n_batch, n_ctx_rows // ROW_BLOCK, seq // ROW_BLOCK)

    p_lb = jax.nn.softmax(hg_lower_bounds.astype(F32), axis=0)
    lbs = jnp.cumsum(p_lb, axis=0) - p_lb[0]

    cos_t, sin_t = _rope_tables(seq, ctx_len)
    lane = np.arange(NA_W)
    group_ones = jnp.asarray((lane[:, None] // HEAD_DIM == lane[None, :] // HEAD_DIM).astype(np.float32), BF16)
    tri_f = jnp.asarray(_hgrn_sum_table(HG_STEP_ROWS, False), BF16)
    tri_b = jnp.asarray(_hgrn_sum_table(HG_STEP_ROWS, True), BF16)
    ones_bf = jnp.ones((LANES, HG_STEP_ROWS), BF16)
    n_mod_rows = -(-(n_batch + 1) // 8) * 8
    cc = jnp.zeros((n_mod_rows, d), F32).at[:n_batch].set(c).at[n_batch].set(c_ctx)
    dup = lambda w: jnp.concatenate([w[:, :HEAD_DIM], w[:, :HEAD_DIM], w[:, HEAD_DIM:], w[:, HEAD_DIM:]], axis=1)
    tile4 = lambda g: jnp.tile(g.astype(F32), 4).reshape(1, 4 * HEAD_DIM)
    pad_e = LANES - N_EXPERTS

    x_all = jnp.concatenate([ctx.reshape(n_ctx_rows, d), x.reshape(n_batch * seq, d)], axis=0)
    for l in range(depth):
        last = l == depth - 1
        mods = _ada_mod(cc, ada_w[l], ada_b[l]).reshape(n_mod_rows, 6, d)
        w = w_in[l]
        kv0 = 3 * NA_W + HG_COLS + SWA_W
        w_ext = jnp.concatenate([w[:, :kv0], dup(w[:, kv0:kv0 + SWA_KV_W]), dup(w[:, kv0 + SWA_KV_W:])],
                                axis=1).astype(BF16)
        hg, naq, nak, nav, swq, swk, swv = _inproj(
            x_all, mods, norm1_g[l].reshape(1, d), w_ext, tile4(na_q_norm[l]), tile4(na_k_norm[l]),
            tile4(swa_q_norm[l]), tile4(swa_k_norm[l]), cos_t, sin_t, group_ones, lbs[l].reshape(1, HG_W), geom)

        y_na = _na_attention(naq, nak, nav, _na_bias_table(na_rpb[l], seq), geom)
        y_sw = _swa_attention(swq, swk, swv, swa_sink[l].astype(F32), geom, seq, ctx_len)
        o_f, o_b = _hgrn(hg, tri_f, tri_b, ones_bf, n_batch, seq, ctx_len)

        if last:
            stream_off = geom[1]
        else:
            stream_off = 0
            yc_na, yc_sw = _ctx_attention(naq, nak, nav, swq, swk, swv, swa_sink[l].astype(F32), n_batch, ctx_len)
            y_na = jnp.concatenate([yc_na, y_na], axis=0)
            y_sw = jnp.concatenate([yc_sw, y_sw], axis=0)

        rw_hi, rw_lo = _split_bf16(jnp.pad(router_w[l].astype(F32), ((0, 0), (0, pad_e))), 2)
        rw = jnp.concatenate([rw_hi, rw_hi, rw_lo], axis=0)
        rb = jnp.pad(router_b[l].astype(F32), (0, pad_e), constant_values=NEG_INF).reshape(1, LANES)
        x_new, h2, top_e, top_g = _mixout(
            x_all, y_na, o_f, o_b, hg, y_sw, hg_norm_g[l].reshape(1, HG_DK).astype(F32), w_out[l].astype(BF16),
            mods, norm2_g[l].reshape(1, d), rw, rb, geom, stream_off)

        runs, run_first, block_e, n_used, pad_slots, n_slots = _route(top_e[:, :TOP_K], N_EXPERTS, ROW_BLOCK, MOE_ROWS)
        xs, pos = _dispatch(h2, top_e, runs, run_first, pad_slots, n_used, n_slots)
        y_slots = _moe_ffn(xs, block_e, n_used, w_gu, b_gu, w_down, b_down, l)
        x_all = _combine(x_new, y_slots, runs, pos, top_g, mods, geom, stream_off)
    return x_all.reshape(n_batch, seq, d)
```

```python
import functools

import numpy as np
import jax
import jax.numpy as jnp
from jax import lax
from jax.experimental import pallas as pl
from jax.experimental.pallas import tpu as pltpu

D_MODEL = 1024
GRID_W = 64
HEAD_DIM = 64
ATTN_SCALE = HEAD_DIM ** -0.5
NA_HEADS = 4
NA_WIN_H = 8
NA_WIN_W = 16
HG_HEADS = 4
HG_DK = 128
SWA_Q_HEADS = 4
SWA_KV_HEADS = 2
SWA_WINDOW = 128
SWA_BLOCK = 128
ROPE_BASE = 10000.0
N_EXPERTS = 32
TOP_K = 4
D_FF = 1024
SWIGLU_LIMIT = 7.0
SWIGLU_ALPHA = 1.702
NORM_EPS = 1e-6
NEG_INF = -1e30

NA_W = NA_HEADS * HEAD_DIM
HG_W = HG_HEADS * HG_DK
SWA_W = SWA_Q_HEADS * HEAD_DIM
SWA_KV_W = SWA_KV_HEADS * HEAD_DIM
HG_COLS = 5 * HG_W
IN_COLS_EXT = 3 * NA_W + HG_COLS + 3 * SWA_W

LANES = 128
SUBLANES = 8
ROW_BLOCK = 256
NA_QROWS = 4
NA_KROWS = 12
NA_HEADS_PER_PASS = 1
SWA_HEADS_PER_PASS = 4
HG_STEP_ROWS = 256
HG_HEADS_PER_STEP = 4
MOE_ROWS = 256
RUN_BITS = ROW_BLOCK.bit_length()
VMEM_LIMIT = 56 * 1024 * 1024

F32 = jnp.float32
BF16 = jnp.bfloat16
HIGHEST = lax.Precision.HIGHEST


def _cparams(*sem):
    return pltpu.CompilerParams(dimension_semantics=sem, vmem_limit_bytes=VMEM_LIMIT)


def _dot(a, b, precision=None):
    return jnp.dot(a, b, preferred_element_type=F32, precision=precision)


def _dot_nt(a, b):
    return lax.dot_general(a, b, (((1,), (1,)), ((), ())), preferred_element_type=F32)


def _dot_tn(a, b):
    return lax.dot_general(a, b, (((0,), (0,)), ((), ())), preferred_element_type=F32)


def _store_token_tiles(ref, x):
    n = x.shape[0]
    for c in range(SUBLANES):
        ref[pl.ds(c, n, stride=SUBLANES), :] = x[:, c * LANES:(c + 1) * LANES]


def _load_token_tiles(ref, first_token, n):
    return jnp.concatenate([ref[pl.ds(first_token * SUBLANES + c, n, stride=SUBLANES), :] for c in range(SUBLANES)],
                           axis=1)


def _sigmoid(x):
    return 1.0 / (1.0 + jnp.exp(-x))


def _silu(x):
    return x * _sigmoid(x)


def _ada_kernel(c_ref, w_ref, b_ref, o_ref):
    o_ref[...] = _dot(_silu(c_ref[...]), w_ref[...], HIGHEST) + b_ref[...]


def _ada_mod(cc, w, b):
    m, d = cc.shape
    n = w.shape[1]
    tn = 1024
    return pl.pallas_call(
        _ada_kernel,
        grid=(n // tn,),
        in_specs=[pl.BlockSpec((m, d), lambda j: (0, 0)),
                  pl.BlockSpec((d, tn), lambda j: (0, j)),
                  pl.BlockSpec((1, tn), lambda j: (0, j))],
        out_specs=pl.BlockSpec((m, tn), lambda j: (0, j)),
        out_shape=jax.ShapeDtypeStruct((m, n), F32),
        compiler_params=_cparams("parallel"),
        name="ada_mod",
    )(cc, w, b.reshape(1, n))


def _mod_rmsnorm(x, g, mod, shift_row, scale_row):
    y = x * lax.rsqrt(jnp.mean(x * x, axis=-1, keepdims=True) + NORM_EPS) * g
    return y * (1.0 + mod[scale_row:scale_row + 1]) + mod[shift_row:shift_row + 1]


def _split_bf16(x, terms):
    parts = []
    for _ in range(terms - 1):
        parts.append(x.astype(BF16))
        x = x - parts[-1].astype(F32)
    return parts + [x.astype(BF16)]


def _head_rmsnorm(x, w, group_ones):
    hi, lo = _split_bf16(x * x, 2)
    n = x.shape[0]
    ss = _dot(jnp.concatenate([hi, lo], axis=0), group_ones)
    return x * lax.rsqrt((ss[:n] + ss[n:]) * (1.0 / HEAD_DIM) + NORM_EPS) * w


def _rope(x, cos, sin_signed):
    n = x.shape[-1]
    lane = lax.broadcasted_iota(jnp.int32, x.shape, 1)
    quarter = HEAD_DIM // 4
    partner = jnp.where(lane % (2 * quarter) < quarter,
                        pltpu.roll(x, n - quarter, 1), pltpu.roll(x, quarter, 1))
    return x * cos + partner * sin_signed


def _stream_specs(stream, width, first_block=0):
    head, tail = stream
    tm = ROW_BLOCK
    if head is None:
        return [pl.BlockSpec((tm, width), lambda i: (i + first_block, 0))], [tail]
    n_head = head.shape[0] // tm
    return ([pl.BlockSpec((tm, width), lambda i: (jnp.minimum(i + first_block, n_head - 1), 0)),
             pl.BlockSpec((tm, width), lambda i: (jnp.maximum(i + first_block - n_head, 0), 0))], [head, tail])


def _stream_block(refs, n_head, first_block=0):
    if n_head is None:
        return refs[0][...]
    return jnp.where(pl.program_id(0) + first_block < n_head, refs[0][...], refs[1][...])


def _n_head(stream):
    return None if stream[0] is None else stream[0].shape[0] // ROW_BLOCK


def _inproj_kernel(*refs, n_head):
    n_x = 1 if n_head is None else 2
    (mod_ref, g_ref, w_ref, naq_w, nak_w, swq_w, swk_w, cos_ref, sin_ref, ones_ref, lb_ref,
     hg_ref, naq_ref, nak_ref, nav_ref, swq_ref, swk_ref, swv_ref) = refs[n_x:]
    h = _mod_rmsnorm(_stream_block(refs[:n_x], n_head), g_ref[...], mod_ref[...], 0, 1).astype(BF16)
    ones = ones_ref[...]

    def proj(lo, width):
        return _dot(h, w_ref[:, lo:lo + width])

    naq_ref[...] = (_head_rmsnorm(proj(0, NA_W), naq_w[...], ones) * ATTN_SCALE).astype(BF16)
    nak_ref[...] = _head_rmsnorm(proj(NA_W, NA_W), nak_w[...], ones).astype(BF16)
    nav_ref[...] = proj(2 * NA_W, NA_W).astype(BF16)
    base = 3 * NA_W
    lb = lb_ref[...]
    hg_ref[:, :HG_W] = _silu(proj(base, HG_W)) * (HG_DK ** -0.5)
    for j in (1, 2):
        hg_ref[:, j * HG_W:(j + 1) * HG_W] = jnp.log2(lb + (1.0 - lb) * _sigmoid(proj(base + j * HG_W, HG_W)))
    for j in (3, 4):
        hg_ref[:, j * HG_W:(j + 1) * HG_W] = proj(base + j * HG_W, HG_W)
    base += HG_COLS
    cos, sin = cos_ref[...], sin_ref[...]
    swq = _rope(_head_rmsnorm(proj(base, SWA_W), swq_w[...], ones), cos, sin)
    swq_ref[...] = (swq * ATTN_SCALE).astype(BF16)
    swk_ref[...] = _rope(_head_rmsnorm(proj(base + SWA_W, SWA_W), swk_w[...], ones), cos, sin).astype(BF16)
    swv_ref[...] = proj(base + 2 * SWA_W, SWA_W).astype(BF16)


def _mod_row(i, n_ctx_blocks, blocks_per_batch, n_batch):
    return jnp.where(i < n_ctx_blocks, n_batch, (i - n_ctx_blocks) // blocks_per_batch)


def _inproj(x_stream, mods, norm_g, w_ext, naq_w, nak_w, swq_w, swk_w, cos_t, sin_t, ones, lb, geom):
    n_batch, n_ctx_blocks, blocks_per_batch = geom
    tm = ROW_BLOCK
    t = (n_ctx_blocks + n_batch * blocks_per_batch) * tm
    row = lambda w: pl.BlockSpec((tm, w), lambda i: (i, 0))
    const = lambda a: pl.BlockSpec(a.shape, lambda i: (0,) * a.ndim)
    rope_blk = lambda i: (jnp.where(i < n_ctx_blocks, 0, 1 + (i - n_ctx_blocks) % blocks_per_batch), 0)
    bf = lambda w: jax.ShapeDtypeStruct((t, w), BF16)
    x_specs, x_arrays = _stream_specs(x_stream, D_MODEL)
    return pl.pallas_call(
        functools.partial(_inproj_kernel, n_head=_n_head(x_stream)),
        grid=(t // tm,),
        in_specs=x_specs + [
            pl.BlockSpec((None, 6, D_MODEL), lambda i: (_mod_row(i, n_ctx_blocks, blocks_per_batch, n_batch), 0, 0)),
            const(norm_g), const(w_ext), const(naq_w), const(nak_w), const(swq_w), const(swk_w),
            pl.BlockSpec((tm, SWA_W), rope_blk), pl.BlockSpec((tm, SWA_W), rope_blk), const(ones), const(lb)],
        out_specs=[row(HG_COLS), row(NA_W), row(NA_W), row(NA_W), row(SWA_W), row(SWA_W), row(SWA_W)],
        out_shape=[jax.ShapeDtypeStruct((t, HG_COLS), F32), bf(NA_W), bf(NA_W), bf(NA_W),
                   bf(SWA_W), bf(SWA_W), bf(SWA_W)],
        compiler_params=_cparams("parallel"),
        name="inproj",
    )(*x_arrays, mods, norm_g, w_ext, naq_w, nak_w, swq_w, swk_w, cos_t, sin_t, ones, lb)


def _attend(q, parts, sink_vals, heads_per_pass):
    m_rows, width = q.shape
    n_heads = width // HEAD_DIM
    lane = lax.broadcasted_iota(jnp.int32, (m_rows, width), 1)
    out = jnp.zeros((m_rows, width), F32)
    for h0 in range(0, n_heads, heads_per_pass):
        heads = range(h0, h0 + heads_per_pass)
        rows = slice(h0 * m_rows, (h0 + heads_per_pass) * m_rows)
        in_head = [(lane // HEAD_DIM) == h for h in heads]
        q_stack = jnp.concatenate([jnp.where(m, q, jnp.zeros_like(q)) for m in in_head], axis=0)
        scores = []
        for k, _, add in parts:
            s = _dot_nt(q_stack, k)
            scores.append(s if add is None else s + add[rows])
        mx = functools.reduce(jnp.maximum, [jnp.max(s, axis=-1, keepdims=True) for s in scores])
        if sink_vals is not None:
            head_row = lax.broadcasted_iota(jnp.int32, (heads_per_pass * m_rows, 1), 0) // m_rows + h0
            sink = functools.reduce(jnp.add, [jnp.where(head_row == h, sink_vals[h], 0.0) for h in heads])
            mx = jnp.maximum(mx, sink)
        ps = [jnp.exp(s - mx) for s in scores]
        den = functools.reduce(jnp.add, [jnp.sum(p, axis=-1, keepdims=True) for p in ps])
        if sink_vals is not None:
            den = den + jnp.exp(sink - mx)
        acc = functools.reduce(jnp.add, [_dot(p.astype(BF16), v) for p, (_, v, _) in zip(ps, parts)]) / den
        for i, m in enumerate(in_head):
            out = jnp.where(m, acc[i * m_rows:(i + 1) * m_rows], out)
    return out


def _na_kernel(q_ref, k0, k1, k2, kc, v0, v1, v2, vc, bias_ref, o_ref):
    k_lat = jnp.concatenate([k0[...], k1[...], k2[...]], axis=0)
    v_lat = jnp.concatenate([v0[...], v1[...], v2[...]], axis=0)
    bias = bias_ref[...]
    bias = bias.reshape(bias.shape[0] * bias.shape[1], bias.shape[2])
    o = _attend(q_ref[...], [(k_lat, v_lat, bias), (kc[...], vc[...], None)], None, NA_HEADS_PER_PASS)
    o_ref[...] = o.astype(o_ref.dtype)


def _na_attention(naq, nak, nav, bias, geom):
    n_batch, n_ctx_blocks, blocks_per_batch = geom
    tm = ROW_BLOCK
    n_rb = blocks_per_batch
    kb_max = n_rb - NA_KROWS // NA_QROWS
    lat = lambda rb, b: n_ctx_blocks + b * blocks_per_batch

    def band(j):
        return pl.BlockSpec((tm, NA_W), lambda rb, b: (lat(rb, b) + jnp.clip(rb - 1, 0, kb_max) + j, 0))

    ctx = pl.BlockSpec((tm, NA_W), lambda rb, b: (b, 0))
    return pl.pallas_call(
        _na_kernel,
        grid=(n_rb, n_batch),
        in_specs=[pl.BlockSpec((tm, NA_W), lambda rb, b: (lat(rb, b) + rb, 0)),
                  band(0), band(1), band(2), ctx, band(0), band(1), band(2), ctx,
                  pl.BlockSpec((None, NA_HEADS, tm, NA_KROWS * GRID_W), lambda rb, b: (
                      jnp.where(rb == 0, 0, jnp.where(rb == n_rb - 1, 2, 1)), 0, 0, 0))],
        out_specs=pl.BlockSpec((tm, NA_W), lambda rb, b: (b * blocks_per_batch + rb, 0)),
        out_shape=jax.ShapeDtypeStruct((n_batch * blocks_per_batch * tm, NA_W), BF16),
        compiler_params=_cparams("parallel", "parallel"),
        name="na_attn",
    )(naq, nak, nak, nak, nak, nav, nav, nav, nav, bias)


def _swa_kernel(sink_ref, q_ref, kp, kc_, kn, kx, vp, vc_, vn, vx, o_ref, *, n_blocks):
    n = pl.program_id(1)
    tq = q_ref.shape[0]
    k_lat = jnp.concatenate([kp[...], kc_[...], kn[...]], axis=0)
    v_lat = jnp.concatenate([vp[...], vc_[...], vn[...]], axis=0)
    qi = lax.broadcasted_iota(jnp.int32, (SWA_Q_HEADS * tq, 3 * tq), 0) % tq
    mi = lax.broadcasted_iota(jnp.int32, (SWA_Q_HEADS * tq, 3 * tq), 1)
    kpos = (n - 1) * tq + mi
    ok = (jnp.abs(qi + tq - mi) <= SWA_WINDOW) & (kpos >= 0) & (kpos < n_blocks * tq)
    add = jnp.where(ok, 0.0, NEG_INF).astype(F32)
    sinks = [sink_ref[h] for h in range(SWA_Q_HEADS)]
    o = _attend(q_ref[...], [(k_lat, v_lat, add), (kx[...], vx[...], None)], sinks, SWA_HEADS_PER_PASS)
    o_ref[...] = o.astype(o_ref.dtype)


def _swa_attention(swq, swk, swv, sink, geom, seq, ctx_len):
    n_batch, _, _ = geom
    tq = SWA_BLOCK
    nb = seq // tq
    first = n_batch * ctx_len // tq
    blk = lambda f: pl.BlockSpec((tq, SWA_W), lambda b, n: (first + b * nb + f(n), 0))
    prev, cur, nxt = blk(lambda n: jnp.maximum(n - 1, 0)), blk(lambda n: n), blk(lambda n: jnp.minimum(n + 1, nb - 1))
    ctx = pl.BlockSpec((ctx_len, SWA_W), lambda b, n: (b, 0))
    return pl.pallas_call(
        functools.partial(_swa_kernel, n_blocks=nb),
        grid=(n_batch, nb),
        in_specs=[pl.BlockSpec(memory_space=pltpu.SMEM), cur, prev, cur, nxt, ctx, prev, cur, nxt, ctx],
        out_specs=pl.BlockSpec((tq, SWA_W), lambda b, n: (b * nb + n, 0)),
        out_shape=jax.ShapeDtypeStruct((n_batch * seq, SWA_W), BF16),
        compiler_params=_cparams("parallel", "parallel"),
        name="swa_attn",
    )(sink, swq, swk, swk, swk, swk, swv, swv, swv, swv)


def _ctx_attn_kernel(sink_ref, naq, nak, nav, swq, swk, swv, ona_ref, osw_ref):
    ona_ref[...] = _attend(naq[...], [(nak[...], nav[...], None)], None, NA_HEADS_PER_PASS).astype(ona_ref.dtype)
    sinks = [sink_ref[h] for h in range(SWA_Q_HEADS)]
    osw_ref[...] = _attend(swq[...], [(swk[...], swv[...], None)], sinks, NA_HEADS_PER_PASS).astype(osw_ref.dtype)


def _ctx_attention(naq, nak, nav, swq, swk, swv, sink, n_batch, ctx_len):
    blk = pl.BlockSpec((ctx_len, NA_W), lambda b: (b, 0))
    out = jax.ShapeDtypeStruct((n_batch * ctx_len, NA_W), BF16)
    return pl.pallas_call(
        _ctx_attn_kernel,
        grid=(n_batch,),
        in_specs=[pl.BlockSpec(memory_space=pltpu.SMEM)] + [blk] * 6,
        out_specs=[blk, blk],
        out_shape=[out, out],
        compiler_params=_cparams("parallel"),
        name="ctx_attn",
    )(sink, naq, nak, nav, swq, swk, swv)


def _dot_exact_lhs(m, x):
    out = _dot(m, jnp.concatenate(_split_bf16(x, 3), axis=1))
    n = x.shape[1]
    return out[:, :n] + out[:, n:2 * n] + out[:, 2 * n:]


def _hgrn_sum_table(n_rows, reverse):
    tri = np.tril(np.ones((n_rows, n_rows), np.float32))
    return tri.T if reverse else tri


def _tile_row(x, r):
    tiles = x.reshape(x.shape[0] // SUBLANES, SUBLANES, x.shape[1])
    return jnp.broadcast_to(tiles[:, r:r + 1, :], tiles.shape).reshape(x.shape)


def _hgrn_masks(n_rows):
    row = lax.broadcasted_iota(jnp.int32, (n_rows, LANES), 0)
    row_a = lax.broadcasted_iota(jnp.int32, (n_rows, n_rows), 0)
    col_a = lax.broadcasted_iota(jnp.int32, (n_rows, n_rows), 1)
    sizes = [1 << b for b in range(1, n_rows.bit_length())]
    return dict(
        diag=row_a == col_a,
        same={s: row_a // s == col_a // s for s in sizes if s < n_rows},
        upper_half={s: row % s >= s // 2 for s in sizes if s < 2 * SUBLANES},
        from_row={r: row % SUBLANES >= r for r in range(2, SUBLANES, 2)})


def _hgrn_block(q, k, g, v, st, sums, ones, masks, reverse):
    n_rows = q.shape[0]
    cum = _dot_exact_lhs(sums, g)
    tot = cum[0:1] if reverse else cum[n_rows - 1:n_rows]
    o_inter = _dot_nt((q * jnp.exp2(cum)).astype(BF16), st.astype(BF16))
    k_end = (k * jnp.exp2(tot - cum)).astype(BF16)
    st_new = st * jnp.exp2(tot) + _dot_tn(v.astype(BF16), k_end)

    a = None
    zero_tile = jnp.zeros((SUBLANES, LANES), F32)
    size = n_rows
    while size >= 2:
        half = size // 2
        if size < 2 * SUBLANES:
            ref_off = half if reverse else half - 1
            ref = _tile_row(cum, ref_off)
            for first in range(size, SUBLANES, size):
                ref = jnp.where(masks["from_row"][first], _tile_row(cum, first + ref_off), ref)
            upper = masks["upper_half"][size]
            d = cum - ref
            q_exp = jnp.where(upper, NEG_INF, d) if reverse else jnp.where(upper, d, NEG_INF)
            k_exp = jnp.where(upper, -d, NEG_INF) if reverse else jnp.where(upper, NEG_INF, -d)
            q_t = (q * jnp.exp2(q_exp)).astype(BF16)
            k_t = (k * jnp.exp2(k_exp)).astype(BF16)
        else:
            q_tiles, k_tiles = [], []
            for r0 in range(0, n_rows, SUBLANES):
                sl = slice(r0, r0 + SUBLANES)
                first = r0 // size * size
                ref_row = first + (half if reverse else half - 1)
                ref = cum[ref_row:ref_row + 1]
                if (r0 - first < half) if reverse else (r0 - first >= half):
                    q_tiles.append(q[sl] * jnp.exp2(cum[sl] - ref))
                    k_tiles.append(zero_tile)
                else:
                    q_tiles.append(zero_tile)
                    k_tiles.append(k[sl] * jnp.exp2(ref - cum[sl]))
            q_t = jnp.concatenate(q_tiles, axis=0).astype(BF16)
            k_t = jnp.concatenate(k_tiles, axis=0).astype(BF16)
        a_l = _dot_nt(q_t, k_t)
        a = a_l if a is None else jnp.where(masks["same"][size], a_l, a)
        size //= 2
    a = jnp.where(masks["diag"], _dot((q * k).astype(BF16), ones), a)
    return o_inter + _dot(a.astype(BF16), v.astype(BF16)), st_new


def _hgrn_kernel(q_f, g_f, v_f, q_b, g_b, v_b, trif_ref, trib_ref, ones_ref, of_ref, ob_ref, st_f, st_b):
    @pl.when(pl.program_id(2) == 0)
    def _():
        st_f[...] = jnp.zeros_like(st_f)
        st_b[...] = jnp.zeros_like(st_b)

    ones = ones_ref[...]
    masks = _hgrn_masks(q_f.shape[0])
    for h in range(HG_HEADS_PER_STEP):
        sl = slice(h * HG_DK, (h + 1) * HG_DK)
        g = g_f[:, sl]
        of_ref[:, sl], st_f[h] = _hgrn_block(q_f[:, sl], 1.0 - jnp.exp2(g), g, v_f[:, sl], st_f[h],
                                             trif_ref[...], ones, masks, False)
        g = g_b[:, sl]
        ob_ref[:, sl], st_b[h] = _hgrn_block(q_b[:, sl], 1.0 - jnp.exp2(g), g, v_b[:, sl], st_b[h],
                                             trib_ref[...], ones, masks, True)


def _hgrn(hg, tri_f, tri_b, ones, n_batch, seq, ctx_len):
    tr = HG_STEP_ROWS
    nc, nl = ctx_len // tr, seq // tr
    first = n_batch * nc

    def fwd(b, j):
        return jnp.where(j < nc, b * nc + j, first + b * nl + (j - nc))

    def bwd(b, j):
        return jnp.where(j < nc, b * nc + (nc - 1 - j), first + b * nl + (nl - 1 - (j - nc)))

    hps = HG_HEADS_PER_STEP
    groups = HG_HEADS // hps
    wide = hps * HG_DK
    col = lambda row_fn, group: pl.BlockSpec((tr, wide), lambda b, h, j: (row_fn(b, j), group * groups + h))
    const = lambda a: pl.BlockSpec(a.shape, lambda b, h, j: (0, 0))
    out = jax.ShapeDtypeStruct((hg.shape[0], HG_W), F32)
    return pl.pallas_call(
        _hgrn_kernel,
        grid=(n_batch, groups, nc + nl),
        in_specs=[col(fwd, 0), col(fwd, 1), col(fwd, 3), col(bwd, 0), col(bwd, 2), col(bwd, 3),
                  const(tri_f), const(tri_b), const(ones)],
        out_specs=[pl.BlockSpec((tr, wide), lambda b, h, j: (fwd(b, j), h)),
                   pl.BlockSpec((tr, wide), lambda b, h, j: (bwd(b, j), h))],
        out_shape=[out, out],
        scratch_shapes=[pltpu.VMEM((hps, HG_DK, HG_DK), F32), pltpu.VMEM((hps, HG_DK, HG_DK), F32)],
        compiler_params=_cparams("parallel", "parallel", "arbitrary"),
        name="hgrn",
    )(hg, hg, hg, hg, hg, hg, tri_f, tri_b, ones)


def _mixout_kernel(*refs, n_heads, stream_off):
    counts = [1 if n is None else 2 for n in n_heads]
    streams, pos = [], 0
    for n, c in zip(n_heads, counts):
        streams.append((refs[pos:pos + c], n))
        pos += c
    (of_ref, ob_ref, zg_ref, hgn_ref, w_ref, mod_ref, g2_ref, rw_ref, rb_ref,
     xo_ref, h2_ref, te_ref, tg_ref) = refs[pos:]
    x_in = _stream_block(*streams[0], first_block=stream_off)
    o = of_ref[...] + ob_ref[...]
    zg = zg_ref[...]
    parts = [_stream_block(*streams[1])]
    for h in range(HG_HEADS):
        sl = slice(h * HG_DK, (h + 1) * HG_DK)
        oh = o[:, sl]
        yh = oh * lax.rsqrt(jnp.mean(oh * oh, axis=-1, keepdims=True) + NORM_EPS) * hgn_ref[...]
        parts.append((yh * _silu(zg[:, sl])).astype(BF16))
    parts.append(_stream_block(*streams[2]))
    y = _dot(jnp.concatenate(parts, axis=1), w_ref[...])
    mod = mod_ref[...]
    x_new = x_in + mod[2:3] * y
    xo_ref[...] = x_new
    h2 = _mod_rmsnorm(x_new, g2_ref[...], mod, 3, 4)
    h2_ref[...] = h2.astype(BF16)

    h_hi, h_lo = _split_bf16(h2, 2)
    logits = _dot(jnp.concatenate([h_hi, h_lo, h_hi], axis=1), rw_ref[...]) + rb_ref[...]
    lane = lax.broadcasted_iota(jnp.int32, logits.shape, 1).astype(F32)
    top_e = jnp.zeros(logits.shape, F32)
    top_v = jnp.full(logits.shape, NEG_INF, F32)
    for j in range(TOP_K):
        best = jnp.max(logits, axis=-1, keepdims=True)
        arg = jnp.min(jnp.where(logits == best, lane, float(LANES)), axis=-1, keepdims=True)
        top_e = jnp.where(lane == j, arg, top_e)
        top_v = jnp.where(lane == j, best, top_v)
        logits = jnp.where(lane == arg, -jnp.inf, logits)
    ex = jnp.exp(top_v - jnp.max(top_v, axis=-1, keepdims=True))
    te_ref[...] = top_e.astype(jnp.int32)
    tg_ref[...] = ex / jnp.sum(ex, axis=-1, keepdims=True)


def _mixout(x_stream, yna_stream, o_f, o_b, hg, ysw_stream, hgn, w_out, mods, norm2_g, rw, rb, geom, stream_off):
    n_batch, n_ctx_blocks, blocks_per_batch = geom
    tm = ROW_BLOCK
    t = (n_ctx_blocks + n_batch * blocks_per_batch - stream_off) * tm
    row = lambda w: pl.BlockSpec((tm, w), lambda i: (i, 0))
    full = lambda w, cb: pl.BlockSpec((tm, w), lambda i: (i + stream_off, cb))
    const = lambda a: pl.BlockSpec(a.shape, lambda i: (0,) * a.ndim)
    x_specs, x_arrays = _stream_specs(x_stream, D_MODEL, stream_off)
    yna_specs, yna_arrays = _stream_specs(yna_stream, NA_W)
    ysw_specs, ysw_arrays = _stream_specs(ysw_stream, SWA_W)
    n_heads = (_n_head(x_stream), _n_head(yna_stream), _n_head(ysw_stream))
    return pl.pallas_call(
        functools.partial(_mixout_kernel, n_heads=n_heads, stream_off=stream_off),
        grid=(t // tm,),
        in_specs=x_specs + yna_specs + ysw_specs + [
            full(HG_W, 0), full(HG_W, 0), full(HG_W, 4), const(hgn), const(w_out),
            pl.BlockSpec((None, 6, D_MODEL), lambda i: (
                _mod_row(i + stream_off, n_ctx_blocks, blocks_per_batch, n_batch), 0, 0)),
            const(norm2_g), const(rw), const(rb)],
        out_specs=[row(D_MODEL), row(D_MODEL), row(LANES), row(LANES)],
        out_shape=[jax.ShapeDtypeStruct((t, D_MODEL), F32), jax.ShapeDtypeStruct((t, D_MODEL), BF16),
                   jax.ShapeDtypeStruct((t, LANES), jnp.int32), jax.ShapeDtypeStruct((t, LANES), F32)],
        compiler_params=_cparams("parallel"),
        name="mixout",
    )(*x_arrays, *yna_arrays, *ysw_arrays, o_f, o_b, hg, hgn, w_out, mods, norm2_g, rw, rb)


def _start_run_copies(run_ref, n_experts, make_copy):
    for e in range(n_experts):
        n = run_ref[0, 0, e]
        local0 = run_ref[0, 0, n_experts + e]
        slot0 = run_ref[0, 0, 2 * n_experts + e]
        for bit in reversed(range(RUN_BITS)):
            @pl.when((n & (1 << bit)) != 0)
            def _():
                done = (n >> (bit + 1)) << (bit + 1)
                make_copy(local0 + done, slot0 + done, 1 << bit).start()


def _tile_rows(first_token, n_tokens):
    return pl.ds(pl.multiple_of(first_token * SUBLANES, SUBLANES), n_tokens * SUBLANES)


def _local_positions(te_ref, off_ref, stri_ref):
    e_tok = te_ref[...]
    lane = lax.broadcasted_iota(jnp.int32, e_tok.shape, 1)
    member = [lane == e_tok[:, j:j + 1] for j in range(TOP_K)]
    count = functools.reduce(jnp.add, [jnp.where(m, 1.0, 0.0) for m in member])
    before = _dot(stri_ref[...], count.astype(BF16)) + off_ref[...]
    return [jnp.sum(jnp.where(m, before, 0.0), axis=1, keepdims=True) for m in member]


def _dispatch_kernel(pad_ref, nu_ref, run_ref, te_ref, off_ref, h_ref, stri_ref, xs_hbm, pos_ref,
                     zeros, xloc, sem, zero_sem):
    n_tok = h_ref.shape[0]
    block_rows = zeros.shape[0]
    n_blocks = xs_hbm.shape[0] // block_rows

    def zero_slot(e, r):
        row = pl.multiple_of((pad_ref[0, e] + r) * SUBLANES, SUBLANES)
        return pltpu.make_async_copy(zeros.at[pl.ds(0, SUBLANES), :], xs_hbm.at[pl.ds(row, SUBLANES), :], zero_sem)

    def zero_block(b):
        row = pl.multiple_of(b * block_rows, block_rows)
        return pltpu.make_async_copy(zeros, xs_hbm.at[pl.ds(row, block_rows), :], zero_sem)

    def for_each_unused(slot_fn, block_fn):
        def per_expert(e, carry):
            lax.fori_loop(0, pad_ref[1, e], lambda r, c: (slot_fn(e, r), c)[1], 0)
            return carry

        lax.fori_loop(0, pad_ref.shape[1], per_expert, 0)
        lax.fori_loop(nu_ref[0], n_blocks, lambda b, c: (block_fn(b), c)[1], 0)

    @pl.when(pl.program_id(0) == 0)
    def _():
        zeros[...] = jnp.zeros_like(zeros)
        for_each_unused(lambda e, r: zero_slot(e, r).start(), lambda b: zero_block(b).start())
        for_each_unused(lambda e, r: zero_slot(e, r).wait(), lambda b: zero_block(b).wait())

    pos = _local_positions(te_ref, off_ref, stri_ref)
    lane = lax.broadcasted_iota(jnp.int32, (n_tok, LANES), 1)
    pos_ref[...] = functools.reduce(jnp.add, [jnp.where(lane == j, p, 0.0) for j, p in enumerate(pos)])
    col = lax.broadcasted_iota(jnp.int32, (n_tok, TOP_K * n_tok), 1).astype(F32)
    chosen = functools.reduce(jnp.logical_or, [col == p for p in pos])
    sorted_rows = _dot_tn(jnp.where(chosen, 1.0, 0.0).astype(BF16), h_ref[...])
    i = pl.program_id(0)
    buf = i % 2
    _store_token_tiles(xloc.at[buf], sorted_rows)

    def wait_copies(b):
        pltpu.make_async_copy(xloc.at[b], xs_hbm.at[pl.ds(0, xloc.shape[1]), :], sem.at[b]).wait()

    @pl.when(i > 0)
    def _():
        wait_copies(1 - buf)

    _start_run_copies(run_ref, pad_ref.shape[1], lambda local, slot, size: pltpu.make_async_copy(
        xloc.at[buf, _tile_rows(local, size), :], xs_hbm.at[_tile_rows(slot, size), :], sem.at[buf]))

    @pl.when(i == pl.num_programs(0) - 1)
    def _():
        wait_copies(buf)


def _dispatch(h, top_e, runs, run_first, pad_slots, n_used, n_slots):
    tm = ROW_BLOCK
    t = h.shape[0]
    nb = t // tm
    assert n_slots % MOE_ROWS == 0
    stri = jnp.asarray(np.tril(np.ones((tm, tm), np.float32), -1), BF16)
    row = lambda w: pl.BlockSpec((tm, w), lambda i: (i, 0))
    return pl.pallas_call(
        _dispatch_kernel,
        grid=(nb,),
        in_specs=[pl.BlockSpec(memory_space=pltpu.SMEM), pl.BlockSpec(memory_space=pltpu.SMEM),
                  pl.BlockSpec((1, 1, runs.shape[2]), lambda i: (i, 0, 0), memory_space=pltpu.SMEM),
                  row(LANES), pl.BlockSpec((None, 1, LANES), lambda i: (i, 0, 0)), row(D_MODEL),
                  pl.BlockSpec((tm, tm), lambda i: (0, 0))],
        out_specs=[pl.BlockSpec(memory_space=pl.ANY), row(LANES)],
        out_shape=[jax.ShapeDtypeStruct((n_slots * SUBLANES, LANES), F32), jax.ShapeDtypeStruct((t, LANES), F32)],
        scratch_shapes=[pltpu.VMEM((MOE_ROWS * SUBLANES, LANES), F32),
                        pltpu.VMEM((2, TOP_K * tm * SUBLANES, LANES), F32), pltpu.SemaphoreType.DMA((2,)),
                        pltpu.SemaphoreType.DMA(())],
        compiler_params=_cparams("arbitrary"),
        name="moe_dispatch",
    )(pad_slots, n_used, runs, top_e, run_first, h, stri)


def _moe_kernel(be_ref, nu_ref, xs_ref, wgu_ref, bgu_ref, wd_ref, bd_ref, y_ref, wgu_bf, wd_bf):
    i = pl.program_id(0)
    live = i < nu_ref[0]
    tm = y_ref.shape[0] // SUBLANES
    changed = (i == 0) | (be_ref[i] != be_ref[jnp.maximum(i - 1, 0)])

    @pl.when(live & changed)
    def _():
        wgu_bf[...] = wgu_ref[...].astype(BF16)
        wd_bf[...] = wd_ref[...].astype(BF16)

    @pl.when(live)
    def _():
        x = _load_token_tiles(xs_ref, 0, tm)
        gu = _dot(x.astype(BF16), wgu_bf[...]) + bgu_ref[...]
        glu = jnp.minimum(gu[:, :D_FF], SWIGLU_LIMIT)
        lin = jnp.clip(gu[:, D_FF:], -SWIGLU_LIMIT, SWIGLU_LIMIT)
        act = glu * _sigmoid(SWIGLU_ALPHA * glu) * (lin + 1.0)
        _store_token_tiles(y_ref, _dot(act.astype(BF16), wd_bf[...]) + bd_ref[...])

    @pl.when(jnp.logical_not(live))
    def _():
        y_ref[...] = jnp.zeros_like(y_ref)


def _moe_ffn(xs_tiles, block_e, n_used, w_gu, b_gu, w_down, b_down, layer):
    tm = MOE_ROWS
    nb = xs_tiles.shape[0] // (tm * SUBLANES)
    _, n_e, d, f2 = w_gu.shape
    weight = lambda r, c: pl.BlockSpec((None, None, r, c), lambda i, be, nu: (layer, be[i], 0, 0))
    grid_spec = pltpu.PrefetchScalarGridSpec(
        num_scalar_prefetch=2,
        grid=(nb,),
        in_specs=[pl.BlockSpec((tm * SUBLANES, LANES), lambda i, be, nu: (jnp.minimum(i, nu[0] - 1), 0)),
                  weight(d, f2), weight(1, f2), weight(f2 // 2, d), weight(1, d)],
        out_specs=pl.BlockSpec((tm * SUBLANES, LANES), lambda i, be, nu: (i, 0)),
        scratch_shapes=[pltpu.VMEM((d, f2), BF16), pltpu.VMEM((f2 // 2, d), BF16)],
    )
    n_l = w_gu.shape[0]
    return pl.pallas_call(
        _moe_kernel,
        grid_spec=grid_spec,
        out_shape=jax.ShapeDtypeStruct((nb * tm * SUBLANES, LANES), F32),
        compiler_params=_cparams("arbitrary"),
        name="moe_ffn",
    )(block_e, n_used, xs_tiles, w_gu, b_gu.reshape(n_l, n_e, 1, f2), w_down,
      b_down.reshape(n_l, n_e, 1, d))


def _combine_kernel(run_ref, pos_ref, tg_ref, x_ref, y_hbm, mod_ref, o_ref, ybuf, sem):
    i = pl.program_id(0)
    tm = o_ref.shape[0]
    n_rows = TOP_K * tm

    @pl.when(i < pl.num_programs(0) - 1)
    def _():
        _start_run_copies(run_ref, run_ref.shape[2] // 3, lambda local, slot, size: pltpu.make_async_copy(
            y_hbm.at[_tile_rows(slot, size), :], ybuf.at[i % 2, _tile_rows(local, size), :], sem.at[i % 2]))

    @pl.when(i > 0)
    def _():
        slot = (i - 1) % 2
        pltpu.make_async_copy(y_hbm.at[pl.ds(0, n_rows * SUBLANES), :], ybuf.at[slot], sem.at[slot]).wait()
        y = _load_token_tiles(ybuf.at[slot], 0, n_rows).astype(BF16)
        pos, gate = pos_ref[...], tg_ref[...]
        col = lax.broadcasted_iota(jnp.int32, (tm, n_rows), 1).astype(F32)
        g = functools.reduce(jnp.add, [jnp.where(col == pos[:, j:j + 1], gate[:, j:j + 1], 0.0)
                                       for j in range(TOP_K)])
        g_hi, g_lo = _split_bf16(g, 2)
        o_ref[...] = x_ref[...] + mod_ref[...][5:6] * (_dot(g_hi, y) + _dot(g_lo, y))


def _combine(x, y_tiles, runs, pos, gates, mods, geom, stream_off):
    n_batch, n_ctx_blocks, blocks_per_batch = geom
    t = x.shape[0]
    tm = ROW_BLOCK
    nb = t // tm
    prev = lambda i: jnp.maximum(i - 1, 0)
    row = lambda w: pl.BlockSpec((tm, w), lambda i: (prev(i), 0))
    return pl.pallas_call(
        _combine_kernel,
        grid=(nb + 1,),
        in_specs=[pl.BlockSpec((1, 1, runs.shape[2]), lambda i: (jnp.minimum(i, nb - 1), 0, 0),
                               memory_space=pltpu.SMEM),
                  row(LANES), row(LANES), row(D_MODEL), pl.BlockSpec(memory_space=pl.ANY),
                  pl.BlockSpec((None, 6, D_MODEL), lambda i: (
                      _mod_row(prev(i) + stream_off, n_ctx_blocks, blocks_per_batch, n_batch), 0, 0))],
        out_specs=row(D_MODEL),
        out_shape=jax.ShapeDtypeStruct((t, D_MODEL), F32),
        scratch_shapes=[pltpu.VMEM((2, TOP_K * tm * SUBLANES, LANES), F32), pltpu.SemaphoreType.DMA((2,))],
        compiler_params=_cparams("arbitrary"),
        name="moe_combine",
    )(runs, pos, gates, x, y_tiles, mods)


def _route(top_e, n_experts, tok_block, slot_block):
    t, k = top_e.shape
    nb = t // tok_block
    onehot = (top_e[:, :, None] == jnp.arange(n_experts, dtype=jnp.int32)).astype(jnp.int32)
    n = onehot.reshape(nb, tok_block * k, n_experts).sum(axis=1)
    counts = n.sum(axis=0)
    padded = (counts + slot_block - 1) // slot_block * slot_block
    pend = jnp.cumsum(padded)
    start = pend - padded
    local0 = jnp.cumsum(n, axis=1) - n
    slot0 = start[None, :] + jnp.cumsum(n, axis=0) - n
    runs = jnp.concatenate([n, local0, slot0], axis=1).astype(jnp.int32).reshape(nb, 1, 3 * n_experts)
    run_first = jnp.pad(local0.astype(F32), ((0, 0), (0, LANES - n_experts))).reshape(nb, 1, LANES)
    n_blocks = (t * k + n_experts * (slot_block - 1) + slot_block - 1) // slot_block
    block_start = jnp.arange(n_blocks, dtype=jnp.int32) * slot_block
    block_e = jnp.minimum((block_start[:, None] >= pend[None, :]).sum(axis=1), n_experts - 1).astype(jnp.int32)
    n_used = (pend[-1] // slot_block).astype(jnp.int32).reshape(1)
    pad_slots = jnp.stack([start + counts, padded - counts], axis=0).astype(jnp.int32)
    return runs, run_first, block_e, n_used, pad_slots, n_blocks * slot_block


def _na_row_pattern(rb, n_rows):
    wr = min(NA_WIN_H, n_rows)
    n_rb = n_rows // NA_QROWS
    q_row = rb * NA_QROWS + np.arange(NA_QROWS)[:, None]
    k_row = np.clip(rb - 1, 0, n_rb - NA_KROWS // NA_QROWS) * NA_QROWS + np.arange(NA_KROWS)[None, :]
    row_start = np.clip(q_row - wr // 2, 0, n_rows - wr)
    ok = (k_row >= row_start) & (k_row < row_start + wr)
    return np.where(ok, k_row - q_row + (NA_WIN_H - 1), -1)


def _na_bias_table(rpb, seq):
    n_rows = seq // GRID_W
    n_rb = n_rows // NA_QROWS
    patterns = [_na_row_pattern(rb, n_rows) for rb in range(n_rb)]
    assert all((p == patterns[1]).all() for p in patterns[1:-1])
    q_col = np.arange(GRID_W)[:, None]
    k_col = np.arange(GRID_W)[None, :]
    col_start = np.clip(q_col - NA_WIN_W // 2, 0, GRID_W - NA_WIN_W)
    col_ok = (k_col >= col_start) & (k_col < col_start + NA_WIN_W)
    dc = np.clip(k_col - q_col + (NA_WIN_W - 1), 0, 2 * NA_WIN_W - 2)
    onehot = ((dc[None] == np.arange(2 * NA_WIN_W - 1)[:, None, None]) & col_ok[None]).astype(np.float32)
    by_col = jnp.einsum('hab,bqk->haqk', rpb.astype(F32), jnp.asarray(onehot), precision=HIGHEST)
    by_col = by_col + jnp.asarray(np.where(col_ok, 0.0, NEG_INF).astype(np.float32))
    masked = jnp.full((rpb.shape[0], GRID_W, GRID_W), NEG_INF, F32)
    variants = []
    for pattern in (patterns[0], patterns[1], patterns[-1]):
        rows = [jnp.concatenate([by_col[:, a] if a >= 0 else masked for a in pattern[qr]], axis=2)
                for qr in range(NA_QROWS)]
        variants.append(jnp.concatenate(rows, axis=1))
    return jnp.stack(variants, axis=0)


def _rope_tables(seq, ctx_len):
    quarter = HEAD_DIM // 4
    lane = np.arange(SWA_W)
    inv = ROPE_BASE ** (-(lane % quarter).astype(np.float64) / quarter)
    t = np.arange(seq)
    pos = np.where((lane % HEAD_DIM < HEAD_DIM // 2)[None, :], (t // GRID_W)[:, None], (t % GRID_W)[:, None])
    ang = jnp.asarray(pos, F32) * jnp.asarray(inv, F32)[None, :]
    sign = np.where(lane % (2 * quarter) < quarter, -1.0, 1.0).astype(np.float32)
    cos = jnp.concatenate([jnp.ones((ctx_len, SWA_W), F32), jnp.cos(ang)], axis=0)
    sin = jnp.concatenate([jnp.zeros((ctx_len, SWA_W), F32), jnp.sin(ang) * sign[None, :]], axis=0)
    return cos, sin


def kernel(x, c, ctx, c_ctx, hg_lower_bounds, ada_w, ada_b, norm1_g, norm2_g, w_in, na_q_norm, na_k_norm, na_rpb,
           hg_norm_g, swa_q_norm, swa_k_norm, swa_sink, w_out, router_w, router_b, w_gu, b_gu, w_down, b_down):
    n_batch, seq, d = x.shape
    ctx_len = ctx.shape[1]
    depth = ada_w.shape[0]
    assert d == D_MODEL and seq % ROW_BLOCK == 0 and ctx_len == ROW_BLOCK
    n_ctx_rows = n_batch * ctx_len
    geom = (n_batch, n_ctx_rows // ROW_BLOCK, seq // ROW_BLOCK)

    p_lb = jax.nn.softmax(hg_lower_bounds.astype(F32), axis=0)
    lbs = jnp.cumsum(p_lb, axis=0) - p_lb[0]

    cos_t, sin_t = _rope_tables(seq, ctx_len)
    lane = np.arange(NA_W)
    group_ones = jnp.asarray((lane[:, None] // HEAD_DIM == lane[None, :] // HEAD_DIM).astype(np.float32), BF16)
    tri_f = jnp.asarray(_hgrn_sum_table(HG_STEP_ROWS, False), BF16)
    tri_b = jnp.asarray(_hgrn_sum_table(HG_STEP_ROWS, True), BF16)
    ones_bf = jnp.ones((LANES, HG_STEP_ROWS), BF16)
    n_mod_rows = -(-(n_batch + 1) // 8) * 8
    cc = jnp.zeros((n_mod_rows, d), F32).at[:n_batch].set(c).at[n_batch].set(c_ctx)
    dup = lambda w: jnp.concatenate([w[:, :HEAD_DIM], w[:, :HEAD_DIM], w[:, HEAD_DIM:], w[:, HEAD_DIM:]], axis=1)
    tile4 = lambda g: jnp.tile(g.astype(F32), 4).reshape(1, 4 * HEAD_DIM)
    pad_e = LANES - N_EXPERTS

    x_stream = (ctx.reshape(n_ctx_rows, d), x.reshape(n_batch * seq, d))
    for l in range(depth):
        last = l == depth - 1
        mods = _ada_mod(cc, ada_w[l], ada_b[l]).reshape(n_mod_rows, 6, d)
        w = w_in[l]
        kv0 = 3 * NA_W + HG_COLS + SWA_W
        w_ext = jnp.concatenate([w[:, :kv0], dup(w[:, kv0:kv0 + SWA_KV_W]), dup(w[:, kv0 + SWA_KV_W:])],
                                axis=1).astype(BF16)
        hg, naq, nak, nav, swq, swk, swv = _inproj(
            x_stream, mods, norm1_g[l].reshape(1, d), w_ext, tile4(na_q_norm[l]), tile4(na_k_norm[l]),
            tile4(swa_q_norm[l]), tile4(swa_k_norm[l]), cos_t, sin_t, group_ones, lbs[l].reshape(1, HG_W), geom)

        y_na = _na_attention(naq, nak, nav, _na_bias_table(na_rpb[l], seq), geom)
        y_sw = _swa_attention(swq, swk, swv, swa_sink[l].astype(F32), geom, seq, ctx_len)
        o_f, o_b = _hgrn(hg, tri_f, tri_b, ones_bf, n_batch, seq, ctx_len)

        if last:
            stream_off = geom[1]
            yna_stream, ysw_stream = (None, y_na), (None, y_sw)
        else:
            stream_off = 0
            yc_na, yc_sw = _ctx_attention(naq, nak, nav, swq, swk, swv, swa_sink[l].astype(F32), n_batch, ctx_len)
            yna_stream, ysw_stream = (yc_na, y_na), (yc_sw, y_sw)

        rw_hi, rw_lo = _split_bf16(jnp.pad(router_w[l].astype(F32), ((0, 0), (0, pad_e))), 2)
        rw = jnp.concatenate([rw_hi, rw_hi, rw_lo], axis=0)
        rb = jnp.pad(router_b[l].astype(F32), (0, pad_e), constant_values=NEG_INF).reshape(1, LANES)
        x_new, h2, top_e, top_g = _mixout(
            x_stream, yna_stream, o_f, o_b, hg, ysw_stream, hg_norm_g[l].reshape(1, HG_DK).astype(F32),
            w_out[l].astype(BF16), mods, norm2_g[l].reshape(1, d), rw, rb, geom, stream_off)

        runs, run_first, block_e, n_used, pad_slots, n_slots = _route(top_e[:, :TOP_K], N_EXPERTS, ROW_BLOCK, MOE_ROWS)
        xs, pos = _dispatch(h2, top_e, runs, run_first, pad_slots, n_used, n_slots)
        y_slots = _moe_ffn(xs, block_e, n_used, w_gu, b_gu, w_down, b_down, l)
        x_stream = (None, _combine(x_new, y_slots, runs, pos, top_g, mods, geom, stream_off))
    return x_stream[1].reshape(n_batch, seq, d)
```

```python
import functools

import numpy as np
import jax
import jax.numpy as jnp
from jax import lax
from jax.experimental import pallas as pl
from jax.experimental.pallas import tpu as pltpu

D_MODEL = 1024
GRID_W = 64
HEAD_DIM = 64
ATTN_SCALE = HEAD_DIM ** -0.5
NA_HEADS = 4
NA_WIN_H = 8
NA_WIN_W = 16
HG_HEADS = 4
HG_DK = 128
SWA_Q_HEADS = 4
SWA_KV_HEADS = 2
SWA_WINDOW = 128
SWA_BLOCK = 128
ROPE_BASE = 10000.0
N_EXPERTS = 32
TOP_K = 4
D_FF = 1024
SWIGLU_LIMIT = 7.0
SWIGLU_ALPHA = 1.702
NORM_EPS = 1e-6
NEG_INF = -1e30

NA_W = NA_HEADS * HEAD_DIM
HG_W = HG_HEADS * HG_DK
SWA_W = SWA_Q_HEADS * HEAD_DIM
SWA_KV_W = SWA_KV_HEADS * HEAD_DIM
HG_COLS = 5 * HG_W
IN_COLS_EXT = 3 * NA_W + HG_COLS + 3 * SWA_W

LANES = 128
SUBLANES = 8
ROW_BLOCK = 256
NA_QROWS = 4
NA_KROWS = 12
NA_HEADS_PER_PASS = 1
SWA_HEADS_PER_PASS = 4
HG_STEP_ROWS = 256
HG_HEADS_PER_STEP = 4
MOE_ROWS = 256
RUN_BITS = ROW_BLOCK.bit_length()
VMEM_LIMIT = 56 * 1024 * 1024

F32 = jnp.float32
BF16 = jnp.bfloat16
HIGHEST = lax.Precision.HIGHEST


def _cparams(*sem):
    return pltpu.CompilerParams(dimension_semantics=sem, vmem_limit_bytes=VMEM_LIMIT)


def _dot(a, b, precision=None):
    return jnp.dot(a, b, preferred_element_type=F32, precision=precision)


def _dot_nt(a, b):
    return lax.dot_general(a, b, (((1,), (1,)), ((), ())), preferred_element_type=F32)


def _dot_tn(a, b):
    return lax.dot_general(a, b, (((0,), (0,)), ((), ())), preferred_element_type=F32)


def _store_token_tiles(ref, x):
    n = x.shape[0]
    for c in range(SUBLANES):
        ref[pl.ds(c, n, stride=SUBLANES), :] = x[:, c * LANES:(c + 1) * LANES]


def _load_token_tiles(ref, first_token, n):
    return jnp.concatenate([ref[pl.ds(first_token * SUBLANES + c, n, stride=SUBLANES), :] for c in range(SUBLANES)],
                           axis=1)


def _sigmoid(x):
    return 1.0 / (1.0 + jnp.exp(-x))


def _silu(x):
    return x * _sigmoid(x)


def _ada_kernel(c_ref, w_ref, b_ref, o_ref):
    o_ref[...] = _dot(_silu(c_ref[...]), w_ref[...], HIGHEST) + b_ref[...]


def _ada_mod(cc, w, b):
    m, d = cc.shape
    n = w.shape[1]
    tn = 1024
    return pl.pallas_call(
        _ada_kernel,
        grid=(n // tn,),
        in_specs=[pl.BlockSpec((m, d), lambda j: (0, 0)),
                  pl.BlockSpec((d, tn), lambda j: (0, j)),
                  pl.BlockSpec((1, tn), lambda j: (0, j))],
        out_specs=pl.BlockSpec((m, tn), lambda j: (0, j)),
        out_shape=jax.ShapeDtypeStruct((m, n), F32),
        compiler_params=_cparams("parallel"),
        name="ada_mod",
    )(cc, w, b.reshape(1, n))


def _mod_rmsnorm(x, g, mod, shift_row, scale_row):
    y = x * lax.rsqrt(jnp.mean(x * x, axis=-1, keepdims=True) + NORM_EPS) * g
    return y * (1.0 + mod[scale_row:scale_row + 1]) + mod[shift_row:shift_row + 1]


def _split_bf16(x, terms):
    parts = []
    for _ in range(terms - 1):
        parts.append(x.astype(BF16))
        x = x - parts[-1].astype(F32)
    return parts + [x.astype(BF16)]


def _head_rmsnorm(x, w, group_ones):
    hi, lo = _split_bf16(x * x, 2)
    n = x.shape[0]
    ss = _dot(jnp.concatenate([hi, lo], axis=0), group_ones)
    return x * lax.rsqrt((ss[:n] + ss[n:]) * (1.0 / HEAD_DIM) + NORM_EPS) * w


def _rope(x, cos, sin_signed):
    n = x.shape[-1]
    lane = lax.broadcasted_iota(jnp.int32, x.shape, 1)
    quarter = HEAD_DIM // 4
    partner = jnp.where(lane % (2 * quarter) < quarter,
                        pltpu.roll(x, n - quarter, 1), pltpu.roll(x, quarter, 1))
    return x * cos + partner * sin_signed


def _stream_specs(stream, width, first_block=0):
    head, tail = stream
    tm = ROW_BLOCK
    if head is None:
        return [pl.BlockSpec((tm, width), lambda i: (i + first_block, 0))], [tail]
    n_head = head.shape[0] // tm
    return ([pl.BlockSpec((tm, width), lambda i: (jnp.minimum(i + first_block, n_head - 1), 0)),
             pl.BlockSpec((tm, width), lambda i: (jnp.maximum(i + first_block - n_head, 0), 0))], [head, tail])


def _stream_block(refs, n_head, first_block=0):
    if n_head is None:
        return refs[0][...]
    return jnp.where(pl.program_id(0) + first_block < n_head, refs[0][...], refs[1][...])


def _n_head(stream):
    return None if stream[0] is None else stream[0].shape[0] // ROW_BLOCK


def _inproj_kernel(*refs, n_head):
    n_x = 1 if n_head is None else 2
    (mod_ref, g_ref, w_ref, naq_w, nak_w, swq_w, swk_w, cos_ref, sin_ref, ones_ref, lb_ref,
     hg_ref, naq_ref, nak_ref, nav_ref, swq_ref, swk_ref, swv_ref) = refs[n_x:]
    h = _mod_rmsnorm(_stream_block(refs[:n_x], n_head), g_ref[...], mod_ref[...], 0, 1).astype(BF16)
    ones = ones_ref[...]

    def proj(lo, width):
        return _dot(h, w_ref[:, lo:lo + width])

    naq_ref[...] = (_head_rmsnorm(proj(0, NA_W), naq_w[...], ones) * ATTN_SCALE).astype(BF16)
    nak_ref[...] = _head_rmsnorm(proj(NA_W, NA_W), nak_w[...], ones).astype(BF16)
    nav_ref[...] = proj(2 * NA_W, NA_W).astype(BF16)
    base = 3 * NA_W
    lb = lb_ref[...]
    hg_ref[:, :HG_W] = _silu(proj(base, HG_W)) * (HG_DK ** -0.5)
    for j in (1, 2):
        hg_ref[:, j * HG_W:(j + 1) * HG_W] = jnp.log2(lb + (1.0 - lb) * _sigmoid(proj(base + j * HG_W, HG_W)))
    for j in (3, 4):
        hg_ref[:, j * HG_W:(j + 1) * HG_W] = proj(base + j * HG_W, HG_W)
    base += HG_COLS
    cos, sin = cos_ref[...], sin_ref[...]
    swq = _rope(_head_rmsnorm(proj(base, SWA_W), swq_w[...], ones), cos, sin)
    swq_ref[...] = (swq * ATTN_SCALE).astype(BF16)
    swk_ref[...] = _rope(_head_rmsnorm(proj(base + SWA_W, SWA_W), swk_w[...], ones), cos, sin).astype(BF16)
    swv_ref[...] = proj(base + 2 * SWA_W, SWA_W).astype(BF16)


def _mod_row(i, n_ctx_blocks, blocks_per_batch, n_batch):
    return jnp.where(i < n_ctx_blocks, n_batch, (i - n_ctx_blocks) // blocks_per_batch)


def _inproj(x_stream, mods, norm_g, w_ext, naq_w, nak_w, swq_w, swk_w, cos_t, sin_t, ones, lb, geom):
    n_batch, n_ctx_blocks, blocks_per_batch = geom
    tm = ROW_BLOCK
    t = (n_ctx_blocks + n_batch * blocks_per_batch) * tm
    row = lambda w: pl.BlockSpec((tm, w), lambda i: (i, 0))
    const = lambda a: pl.BlockSpec(a.shape, lambda i: (0,) * a.ndim)
    rope_blk = lambda i: (jnp.where(i < n_ctx_blocks, 0, 1 + (i - n_ctx_blocks) % blocks_per_batch), 0)
    bf = lambda w: jax.ShapeDtypeStruct((t, w), BF16)
    x_specs, x_arrays = _stream_specs(x_stream, D_MODEL)
    return pl.pallas_call(
        functools.partial(_inproj_kernel, n_head=_n_head(x_stream)),
        grid=(t // tm,),
        in_specs=x_specs + [
            pl.BlockSpec((None, 6, D_MODEL), lambda i: (_mod_row(i, n_ctx_blocks, blocks_per_batch, n_batch), 0, 0)),
            const(norm_g), const(w_ext), const(naq_w), const(nak_w), const(swq_w), const(swk_w),
            pl.BlockSpec((tm, SWA_W), rope_blk), pl.BlockSpec((tm, SWA_W), rope_blk), const(ones), const(lb)],
        out_specs=[row(HG_COLS), row(NA_W), row(NA_W), row(NA_W), row(SWA_W), row(SWA_W), row(SWA_W)],
        out_shape=[jax.ShapeDtypeStruct((t, HG_COLS), F32), bf(NA_W), bf(NA_W), bf(NA_W),
                   bf(SWA_W), bf(SWA_W), bf(SWA_W)],
        compiler_params=_cparams("parallel"),
        name="inproj",
    )(*x_arrays, mods, norm_g, w_ext, naq_w, nak_w, swq_w, swk_w, cos_t, sin_t, ones, lb)


def _attend(q, parts, sink_vals, heads_per_pass):
    m_rows, width = q.shape
    n_heads = width // HEAD_DIM
    lane = lax.broadcasted_iota(jnp.int32, (m_rows, width), 1)
    out = jnp.zeros((m_rows, width), F32)
    for h0 in range(0, n_heads, heads_per_pass):
        heads = range(h0, h0 + heads_per_pass)
        rows = slice(h0 * m_rows, (h0 + heads_per_pass) * m_rows)
        in_head = [(lane // HEAD_DIM) == h for h in heads]
        q_stack = jnp.concatenate([jnp.where(m, q, jnp.zeros_like(q)) for m in in_head], axis=0)
        scores = []
        for k, _, add in parts:
            s = _dot_nt(q_stack, k)
            scores.append(s if add is None else s + add[rows])
        mx = functools.reduce(jnp.maximum, [jnp.max(s, axis=-1, keepdims=True) for s in scores])
        if sink_vals is not None:
            head_row = lax.broadcasted_iota(jnp.int32, (heads_per_pass * m_rows, 1), 0) // m_rows + h0
            sink = functools.reduce(jnp.add, [jnp.where(head_row == h, sink_vals[h], 0.0) for h in heads])
            mx = jnp.maximum(mx, sink)
        ps = [jnp.exp(s - mx) for s in scores]
        den = functools.reduce(jnp.add, [jnp.sum(p, axis=-1, keepdims=True) for p in ps])
        if sink_vals is not None:
            den = den + jnp.exp(sink - mx)
        acc = functools.reduce(jnp.add, [_dot(p.astype(BF16), v) for p, (_, v, _) in zip(ps, parts)]) / den
        for i, m in enumerate(in_head):
            out = jnp.where(m, acc[i * m_rows:(i + 1) * m_rows], out)
    return out


def _na_kernel(q_ref, k0, k1, k2, kc, v0, v1, v2, vc, bias_ref, o_ref):
    k_lat = jnp.concatenate([k0[...], k1[...], k2[...]], axis=0)
    v_lat = jnp.concatenate([v0[...], v1[...], v2[...]], axis=0)
    bias = bias_ref[...]
    bias = bias.reshape(bias.shape[0] * bias.shape[1], bias.shape[2])
    o = _attend(q_ref[...], [(k_lat, v_lat, bias), (kc[...], vc[...], None)], None, NA_HEADS_PER_PASS)
    o_ref[...] = o.astype(o_ref.dtype)


def _na_attention(naq, nak, nav, bias, geom):
    n_batch, n_ctx_blocks, blocks_per_batch = geom
    tm = ROW_BLOCK
    n_rb = blocks_per_batch
    kb_max = n_rb - NA_KROWS // NA_QROWS
    lat = lambda rb, b: n_ctx_blocks + b * blocks_per_batch

    def band(j):
        return pl.BlockSpec((tm, NA_W), lambda rb, b: (lat(rb, b) + jnp.clip(rb - 1, 0, kb_max) + j, 0))

    ctx = pl.BlockSpec((tm, NA_W), lambda rb, b: (b, 0))
    return pl.pallas_call(
        _na_kernel,
        grid=(n_rb, n_batch),
        in_specs=[pl.BlockSpec((tm, NA_W), lambda rb, b: (lat(rb, b) + rb, 0)),
                  band(0), band(1), band(2), ctx, band(0), band(1), band(2), ctx,
                  pl.BlockSpec((None, NA_HEADS, tm, NA_KROWS * GRID_W), lambda rb, b: (
                      jnp.where(rb == 0, 0, jnp.where(rb == n_rb - 1, 2, 1)), 0, 0, 0))],
        out_specs=pl.BlockSpec((tm, NA_W), lambda rb, b: (b * blocks_per_batch + rb, 0)),
        out_shape=jax.ShapeDtypeStruct((n_batch * blocks_per_batch * tm, NA_W), BF16),
        compiler_params=_cparams("parallel", "parallel"),
        name="na_attn",
    )(naq, nak, nak, nak, nak, nav, nav, nav, nav, bias)


def _swa_kernel(sink_ref, q_ref, kp, kc_, kn, kx, vp, vc_, vn, vx, o_ref, *, n_blocks):
    n = pl.program_id(1)
    tq = q_ref.shape[0]
    k_lat = jnp.concatenate([kp[...], kc_[...], kn[...]], axis=0)
    v_lat = jnp.concatenate([vp[...], vc_[...], vn[...]], axis=0)
    qi = lax.broadcasted_iota(jnp.int32, (SWA_Q_HEADS * tq, 3 * tq), 0) % tq
    mi = lax.broadcasted_iota(jnp.int32, (SWA_Q_HEADS * tq, 3 * tq), 1)
    kpos = (n - 1) * tq + mi
    ok = (jnp.abs(qi + tq - mi) <= SWA_WINDOW) & (kpos >= 0) & (kpos < n_blocks * tq)
    add = jnp.where(ok, 0.0, NEG_INF).astype(F32)
    sinks = [sink_ref[h] for h in range(SWA_Q_HEADS)]
    o = _attend(q_ref[...], [(k_lat, v_lat, add), (kx[...], vx[...], None)], sinks, SWA_HEADS_PER_PASS)
    o_ref[...] = o.astype(o_ref.dtype)


def _swa_attention(swq, swk, swv, sink, geom, seq, ctx_len):
    n_batch, _, _ = geom
    tq = SWA_BLOCK
    nb = seq // tq
    first = n_batch * ctx_len // tq
    blk = lambda f: pl.BlockSpec((tq, SWA_W), lambda b, n: (first + b * nb + f(n), 0))
    prev, cur, nxt = blk(lambda n: jnp.maximum(n - 1, 0)), blk(lambda n: n), blk(lambda n: jnp.minimum(n + 1, nb - 1))
    ctx = pl.BlockSpec((ctx_len, SWA_W), lambda b, n: (b, 0))
    return pl.pallas_call(
        functools.partial(_swa_kernel, n_blocks=nb),
        grid=(n_batch, nb),
        in_specs=[pl.BlockSpec(memory_space=pltpu.SMEM), cur, prev, cur, nxt, ctx, prev, cur, nxt, ctx],
        out_specs=pl.BlockSpec((tq, SWA_W), lambda b, n: (b * nb + n, 0)),
        out_shape=jax.ShapeDtypeStruct((n_batch * seq, SWA_W), BF16),
        compiler_params=_cparams("parallel", "parallel"),
        name="swa_attn",
    )(sink, swq, swk, swk, swk, swk, swv, swv, swv, swv)


def _ctx_attn_kernel(sink_ref, naq, nak, nav, swq, swk, swv, ona_ref, osw_ref):
    ona_ref[...] = _attend(naq[...], [(nak[...], nav[...], None)], None, NA_HEADS_PER_PASS).astype(ona_ref.dtype)
    sinks = [sink_ref[h] for h in range(SWA_Q_HEADS)]
    osw_ref[...] = _attend(swq[...], [(swk[...], swv[...], None)], sinks, NA_HEADS_PER_PASS).astype(osw_ref.dtype)


def _ctx_attention(naq, nak, nav, swq, swk, swv, sink, n_batch, ctx_len):
    blk = pl.BlockSpec((ctx_len, NA_W), lambda b: (b, 0))
    out = jax.ShapeDtypeStruct((n_batch * ctx_len, NA_W), BF16)
    return pl.pallas_call(
        _ctx_attn_kernel,
        grid=(n_batch,),
        in_specs=[pl.BlockSpec(memory_space=pltpu.SMEM)] + [blk] * 6,
        out_specs=[blk, blk],
        out_shape=[out, out],
        compiler_params=_cparams("parallel"),
        name="ctx_attn",
    )(sink, naq, nak, nav, swq, swk, swv)


def _dot_exact_lhs(m, x):
    out = _dot(m, jnp.concatenate(_split_bf16(x, 3), axis=1))
    n = x.shape[1]
    return out[:, :n] + out[:, n:2 * n] + out[:, 2 * n:]


def _hgrn_sum_table(n_rows, reverse):
    tri = np.tril(np.ones((n_rows, n_rows), np.float32))
    return tri.T if reverse else tri


def _tile_row(x, r):
    tiles = x.reshape(x.shape[0] // SUBLANES, SUBLANES, x.shape[1])
    return jnp.broadcast_to(tiles[:, r:r + 1, :], tiles.shape).reshape(x.shape)


def _hgrn_masks(n_rows):
    row = lax.broadcasted_iota(jnp.int32, (n_rows, LANES), 0)
    row_a = lax.broadcasted_iota(jnp.int32, (n_rows, n_rows), 0)
    col_a = lax.broadcasted_iota(jnp.int32, (n_rows, n_rows), 1)
    sizes = [1 << b for b in range(1, n_rows.bit_length())]
    return dict(
        diag=row_a == col_a,
        same={s: row_a // s == col_a // s for s in sizes if s < n_rows},
        upper_half={s: row % s >= s // 2 for s in sizes if s < 2 * SUBLANES},
        from_row={r: row % SUBLANES >= r for r in range(2, SUBLANES, 2)})


def _hgrn_block(q, k, g, v, st, sums, ones, masks, reverse):
    n_rows = q.shape[0]
    cum = _dot_exact_lhs(sums, g)
    tot = cum[0:1] if reverse else cum[n_rows - 1:n_rows]
    o_inter = _dot_nt((q * jnp.exp2(cum)).astype(BF16), st.astype(BF16))
    k_end = (k * jnp.exp2(tot - cum)).astype(BF16)
    st_new = st * jnp.exp2(tot) + _dot_tn(v.astype(BF16), k_end)

    a = None
    zero_tile = jnp.zeros((SUBLANES, LANES), F32)
    size = n_rows
    while size >= 2:
        half = size // 2
        if size < 2 * SUBLANES:
            ref_off = half if reverse else half - 1
            ref = _tile_row(cum, ref_off)
            for first in range(size, SUBLANES, size):
                ref = jnp.where(masks["from_row"][first], _tile_row(cum, first + ref_off), ref)
            upper = masks["upper_half"][size]
            d = cum - ref
            q_exp = jnp.where(upper, NEG_INF, d) if reverse else jnp.where(upper, d, NEG_INF)
            k_exp = jnp.where(upper, -d, NEG_INF) if reverse else jnp.where(upper, NEG_INF, -d)
            q_t = (q * jnp.exp2(q_exp)).astype(BF16)
            k_t = (k * jnp.exp2(k_exp)).astype(BF16)
        else:
            q_tiles, k_tiles = [], []
            for r0 in range(0, n_rows, SUBLANES):
                sl = slice(r0, r0 + SUBLANES)
                first = r0 // size * size
                ref_row = first + (half if reverse else half - 1)
                ref = cum[ref_row:ref_row + 1]
                if (r0 - first < half) if reverse else (r0 - first >= half):
                    q_tiles.append(q[sl] * jnp.exp2(cum[sl] - ref))
                    k_tiles.append(zero_tile)
                else:
                    q_tiles.append(zero_tile)
                    k_tiles.append(k[sl] * jnp.exp2(ref - cum[sl]))
            q_t = jnp.concatenate(q_tiles, axis=0).astype(BF16)
            k_t = jnp.concatenate(k_tiles, axis=0).astype(BF16)
        a_l = _dot_nt(q_t, k_t)
        a = a_l if a is None else jnp.where(masks["same"][size], a_l, a)
        size //= 2
    a = jnp.where(masks["diag"], _dot((q * k).astype(BF16), ones), a)
    return o_inter + _dot(a.astype(BF16), v.astype(BF16)), st_new


def _hgrn_kernel(q_f, g_f, v_f, q_b, g_b, v_b, trif_ref, trib_ref, ones_ref, of_ref, ob_ref, st_f, st_b):
    @pl.when(pl.program_id(2) == 0)
    def _():
        st_f[...] = jnp.zeros_like(st_f)
        st_b[...] = jnp.zeros_like(st_b)

    ones = ones_ref[...]
    masks = _hgrn_masks(q_f.shape[0])
    for h in range(HG_HEADS_PER_STEP):
        sl = slice(h * HG_DK, (h + 1) * HG_DK)
        g = g_f[:, sl]
        of_ref[:, sl], st_f[h] = _hgrn_block(q_f[:, sl], 1.0 - jnp.exp2(g), g, v_f[:, sl], st_f[h],
                                             trif_ref[...], ones, masks, False)
        g = g_b[:, sl]
        ob_ref[:, sl], st_b[h] = _hgrn_block(q_b[:, sl], 1.0 - jnp.exp2(g), g, v_b[:, sl], st_b[h],
                                             trib_ref[...], ones, masks, True)


def _hgrn(hg, tri_f, tri_b, ones, n_batch, seq, ctx_len):
    tr = HG_STEP_ROWS
    nc, nl = ctx_len // tr, seq // tr
    first = n_batch * nc

    def fwd(b, j):
        return jnp.where(j < nc, b * nc + j, first + b * nl + (j - nc))

    def bwd(b, j):
        return jnp.where(j < nc, b * nc + (nc - 1 - j), first + b * nl + (nl - 1 - (j - nc)))

    hps = HG_HEADS_PER_STEP
    groups = HG_HEADS // hps
    wide = hps * HG_DK
    col = lambda row_fn, group: pl.BlockSpec((tr, wide), lambda b, h, j: (row_fn(b, j), group * groups + h))
    const = lambda a: pl.BlockSpec(a.shape, lambda b, h, j: (0, 0))
    out = jax.ShapeDtypeStruct((hg.shape[0], HG_W), F32)
    return pl.pallas_call(
        _hgrn_kernel,
        grid=(n_batch, groups, nc + nl),
        in_specs=[col(fwd, 0), col(fwd, 1), col(fwd, 3), col(bwd, 0), col(bwd, 2), col(bwd, 3),
                  const(tri_f), const(tri_b), const(ones)],
        out_specs=[pl.BlockSpec((tr, wide), lambda b, h, j: (fwd(b, j), h)),
                   pl.BlockSpec((tr, wide), lambda b, h, j: (bwd(b, j), h))],
        out_shape=[out, out],
        scratch_shapes=[pltpu.VMEM((hps, HG_DK, HG_DK), F32), pltpu.VMEM((hps, HG_DK, HG_DK), F32)],
        compiler_params=_cparams("parallel", "parallel", "arbitrary"),
        name="hgrn",
    )(hg, hg, hg, hg, hg, hg, tri_f, tri_b, ones)


def _mixout_kernel(*refs, n_heads, stream_off):
    counts = [1 if n is None else 2 for n in n_heads]
    streams, pos = [], 0
    for n, c in zip(n_heads, counts):
        streams.append((refs[pos:pos + c], n))
        pos += c
    (of_ref, ob_ref, zg_ref, hgn_ref, w_ref, mod_ref, g2_ref, rw_ref, rb_ref,
     xo_ref, h2_ref, te_ref, tg_ref) = refs[pos:]
    x_in = _stream_block(*streams[0], first_block=stream_off)
    o = of_ref[...] + ob_ref[...]
    zg = zg_ref[...]
    parts = [_stream_block(*streams[1])]
    for h in range(HG_HEADS):
        sl = slice(h * HG_DK, (h + 1) * HG_DK)
        oh = o[:, sl]
        yh = oh * lax.rsqrt(jnp.mean(oh * oh, axis=-1, keepdims=True) + NORM_EPS) * hgn_ref[...]
        parts.append((yh * _silu(zg[:, sl])).astype(BF16))
    parts.append(_stream_block(*streams[2]))
    y = _dot(jnp.concatenate(parts, axis=1), w_ref[...])
    mod = mod_ref[...]
    x_new = x_in + mod[2:3] * y
    xo_ref[...] = x_new
    h2 = _mod_rmsnorm(x_new, g2_ref[...], mod, 3, 4)
    h2_ref[...] = h2.astype(BF16)

    h_hi, h_lo = _split_bf16(h2, 2)
    logits = _dot(jnp.concatenate([h_hi, h_lo, h_hi], axis=1), rw_ref[...]) + rb_ref[...]
    lane = lax.broadcasted_iota(jnp.int32, logits.shape, 1).astype(F32)
    top_e = jnp.zeros(logits.shape, F32)
    top_v = jnp.full(logits.shape, NEG_INF, F32)
    for j in range(TOP_K):
        best = jnp.max(logits, axis=-1, keepdims=True)
        arg = jnp.min(jnp.where(logits == best, lane, float(LANES)), axis=-1, keepdims=True)
        top_e = jnp.where(lane == j, arg, top_e)
        top_v = jnp.where(lane == j, best, top_v)
        logits = jnp.where(lane == arg, -jnp.inf, logits)
    ex = jnp.exp(top_v - jnp.max(top_v, axis=-1, keepdims=True))
    te_ref[...] = top_e.astype(jnp.int32)
    tg_ref[...] = ex / jnp.sum(ex, axis=-1, keepdims=True)


def _mixout(x_stream, yna_stream, o_f, o_b, hg, ysw_stream, hgn, w_out, mods, norm2_g, rw, rb, geom, stream_off):
    n_batch, n_ctx_blocks, blocks_per_batch = geom
    tm = ROW_BLOCK
    t = (n_ctx_blocks + n_batch * blocks_per_batch - stream_off) * tm
    row = lambda w: pl.BlockSpec((tm, w), lambda i: (i, 0))
    full = lambda w, cb: pl.BlockSpec((tm, w), lambda i: (i + stream_off, cb))
    const = lambda a: pl.BlockSpec(a.shape, lambda i: (0,) * a.ndim)
    x_specs, x_arrays = _stream_specs(x_stream, D_MODEL, stream_off)
    yna_specs, yna_arrays = _stream_specs(yna_stream, NA_W)
    ysw_specs, ysw_arrays = _stream_specs(ysw_stream, SWA_W)
    n_heads = (_n_head(x_stream), _n_head(yna_stream), _n_head(ysw_stream))
    return pl.pallas_call(
        functools.partial(_mixout_kernel, n_heads=n_heads, stream_off=stream_off),
        grid=(t // tm,),
        in_specs=x_specs + yna_specs + ysw_specs + [
            full(HG_W, 0), full(HG_W, 0), full(HG_W, 4), const(hgn), const(w_out),
            pl.BlockSpec((None, 6, D_MODEL), lambda i: (
                _mod_row(i + stream_off, n_ctx_blocks, blocks_per_batch, n_batch), 0, 0)),
            const(norm2_g), const(rw), const(rb)],
        out_specs=[row(D_MODEL), row(D_MODEL), row(LANES), row(LANES)],
        out_shape=[jax.ShapeDtypeStruct((t, D_MODEL), F32), jax.ShapeDtypeStruct((t, D_MODEL), BF16),
                   jax.ShapeDtypeStruct((t, LANES), jnp.int32), jax.ShapeDtypeStruct((t, LANES), F32)],
        compiler_params=_cparams("parallel"),
        name="mixout",
    )(*x_arrays, *yna_arrays, *ysw_arrays, o_f, o_b, hg, hgn, w_out, mods, norm2_g, rw, rb)


def _start_run_copies(run_ref, n_experts, make_copy):
    for e in range(n_experts):
        n = run_ref[0, 0, e]
        local0 = run_ref[0, 0, n_experts + e]
        slot0 = run_ref[0, 0, 2 * n_experts + e]
        for bit in reversed(range(RUN_BITS)):
            @pl.when((n & (1 << bit)) != 0)
            def _():
                done = (n >> (bit + 1)) << (bit + 1)
                make_copy(local0 + done, slot0 + done, 1 << bit).start()


def _tile_rows(first_token, n_tokens):
    return pl.ds(pl.multiple_of(first_token * SUBLANES, SUBLANES), n_tokens * SUBLANES)


def _local_positions(te_ref, off_ref, stri_ref):
    e_tok = te_ref[...]
    lane = lax.broadcasted_iota(jnp.int32, e_tok.shape, 1)
    member = [lane == e_tok[:, j:j + 1] for j in range(TOP_K)]
    count = functools.reduce(jnp.add, [jnp.where(m, 1.0, 0.0) for m in member])
    before = _dot(stri_ref[...], count.astype(BF16)) + off_ref[...]
    return [jnp.sum(jnp.where(m, before, 0.0), axis=1, keepdims=True) for m in member]


def _dispatch_kernel(pad_ref, nu_ref, run_ref, te_ref, off_ref, h_ref, stri_ref, xs_hbm, pos_ref,
                     zeros, xloc0, xloc1, sem, zero_sem, *, n_tok_blocks):
    n_tok = h_ref.shape[0]
    block_rows = zeros.shape[0]
    n_blocks = xs_hbm.shape[0] // block_rows

    def zero_slot(e, r):
        row = pl.multiple_of((pad_ref[0, e] + r) * SUBLANES, SUBLANES)
        return pltpu.make_async_copy(zeros.at[pl.ds(0, SUBLANES), :], xs_hbm.at[pl.ds(row, SUBLANES), :], zero_sem)

    def zero_block(b):
        row = pl.multiple_of(b * block_rows, block_rows)
        return pltpu.make_async_copy(zeros, xs_hbm.at[pl.ds(row, block_rows), :], zero_sem)

    def for_each_unused(slot_fn, block_fn):
        def per_expert(e, carry):
            lax.fori_loop(0, pad_ref[1, e], lambda r, c: (slot_fn(e, r), c)[1], 0)
            return carry

        lax.fori_loop(0, pad_ref.shape[1], per_expert, 0)
        lax.fori_loop(nu_ref[0], n_blocks, lambda b, c: (block_fn(b), c)[1], 0)

    @pl.when(pl.program_id(0) == 0)
    def _():
        zeros[...] = jnp.zeros_like(zeros)
        for_each_unused(lambda e, r: zero_slot(e, r).start(), lambda b: zero_block(b).start())
        for_each_unused(lambda e, r: zero_slot(e, r).wait(), lambda b: zero_block(b).wait())

    bufs = (xloc0, xloc1)

    def sort_into(b):
        pos = _local_positions(te_ref, off_ref, stri_ref)
        lane = lax.broadcasted_iota(jnp.int32, (n_tok, LANES), 1)
        pos_ref[...] = functools.reduce(jnp.add, [jnp.where(lane == j, p, 0.0) for j, p in enumerate(pos)])
        col = lax.broadcasted_iota(jnp.int32, (n_tok, TOP_K * n_tok), 1).astype(F32)
        chosen = functools.reduce(jnp.logical_or, [col == p for p in pos])
        sorted_rows = _dot_tn(jnp.where(chosen, 1.0, 0.0).astype(BF16), h_ref[...])
        _store_token_tiles(bufs[b], sorted_rows)

    def start_copies(b):
        _start_run_copies(run_ref, pad_ref.shape[1], lambda local, slot, size: pltpu.make_async_copy(
            bufs[b].at[_tile_rows(local, size), :], xs_hbm.at[_tile_rows(slot, size), :], sem.at[b]))

    def wait_copies(b):
        pltpu.make_async_copy(bufs[b], xs_hbm.at[pl.ds(0, bufs[b].shape[0]), :], sem.at[b]).wait()

    i = pl.program_id(0)

    @pl.when(i == 0)
    def _():
        sort_into(0)

    for parity in (0, 1):
        @pl.when((i % 2 == parity) & (i >= 1) & (i < n_tok_blocks))
        def _():
            @pl.when(i >= 2)
            def _():
                wait_copies(parity)

            start_copies(1 - parity)
            sort_into(parity)

    @pl.when(i == n_tok_blocks)
    def _():
        if n_tok_blocks >= 2:
            wait_copies(n_tok_blocks % 2)
        start_copies((n_tok_blocks - 1) % 2)
        wait_copies((n_tok_blocks - 1) % 2)


def _dispatch(h, top_e, runs, run_first, pad_slots, n_used, n_slots):
    tm = ROW_BLOCK
    t = h.shape[0]
    nb = t // tm
    assert n_slots % MOE_ROWS == 0
    stri = jnp.asarray(np.tril(np.ones((tm, tm), np.float32), -1), BF16)
    cur = lambda i: jnp.minimum(i, nb - 1)
    row = lambda w: pl.BlockSpec((tm, w), lambda i: (cur(i), 0))
    sort_buffer = pltpu.VMEM((TOP_K * tm * SUBLANES, LANES), F32)
    return pl.pallas_call(
        functools.partial(_dispatch_kernel, n_tok_blocks=nb),
        grid=(nb + 1,),
        in_specs=[pl.BlockSpec(memory_space=pltpu.SMEM), pl.BlockSpec(memory_space=pltpu.SMEM),
                  pl.BlockSpec((1, 1, runs.shape[2]), lambda i: (jnp.maximum(i - 1, 0), 0, 0),
                               memory_space=pltpu.SMEM),
                  row(LANES), pl.BlockSpec((None, 1, LANES), lambda i: (cur(i), 0, 0)), row(D_MODEL),
                  pl.BlockSpec((tm, tm), lambda i: (0, 0))],
        out_specs=[pl.BlockSpec(memory_space=pl.ANY), row(LANES)],
        out_shape=[jax.ShapeDtypeStruct((n_slots * SUBLANES, LANES), F32), jax.ShapeDtypeStruct((t, LANES), F32)],
        scratch_shapes=[pltpu.VMEM((MOE_ROWS * SUBLANES, LANES), F32), sort_buffer, sort_buffer,
                        pltpu.SemaphoreType.DMA((2,)), pltpu.SemaphoreType.DMA(())],
        compiler_params=_cparams("arbitrary"),
        name="moe_dispatch",
    )(pad_slots, n_used, runs, top_e, run_first, h, stri)


def _moe_kernel(be_ref, nu_ref, xs_ref, wgu_ref, bgu_ref, wd_ref, bd_ref, y_ref, wgu_bf, wd_bf):
    i = pl.program_id(0)
    live = i < nu_ref[0]
    tm = y_ref.shape[0] // SUBLANES
    changed = (i == 0) | (be_ref[i] != be_ref[jnp.maximum(i - 1, 0)])

    @pl.when(live & changed)
    def _():
        wgu_bf[...] = wgu_ref[...].astype(BF16)
        wd_bf[...] = wd_ref[...].astype(BF16)

    @pl.when(live)
    def _():
        x = _load_token_tiles(xs_ref, 0, tm)
        gu = _dot(x.astype(BF16), wgu_bf[...]) + bgu_ref[...]
        glu = jnp.minimum(gu[:, :D_FF], SWIGLU_LIMIT)
        lin = jnp.clip(gu[:, D_FF:], -SWIGLU_LIMIT, SWIGLU_LIMIT)
        act = glu * _sigmoid(SWIGLU_ALPHA * glu) * (lin + 1.0)
        _store_token_tiles(y_ref, _dot(act.astype(BF16), wd_bf[...]) + bd_ref[...])

    @pl.when(jnp.logical_not(live))
    def _():
        y_ref[...] = jnp.zeros_like(y_ref)


def _moe_ffn(xs_tiles, block_e, n_used, w_gu, b_gu, w_down, b_down, layer):
    tm = MOE_ROWS
    nb = xs_tiles.shape[0] // (tm * SUBLANES)
    _, n_e, d, f2 = w_gu.shape
    weight = lambda r, c: pl.BlockSpec((None, None, r, c), lambda i, be, nu: (layer, be[i], 0, 0))
    grid_spec = pltpu.PrefetchScalarGridSpec(
        num_scalar_prefetch=2,
        grid=(nb,),
        in_specs=[pl.BlockSpec((tm * SUBLANES, LANES), lambda i, be, nu: (jnp.minimum(i, nu[0] - 1), 0)),
                  weight(d, f2), weight(1, f2), weight(f2 // 2, d), weight(1, d)],
        out_specs=pl.BlockSpec((tm * SUBLANES, LANES), lambda i, be, nu: (i, 0)),
        scratch_shapes=[pltpu.VMEM((d, f2), BF16), pltpu.VMEM((f2 // 2, d), BF16)],
    )
    n_l = w_gu.shape[0]
    return pl.pallas_call(
        _moe_kernel,
        grid_spec=grid_spec,
        out_shape=jax.ShapeDtypeStruct((nb * tm * SUBLANES, LANES), F32),
        compiler_params=_cparams("arbitrary"),
        name="moe_ffn",
    )(block_e, n_used, xs_tiles, w_gu, b_gu.reshape(n_l, n_e, 1, f2), w_down,
      b_down.reshape(n_l, n_e, 1, d))


def _combine_kernel(run_ref, pos_ref, tg_ref, x_ref, y_hbm, mod_ref, o_ref, ybuf0, ybuf1, sem, *, n_tok_blocks):
    i = pl.program_id(0)
    tm = o_ref.shape[0]
    n_rows = TOP_K * tm
    bufs = (ybuf0, ybuf1)

    def fetch(b):
        _start_run_copies(run_ref, run_ref.shape[2] // 3, lambda local, slot, size: pltpu.make_async_copy(
            y_hbm.at[_tile_rows(slot, size), :], bufs[b].at[_tile_rows(local, size), :], sem.at[b]))

    def wait_fetch(b):
        pltpu.make_async_copy(y_hbm.at[pl.ds(0, n_rows * SUBLANES), :], bufs[b], sem.at[b]).wait()

    def combine(b):
        y = _load_token_tiles(bufs[b], 0, n_rows).astype(BF16)
        pos, gate = pos_ref[...], tg_ref[...]
        col = lax.broadcasted_iota(jnp.int32, (tm, n_rows), 1).astype(F32)
        g = functools.reduce(jnp.add, [jnp.where(col == pos[:, j:j + 1], gate[:, j:j + 1], 0.0)
                                       for j in range(TOP_K)])
        g_hi, g_lo = _split_bf16(g, 2)
        o_ref[...] = x_ref[...] + mod_ref[...][5:6] * (_dot(g_hi, y) + _dot(g_lo, y))

    @pl.when(i == 0)
    def _():
        fetch(0)

    for parity in (0, 1):
        @pl.when((i % 2 == parity) & (i >= 1) & (i < n_tok_blocks))
        def _():
            wait_fetch(1 - parity)
            fetch(parity)
            combine(1 - parity)

    @pl.when(i == n_tok_blocks)
    def _():
        wait_fetch((n_tok_blocks - 1) % 2)
        combine((n_tok_blocks - 1) % 2)


def _combine(x, y_tiles, runs, pos, gates, mods, geom, stream_off):
    n_batch, n_ctx_blocks, blocks_per_batch = geom
    t = x.shape[0]
    tm = ROW_BLOCK
    nb = t // tm
    prev = lambda i: jnp.maximum(i - 1, 0)
    row = lambda w: pl.BlockSpec((tm, w), lambda i: (prev(i), 0))
    sorted_buffer = pltpu.VMEM((TOP_K * tm * SUBLANES, LANES), F32)
    return pl.pallas_call(
        functools.partial(_combine_kernel, n_tok_blocks=nb),
        grid=(nb + 1,),
        in_specs=[pl.BlockSpec((1, 1, runs.shape[2]), lambda i: (jnp.minimum(i, nb - 1), 0, 0),
                               memory_space=pltpu.SMEM),
                  row(LANES), row(LANES), row(D_MODEL), pl.BlockSpec(memory_space=pl.ANY),
                  pl.BlockSpec((None, 6, D_MODEL), lambda i: (
                      _mod_row(prev(i) + stream_off, n_ctx_blocks, blocks_per_batch, n_batch), 0, 0))],
        out_specs=row(D_MODEL),
        out_shape=jax.ShapeDtypeStruct((t, D_MODEL), F32),
        scratch_shapes=[sorted_buffer, sorted_buffer, pltpu.SemaphoreType.DMA((2,))],
        compiler_params=_cparams("arbitrary"),
        name="moe_combine",
    )(runs, pos, gates, x, y_tiles, mods)


def _route(top_e, n_experts, tok_block, slot_block):
    t, k = top_e.shape
    nb = t // tok_block
    onehot = (top_e[:, :, None] == jnp.arange(n_experts, dtype=jnp.int32)).astype(jnp.int32)
    n = onehot.reshape(nb, tok_block * k, n_experts).sum(axis=1)
    counts = n.sum(axis=0)
    padded = (counts + slot_block - 1) // slot_block * slot_block
    pend = jnp.cumsum(padded)
    start = pend - padded
    local0 = jnp.cumsum(n, axis=1) - n
    slot0 = start[None, :] + jnp.cumsum(n, axis=0) - n
    runs = jnp.concatenate([n, local0, slot0], axis=1).astype(jnp.int32).reshape(nb, 1, 3 * n_experts)
    run_first = jnp.pad(local0.astype(F32), ((0, 0), (0, LANES - n_experts))).reshape(nb, 1, LANES)
    n_blocks = (t * k + n_experts * (slot_block - 1) + slot_block - 1) // slot_block
    block_start = jnp.arange(n_blocks, dtype=jnp.int32) * slot_block
    block_e = jnp.minimum((block_start[:, None] >= pend[None, :]).sum(axis=1), n_experts - 1).astype(jnp.int32)
    n_used = (pend[-1] // slot_block).astype(jnp.int32).reshape(1)
    pad_slots = jnp.stack([start + counts, padded - counts], axis=0).astype(jnp.int32)
    return runs, run_first, block_e, n_used, pad_slots, n_blocks * slot_block


def _na_row_pattern(rb, n_rows):
    wr = min(NA_WIN_H, n_rows)
    n_rb = n_rows // NA_QROWS
    q_row = rb * NA_QROWS + np.arange(NA_QROWS)[:, None]
    k_row = np.clip(rb - 1, 0, n_rb - NA_KROWS // NA_QROWS) * NA_QROWS + np.arange(NA_KROWS)[None, :]
    row_start = np.clip(q_row - wr // 2, 0, n_rows - wr)
    ok = (k_row >= row_start) & (k_row < row_start + wr)
    return np.where(ok, k_row - q_row + (NA_WIN_H - 1), -1)


def _na_bias_table(rpb, seq):
    n_rows = seq // GRID_W
    n_rb = n_rows // NA_QROWS
    patterns = [_na_row_pattern(rb, n_rows) for rb in range(n_rb)]
    assert all((p == patterns[1]).all() for p in patterns[1:-1])
    q_col = np.arange(GRID_W)[:, None]
    k_col = np.arange(GRID_W)[None, :]
    col_start = np.clip(q_col - NA_WIN_W // 2, 0, GRID_W - NA_WIN_W)
    col_ok = (k_col >= col_start) & (k_col < col_start + NA_WIN_W)
    dc = np.clip(k_col - q_col + (NA_WIN_W - 1), 0, 2 * NA_WIN_W - 2)
    onehot = ((dc[None] == np.arange(2 * NA_WIN_W - 1)[:, None, None]) & col_ok[None]).astype(np.float32)
    by_col = jnp.einsum('hab,bqk->haqk', rpb.astype(F32), jnp.asarray(onehot), precision=HIGHEST)
    by_col = by_col + jnp.asarray(np.where(col_ok, 0.0, NEG_INF).astype(np.float32))
    masked = jnp.full((rpb.shape[0], GRID_W, GRID_W), NEG_INF, F32)
    variants = []
    for pattern in (patterns[0], patterns[1], patterns[-1]):
        rows = [jnp.concatenate([by_col[:, a] if a >= 0 else masked for a in pattern[qr]], axis=2)
                for qr in range(NA_QROWS)]
        variants.append(jnp.concatenate(rows, axis=1))
    return jnp.stack(variants, axis=0)


def _rope_tables(seq, ctx_len):
    quarter = HEAD_DIM // 4
    lane = np.arange(SWA_W)
    inv = ROPE_BASE ** (-(lane % quarter).astype(np.float64) / quarter)
    t = np.arange(seq)
    pos = np.where((lane % HEAD_DIM < HEAD_DIM // 2)[None, :], (t // GRID_W)[:, None], (t % GRID_W)[:, None])
    ang = jnp.asarray(pos, F32) * jnp.asarray(inv, F32)[None, :]
    sign = np.where(lane % (2 * quarter) < quarter, -1.0, 1.0).astype(np.float32)
    cos = jnp.concatenate([jnp.ones((ctx_len, SWA_W), F32), jnp.cos(ang)], axis=0)
    sin = jnp.concatenate([jnp.zeros((ctx_len, SWA_W), F32), jnp.sin(ang) * sign[None, :]], axis=0)
    return cos, sin


def kernel(x, c, ctx, c_ctx, hg_lower_bounds, ada_w, ada_b, norm1_g, norm2_g, w_in, na_q_norm, na_k_norm, na_rpb,
           hg_norm_g, swa_q_norm, swa_k_norm, swa_sink, w_out, router_w, router_b, w_gu, b_gu, w_down, b_down):
    n_batch, seq, d = x.shape
    ctx_len = ctx.shape[1]
    depth = ada_w.shape[0]
    assert d == D_MODEL and seq % ROW_BLOCK == 0 and ctx_len == ROW_BLOCK
    n_ctx_rows = n_batch * ctx_len
    geom = (n_batch, n_ctx_rows // ROW_BLOCK, seq // ROW_BLOCK)

    p_lb = jax.nn.softmax(hg_lower_bounds.astype(F32), axis=0)
    lbs = jnp.cumsum(p_lb, axis=0) - p_lb[0]

    cos_t, sin_t = _rope_tables(seq, ctx_len)
    lane = np.arange(NA_W)
    group_ones = jnp.asarray((lane[:, None] // HEAD_DIM == lane[None, :] // HEAD_DIM).astype(np.float32), BF16)
    tri_f = jnp.asarray(_hgrn_sum_table(HG_STEP_ROWS, False), BF16)
    tri_b = jnp.asarray(_hgrn_sum_table(HG_STEP_ROWS, True), BF16)
    ones_bf = jnp.ones((LANES, HG_STEP_ROWS), BF16)
    n_mod_rows = -(-(n_batch + 1) // 8) * 8
    cc = jnp.zeros((n_mod_rows, d), F32).at[:n_batch].set(c).at[n_batch].set(c_ctx)
    dup = lambda w: jnp.concatenate([w[:, :HEAD_DIM], w[:, :HEAD_DIM], w[:, HEAD_DIM:], w[:, HEAD_DIM:]], axis=1)
    tile4 = lambda g: jnp.tile(g.astype(F32), 4).reshape(1, 4 * HEAD_DIM)
    pad_e = LANES - N_EXPERTS

    x_stream = (ctx.reshape(n_ctx_rows, d), x.reshape(n_batch * seq, d))
    for l in range(depth):
        last = l == depth - 1
        mods = _ada_mod(cc, ada_w[l], ada_b[l]).reshape(n_mod_rows, 6, d)
        w = w_in[l]
        kv0 = 3 * NA_W + HG_COLS + SWA_W
        w_ext = jnp.concatenate([w[:, :kv0], dup(w[:, kv0:kv0 + SWA_KV_W]), dup(w[:, kv0 + SWA_KV_W:])],
                                axis=1).astype(BF16)
        hg, naq, nak, nav, swq, swk, swv = _inproj(
            x_stream, mods, norm1_g[l].reshape(1, d), w_ext, tile4(na_q_norm[l]), tile4(na_k_norm[l]),
            tile4(swa_q_norm[l]), tile4(swa_k_norm[l]), cos_t, sin_t, group_ones, lbs[l].reshape(1, HG_W), geom)

        y_na = _na_attention(naq, nak, nav, _na_bias_table(na_rpb[l], seq), geom)
        y_sw = _swa_attention(swq, swk, swv, swa_sink[l].astype(F32), geom, seq, ctx_len)
        o_f, o_b = _hgrn(hg, tri_f, tri_b, ones_bf, n_batch, seq, ctx_len)

        if last:
            stream_off = geom[1]
            yna_stream, ysw_stream = (None, y_na), (None, y_sw)
        else:
            stream_off = 0
            yc_na, yc_sw = _ctx_attention(naq, nak, nav, swq, swk, swv, swa_sink[l].astype(F32), n_batch, ctx_len)
            yna_stream, ysw_stream = (yc_na, y_na), (yc_sw, y_sw)

        rw_hi, rw_lo = _split_bf16(jnp.pad(router_w[l].astype(F32), ((0, 0), (0, pad_e))), 2)
        rw = jnp.concatenate([rw_hi, rw_hi, rw_lo], axis=0)
        rb = jnp.pad(router_b[l].astype(F32), (0, pad_e), constant_values=NEG_INF).reshape(1, LANES)
        x_new, h2, top_e, top_g = _mixout(
            x_stream, yna_stream, o_f, o_b, hg, ysw_stream, hg_norm_g[l].reshape(1, HG_DK).astype(F32),
            w_out[l].astype(BF16), mods, norm2_g[l].reshape(1, d), rw, rb, geom, stream_off)

        runs, run_first, block_e, n_used, pad_slots, n_slots = _route(top_e[:, :TOP_K], N_EXPERTS, ROW_BLOCK, MOE_ROWS)
        xs, pos = _dispatch(h2, top_e, runs, run_first, pad_slots, n_used, n_slots)
        y_slots = _moe_ffn(xs, block_e, n_used, w_gu, b_gu, w_down, b_down, l)
        x_stream = (None, _combine(x_new, y_slots, runs, pos, top_g, mods, geom, stream_off))
    return x_stream[1].reshape(n_batch, seq, d)
```

```python
import functools

import numpy as np
import jax
import jax.numpy as jnp
from jax import lax
from jax.experimental import pallas as pl
from jax.experimental.pallas import tpu as pltpu

D_MODEL = 1024
GRID_W = 64
HEAD_DIM = 64
ATTN_SCALE = HEAD_DIM ** -0.5
NA_HEADS = 4
NA_WIN_H = 8
NA_WIN_W = 16
HG_HEADS = 4
HG_DK = 128
SWA_Q_HEADS = 4
SWA_KV_HEADS = 2
SWA_WINDOW = 128
SWA_BLOCK = 128
ROPE_BASE = 10000.0
N_EXPERTS = 32
TOP_K = 4
D_FF = 1024
SWIGLU_LIMIT = 7.0
SWIGLU_ALPHA = 1.702
NORM_EPS = 1e-6
NEG_INF = -1e30

NA_W = NA_HEADS * HEAD_DIM
HG_W = HG_HEADS * HG_DK
SWA_W = SWA_Q_HEADS * HEAD_DIM
SWA_KV_W = SWA_KV_HEADS * HEAD_DIM
HG_COLS = 5 * HG_W
IN_COLS_EXT = 3 * NA_W + HG_COLS + 3 * SWA_W

LANES = 128
SUBLANES = 8
ROW_BLOCK = 256
NA_QROWS = 4
NA_KROWS = 12
NA_HEADS_PER_PASS = 1
SWA_HEADS_PER_PASS = 4
HG_STEP_ROWS = 256
HG_HEADS_PER_STEP = 4
MOE_ROWS = 256
RUN_BITS = ROW_BLOCK.bit_length()
VMEM_LIMIT = 56 * 1024 * 1024

F32 = jnp.float32
BF16 = jnp.bfloat16
HIGHEST = lax.Precision.HIGHEST


def _cparams(*sem):
    return pltpu.CompilerParams(dimension_semantics=sem, vmem_limit_bytes=VMEM_LIMIT)


def _dot(a, b, precision=None):
    return jnp.dot(a, b, preferred_element_type=F32, precision=precision)


def _dot_nt(a, b):
    return lax.dot_general(a, b, (((1,), (1,)), ((), ())), preferred_element_type=F32)


def _dot_tn(a, b):
    return lax.dot_general(a, b, (((0,), (0,)), ((), ())), preferred_element_type=F32)


def _store_token_tiles(ref, x):
    n = x.shape[0]
    for c in range(SUBLANES):
        ref[pl.ds(c, n, stride=SUBLANES), :] = x[:, c * LANES:(c + 1) * LANES]


def _load_token_tiles(ref, first_token, n):
    return jnp.concatenate([ref[pl.ds(first_token * SUBLANES + c, n, stride=SUBLANES), :] for c in range(SUBLANES)],
                           axis=1)


def _sigmoid(x):
    return 1.0 / (1.0 + jnp.exp(-x))


def _silu(x):
    return x * _sigmoid(x)


def _ada_kernel(c_ref, w_ref, b_ref, o_ref):
    o_ref[...] = _dot(_silu(c_ref[...]), w_ref[...], HIGHEST) + b_ref[...]


def _ada_mod(cc, w, b):
    m, d = cc.shape
    n = w.shape[1]
    tn = 1024
    return pl.pallas_call(
        _ada_kernel,
        grid=(n // tn,),
        in_specs=[pl.BlockSpec((m, d), lambda j: (0, 0)),
                  pl.BlockSpec((d, tn), lambda j: (0, j)),
                  pl.BlockSpec((1, tn), lambda j: (0, j))],
        out_specs=pl.BlockSpec((m, tn), lambda j: (0, j)),
        out_shape=jax.ShapeDtypeStruct((m, n), F32),
        compiler_params=_cparams("parallel"),
        name="ada_mod",
    )(cc, w, b.reshape(1, n))


def _mod_rmsnorm(x, g, mod, shift_row, scale_row):
    y = x * lax.rsqrt(jnp.mean(x * x, axis=-1, keepdims=True) + NORM_EPS) * g
    return y * (1.0 + mod[scale_row:scale_row + 1]) + mod[shift_row:shift_row + 1]


def _split_bf16(x, terms):
    parts = []
    for _ in range(terms - 1):
        parts.append(x.astype(BF16))
        x = x - parts[-1].astype(F32)
    return parts + [x.astype(BF16)]


def _head_rmsnorm(x, w, group_ones):
    hi, lo = _split_bf16(x * x, 2)
    n = x.shape[0]
    ss = _dot(jnp.concatenate([hi, lo], axis=0), group_ones)
    return x * lax.rsqrt((ss[:n] + ss[n:]) * (1.0 / HEAD_DIM) + NORM_EPS) * w


def _rope(x, cos, sin_signed):
    n = x.shape[-1]
    lane = lax.broadcasted_iota(jnp.int32, x.shape, 1)
    quarter = HEAD_DIM // 4
    partner = jnp.where(lane % (2 * quarter) < quarter,
                        pltpu.roll(x, n - quarter, 1), pltpu.roll(x, quarter, 1))
    return x * cos + partner * sin_signed


def _stream_specs(stream, width, first_block=0):
    head, tail = stream
    tm = ROW_BLOCK
    if head is None:
        return [pl.BlockSpec((tm, width), lambda i: (i + first_block, 0))], [tail]
    n_head = head.shape[0] // tm
    return ([pl.BlockSpec((tm, width), lambda i: (jnp.minimum(i + first_block, n_head - 1), 0)),
             pl.BlockSpec((tm, width), lambda i: (jnp.maximum(i + first_block - n_head, 0), 0))], [head, tail])


def _stream_block(refs, n_head, first_block=0):
    if n_head is None:
        return refs[0][...]
    return jnp.where(pl.program_id(0) + first_block < n_head, refs[0][...], refs[1][...])


def _n_head(stream):
    return None if stream[0] is None else stream[0].shape[0] // ROW_BLOCK


def _inproj_kernel(*refs, n_head):
    n_x = 1 if n_head is None else 2
    (mod_ref, g_ref, w_ref, naq_w, nak_w, swq_w, swk_w, cos_ref, sin_ref, ones_ref, lb_ref,
     hg_ref, naq_ref, nak_ref, nav_ref, swq_ref, swk_ref, swv_ref) = refs[n_x:]
    h = _mod_rmsnorm(_stream_block(refs[:n_x], n_head), g_ref[...], mod_ref[...], 0, 1).astype(BF16)
    ones = ones_ref[...]

    def proj(lo, width):
        return _dot(h, w_ref[:, lo:lo + width])

    naq_ref[...] = (_head_rmsnorm(proj(0, NA_W), naq_w[...], ones) * ATTN_SCALE).astype(BF16)
    nak_ref[...] = _head_rmsnorm(proj(NA_W, NA_W), nak_w[...], ones).astype(BF16)
    nav_ref[...] = proj(2 * NA_W, NA_W).astype(BF16)
    base = 3 * NA_W
    lb = lb_ref[...]
    hg_ref[:, :HG_W] = _silu(proj(base, HG_W)) * (HG_DK ** -0.5)
    for j in (1, 2):
        hg_ref[:, j * HG_W:(j + 1) * HG_W] = jnp.log2(lb + (1.0 - lb) * _sigmoid(proj(base + j * HG_W, HG_W)))
    for j in (3, 4):
        hg_ref[:, j * HG_W:(j + 1) * HG_W] = proj(base + j * HG_W, HG_W)
    base += HG_COLS
    cos, sin = cos_ref[...], sin_ref[...]
    swq = _rope(_head_rmsnorm(proj(base, SWA_W), swq_w[...], ones), cos, sin)
    swq_ref[...] = (swq * ATTN_SCALE).astype(BF16)
    swk_ref[...] = _rope(_head_rmsnorm(proj(base + SWA_W, SWA_W), swk_w[...], ones), cos, sin).astype(BF16)
    swv_ref[...] = proj(base + 2 * SWA_W, SWA_W).astype(BF16)


def _mod_row(i, n_ctx_blocks, blocks_per_batch, n_batch):
    return jnp.where(i < n_ctx_blocks, n_batch, (i - n_ctx_blocks) // blocks_per_batch)


def _inproj(x_stream, mods, norm_g, w_ext, naq_w, nak_w, swq_w, swk_w, cos_t, sin_t, ones, lb, geom):
    n_batch, n_ctx_blocks, blocks_per_batch = geom
    tm = ROW_BLOCK
    t = (n_ctx_blocks + n_batch * blocks_per_batch) * tm
    row = lambda w: pl.BlockSpec((tm, w), lambda i: (i, 0))
    const = lambda a: pl.BlockSpec(a.shape, lambda i: (0,) * a.ndim)
    rope_blk = lambda i: (jnp.where(i < n_ctx_blocks, 0, 1 + (i - n_ctx_blocks) % blocks_per_batch), 0)
    bf = lambda w: jax.ShapeDtypeStruct((t, w), BF16)
    x_specs, x_arrays = _stream_specs(x_stream, D_MODEL)
    return pl.pallas_call(
        functools.partial(_inproj_kernel, n_head=_n_head(x_stream)),
        grid=(t // tm,),
        in_specs=x_specs + [
            pl.BlockSpec((None, 6, D_MODEL), lambda i: (_mod_row(i, n_ctx_blocks, blocks_per_batch, n_batch), 0, 0)),
            const(norm_g), const(w_ext), const(naq_w), const(nak_w), const(swq_w), const(swk_w),
            pl.BlockSpec((tm, SWA_W), rope_blk), pl.BlockSpec((tm, SWA_W), rope_blk), const(ones), const(lb)],
        out_specs=[row(HG_COLS), row(NA_W), row(NA_W), row(NA_W), row(SWA_W), row(SWA_W), row(SWA_W)],
        out_shape=[jax.ShapeDtypeStruct((t, HG_COLS), F32), bf(NA_W), bf(NA_W), bf(NA_W),
                   bf(SWA_W), bf(SWA_W), bf(SWA_W)],
        compiler_params=_cparams("parallel"),
        name="inproj",
    )(*x_arrays, mods, norm_g, w_ext, naq_w, nak_w, swq_w, swk_w, cos_t, sin_t, ones, lb)


def _attend(q, parts, sink_vals, heads_per_pass):
    m_rows, width = q.shape
    n_heads = width // HEAD_DIM
    lane = lax.broadcasted_iota(jnp.int32, (m_rows, width), 1)
    out = jnp.zeros((m_rows, width), F32)
    for h0 in range(0, n_heads, heads_per_pass):
        heads = range(h0, h0 + heads_per_pass)
        rows = slice(h0 * m_rows, (h0 + heads_per_pass) * m_rows)
        in_head = [(lane // HEAD_DIM) == h for h in heads]
        q_stack = jnp.concatenate([jnp.where(m, q, jnp.zeros_like(q)) for m in in_head], axis=0)
        scores = []
        for k, _, add in parts:
            s = _dot_nt(q_stack, k)
            scores.append(s if add is None else s + add[rows])
        mx = functools.reduce(jnp.maximum, [jnp.max(s, axis=-1, keepdims=True) for s in scores])
        if sink_vals is not None:
            head_row = lax.broadcasted_iota(jnp.int32, (heads_per_pass * m_rows, 1), 0) // m_rows + h0
            sink = functools.reduce(jnp.add, [jnp.where(head_row == h, sink_vals[h], 0.0) for h in heads])
            mx = jnp.maximum(mx, sink)
        ps = [jnp.exp(s - mx) for s in scores]
        den = functools.reduce(jnp.add, [jnp.sum(p, axis=-1, keepdims=True) for p in ps])
        if sink_vals is not None:
            den = den + jnp.exp(sink - mx)
        acc = functools.reduce(jnp.add, [_dot(p.astype(BF16), v) for p, (_, v, _) in zip(ps, parts)]) / den
        for i, m in enumerate(in_head):
            out = jnp.where(m, acc[i * m_rows:(i + 1) * m_rows], out)
    return out


def _na_kernel(q_ref, k0, k1, k2, kc, v0, v1, v2, vc, bias_ref, o_ref):
    k_lat = jnp.concatenate([k0[...], k1[...], k2[...]], axis=0)
    v_lat = jnp.concatenate([v0[...], v1[...], v2[...]], axis=0)
    bias = bias_ref[...]
    bias = bias.reshape(bias.shape[0] * bias.shape[1], bias.shape[2])
    o = _attend(q_ref[...], [(k_lat, v_lat, bias), (kc[...], vc[...], None)], None, NA_HEADS_PER_PASS)
    o_ref[...] = o.astype(o_ref.dtype)


def _na_attention(naq, nak, nav, bias, geom):
    n_batch, n_ctx_blocks, blocks_per_batch = geom
    tm = ROW_BLOCK
    n_rb = blocks_per_batch
    kb_max = n_rb - NA_KROWS // NA_QROWS
    lat = lambda rb, b: n_ctx_blocks + b * blocks_per_batch

    def band(j):
        return pl.BlockSpec((tm, NA_W), lambda rb, b: (lat(rb, b) + jnp.clip(rb - 1, 0, kb_max) + j, 0))

    ctx = pl.BlockSpec((tm, NA_W), lambda rb, b: (b, 0))
    return pl.pallas_call(
        _na_kernel,
        grid=(n_rb, n_batch),
        in_specs=[pl.BlockSpec((tm, NA_W), lambda rb, b: (lat(rb, b) + rb, 0)),
                  band(0), band(1), band(2), ctx, band(0), band(1), band(2), ctx,
                  pl.BlockSpec((None, NA_HEADS, tm, NA_KROWS * GRID_W), lambda rb, b: (
                      jnp.where(rb == 0, 0, jnp.where(rb == n_rb - 1, 2, 1)), 0, 0, 0))],
        out_specs=pl.BlockSpec((tm, NA_W), lambda rb, b: (b * blocks_per_batch + rb, 0)),
        out_shape=jax.ShapeDtypeStruct((n_batch * blocks_per_batch * tm, NA_W), BF16),
        compiler_params=_cparams("parallel", "parallel"),
        name="na_attn",
    )(naq, nak, nak, nak, nak, nav, nav, nav, nav, bias)


def _swa_kernel(sink_ref, q_ref, kp, kc_, kn, kx, vp, vc_, vn, vx, o_ref, *, n_blocks):
    n = pl.program_id(1)
    tq = q_ref.shape[0]
    k_lat = jnp.concatenate([kp[...], kc_[...], kn[...]], axis=0)
    v_lat = jnp.concatenate([vp[...], vc_[...], vn[...]], axis=0)
    qi = lax.broadcasted_iota(jnp.int32, (SWA_Q_HEADS * tq, 3 * tq), 0) % tq
    mi = lax.broadcasted_iota(jnp.int32, (SWA_Q_HEADS * tq, 3 * tq), 1)
    kpos = (n - 1) * tq + mi
    ok = (jnp.abs(qi + tq - mi) <= SWA_WINDOW) & (kpos >= 0) & (kpos < n_blocks * tq)
    add = jnp.where(ok, 0.0, NEG_INF).astype(F32)
    sinks = [sink_ref[h] for h in range(SWA_Q_HEADS)]
    o = _attend(q_ref[...], [(k_lat, v_lat, add), (kx[...], vx[...], None)], sinks, SWA_HEADS_PER_PASS)
    o_ref[...] = o.astype(o_ref.dtype)


def _swa_attention(swq, swk, swv, sink, geom, seq, ctx_len):
    n_batch, _, _ = geom
    tq = SWA_BLOCK
    nb = seq // tq
    first = n_batch * ctx_len // tq
    blk = lambda f: pl.BlockSpec((tq, SWA_W), lambda b, n: (first + b * nb + f(n), 0))
    prev, cur, nxt = blk(lambda n: jnp.maximum(n - 1, 0)), blk(lambda n: n), blk(lambda n: jnp.minimum(n + 1, nb - 1))
    ctx = pl.BlockSpec((ctx_len, SWA_W), lambda b, n: (b, 0))
    return pl.pallas_call(
        functools.partial(_swa_kernel, n_blocks=nb),
        grid=(n_batch, nb),
        in_specs=[pl.BlockSpec(memory_space=pltpu.SMEM), cur, prev, cur, nxt, ctx, prev, cur, nxt, ctx],
        out_specs=pl.BlockSpec((tq, SWA_W), lambda b, n: (b * nb + n, 0)),
        out_shape=jax.ShapeDtypeStruct((n_batch * seq, SWA_W), BF16),
        compiler_params=_cparams("parallel", "parallel"),
        name="swa_attn",
    )(sink, swq, swk, swk, swk, swk, swv, swv, swv, swv)


def _ctx_attn_kernel(sink_ref, naq, nak, nav, swq, swk, swv, ona_ref, osw_ref):
    ona_ref[...] = _attend(naq[...], [(nak[...], nav[...], None)], None, NA_HEADS_PER_PASS).astype(ona_ref.dtype)
    sinks = [sink_ref[h] for h in range(SWA_Q_HEADS)]
    osw_ref[...] = _attend(swq[...], [(swk[...], swv[...], None)], sinks, NA_HEADS_PER_PASS).astype(osw_ref.dtype)


def _ctx_attention(naq, nak, nav, swq, swk, swv, sink, n_batch, ctx_len):
    blk = pl.BlockSpec((ctx_len, NA_W), lambda b: (b, 0))
    out = jax.ShapeDtypeStruct((n_batch * ctx_len, NA_W), BF16)
    return pl.pallas_call(
        _ctx_attn_kernel,
        grid=(n_batch,),
        in_specs=[pl.BlockSpec(memory_space=pltpu.SMEM)] + [blk] * 6,
        out_specs=[blk, blk],
        out_shape=[out, out],
        compiler_params=_cparams("parallel"),
        name="ctx_attn",
    )(sink, naq, nak, nav, swq, swk, swv)


def _dot_exact_lhs(m, x):
    out = _dot(m, jnp.concatenate(_split_bf16(x, 3), axis=1))
    n = x.shape[1]
    return out[:, :n] + out[:, n:2 * n] + out[:, 2 * n:]


def _hgrn_sum_table(n_rows, reverse):
    tri = np.tril(np.ones((n_rows, n_rows), np.float32))
    return tri.T if reverse else tri


def _tile_row(x, r):
    tiles = x.reshape(x.shape[0] // SUBLANES, SUBLANES, x.shape[1])
    return jnp.broadcast_to(tiles[:, r:r + 1, :], tiles.shape).reshape(x.shape)


def _hgrn_masks(n_rows):
    row = lax.broadcasted_iota(jnp.int32, (n_rows, LANES), 0)
    row_a = lax.broadcasted_iota(jnp.int32, (n_rows, n_rows), 0)
    col_a = lax.broadcasted_iota(jnp.int32, (n_rows, n_rows), 1)
    sizes = [1 << b for b in range(1, n_rows.bit_length())]
    return dict(
        diag=row_a == col_a,
        same={s: row_a // s == col_a // s for s in sizes if s < n_rows},
        upper_half={s: row % s >= s // 2 for s in sizes if s < 2 * SUBLANES},
        from_row={r: row % SUBLANES >= r for r in range(2, SUBLANES, 2)})


def _hgrn_block(q, k, g, v, st, sums, ones, masks, reverse):
    n_rows = q.shape[0]
    cum = _dot_exact_lhs(sums, g)
    tot = cum[0:1] if reverse else cum[n_rows - 1:n_rows]
    o_inter = _dot_nt((q * jnp.exp2(cum)).astype(BF16), st.astype(BF16))
    k_end = (k * jnp.exp2(tot - cum)).astype(BF16)
    st_new = st * jnp.exp2(tot) + _dot_tn(v.astype(BF16), k_end)

    a = None
    zero_tile = jnp.zeros((SUBLANES, LANES), F32)
    size = n_rows
    while size >= 2:
        half = size // 2
        if size < 2 * SUBLANES:
            ref_off = half if reverse else half - 1
            ref = _tile_row(cum, ref_off)
            for first in range(size, SUBLANES, size):
                ref = jnp.where(masks["from_row"][first], _tile_row(cum, first + ref_off), ref)
            upper = masks["upper_half"][size]
            d = cum - ref
            q_exp = jnp.where(upper, NEG_INF, d) if reverse else jnp.where(upper, d, NEG_INF)
            k_exp = jnp.where(upper, -d, NEG_INF) if reverse else jnp.where(upper, NEG_INF, -d)
            q_t = (q * jnp.exp2(q_exp)).astype(BF16)
            k_t = (k * jnp.exp2(k_exp)).astype(BF16)
        else:
            q_tiles, k_tiles = [], []
            for r0 in range(0, n_rows, SUBLANES):
                sl = slice(r0, r0 + SUBLANES)
                first = r0 // size * size
                ref_row = first + (half if reverse else half - 1)
                ref = cum[ref_row:ref_row + 1]
                if (r0 - first < half) if reverse else (r0 - first >= half):
                    q_tiles.append(q[sl] * jnp.exp2(cum[sl] - ref))
                    k_tiles.append(zero_tile)
                else:
                    q_tiles.append(zero_tile)
                    k_tiles.append(k[sl] * jnp.exp2(ref - cum[sl]))
            q_t = jnp.concatenate(q_tiles, axis=0).astype(BF16)
            k_t = jnp.concatenate(k_tiles, axis=0).astype(BF16)
        a_l = _dot_nt(q_t, k_t)
        a = a_l if a is None else jnp.where(masks["same"][size], a_l, a)
        size //= 2
    a = jnp.where(masks["diag"], _dot((q * k).astype(BF16), ones), a)
    return o_inter + _dot(a.astype(BF16), v.astype(BF16)), st_new


def _hgrn_kernel(q_f, g_f, v_f, q_b, g_b, v_b, trif_ref, trib_ref, ones_ref, of_ref, ob_ref, st_f, st_b):
    @pl.when(pl.program_id(2) == 0)
    def _():
        st_f[...] = jnp.zeros_like(st_f)
        st_b[...] = jnp.zeros_like(st_b)

    ones = ones_ref[...]
    masks = _hgrn_masks(q_f.shape[0])
    for h in range(HG_HEADS_PER_STEP):
        sl = slice(h * HG_DK, (h + 1) * HG_DK)
        g = g_f[:, sl]
        of_ref[:, sl], st_f[h] = _hgrn_block(q_f[:, sl], 1.0 - jnp.exp2(g), g, v_f[:, sl], st_f[h],
                                             trif_ref[...], ones, masks, False)
        g = g_b[:, sl]
        ob_ref[:, sl], st_b[h] = _hgrn_block(q_b[:, sl], 1.0 - jnp.exp2(g), g, v_b[:, sl], st_b[h],
                                             trib_ref[...], ones, masks, True)


def _hgrn(hg, tri_f, tri_b, ones, n_batch, seq, ctx_len):
    tr = HG_STEP_ROWS
    nc, nl = ctx_len // tr, seq // tr
    first = n_batch * nc

    def fwd(b, j):
        return jnp.where(j < nc, b * nc + j, first + b * nl + (j - nc))

    def bwd(b, j):
        return jnp.where(j < nc, b * nc + (nc - 1 - j), first + b * nl + (nl - 1 - (j - nc)))

    hps = HG_HEADS_PER_STEP
    groups = HG_HEADS // hps
    wide = hps * HG_DK
    col = lambda row_fn, group: pl.BlockSpec((tr, wide), lambda b, h, j: (row_fn(b, j), group * groups + h))
    const = lambda a: pl.BlockSpec(a.shape, lambda b, h, j: (0, 0))
    out = jax.ShapeDtypeStruct((hg.shape[0], HG_W), F32)
    return pl.pallas_call(
        _hgrn_kernel,
        grid=(n_batch, groups, nc + nl),
        in_specs=[col(fwd, 0), col(fwd, 1), col(fwd, 3), col(bwd, 0), col(bwd, 2), col(bwd, 3),
                  const(tri_f), const(tri_b), const(ones)],
        out_specs=[pl.BlockSpec((tr, wide), lambda b, h, j: (fwd(b, j), h)),
                   pl.BlockSpec((tr, wide), lambda b, h, j: (bwd(b, j), h))],
        out_shape=[out, out],
        scratch_shapes=[pltpu.VMEM((hps, HG_DK, HG_DK), F32), pltpu.VMEM((hps, HG_DK, HG_DK), F32)],
        compiler_params=_cparams("parallel", "parallel", "arbitrary"),
        name="hgrn",
    )(hg, hg, hg, hg, hg, hg, tri_f, tri_b, ones)


def _mixout_kernel(*refs, n_heads, stream_off):
    counts = [1 if n is None else 2 for n in n_heads]
    streams, pos = [], 0
    for n, c in zip(n_heads, counts):
        streams.append((refs[pos:pos + c], n))
        pos += c
    (of_ref, ob_ref, zg_ref, hgn_ref, w_ref, mod_ref, g2_ref, rw_ref, rb_ref,
     xo_ref, h2_ref, te_ref, tg_ref) = refs[pos:]
    x_in = _stream_block(*streams[0], first_block=stream_off)
    o = of_ref[...] + ob_ref[...]
    zg = zg_ref[...]
    parts = [_stream_block(*streams[1])]
    for h in range(HG_HEADS):
        sl = slice(h * HG_DK, (h + 1) * HG_DK)
        oh = o[:, sl]
        yh = oh * lax.rsqrt(jnp.mean(oh * oh, axis=-1, keepdims=True) + NORM_EPS) * hgn_ref[...]
        parts.append((yh * _silu(zg[:, sl])).astype(BF16))
    parts.append(_stream_block(*streams[2]))
    y = _dot(jnp.concatenate(parts, axis=1), w_ref[...])
    mod = mod_ref[...]
    x_new = x_in + mod[2:3] * y
    xo_ref[...] = x_new
    h2 = _mod_rmsnorm(x_new, g2_ref[...], mod, 3, 4)
    h2_ref[...] = h2.astype(BF16)

    h_hi, h_lo = _split_bf16(h2, 2)
    logits = _dot(jnp.concatenate([h_hi, h_lo, h_hi], axis=1), rw_ref[...]) + rb_ref[...]
    lane = lax.broadcasted_iota(jnp.int32, logits.shape, 1).astype(F32)
    top_e = jnp.zeros(logits.shape, F32)
    top_v = jnp.full(logits.shape, NEG_INF, F32)
    for j in range(TOP_K):
        best = jnp.max(logits, axis=-1, keepdims=True)
        arg = jnp.min(jnp.where(logits == best, lane, float(LANES)), axis=-1, keepdims=True)
        top_e = jnp.where(lane == j, arg, top_e)
        top_v = jnp.where(lane == j, best, top_v)
        logits = jnp.where(lane == arg, -jnp.inf, logits)
    ex = jnp.exp(top_v - jnp.max(top_v, axis=-1, keepdims=True))
    te_ref[...] = top_e.astype(jnp.int32)
    tg_ref[...] = ex / jnp.sum(ex, axis=-1, keepdims=True)


def _mixout(x_stream, yna_stream, o_f, o_b, hg, ysw_stream, hgn, w_out, mods, norm2_g, rw, rb, geom, stream_off):
    n_batch, n_ctx_blocks, blocks_per_batch = geom
    tm = ROW_BLOCK
    t = (n_ctx_blocks + n_batch * blocks_per_batch - stream_off) * tm
    row = lambda w: pl.BlockSpec((tm, w), lambda i: (i, 0))
    full = lambda w, cb: pl.BlockSpec((tm, w), lambda i: (i + stream_off, cb))
    const = lambda a: pl.BlockSpec(a.shape, lambda i: (0,) * a.ndim)
    x_specs, x_arrays = _stream_specs(x_stream, D_MODEL, stream_off)
    yna_specs, yna_arrays = _stream_specs(yna_stream, NA_W)
    ysw_specs, ysw_arrays = _stream_specs(ysw_stream, SWA_W)
    n_heads = (_n_head(x_stream), _n_head(yna_stream), _n_head(ysw_stream))
    return pl.pallas_call(
        functools.partial(_mixout_kernel, n_heads=n_heads, stream_off=stream_off),
        grid=(t // tm,),
        in_specs=x_specs + yna_specs + ysw_specs + [
            full(HG_W, 0), full(HG_W, 0), full(HG_W, 4), const(hgn), const(w_out),
            pl.BlockSpec((None, 6, D_MODEL), lambda i: (
                _mod_row(i + stream_off, n_ctx_blocks, blocks_per_batch, n_batch), 0, 0)),
            const(norm2_g), const(rw), const(rb)],
        out_specs=[row(D_MODEL), row(D_MODEL), row(LANES), row(LANES)],
        out_shape=[jax.ShapeDtypeStruct((t, D_MODEL), F32), jax.ShapeDtypeStruct((t, D_MODEL), BF16),
                   jax.ShapeDtypeStruct((t, LANES), jnp.int32), jax.ShapeDtypeStruct((t, LANES), F32)],
        compiler_params=_cparams("parallel"),
        name="mixout",
    )(*x_arrays, *yna_arrays, *ysw_arrays, o_f, o_b, hg, hgn, w_out, mods, norm2_g, rw, rb)


def _start_run_copies(run_ref, n_experts, make_copy):
    for e in range(n_experts):
        n = run_ref[0, 0, e]
        local0 = run_ref[0, 0, n_experts + e]
        slot0 = run_ref[0, 0, 2 * n_experts + e]
        for bit in reversed(range(RUN_BITS)):
            @pl.when((n & (1 << bit)) != 0)
            def _():
                done = (n >> (bit + 1)) << (bit + 1)
                make_copy(local0 + done, slot0 + done, 1 << bit).start()


def _tile_rows(first_token, n_tokens):
    return pl.ds(pl.multiple_of(first_token * SUBLANES, SUBLANES), n_tokens * SUBLANES)


def _local_positions(te_ref, off_ref, stri_ref):
    e_tok = te_ref[...]
    lane = lax.broadcasted_iota(jnp.int32, e_tok.shape, 1)
    member = [lane == e_tok[:, j:j + 1] for j in range(TOP_K)]
    count = functools.reduce(jnp.add, [jnp.where(m, 1.0, 0.0) for m in member])
    before = _dot(stri_ref[...], count.astype(BF16)) + off_ref[...]
    return [jnp.sum(jnp.where(m, before, 0.0), axis=1, keepdims=True) for m in member]


def _dispatch_kernel(pad_ref, nu_ref, run_ref, te_ref, off_ref, h_ref, stri_ref, xs_hbm, pos_ref,
                     zeros, xloc0, xloc1, sem, zero_sem, *, n_tok_blocks):
    n_tok = h_ref.shape[0]
    block_rows = zeros.shape[0]
    n_blocks = xs_hbm.shape[0] // block_rows

    def zero_slot(e, r):
        row = pl.multiple_of((pad_ref[0, e] + r) * SUBLANES, SUBLANES)
        return pltpu.make_async_copy(zeros.at[pl.ds(0, SUBLANES), :], xs_hbm.at[pl.ds(row, SUBLANES), :], zero_sem)

    def zero_block(b):
        row = pl.multiple_of(b * block_rows, block_rows)
        return pltpu.make_async_copy(zeros, xs_hbm.at[pl.ds(row, block_rows), :], zero_sem)

    def for_each_unused(slot_fn, block_fn):
        def per_expert(e, carry):
            lax.fori_loop(0, pad_ref[1, e], lambda r, c: (slot_fn(e, r), c)[1], 0)
            return carry

        lax.fori_loop(0, pad_ref.shape[1], per_expert, 0)
        lax.fori_loop(nu_ref[0], n_blocks, lambda b, c: (block_fn(b), c)[1], 0)

    @pl.when(pl.program_id(0) == 0)
    def _():
        zeros[...] = jnp.zeros_like(zeros)
        for_each_unused(lambda e, r: zero_slot(e, r).start(), lambda b: zero_block(b).start())
        for_each_unused(lambda e, r: zero_slot(e, r).wait(), lambda b: zero_block(b).wait())

    bufs = (xloc0, xloc1)

    def sort_into(b):
        pos = _local_positions(te_ref, off_ref, stri_ref)
        lane = lax.broadcasted_iota(jnp.int32, (n_tok, LANES), 1)
        pos_ref[...] = functools.reduce(jnp.add, [jnp.where(lane == j, p, 0.0) for j, p in enumerate(pos)])
        col = lax.broadcasted_iota(jnp.int32, (n_tok, TOP_K * n_tok), 1).astype(F32)
        chosen = functools.reduce(jnp.logical_or, [col == p for p in pos])
        sorted_rows = _dot_tn(jnp.where(chosen, 1.0, 0.0).astype(BF16), h_ref[...])
        _store_token_tiles(bufs[b], sorted_rows)

    def start_copies(b):
        _start_run_copies(run_ref, pad_ref.shape[1], lambda local, slot, size: pltpu.make_async_copy(
            bufs[b].at[_tile_rows(local, size), :], xs_hbm.at[_tile_rows(slot, size), :], sem.at[b]))

    def wait_copies(b):
        pltpu.make_async_copy(bufs[b], xs_hbm.at[pl.ds(0, bufs[b].shape[0]), :], sem.at[b]).wait()

    i = pl.program_id(0)

    @pl.when(i == 0)
    def _():
        sort_into(0)

    for parity in (0, 1):
        @pl.when((i % 2 == parity) & (i >= 1) & (i < n_tok_blocks))
        def _():
            @pl.when(i >= 2)
            def _():
                wait_copies(parity)

            start_copies(1 - parity)
            sort_into(parity)

    @pl.when(i == n_tok_blocks)
    def _():
        if n_tok_blocks >= 2:
            wait_copies(n_tok_blocks % 2)
        start_copies((n_tok_blocks - 1) % 2)
        wait_copies((n_tok_blocks - 1) % 2)


def _dispatch(h, top_e, runs, run_first, pad_slots, n_used, n_slots):
    tm = ROW_BLOCK
    t = h.shape[0]
    nb = t // tm
    assert n_slots % MOE_ROWS == 0
    stri = jnp.asarray(np.tril(np.ones((tm, tm), np.float32), -1), BF16)
    cur = lambda i: jnp.minimum(i, nb - 1)
    row = lambda w: pl.BlockSpec((tm, w), lambda i: (cur(i), 0))
    sort_buffer = pltpu.VMEM((TOP_K * tm * SUBLANES, LANES), F32)
    return pl.pallas_call(
        functools.partial(_dispatch_kernel, n_tok_blocks=nb),
        grid=(nb + 1,),
        in_specs=[pl.BlockSpec(memory_space=pltpu.SMEM), pl.BlockSpec(memory_space=pltpu.SMEM),
                  pl.BlockSpec((1, 1, runs.shape[2]), lambda i: (jnp.maximum(i - 1, 0), 0, 0),
                               memory_space=pltpu.SMEM),
                  row(LANES), pl.BlockSpec((None, 1, LANES), lambda i: (cur(i), 0, 0)), row(D_MODEL),
                  pl.BlockSpec((tm, tm), lambda i: (0, 0))],
        out_specs=[pl.BlockSpec(memory_space=pl.ANY), row(LANES)],
        out_shape=[jax.ShapeDtypeStruct((n_slots * SUBLANES, LANES), F32), jax.ShapeDtypeStruct((t, LANES), F32)],
        scratch_shapes=[pltpu.VMEM((MOE_ROWS * SUBLANES, LANES), F32), sort_buffer, sort_buffer,
                        pltpu.SemaphoreType.DMA((2,)), pltpu.SemaphoreType.DMA(())],
        compiler_params=_cparams("arbitrary"),
        name="moe_dispatch",
    )(pad_slots, n_used, runs, top_e, run_first, h, stri)


def _moe_kernel(be_ref, nu_ref, next_ref, run_ref, xs_ref, wgu_hbm, bgu_ref, wd_hbm, bd_ref, y_ref,
                wgu_f32, wd_f32, wgu_bf, wd_bf, sem, *, layer):
    i = pl.program_id(0)
    live = i < nu_ref[0]
    tm = y_ref.shape[0] // SUBLANES
    changed = (i == 0) | (be_ref[i] != be_ref[jnp.maximum(i - 1, 0)])

    def weight_copies(expert, buf):
        return (pltpu.make_async_copy(wgu_hbm.at[layer, expert], wgu_f32.at[buf], sem.at[buf]),
                pltpu.make_async_copy(wd_hbm.at[layer, expert], wd_f32.at[buf], sem.at[buf]))

    @pl.when(live & changed)
    def _():
        buf = run_ref[i] % 2

        @pl.when(i == 0)
        def _():
            for copy in weight_copies(be_ref[0], 0):
                copy.start()

        for copy in weight_copies(be_ref[i], buf):
            copy.wait()
        wgu_bf[...] = wgu_f32[buf].astype(BF16)
        wd_bf[...] = wd_f32[buf].astype(BF16)

        @pl.when(next_ref[i] >= 0)
        def _():
            for copy in weight_copies(next_ref[i], 1 - buf):
                copy.start()

    @pl.when(live)
    def _():
        x = _load_token_tiles(xs_ref, 0, tm)
        gu = _dot(x.astype(BF16), wgu_bf[...]) + bgu_ref[...]
        glu = jnp.minimum(gu[:, :D_FF], SWIGLU_LIMIT)
        lin = jnp.clip(gu[:, D_FF:], -SWIGLU_LIMIT, SWIGLU_LIMIT)
        act = glu * _sigmoid(SWIGLU_ALPHA * glu) * (lin + 1.0)
        _store_token_tiles(y_ref, _dot(act.astype(BF16), wd_bf[...]) + bd_ref[...])

    @pl.when(jnp.logical_not(live))
    def _():
        y_ref[...] = jnp.zeros_like(y_ref)


def _moe_ffn(xs_tiles, block_e, n_used, w_gu, b_gu, w_down, b_down, layer):
    tm = MOE_ROWS
    nb = xs_tiles.shape[0] // (tm * SUBLANES)
    n_l, n_e, d, f2 = w_gu.shape
    idx = jnp.arange(nb, dtype=jnp.int32)
    starts = jnp.concatenate([jnp.ones((1,), bool), block_e[1:] != block_e[:-1]]) & (idx < n_used[0])
    run_idx = jnp.cumsum(starts.astype(jnp.int32)) - 1
    run_expert = jnp.full((nb + 1,), -1, jnp.int32).at[jnp.where(starts, run_idx, nb)].set(block_e)[:nb]
    next_e = jnp.concatenate([run_expert[1:], jnp.full((1,), -1, jnp.int32)])[run_idx]
    bias = lambda c: pl.BlockSpec((None, None, 1, c), lambda i, be, nu, nx, ru: (layer, be[i], 0, 0))
    grid_spec = pltpu.PrefetchScalarGridSpec(
        num_scalar_prefetch=4,
        grid=(nb,),
        in_specs=[pl.BlockSpec((tm * SUBLANES, LANES), lambda i, be, nu, nx, ru: (jnp.minimum(i, nu[0] - 1), 0)),
                  pl.BlockSpec(memory_space=pl.ANY), bias(f2), pl.BlockSpec(memory_space=pl.ANY), bias(d)],
        out_specs=pl.BlockSpec((tm * SUBLANES, LANES), lambda i, be, nu, nx, ru: (i, 0)),
        scratch_shapes=[pltpu.VMEM((2, d, f2), F32), pltpu.VMEM((2, f2 // 2, d), F32),
                        pltpu.VMEM((d, f2), BF16), pltpu.VMEM((f2 // 2, d), BF16), pltpu.SemaphoreType.DMA((2,))],
    )
    return pl.pallas_call(
        functools.partial(_moe_kernel, layer=layer),
        grid_spec=grid_spec,
        out_shape=jax.ShapeDtypeStruct((nb * tm * SUBLANES, LANES), F32),
        compiler_params=_cparams("arbitrary"),
        name="moe_ffn",
    )(block_e, n_used, next_e, run_idx, xs_tiles, w_gu, b_gu.reshape(n_l, n_e, 1, f2), w_down,
      b_down.reshape(n_l, n_e, 1, d))


def _combine_kernel(run_ref, pos_ref, tg_ref, x_ref, y_hbm, mod_ref, o_ref, ybuf0, ybuf1, sem, *, n_tok_blocks):
    i = pl.program_id(0)
    tm = o_ref.shape[0]
    n_rows = TOP_K * tm
    bufs = (ybuf0, ybuf1)

    def fetch(b):
        _start_run_copies(run_ref, run_ref.shape[2] // 3, lambda local, slot, size: pltpu.make_async_copy(
            y_hbm.at[_tile_rows(slot, size), :], bufs[b].at[_tile_rows(local, size), :], sem.at[b]))

    def wait_fetch(b):
        pltpu.make_async_copy(y_hbm.at[pl.ds(0, n_rows * SUBLANES), :], bufs[b], sem.at[b]).wait()

    def combine(b):
        y = _load_token_tiles(bufs[b], 0, n_rows).astype(BF16)
        pos, gate = pos_ref[...], tg_ref[...]
        col = lax.broadcasted_iota(jnp.int32, (tm, n_rows), 1).astype(F32)
        g = functools.reduce(jnp.add, [jnp.where(col == pos[:, j:j + 1], gate[:, j:j + 1], 0.0)
                                       for j in range(TOP_K)])
        g_hi, g_lo = _split_bf16(g, 2)
        o_ref[...] = x_ref[...] + mod_ref[...][5:6] * (_dot(g_hi, y) + _dot(g_lo, y))

    @pl.when(i == 0)
    def _():
        fetch(0)

    for parity in (0, 1):
        @pl.when((i % 2 == parity) & (i >= 1) & (i < n_tok_blocks))
        def _():
            wait_fetch(1 - parity)
            fetch(parity)
            combine(1 - parity)

    @pl.when(i == n_tok_blocks)
    def _():
        wait_fetch((n_tok_blocks - 1) % 2)
        combine((n_tok_blocks - 1) % 2)


def _combine(x, y_tiles, runs, pos, gates, mods, geom, stream_off):
    n_batch, n_ctx_blocks, blocks_per_batch = geom
    t = x.shape[0]
    tm = ROW_BLOCK
    nb = t // tm
    prev = lambda i: jnp.maximum(i - 1, 0)
    row = lambda w: pl.BlockSpec((tm, w), lambda i: (prev(i), 0))
    sorted_buffer = pltpu.VMEM((TOP_K * tm * SUBLANES, LANES), F32)
    return pl.pallas_call(
        functools.partial(_combine_kernel, n_tok_blocks=nb),
        grid=(nb + 1,),
        in_specs=[pl.BlockSpec((1, 1, runs.shape[2]), lambda i: (jnp.minimum(i, nb - 1), 0, 0),
                               memory_space=pltpu.SMEM),
                  row(LANES), row(LANES), row(D_MODEL), pl.BlockSpec(memory_space=pl.ANY),
                  pl.BlockSpec((None, 6, D_MODEL), lambda i: (
                      _mod_row(prev(i) + stream_off, n_ctx_blocks, blocks_per_batch, n_batch), 0, 0))],
        out_specs=row(D_MODEL),
        out_shape=jax.ShapeDtypeStruct((t, D_MODEL), F32),
        scratch_shapes=[sorted_buffer, sorted_buffer, pltpu.SemaphoreType.DMA((2,))],
        compiler_params=_cparams("arbitrary"),
        name="moe_combine",
    )(runs, pos, gates, x, y_tiles, mods)


def _route(top_e, n_experts, tok_block, slot_block):
    t, k = top_e.shape
    nb = t // tok_block
    onehot = (top_e[:, :, None] == jnp.arange(n_experts, dtype=jnp.int32)).astype(jnp.int32)
    n = onehot.reshape(nb, tok_block * k, n_experts).sum(axis=1)
    counts = n.sum(axis=0)
    padded = (counts + slot_block - 1) // slot_block * slot_block
    pend = jnp.cumsum(padded)
    start = pend - padded
    local0 = jnp.cumsum(n, axis=1) - n
    slot0 = start[None, :] + jnp.cumsum(n, axis=0) - n
    runs = jnp.concatenate([n, local0, slot0], axis=1).astype(jnp.int32).reshape(nb, 1, 3 * n_experts)
    run_first = jnp.pad(local0.astype(F32), ((0, 0), (0, LANES - n_experts))).reshape(nb, 1, LANES)
    n_blocks = (t * k + n_experts * (slot_block - 1) + slot_block - 1) // slot_block
    block_start = jnp.arange(n_blocks, dtype=jnp.int32) * slot_block
    block_e = jnp.minimum((block_start[:, None] >= pend[None, :]).sum(axis=1), n_experts - 1).astype(jnp.int32)
    n_used = (pend[-1] // slot_block).astype(jnp.int32).reshape(1)
    pad_slots = jnp.stack([start + counts, padded - counts], axis=0).astype(jnp.int32)
    return runs, run_first, block_e, n_used, pad_slots, n_blocks * slot_block


def _na_row_pattern(rb, n_rows):
    wr = min(NA_WIN_H, n_rows)
    n_rb = n_rows // NA_QROWS
    q_row = rb * NA_QROWS + np.arange(NA_QROWS)[:, None]
    k_row = np.clip(rb - 1, 0, n_rb - NA_KROWS // NA_QROWS) * NA_QROWS + np.arange(NA_KROWS)[None, :]
    row_start = np.clip(q_row - wr // 2, 0, n_rows - wr)
    ok = (k_row >= row_start) & (k_row < row_start + wr)
    return np.where(ok, k_row - q_row + (NA_WIN_H - 1), -1)


def _na_bias_table(rpb, seq):
    n_rows = seq // GRID_W
    n_rb = n_rows // NA_QROWS
    patterns = [_na_row_pattern(rb, n_rows) for rb in range(n_rb)]
    assert all((p == patterns[1]).all() for p in patterns[1:-1])
    q_col = np.arange(GRID_W)[:, None]
    k_col = np.arange(GRID_W)[None, :]
    col_start = np.clip(q_col - NA_WIN_W // 2, 0, GRID_W - NA_WIN_W)
    col_ok = (k_col >= col_start) & (k_col < col_start + NA_WIN_W)
    dc = np.clip(k_col - q_col + (NA_WIN_W - 1), 0, 2 * NA_WIN_W - 2)
    onehot = ((dc[None] == np.arange(2 * NA_WIN_W - 1)[:, None, None]) & col_ok[None]).astype(np.float32)
    by_col = jnp.einsum('hab,bqk->haqk', rpb.astype(F32), jnp.asarray(onehot), precision=HIGHEST)
    by_col = by_col + jnp.asarray(np.where(col_ok, 0.0, NEG_INF).astype(np.float32))
    masked = jnp.full((rpb.shape[0], GRID_W, GRID_W), NEG_INF, F32)
    variants = []
    for pattern in (patterns[0], patterns[1], patterns[-1]):
        rows = [jnp.concatenate([by_col[:, a] if a >= 0 else masked for a in pattern[qr]], axis=2)
                for qr in range(NA_QROWS)]
        variants.append(jnp.concatenate(rows, axis=1))
    return jnp.stack(variants, axis=0)


def _rope_tables(seq, ctx_len):
    quarter = HEAD_DIM // 4
    lane = np.arange(SWA_W)
    inv = ROPE_BASE ** (-(lane % quarter).astype(np.float64) / quarter)
    t = np.arange(seq)
    pos = np.where((lane % HEAD_DIM < HEAD_DIM // 2)[None, :], (t // GRID_W)[:, None], (t % GRID_W)[:, None])
    ang = jnp.asarray(pos, F32) * jnp.asarray(inv, F32)[None, :]
    sign = np.where(lane % (2 * quarter) < quarter, -1.0, 1.0).astype(np.float32)
    cos = jnp.concatenate([jnp.ones((ctx_len, SWA_W), F32), jnp.cos(ang)], axis=0)
    sin = jnp.concatenate([jnp.zeros((ctx_len, SWA_W), F32), jnp.sin(ang) * sign[None, :]], axis=0)
    return cos, sin


def kernel(x, c, ctx, c_ctx, hg_lower_bounds, ada_w, ada_b, norm1_g, norm2_g, w_in, na_q_norm, na_k_norm, na_rpb,
           hg_norm_g, swa_q_norm, swa_k_norm, swa_sink, w_out, router_w, router_b, w_gu, b_gu, w_down, b_down):
    n_batch, seq, d = x.shape
    ctx_len = ctx.shape[1]
    depth = ada_w.shape[0]
    assert d == D_MODEL and seq % ROW_BLOCK == 0 and ctx_len == ROW_BLOCK
    n_ctx_rows = n_batch * ctx_len
    geom = (n_batch, n_ctx_rows // ROW_BLOCK, seq // ROW_BLOCK)

    p_lb = jax.nn.softmax(hg_lower_bounds.astype(F32), axis=0)
    lbs = jnp.cumsum(p_lb, axis=0) - p_lb[0]

    cos_t, sin_t = _rope_tables(seq, ctx_len)
    lane = np.arange(NA_W)
    group_ones = jnp.asarray((lane[:, None] // HEAD_DIM == lane[None, :] // HEAD_DIM).astype(np.float32), BF16)
    tri_f = jnp.asarray(_hgrn_sum_table(HG_STEP_ROWS, False), BF16)
    tri_b = jnp.asarray(_hgrn_sum_table(HG_STEP_ROWS, True), BF16)
    ones_bf = jnp.ones((LANES, HG_STEP_ROWS), BF16)
    n_mod_rows = -(-(n_batch + 1) // 8) * 8
    cc = jnp.zeros((n_mod_rows, d), F32).at[:n_batch].set(c).at[n_batch].set(c_ctx)
    dup = lambda w: jnp.concatenate([w[:, :HEAD_DIM], w[:, :HEAD_DIM], w[:, HEAD_DIM:], w[:, HEAD_DIM:]], axis=1)
    tile4 = lambda g: jnp.tile(g.astype(F32), 4).reshape(1, 4 * HEAD_DIM)
    pad_e = LANES - N_EXPERTS

    x_stream = (ctx.reshape(n_ctx_rows, d), x.reshape(n_batch * seq, d))
    for l in range(depth):
        last = l == depth - 1
        mods = _ada_mod(cc, ada_w[l], ada_b[l]).reshape(n_mod_rows, 6, d)
        w = w_in[l]
        kv0 = 3 * NA_W + HG_COLS + SWA_W
        w_ext = jnp.concatenate([w[:, :kv0], dup(w[:, kv0:kv0 + SWA_KV_W]), dup(w[:, kv0 + SWA_KV_W:])],
                                axis=1).astype(BF16)
        hg, naq, nak, nav, swq, swk, swv = _inproj(
            x_stream, mods, norm1_g[l].reshape(1, d), w_ext, tile4(na_q_norm[l]), tile4(na_k_norm[l]),
            tile4(swa_q_norm[l]), tile4(swa_k_norm[l]), cos_t, sin_t, group_ones, lbs[l].reshape(1, HG_W), geom)

        y_na = _na_attention(naq, nak, nav, _na_bias_table(na_rpb[l], seq), geom)
        y_sw = _swa_attention(swq, swk, swv, swa_sink[l].astype(F32), geom, seq, ctx_len)
        o_f, o_b = _hgrn(hg, tri_f, tri_b, ones_bf, n_batch, seq, ctx_len)

        if last:
            stream_off = geom[1]
            yna_stream, ysw_stream = (None, y_na), (None, y_sw)
        else:
            stream_off = 0
            yc_na, yc_sw = _ctx_attention(naq, nak, nav, swq, swk, swv, swa_sink[l].astype(F32), n_batch, ctx_len)
            yna_stream, ysw_stream = (yc_na, y_na), (yc_sw, y_sw)

        rw_hi, rw_lo = _split_bf16(jnp.pad(router_w[l].astype(F32), ((0, 0), (0, pad_e))), 2)
        rw = jnp.concatenate([rw_hi, rw_hi, rw_lo], axis=0)
        rb = jnp.pad(router_b[l].astype(F32), (0, pad_e), constant_values=NEG_INF).reshape(1, LANES)
        x_new, h2, top_e, top_g = _mixout(
            x_stream, yna_stream, o_f, o_b, hg, ysw_stream, hg_norm_g[l].reshape(1, HG_DK).astype(F32),
            w_out[l].astype(BF16), mods, norm2_g[l].reshape(1, d), rw, rb, geom, stream_off)

        runs, run_first, block_e, n_used, pad_slots, n_slots = _route(top_e[:, :TOP_K], N_EXPERTS, ROW_BLOCK, MOE_ROWS)
        xs, pos = _dispatch(h2, top_e, runs, run_first, pad_slots, n_used, n_slots)
        y_slots = _moe_ffn(xs, block_e, n_used, w_gu, b_gu, w_down, b_down, l)
        x_stream = (None, _combine(x_new, y_slots, runs, pos, top_g, mods, geom, stream_off))
    return x_stream[1].reshape(n_batch, seq, d)
```

```python
import functools

import numpy as np
import jax
import jax.numpy as jnp
from jax import lax
from jax.experimental import pallas as pl
from jax.experimental.pallas import tpu as pltpu

D_MODEL = 1024
GRID_W = 64
HEAD_DIM = 64
ATTN_SCALE = HEAD_DIM ** -0.5
NA_HEADS = 4
NA_WIN_H = 8
NA_WIN_W = 16
HG_HEADS = 4
HG_DK = 128
SWA_Q_HEADS = 4
SWA_KV_HEADS = 2
SWA_WINDOW = 128
SWA_BLOCK = 128
ROPE_BASE = 10000.0
N_EXPERTS = 32
TOP_K = 4
D_FF = 1024
SWIGLU_LIMIT = 7.0
SWIGLU_ALPHA = 1.702
NORM_EPS = 1e-6
NEG_INF = -1e30

NA_W = NA_HEADS * HEAD_DIM
HG_W = HG_HEADS * HG_DK
SWA_W = SWA_Q_HEADS * HEAD_DIM
SWA_KV_W = SWA_KV_HEADS * HEAD_DIM
HG_COLS = 5 * HG_W

LANES = 128
SUBLANES = 8
ROW_BLOCK = 256
NA_QROWS = 4
NA_KROWS = 12
NA_HEADS_PER_PASS = 1
SWA_HEADS_PER_PASS = 4
HG_STEP_ROWS = 256
HG_HEADS_PER_STEP = 4
MOE_ROWS = 256
RUN_BITS = ROW_BLOCK.bit_length()
VMEM_LIMIT = 56 * 1024 * 1024

F32 = jnp.float32
BF16 = jnp.bfloat16
HIGHEST = lax.Precision.HIGHEST


def _cparams(*sem):
    return pltpu.CompilerParams(dimension_semantics=sem, vmem_limit_bytes=VMEM_LIMIT)


def _dot(a, b, precision=None):
    return jnp.dot(a, b, preferred_element_type=F32, precision=precision)


def _dot_nt(a, b):
    return lax.dot_general(a, b, (((1,), (1,)), ((), ())), preferred_element_type=F32)


def _dot_tn(a, b):
    return lax.dot_general(a, b, (((0,), (0,)), ((), ())), preferred_element_type=F32)


def _store_token_tiles(ref, x):
    n = x.shape[0]
    for c in range(SUBLANES):
        ref[pl.ds(c, n, stride=SUBLANES), :] = x[:, c * LANES:(c + 1) * LANES]


def _load_token_tiles(ref, first_token, n):
    return jnp.concatenate([ref[pl.ds(first_token * SUBLANES + c, n, stride=SUBLANES), :] for c in range(SUBLANES)],
                           axis=1)


def _sigmoid(x):
    return 1.0 / (1.0 + jnp.exp(-x))


def _silu(x):
    return x * _sigmoid(x)


def _ada_kernel(c_ref, w_ref, b_ref, o_ref):
    o_ref[...] = _dot(_silu(c_ref[...]), w_ref[...], HIGHEST) + b_ref[...]


def _ada_mod(cc, w, b):
    m, d = cc.shape
    n = w.shape[1]
    tn = D_MODEL
    return pl.pallas_call(
        _ada_kernel,
        grid=(n // tn,),
        in_specs=[pl.BlockSpec((m, d), lambda j: (0, 0)),
                  pl.BlockSpec((d, tn), lambda j: (0, j)),
                  pl.BlockSpec((1, tn), lambda j: (0, j))],
        out_specs=pl.BlockSpec((m, tn), lambda j: (0, j)),
        out_shape=jax.ShapeDtypeStruct((m, n), F32),
        compiler_params=_cparams("parallel"),
        name="ada_mod",
    )(cc, w, b.reshape(1, n))


def _mod_rmsnorm(x, g, mod, shift_row, scale_row):
    y = x * lax.rsqrt(jnp.mean(x * x, axis=-1, keepdims=True) + NORM_EPS) * g
    return y * (1.0 + mod[scale_row:scale_row + 1]) + mod[shift_row:shift_row + 1]


def _split_bf16(x, terms):
    parts = []
    for _ in range(terms - 1):
        parts.append(x.astype(BF16))
        x = x - parts[-1].astype(F32)
    return parts + [x.astype(BF16)]


def _head_rmsnorm(x, w, group_ones):
    hi, lo = _split_bf16(x * x, 2)
    n = x.shape[0]
    ss = _dot(jnp.concatenate([hi, lo], axis=0), group_ones)
    return x * lax.rsqrt((ss[:n] + ss[n:]) * (1.0 / HEAD_DIM) + NORM_EPS) * w


def _rope(x, cos, sin_signed):
    n = x.shape[-1]
    lane = lax.broadcasted_iota(jnp.int32, x.shape, 1)
    quarter = HEAD_DIM // 4
    partner = jnp.where(lane % (2 * quarter) < quarter,
                        pltpu.roll(x, n - quarter, 1), pltpu.roll(x, quarter, 1))
    return x * cos + partner * sin_signed


def _stream_specs(stream, width, first_block=0):
    head, tail = stream
    tm = ROW_BLOCK
    if head is None:
        return [pl.BlockSpec((tm, width), lambda i: (i + first_block, 0))], [tail]
    n_head = head.shape[0] // tm
    return ([pl.BlockSpec((tm, width), lambda i: (jnp.minimum(i + first_block, n_head - 1), 0)),
             pl.BlockSpec((tm, width), lambda i: (jnp.maximum(i + first_block - n_head, 0), 0))], [head, tail])


def _stream_block(refs, n_head, first_block=0):
    if n_head is None:
        return refs[0][...]
    return jnp.where(pl.program_id(0) + first_block < n_head, refs[0][...], refs[1][...])


def _n_head(stream):
    return None if stream[0] is None else stream[0].shape[0] // ROW_BLOCK


def _inproj_kernel(*refs, n_head):
    n_x = 1 if n_head is None else 2
    (mod_ref, g_ref, w_ref, naq_w, nak_w, swq_w, swk_w, cos_ref, sin_ref, ones_ref, lb_ref,
     hg_ref, naq_ref, nak_ref, nav_ref, swq_ref, swk_ref, swv_ref) = refs[n_x:]
    h = _mod_rmsnorm(_stream_block(refs[:n_x], n_head), g_ref[...], mod_ref[...], 0, 1).astype(BF16)
    ones = ones_ref[...]

    def proj(lo, width):
        return _dot(h, w_ref[:, lo:lo + width])

    naq_ref[...] = (_head_rmsnorm(proj(0, NA_W), naq_w[...], ones) * ATTN_SCALE).astype(BF16)
    nak_ref[...] = _head_rmsnorm(proj(NA_W, NA_W), nak_w[...], ones).astype(BF16)
    nav_ref[...] = proj(2 * NA_W, NA_W).astype(BF16)
    base = 3 * NA_W
    lb = lb_ref[...]
    hg_ref[:, :HG_W] = _silu(proj(base, HG_W)) * (HG_DK ** -0.5)
    for j in (1, 2):
        hg_ref[:, j * HG_W:(j + 1) * HG_W] = jnp.log2(lb + (1.0 - lb) * _sigmoid(proj(base + j * HG_W, HG_W)))
    for j in (3, 4):
        hg_ref[:, j * HG_W:(j + 1) * HG_W] = proj(base + j * HG_W, HG_W)
    base += HG_COLS
    cos, sin = cos_ref[...], sin_ref[...]
    swq = _rope(_head_rmsnorm(proj(base, SWA_W), swq_w[...], ones), cos, sin)
    swq_ref[...] = (swq * ATTN_SCALE).astype(BF16)
    swk_ref[...] = _rope(_head_rmsnorm(proj(base + SWA_W, SWA_W), swk_w[...], ones), cos, sin).astype(BF16)
    swv_ref[...] = proj(base + 2 * SWA_W, SWA_W).astype(BF16)


def _mod_row(i, n_ctx_blocks, blocks_per_batch, n_batch):
    return jnp.where(i < n_ctx_blocks, n_batch, (i - n_ctx_blocks) // blocks_per_batch)


def _inproj(x_stream, mods, norm_g, w_ext, naq_w, nak_w, swq_w, swk_w, cos_t, sin_t, ones, lb, geom):
    n_batch, n_ctx_blocks, blocks_per_batch = geom
    tm = ROW_BLOCK
    t = (n_ctx_blocks + n_batch * blocks_per_batch) * tm
    row = lambda w: pl.BlockSpec((tm, w), lambda i: (i, 0))
    const = lambda a: pl.BlockSpec(a.shape, lambda i: (0,) * a.ndim)
    rope_blk = lambda i: (jnp.where(i < n_ctx_blocks, 0, 1 + (i - n_ctx_blocks) % blocks_per_batch), 0)
    bf = lambda w: jax.ShapeDtypeStruct((t, w), BF16)
    x_specs, x_arrays = _stream_specs(x_stream, D_MODEL)
    return pl.pallas_call(
        functools.partial(_inproj_kernel, n_head=_n_head(x_stream)),
        grid=(t // tm,),
        in_specs=x_specs + [
            pl.BlockSpec((None, 6, D_MODEL), lambda i: (_mod_row(i, n_ctx_blocks, blocks_per_batch, n_batch), 0, 0)),
            const(norm_g), const(w_ext), const(naq_w), const(nak_w), const(swq_w), const(swk_w),
            pl.BlockSpec((tm, SWA_W), rope_blk), pl.BlockSpec((tm, SWA_W), rope_blk), const(ones), const(lb)],
        out_specs=[row(HG_COLS), row(NA_W), row(NA_W), row(NA_W), row(SWA_W), row(SWA_W), row(SWA_W)],
        out_shape=[jax.ShapeDtypeStruct((t, HG_COLS), F32), bf(NA_W), bf(NA_W), bf(NA_W),
                   bf(SWA_W), bf(SWA_W), bf(SWA_W)],
        compiler_params=_cparams("parallel"),
        name="inproj",
    )(*x_arrays, mods, norm_g, w_ext, naq_w, nak_w, swq_w, swk_w, cos_t, sin_t, ones, lb)


def _attend(q, parts, sink_vals, heads_per_pass):
    m_rows, width = q.shape
    n_heads = width // HEAD_DIM
    lane = lax.broadcasted_iota(jnp.int32, (m_rows, width), 1)
    out = jnp.zeros((m_rows, width), F32)
    for h0 in range(0, n_heads, heads_per_pass):
        heads = range(h0, h0 + heads_per_pass)
        rows = slice(h0 * m_rows, (h0 + heads_per_pass) * m_rows)
        in_head = [(lane // HEAD_DIM) == h for h in heads]
        q_stack = jnp.concatenate([jnp.where(m, q, jnp.zeros_like(q)) for m in in_head], axis=0)
        scores = []
        for k, _, add in parts:
            s = _dot_nt(q_stack, k)
            scores.append(s if add is None else s + add[rows])
        mx = functools.reduce(jnp.maximum, [jnp.max(s, axis=-1, keepdims=True) for s in scores])
        if sink_vals is not None:
            head_row = lax.broadcasted_iota(jnp.int32, (heads_per_pass * m_rows, 1), 0) // m_rows + h0
            sink = functools.reduce(jnp.add, [jnp.where(head_row == h, sink_vals[h], 0.0) for h in heads])
            mx = jnp.maximum(mx, sink)
        ps = [jnp.exp(s - mx) for s in scores]
        den = functools.reduce(jnp.add, [jnp.sum(p, axis=-1, keepdims=True) for p in ps])
        if sink_vals is not None:
            den = den + jnp.exp(sink - mx)
        acc = functools.reduce(jnp.add, [_dot(p.astype(BF16), v) for p, (_, v, _) in zip(ps, parts)]) / den
        for i, m in enumerate(in_head):
            out = jnp.where(m, acc[i * m_rows:(i + 1) * m_rows], out)
    return out


def _na_kernel(q_ref, k0, k1, k2, kc, v0, v1, v2, vc, bias_ref, o_ref):
    k_lat = jnp.concatenate([k0[...], k1[...], k2[...]], axis=0)
    v_lat = jnp.concatenate([v0[...], v1[...], v2[...]], axis=0)
    bias = bias_ref[...]
    bias = bias.reshape(bias.shape[0] * bias.shape[1], bias.shape[2])
    o = _attend(q_ref[...], [(k_lat, v_lat, bias), (kc[...], vc[...], None)], None, NA_HEADS_PER_PASS)
    o_ref[...] = o.astype(o_ref.dtype)


def _na_attention(naq, nak, nav, bias, geom):
    n_batch, n_ctx_blocks, blocks_per_batch = geom
    tm = ROW_BLOCK
    n_rb = blocks_per_batch
    kb_max = n_rb - NA_KROWS // NA_QROWS
    lat = lambda rb, b: n_ctx_blocks + b * blocks_per_batch

    def band(j):
        return pl.BlockSpec((tm, NA_W), lambda rb, b: (lat(rb, b) + jnp.clip(rb - 1, 0, kb_max) + j, 0))

    ctx = pl.BlockSpec((tm, NA_W), lambda rb, b: (b, 0))
    return pl.pallas_call(
        _na_kernel,
        grid=(n_rb, n_batch),
        in_specs=[pl.BlockSpec((tm, NA_W), lambda rb, b: (lat(rb, b) + rb, 0)),
                  band(0), band(1), band(2), ctx, band(0), band(1), band(2), ctx,
                  pl.BlockSpec((None, NA_HEADS, tm, NA_KROWS * GRID_W), lambda rb, b: (
                      jnp.where(rb == 0, 0, jnp.where(rb == n_rb - 1, 2, 1)), 0, 0, 0))],
        out_specs=pl.BlockSpec((tm, NA_W), lambda rb, b: (b * blocks_per_batch + rb, 0)),
        out_shape=jax.ShapeDtypeStruct((n_batch * blocks_per_batch * tm, NA_W), BF16),
        compiler_params=_cparams("parallel", "parallel"),
        name="na_attn",
    )(naq, nak, nak, nak, nak, nav, nav, nav, nav, bias)


def _swa_kernel(sink_ref, q_ref, kp, kc_, kn, kx, vp, vc_, vn, vx, mask_ref, o_ref):
    k_lat = jnp.concatenate([kp[...], kc_[...], kn[...]], axis=0)
    v_lat = jnp.concatenate([vp[...], vc_[...], vn[...]], axis=0)
    sinks = [sink_ref[h] for h in range(SWA_Q_HEADS)]
    o = _attend(q_ref[...], [(k_lat, v_lat, mask_ref[...]), (kx[...], vx[...], None)], sinks, SWA_HEADS_PER_PASS)
    o_ref[...] = o.astype(o_ref.dtype)


def _swa_mask_table(tq, n_blocks):
    qi = np.arange(SWA_Q_HEADS * tq)[:, None] % tq
    mi = np.arange(3 * tq)[None, :]
    variants = []
    for n in (0, 1, n_blocks - 1):
        kpos = (n - 1) * tq + mi
        ok = (np.abs(qi + tq - mi) <= SWA_WINDOW) & (kpos >= 0) & (kpos < n_blocks * tq)
        variants.append(np.where(ok, 0.0, NEG_INF).astype(np.float32))
    return jnp.asarray(np.stack(variants, axis=0))


def _swa_attention(swq, swk, swv, sink, geom, seq, ctx_len):
    n_batch, _, _ = geom
    tq = SWA_BLOCK
    nb = seq // tq
    assert nb >= 3
    first = n_batch * ctx_len // tq
    blk = lambda f: pl.BlockSpec((tq, SWA_W), lambda b, n: (first + b * nb + f(n), 0))
    prev, cur, nxt = blk(lambda n: jnp.maximum(n - 1, 0)), blk(lambda n: n), blk(lambda n: jnp.minimum(n + 1, nb - 1))
    ctx = pl.BlockSpec((ctx_len, SWA_W), lambda b, n: (b, 0))
    mask = pl.BlockSpec((None, SWA_Q_HEADS * tq, 3 * tq),
                        lambda b, n: (jnp.where(n == 0, 0, jnp.where(n == nb - 1, 2, 1)), 0, 0))
    return pl.pallas_call(
        _swa_kernel,
        grid=(n_batch, nb),
        in_specs=[pl.BlockSpec(memory_space=pltpu.SMEM), cur, prev, cur, nxt, ctx, prev, cur, nxt, ctx, mask],
        out_specs=pl.BlockSpec((tq, SWA_W), lambda b, n: (b * nb + n, 0)),
        out_shape=jax.ShapeDtypeStruct((n_batch * seq, SWA_W), BF16),
        compiler_params=_cparams("parallel", "parallel"),
        name="swa_attn",
    )(sink, swq, swk, swk, swk, swk, swv, swv, swv, swv, _swa_mask_table(tq, nb))


def _ctx_attn_kernel(sink_ref, naq, nak, nav, swq, swk, swv, ona_ref, osw_ref):
    ona_ref[...] = _attend(naq[...], [(nak[...], nav[...], None)], None, NA_HEADS_PER_PASS).astype(ona_ref.dtype)
    sinks = [sink_ref[h] for h in range(SWA_Q_HEADS)]
    osw_ref[...] = _attend(swq[...], [(swk[...], swv[...], None)], sinks, NA_HEADS_PER_PASS).astype(osw_ref.dtype)


def _ctx_attention(naq, nak, nav, swq, swk, swv, sink, n_batch, ctx_len):
    blk = pl.BlockSpec((ctx_len, NA_W), lambda b: (b, 0))
    out = jax.ShapeDtypeStruct((n_batch * ctx_len, NA_W), BF16)
    return pl.pallas_call(
        _ctx_attn_kernel,
        grid=(n_batch,),
        in_specs=[pl.BlockSpec(memory_space=pltpu.SMEM)] + [blk] * 6,
        out_specs=[blk, blk],
        out_shape=[out, out],
        compiler_params=_cparams("parallel"),
        name="ctx_attn",
    )(sink, naq, nak, nav, swq, swk, swv)


def _dot_exact_lhs(m, x):
    out = _dot(m, jnp.concatenate(_split_bf16(x, 3), axis=1))
    n = x.shape[1]
    return out[:, :n] + out[:, n:2 * n] + out[:, 2 * n:]


def _hgrn_sum_table(n_rows, reverse):
    tri = np.tril(np.ones((n_rows, n_rows), np.float32))
    return tri.T if reverse else tri


def _tile_row(x, r):
    tiles = x.reshape(x.shape[0] // SUBLANES, SUBLANES, x.shape[1])
    return jnp.broadcast_to(tiles[:, r:r + 1, :], tiles.shape).reshape(x.shape)


def _hgrn_masks(n_rows):
    row = lax.broadcasted_iota(jnp.int32, (n_rows, LANES), 0)
    row_a = lax.broadcasted_iota(jnp.int32, (n_rows, n_rows), 0)
    col_a = lax.broadcasted_iota(jnp.int32, (n_rows, n_rows), 1)
    sizes = [1 << b for b in range(1, n_rows.bit_length())]
    return dict(
        diag=row_a == col_a,
        same={s: row_a // s == col_a // s for s in sizes if s < n_rows},
        upper_half={s: row % s >= s // 2 for s in sizes if s < 2 * SUBLANES},
        from_row={r: row % SUBLANES >= r for r in range(2, SUBLANES, 2)})


def _hgrn_block(q, k, g, v, st, sums, ones, masks, reverse):
    n_rows = q.shape[0]
    cum = _dot_exact_lhs(sums, g)
    tot = cum[0:1] if reverse else cum[n_rows - 1:n_rows]
    o_inter = _dot_nt((q * jnp.exp2(cum)).astype(BF16), st.astype(BF16))
    k_end = (k * jnp.exp2(tot - cum)).astype(BF16)
    st_new = st * jnp.exp2(tot) + _dot_tn(v.astype(BF16), k_end)

    a = None
    zero_tile = jnp.zeros((SUBLANES, LANES), F32)
    size = n_rows
    while size >= 2:
        half = size // 2
        if size < 2 * SUBLANES:
            ref_off = half if reverse else half - 1
            ref = _tile_row(cum, ref_off)
            for first in range(size, SUBLANES, size):
                ref = jnp.where(masks["from_row"][first], _tile_row(cum, first + ref_off), ref)
            upper = masks["upper_half"][size]
            d = cum - ref
            q_exp = jnp.where(upper, NEG_INF, d) if reverse else jnp.where(upper, d, NEG_INF)
            k_exp = jnp.where(upper, -d, NEG_INF) if reverse else jnp.where(upper, NEG_INF, -d)
            q_t = (q * jnp.exp2(q_exp)).astype(BF16)
            k_t = (k * jnp.exp2(k_exp)).astype(BF16)
        else:
            q_tiles, k_tiles = [], []
            for r0 in range(0, n_rows, SUBLANES):
                sl = slice(r0, r0 + SUBLANES)
                first = r0 // size * size
                ref_row = first + (half if reverse else half - 1)
                ref = cum[ref_row:ref_row + 1]
                if (r0 - first < half) if reverse else (r0 - first >= half):
                    q_tiles.append(q[sl] * jnp.exp2(cum[sl] - ref))
                    k_tiles.append(zero_tile)
                else:
                    q_tiles.append(zero_tile)
                    k_tiles.append(k[sl] * jnp.exp2(ref - cum[sl]))
            q_t = jnp.concatenate(q_tiles, axis=0).astype(BF16)
            k_t = jnp.concatenate(k_tiles, axis=0).astype(BF16)
        a_l = _dot_nt(q_t, k_t)
        a = a_l if a is None else jnp.where(masks["same"][size], a_l, a)
        size //= 2
    a = jnp.where(masks["diag"], _dot((q * k).astype(BF16), ones), a)
    return o_inter + _dot(a.astype(BF16), v.astype(BF16)), st_new


def _hgrn_kernel(q_f, g_f, v_f, q_b, g_b, v_b, trif_ref, trib_ref, ones_ref, of_ref, ob_ref, st_f, st_b):
    @pl.when(pl.program_id(2) == 0)
    def _():
        st_f[...] = jnp.zeros_like(st_f)
        st_b[...] = jnp.zeros_like(st_b)

    ones = ones_ref[...]
    masks = _hgrn_masks(q_f.shape[0])
    for h in range(HG_HEADS_PER_STEP):
        sl = slice(h * HG_DK, (h + 1) * HG_DK)
        g = g_f[:, sl]
        of_ref[:, sl], st_f[h] = _hgrn_block(q_f[:, sl], 1.0 - jnp.exp2(g), g, v_f[:, sl], st_f[h],
                                             trif_ref[...], ones, masks, False)
        g = g_b[:, sl]
        ob_ref[:, sl], st_b[h] = _hgrn_block(q_b[:, sl], 1.0 - jnp.exp2(g), g, v_b[:, sl], st_b[h],
                                             trib_ref[...], ones, masks, True)


def _hgrn(hg, tri_f, tri_b, ones, n_batch, seq, ctx_len):
    tr = HG_STEP_ROWS
    nc, nl = ctx_len // tr, seq // tr
    first = n_batch * nc

    def fwd(b, j):
        return jnp.where(j < nc, b * nc + j, first + b * nl + (j - nc))

    def bwd(b, j):
        return jnp.where(j < nc, b * nc + (nc - 1 - j), first + b * nl + (nl - 1 - (j - nc)))

    hps = HG_HEADS_PER_STEP
    groups = HG_HEADS // hps
    wide = hps * HG_DK
    col = lambda row_fn, group: pl.BlockSpec((tr, wide), lambda b, h, j: (row_fn(b, j), group * groups + h))
    const = lambda a: pl.BlockSpec(a.shape, lambda b, h, j: (0, 0))
    out = jax.ShapeDtypeStruct((hg.shape[0], HG_W), F32)
    return pl.pallas_call(
        _hgrn_kernel,
        grid=(n_batch, groups, nc + nl),
        in_specs=[col(fwd, 0), col(fwd, 1), col(fwd, 3), col(bwd, 0), col(bwd, 2), col(bwd, 3),
                  const(tri_f), const(tri_b), const(ones)],
        out_specs=[pl.BlockSpec((tr, wide), lambda b, h, j: (fwd(b, j), h)),
                   pl.BlockSpec((tr, wide), lambda b, h, j: (bwd(b, j), h))],
        out_shape=[out, out],
        scratch_shapes=[pltpu.VMEM((hps, HG_DK, HG_DK), F32), pltpu.VMEM((hps, HG_DK, HG_DK), F32)],
        compiler_params=_cparams("parallel", "parallel", "arbitrary"),
        name="hgrn",
    )(hg, hg, hg, hg, hg, hg, tri_f, tri_b, ones)


def _mixout_kernel(*refs, n_heads, stream_off):
    counts = [1 if n is None else 2 for n in n_heads]
    streams, pos = [], 0
    for n, c in zip(n_heads, counts):
        streams.append((refs[pos:pos + c], n))
        pos += c
    (of_ref, ob_ref, zg_ref, hgn_ref, w_ref, mod_ref, g2_ref, rw_ref, rb_ref,
     xo_ref, h2_ref, te_ref, tg_ref) = refs[pos:]
    x_in = _stream_block(*streams[0], first_block=stream_off)
    o = of_ref[...] + ob_ref[...]
    zg = zg_ref[...]
    parts = [_stream_block(*streams[1])]
    for h in range(HG_HEADS):
        sl = slice(h * HG_DK, (h + 1) * HG_DK)
        oh = o[:, sl]
        yh = oh * lax.rsqrt(jnp.mean(oh * oh, axis=-1, keepdims=True) + NORM_EPS) * hgn_ref[...]
        parts.append((yh * _silu(zg[:, sl])).astype(BF16))
    parts.append(_stream_block(*streams[2]))
    y = _dot(jnp.concatenate(parts, axis=1), w_ref[...])
    mod = mod_ref[...]
    x_new = x_in + mod[2:3] * y
    xo_ref[...] = x_new
    h2 = _mod_rmsnorm(x_new, g2_ref[...], mod, 3, 4)
    h2_ref[...] = h2.astype(BF16)

    h_hi, h_lo = _split_bf16(h2, 2)
    logits = _dot(jnp.concatenate([h_hi, h_lo, h_hi], axis=1), rw_ref[...]) + rb_ref[...]
    lane = lax.broadcasted_iota(jnp.int32, logits.shape, 1).astype(F32)
    top_e = jnp.zeros(logits.shape, F32)
    top_v = jnp.full(logits.shape, NEG_INF, F32)
    for j in range(TOP_K):
        best = jnp.max(logits, axis=-1, keepdims=True)
        arg = jnp.min(jnp.where(logits == best, lane, float(LANES)), axis=-1, keepdims=True)
        top_e = jnp.where(lane == j, arg, top_e)
        top_v = jnp.where(lane == j, best, top_v)
        logits = jnp.where(lane == arg, -jnp.inf, logits)
    ex = jnp.exp(top_v - jnp.max(top_v, axis=-1, keepdims=True))
    te_ref[...] = top_e.astype(jnp.int32)
    tg_ref[...] = ex / jnp.sum(ex, axis=-1, keepdims=True)


def _mixout(x_stream, yna_stream, o_f, o_b, hg, ysw_stream, hgn, w_out, mods, norm2_g, rw, rb, geom, stream_off):
    n_batch, n_ctx_blocks, blocks_per_batch = geom
    tm = ROW_BLOCK
    t = (n_ctx_blocks + n_batch * blocks_per_batch - stream_off) * tm
    row = lambda w: pl.BlockSpec((tm, w), lambda i: (i, 0))
    full = lambda w, cb: pl.BlockSpec((tm, w), lambda i: (i + stream_off, cb))
    const = lambda a: pl.BlockSpec(a.shape, lambda i: (0,) * a.ndim)
    x_specs, x_arrays = _stream_specs(x_stream, D_MODEL, stream_off)
    yna_specs, yna_arrays = _stream_specs(yna_stream, NA_W)
    ysw_specs, ysw_arrays = _stream_specs(ysw_stream, SWA_W)
    n_heads = (_n_head(x_stream), _n_head(yna_stream), _n_head(ysw_stream))
    return pl.pallas_call(
        functools.partial(_mixout_kernel, n_heads=n_heads, stream_off=stream_off),
        grid=(t // tm,),
        in_specs=x_specs + yna_specs + ysw_specs + [
            full(HG_W, 0), full(HG_W, 0), full(HG_W, 4), const(hgn), const(w_out),
            pl.BlockSpec((None, 6, D_MODEL), lambda i: (
                _mod_row(i + stream_off, n_ctx_blocks, blocks_per_batch, n_batch), 0, 0)),
            const(norm2_g), const(rw), const(rb)],
        out_specs=[row(D_MODEL), row(D_MODEL), row(LANES), row(LANES)],
        out_shape=[jax.ShapeDtypeStruct((t, D_MODEL), F32), jax.ShapeDtypeStruct((t, D_MODEL), BF16),
                   jax.ShapeDtypeStruct((t, LANES), jnp.int32), jax.ShapeDtypeStruct((t, LANES), F32)],
        compiler_params=_cparams("parallel"),
        name="mixout",
    )(*x_arrays, *yna_arrays, *ysw_arrays, o_f, o_b, hg, hgn, w_out, mods, norm2_g, rw, rb)


def _start_run_copies(run_ref, n_experts, make_copy):
    for e in range(n_experts):
        n = run_ref[0, 0, e]
        local0 = run_ref[0, 0, n_experts + e]
        slot0 = run_ref[0, 0, 2 * n_experts + e]
        for bit in reversed(range(RUN_BITS)):
            @pl.when((n & (1 << bit)) != 0)
            def _():
                done = (n >> (bit + 1)) << (bit + 1)
                make_copy(local0 + done, slot0 + done, 1 << bit).start()


def _tile_rows(first_token, n_tokens):
    return pl.ds(pl.multiple_of(first_token * SUBLANES, SUBLANES), n_tokens * SUBLANES)


def _local_positions(te_ref, off_ref, stri_ref):
    e_tok = te_ref[...]
    lane = lax.broadcasted_iota(jnp.int32, e_tok.shape, 1)
    member = [lane == e_tok[:, j:j + 1] for j in range(TOP_K)]
    count = functools.reduce(jnp.add, [jnp.where(m, 1.0, 0.0) for m in member])
    before = _dot(stri_ref[...], count.astype(BF16)) + off_ref[...]
    return [jnp.sum(jnp.where(m, before, 0.0), axis=1, keepdims=True) for m in member]


def _dispatch_kernel(pad_ref, nu_ref, run_ref, te_ref, off_ref, h_ref, stri_ref, xs_hbm, pos_ref,
                     zeros, xloc0, xloc1, sem, zero_sem, *, n_tok_blocks):
    n_tok = h_ref.shape[0]
    block_rows = zeros.shape[0]
    n_blocks = xs_hbm.shape[0] // block_rows

    def zero_slot(e, r):
        row = pl.multiple_of((pad_ref[0, e] + r) * SUBLANES, SUBLANES)
        return pltpu.make_async_copy(zeros.at[pl.ds(0, SUBLANES), :], xs_hbm.at[pl.ds(row, SUBLANES), :], zero_sem)

    def zero_block(b):
        row = pl.multiple_of(b * block_rows, block_rows)
        return pltpu.make_async_copy(zeros, xs_hbm.at[pl.ds(row, block_rows), :], zero_sem)

    def for_each_unused(slot_fn, block_fn):
        def per_expert(e, carry):
            lax.fori_loop(0, pad_ref[1, e], lambda r, c: (slot_fn(e, r), c)[1], 0)
            return carry

        lax.fori_loop(0, pad_ref.shape[1], per_expert, 0)
        lax.fori_loop(nu_ref[0], n_blocks, lambda b, c: (block_fn(b), c)[1], 0)

    @pl.when(pl.program_id(0) == 0)
    def _():
        zeros[...] = jnp.zeros_like(zeros)
        for_each_unused(lambda e, r: zero_slot(e, r).start(), lambda b: zero_block(b).start())
        for_each_unused(lambda e, r: zero_slot(e, r).wait(), lambda b: zero_block(b).wait())

    bufs = (xloc0, xloc1)

    def sort_into(b):
        pos = _local_positions(te_ref, off_ref, stri_ref)
        lane = lax.broadcasted_iota(jnp.int32, (n_tok, LANES), 1)
        pos_ref[...] = functools.reduce(jnp.add, [jnp.where(lane == j, p, 0.0) for j, p in enumerate(pos)])
        col = lax.broadcasted_iota(jnp.int32, (n_tok, TOP_K * n_tok), 1).astype(F32)
        chosen = functools.reduce(jnp.logical_or, [col == p for p in pos])
        sorted_rows = _dot_tn(jnp.where(chosen, 1.0, 0.0).astype(BF16), h_ref[...])
        _store_token_tiles(bufs[b], sorted_rows)

    def start_copies(b):
        _start_run_copies(run_ref, pad_ref.shape[1], lambda local, slot, size: pltpu.make_async_copy(
            bufs[b].at[_tile_rows(local, size), :], xs_hbm.at[_tile_rows(slot, size), :], sem.at[b]))

    def wait_copies(b):
        pltpu.make_async_copy(bufs[b], xs_hbm.at[pl.ds(0, bufs[b].shape[0]), :], sem.at[b]).wait()

    i = pl.program_id(0)

    @pl.when(i == 0)
    def _():
        sort_into(0)

    for parity in (0, 1):
        @pl.when((i % 2 == parity) & (i >= 1) & (i < n_tok_blocks))
        def _():
            @pl.when(i >= 2)
            def _():
                wait_copies(parity)

            start_copies(1 - parity)
            sort_into(parity)

    @pl.when(i == n_tok_blocks)
    def _():
        if n_tok_blocks >= 2:
            wait_copies(n_tok_blocks % 2)
        start_copies((n_tok_blocks - 1) % 2)
        wait_copies((n_tok_blocks - 1) % 2)


def _dispatch(h, top_e, runs, run_first, pad_slots, n_used, n_slots):
    tm = ROW_BLOCK
    t = h.shape[0]
    nb = t // tm
    assert n_slots % MOE_ROWS == 0
    stri = jnp.asarray(np.tril(np.ones((tm, tm), np.float32), -1), BF16)
    cur = lambda i: jnp.minimum(i, nb - 1)
    row = lambda w: pl.BlockSpec((tm, w), lambda i: (cur(i), 0))
    sort_buffer = pltpu.VMEM((TOP_K * tm * SUBLANES, LANES), F32)
    return pl.pallas_call(
        functools.partial(_dispatch_kernel, n_tok_blocks=nb),
        grid=(nb + 1,),
        in_specs=[pl.BlockSpec(memory_space=pltpu.SMEM), pl.BlockSpec(memory_space=pltpu.SMEM),
                  pl.BlockSpec((1, 1, runs.shape[2]), lambda i: (jnp.maximum(i - 1, 0), 0, 0),
                               memory_space=pltpu.SMEM),
                  row(LANES), pl.BlockSpec((None, 1, LANES), lambda i: (cur(i), 0, 0)), row(D_MODEL),
                  pl.BlockSpec((tm, tm), lambda i: (0, 0))],
        out_specs=[pl.BlockSpec(memory_space=pl.ANY), row(LANES)],
        out_shape=[jax.ShapeDtypeStruct((n_slots * SUBLANES, LANES), F32), jax.ShapeDtypeStruct((t, LANES), F32)],
        scratch_shapes=[pltpu.VMEM((MOE_ROWS * SUBLANES, LANES), F32), sort_buffer, sort_buffer,
                        pltpu.SemaphoreType.DMA((2,)), pltpu.SemaphoreType.DMA(())],
        compiler_params=_cparams("arbitrary"),
        name="moe_dispatch",
    )(pad_slots, n_used, runs, top_e, run_first, h, stri)


def _moe_kernel(be_ref, nu_ref, next_ref, run_ref, xs_ref, wgu_hbm, bgu_ref, wd_hbm, bd_ref, y_ref,
                wgu_f32, wd_f32, wgu_bf, wd_bf, sem, *, layer):
    i = pl.program_id(0)
    live = i < nu_ref[0]
    tm = y_ref.shape[0] // SUBLANES
    changed = (i == 0) | (be_ref[i] != be_ref[jnp.maximum(i - 1, 0)])

    def weight_copies(expert, buf):
        return (pltpu.make_async_copy(wgu_hbm.at[layer, expert], wgu_f32.at[buf], sem.at[buf]),
                pltpu.make_async_copy(wd_hbm.at[layer, expert], wd_f32.at[buf], sem.at[buf]))

    @pl.when(live & changed)
    def _():
        buf = run_ref[i] % 2

        @pl.when(i == 0)
        def _():
            for copy in weight_copies(be_ref[0], 0):
                copy.start()

        for copy in weight_copies(be_ref[i], buf):
            copy.wait()
        wgu_bf[...] = wgu_f32[buf].astype(BF16)
        wd_bf[...] = wd_f32[buf].astype(BF16)

        @pl.when(next_ref[i] >= 0)
        def _():
            for copy in weight_copies(next_ref[i], 1 - buf):
                copy.start()

    @pl.when(live)
    def _():
        x = _load_token_tiles(xs_ref, 0, tm)
        gu = _dot(x.astype(BF16), wgu_bf[...]) + bgu_ref[...]
        glu = jnp.minimum(gu[:, :D_FF], SWIGLU_LIMIT)
        lin = jnp.clip(gu[:, D_FF:], -SWIGLU_LIMIT, SWIGLU_LIMIT)
        act = glu * _sigmoid(SWIGLU_ALPHA * glu) * (lin + 1.0)
        _store_token_tiles(y_ref, _dot(act.astype(BF16), wd_bf[...]) + bd_ref[...])

    @pl.when(jnp.logical_not(live))
    def _():
        y_ref[...] = jnp.zeros_like(y_ref)


def _moe_ffn(xs_tiles, block_e, n_used, w_gu, b_gu, w_down, b_down, layer):
    tm = MOE_ROWS
    nb = xs_tiles.shape[0] // (tm * SUBLANES)
    n_l, n_e, d, f2 = w_gu.shape
    idx = jnp.arange(nb, dtype=jnp.int32)
    starts = jnp.concatenate([jnp.ones((1,), bool), block_e[1:] != block_e[:-1]]) & (idx < n_used[0])
    run_idx = jnp.cumsum(starts.astype(jnp.int32)) - 1
    run_expert = jnp.full((nb + 1,), -1, jnp.int32).at[jnp.where(starts, run_idx, nb)].set(block_e)[:nb]
    next_e = jnp.concatenate([run_expert[1:], jnp.full((1,), -1, jnp.int32)])[run_idx]
    bias = lambda c: pl.BlockSpec((None, None, 1, c), lambda i, be, nu, nx, ru: (layer, be[i], 0, 0))
    grid_spec = pltpu.PrefetchScalarGridSpec(
        num_scalar_prefetch=4,
        grid=(nb,),
        in_specs=[pl.BlockSpec((tm * SUBLANES, LANES), lambda i, be, nu, nx, ru: (jnp.minimum(i, nu[0] - 1), 0)),
                  pl.BlockSpec(memory_space=pl.ANY), bias(f2), pl.BlockSpec(memory_space=pl.ANY), bias(d)],
        out_specs=pl.BlockSpec((tm * SUBLANES, LANES), lambda i, be, nu, nx, ru: (i, 0)),
        scratch_shapes=[pltpu.VMEM((2, d, f2), F32), pltpu.VMEM((2, f2 // 2, d), F32),
                        pltpu.VMEM((d, f2), BF16), pltpu.VMEM((f2 // 2, d), BF16), pltpu.SemaphoreType.DMA((2,))],
    )
    return pl.pallas_call(
        functools.partial(_moe_kernel, layer=layer),
        grid_spec=grid_spec,
        out_shape=jax.ShapeDtypeStruct((nb * tm * SUBLANES, LANES), F32),
        compiler_params=_cparams("arbitrary"),
        name="moe_ffn",
    )(block_e, n_used, next_e, run_idx, xs_tiles, w_gu, b_gu.reshape(n_l, n_e, 1, f2), w_down,
      b_down.reshape(n_l, n_e, 1, d))


def _combine_kernel(run_ref, pos_ref, tg_ref, x_ref, y_hbm, mod_ref, o_ref, ybuf0, ybuf1, sem, *, n_tok_blocks):
    i = pl.program_id(0)
    tm = o_ref.shape[0]
    n_rows = TOP_K * tm
    bufs = (ybuf0, ybuf1)

    def fetch(b):
        _start_run_copies(run_ref, run_ref.shape[2] // 3, lambda local, slot, size: pltpu.make_async_copy(
            y_hbm.at[_tile_rows(slot, size), :], bufs[b].at[_tile_rows(local, size), :], sem.at[b]))

    def wait_fetch(b):
        pltpu.make_async_copy(y_hbm.at[pl.ds(0, n_rows * SUBLANES), :], bufs[b], sem.at[b]).wait()

    def combine(b):
        y = _load_token_tiles(bufs[b], 0, n_rows).astype(BF16)
        pos, gate = pos_ref[...], tg_ref[...]
        col = lax.broadcasted_iota(jnp.int32, (tm, n_rows), 1).astype(F32)
        g = functools.reduce(jnp.add, [jnp.where(col == pos[:, j:j + 1], gate[:, j:j + 1], 0.0)
                                       for j in range(TOP_K)])
        g_hi, g_lo = _split_bf16(g, 2)
        o_ref[...] = x_ref[...] + mod_ref[...][5:6] * (_dot(g_hi, y) + _dot(g_lo, y))

    @pl.when(i == 0)
    def _():
        fetch(0)

    for parity in (0, 1):
        @pl.when((i % 2 == parity) & (i >= 1) & (i < n_tok_blocks))
        def _():
            wait_fetch(1 - parity)
            fetch(parity)
            combine(1 - parity)

    @pl.when(i == n_tok_blocks)
    def _():
        wait_fetch((n_tok_blocks - 1) % 2)
        combine((n_tok_blocks - 1) % 2)


def _combine(x, y_tiles, runs, pos, gates, mods, geom, stream_off):
    n_batch, n_ctx_blocks, blocks_per_batch = geom
    t = x.shape[0]
    tm = ROW_BLOCK
    nb = t // tm
    prev = lambda i: jnp.maximum(i - 1, 0)
    row = lambda w: pl.BlockSpec((tm, w), lambda i: (prev(i), 0))
    sorted_buffer = pltpu.VMEM((TOP_K * tm * SUBLANES, LANES), F32)
    return pl.pallas_call(
        functools.partial(_combine_kernel, n_tok_blocks=nb),
        grid=(nb + 1,),
        in_specs=[pl.BlockSpec((1, 1, runs.shape[2]), lambda i: (jnp.minimum(i, nb - 1), 0, 0),
                               memory_space=pltpu.SMEM),
                  row(LANES), row(LANES), row(D_MODEL), pl.BlockSpec(memory_space=pl.ANY),
                  pl.BlockSpec((None, 6, D_MODEL), lambda i: (
                      _mod_row(prev(i) + stream_off, n_ctx_blocks, blocks_per_batch, n_batch), 0, 0))],
        out_specs=row(D_MODEL),
        out_shape=jax.ShapeDtypeStruct((t, D_MODEL), F32),
        scratch_shapes=[sorted_buffer, sorted_buffer, pltpu.SemaphoreType.DMA((2,))],
        compiler_params=_cparams("arbitrary"),
        name="moe_combine",
    )(runs, pos, gates, x, y_tiles, mods)


def _route(top_e, n_experts, tok_block, slot_block):
    t, k = top_e.shape
    nb = t // tok_block
    onehot = (top_e[:, :, None] == jnp.arange(n_experts, dtype=jnp.int32)).astype(jnp.int32)
    n = onehot.reshape(nb, tok_block * k, n_experts).sum(axis=1)
    counts = n.sum(axis=0)
    padded = (counts + slot_block - 1) // slot_block * slot_block
    pend = jnp.cumsum(padded)
    start = pend - padded
    local0 = jnp.cumsum(n, axis=1) - n
    slot0 = start[None, :] + jnp.cumsum(n, axis=0) - n
    runs = jnp.concatenate([n, local0, slot0], axis=1).astype(jnp.int32).reshape(nb, 1, 3 * n_experts)
    run_first = jnp.pad(local0.astype(F32), ((0, 0), (0, LANES - n_experts))).reshape(nb, 1, LANES)
    n_blocks = (t * k + n_experts * (slot_block - 1) + slot_block - 1) // slot_block
    block_start = jnp.arange(n_blocks, dtype=jnp.int32) * slot_block
    block_e = jnp.minimum((block_start[:, None] >= pend[None, :]).sum(axis=1), n_experts - 1).astype(jnp.int32)
    n_used = (pend[-1] // slot_block).astype(jnp.int32).reshape(1)
    pad_slots = jnp.stack([start + counts, padded - counts], axis=0).astype(jnp.int32)
    return runs, run_first, block_e, n_used, pad_slots, n_blocks * slot_block


def _na_row_pattern(rb, n_rows):
    wr = min(NA_WIN_H, n_rows)
    n_rb = n_rows // NA_QROWS
    q_row = rb * NA_QROWS + np.arange(NA_QROWS)[:, None]
    k_row = np.clip(rb - 1, 0, n_rb - NA_KROWS // NA_QROWS) * NA_QROWS + np.arange(NA_KROWS)[None, :]
    row_start = np.clip(q_row - wr // 2, 0, n_rows - wr)
    ok = (k_row >= row_start) & (k_row < row_start + wr)
    return np.where(ok, k_row - q_row + (NA_WIN_H - 1), -1)


def _na_bias_table(rpb, seq):
    n_rows = seq // GRID_W
    n_rb = n_rows // NA_QROWS
    patterns = [_na_row_pattern(rb, n_rows) for rb in range(n_rb)]
    assert all((p == patterns[1]).all() for p in patterns[1:-1])
    q_col = np.arange(GRID_W)[:, None]
    k_col = np.arange(GRID_W)[None, :]
    col_start = np.clip(q_col - NA_WIN_W // 2, 0, GRID_W - NA_WIN_W)
    col_ok = (k_col >= col_start) & (k_col < col_start + NA_WIN_W)
    dc = np.clip(k_col - q_col + (NA_WIN_W - 1), 0, 2 * NA_WIN_W - 2)
    onehot = ((dc[None] == np.arange(2 * NA_WIN_W - 1)[:, None, None]) & col_ok[None]).astype(np.float32)
    by_col = jnp.einsum('hab,bqk->haqk', rpb.astype(F32), jnp.asarray(onehot), precision=HIGHEST)
    by_col = by_col + jnp.asarray(np.where(col_ok, 0.0, NEG_INF).astype(np.float32))
    masked = jnp.full((rpb.shape[0], GRID_W, GRID_W), NEG_INF, F32)
    variants = []
    for pattern in (patterns[0], patterns[1], patterns[-1]):
        rows = [jnp.concatenate([by_col[:, a] if a >= 0 else masked for a in pattern[qr]], axis=2)
                for qr in range(NA_QROWS)]
        variants.append(jnp.concatenate(rows, axis=1))
    return jnp.stack(variants, axis=0)


def _rope_tables(seq, ctx_len):
    quarter = HEAD_DIM // 4
    lane = np.arange(SWA_W)
    inv = ROPE_BASE ** (-(lane % quarter).astype(np.float64) / quarter)
    t = np.arange(seq)
    pos = np.where((lane % HEAD_DIM < HEAD_DIM // 2)[None, :], (t // GRID_W)[:, None], (t % GRID_W)[:, None])
    ang = jnp.asarray(pos, F32) * jnp.asarray(inv, F32)[None, :]
    sign = np.where(lane % (2 * quarter) < quarter, -1.0, 1.0).astype(np.float32)
    cos = jnp.concatenate([jnp.ones((ctx_len, SWA_W), F32), jnp.cos(ang)], axis=0)
    sin = jnp.concatenate([jnp.zeros((ctx_len, SWA_W), F32), jnp.sin(ang) * sign[None, :]], axis=0)
    return cos, sin


def kernel(x, c, ctx, c_ctx, hg_lower_bounds, ada_w, ada_b, norm1_g, norm2_g, w_in, na_q_norm, na_k_norm, na_rpb,
           hg_norm_g, swa_q_norm, swa_k_norm, swa_sink, w_out, router_w, router_b, w_gu, b_gu, w_down, b_down):
    n_batch, seq, d = x.shape
    ctx_len = ctx.shape[1]
    depth = ada_w.shape[0]
    assert d == D_MODEL and seq % ROW_BLOCK == 0 and ctx_len == ROW_BLOCK
    n_ctx_rows = n_batch * ctx_len
    geom = (n_batch, n_ctx_rows // ROW_BLOCK, seq // ROW_BLOCK)

    p_lb = jax.nn.softmax(hg_lower_bounds.astype(F32), axis=0)
    lbs = jnp.cumsum(p_lb, axis=0) - p_lb[0]

    cos_t, sin_t = _rope_tables(seq, ctx_len)
    lane = np.arange(NA_W)
    group_ones = jnp.asarray((lane[:, None] // HEAD_DIM == lane[None, :] // HEAD_DIM).astype(np.float32), BF16)
    tri_f = jnp.asarray(_hgrn_sum_table(HG_STEP_ROWS, False), BF16)
    tri_b = jnp.asarray(_hgrn_sum_table(HG_STEP_ROWS, True), BF16)
    ones_bf = jnp.ones((LANES, HG_STEP_ROWS), BF16)
    n_mod_rows = -(-(n_batch + 1) // 8) * 8
    cc = jnp.zeros((n_mod_rows, d), F32).at[:n_batch].set(c).at[n_batch].set(c_ctx)
    dup = lambda w: jnp.concatenate([w[:, :HEAD_DIM], w[:, :HEAD_DIM], w[:, HEAD_DIM:], w[:, HEAD_DIM:]], axis=1)
    tile4 = lambda g: jnp.tile(g.astype(F32), 4).reshape(1, 4 * HEAD_DIM)
    pad_e = LANES - N_EXPERTS

    x_stream = (ctx.reshape(n_ctx_rows, d), x.reshape(n_batch * seq, d))
    for l in range(depth):
        last = l == depth - 1
        mods = _ada_mod(cc, ada_w[l], ada_b[l]).reshape(n_mod_rows, 6, d)
        w = w_in[l]
        kv0 = 3 * NA_W + HG_COLS + SWA_W
        w_ext = jnp.concatenate([w[:, :kv0], dup(w[:, kv0:kv0 + SWA_KV_W]), dup(w[:, kv0 + SWA_KV_W:])],
                                axis=1).astype(BF16)
        hg, naq, nak, nav, swq, swk, swv = _inproj(
            x_stream, mods, norm1_g[l].reshape(1, d), w_ext, tile4(na_q_norm[l]), tile4(na_k_norm[l]),
            tile4(swa_q_norm[l]), tile4(swa_k_norm[l]), cos_t, sin_t, group_ones, lbs[l].reshape(1, HG_W), geom)

        y_na = _na_attention(naq, nak, nav, _na_bias_table(na_rpb[l], seq), geom)
        y_sw = _swa_attention(swq, swk, swv, swa_sink[l].astype(F32), geom, seq, ctx_len)
        o_f, o_b = _hgrn(hg, tri_f, tri_b, ones_bf, n_batch, seq, ctx_len)

        if last:
            stream_off = geom[1]
            yna_stream, ysw_stream = (None, y_na), (None, y_sw)
        else:
            stream_off = 0
            yc_na, yc_sw = _ctx_attention(naq, nak, nav, swq, swk, swv, swa_sink[l].astype(F32), n_batch, ctx_len)
            yna_stream, ysw_stream = (yc_na, y_na), (yc_sw, y_sw)

        rw_hi, rw_lo = _split_bf16(jnp.pad(router_w[l].astype(F32), ((0, 0), (0, pad_e))), 2)
        rw = jnp.concatenate([rw_hi, rw_hi, rw_lo], axis=0)
        rb = jnp.pad(router_b[l].astype(F32), (0, pad_e), constant_values=NEG_INF).reshape(1, LANES)
        x_new, h2, top_e, top_g = _mixout(
            x_stream, yna_stream, o_f, o_b, hg, ysw_stream, hg_norm_g[l].reshape(1, HG_DK).astype(F32),
            w_out[l].astype(BF16), mods, norm2_g[l].reshape(1, d), rw, rb, geom, stream_off)

        runs, run_first, block_e, n_used, pad_slots, n_slots = _route(top_e[:, :TOP_K], N_EXPERTS, ROW_BLOCK, MOE_ROWS)
        xs, pos = _dispatch(h2, top_e, runs, run_first, pad_slots, n_used, n_slots)
        y_slots = _moe_ffn(xs, block_e, n_used, w_gu, b_gu, w_down, b_down, l)
        x_stream = (None, _combine(x_new, y_slots, runs, pos, top_g, mods, geom, stream_off))
    return x_stream[1].reshape(n_batch, seq, d)
```

```python
import functools

import numpy as np
import jax
import jax.numpy as jnp
from jax import lax
from jax.experimental import pallas as pl
from jax.experimental.pallas import tpu as pltpu

D_MODEL = 1024
GRID_W = 64
HEAD_DIM = 64
ATTN_SCALE = HEAD_DIM ** -0.5
NA_HEADS = 4
NA_WIN_H = 8
NA_WIN_W = 16
HG_HEADS = 4
HG_DK = 128
SWA_Q_HEADS = 4
SWA_KV_HEADS = 2
SWA_WINDOW = 128
SWA_BLOCK = 128
ROPE_BASE = 10000.0
N_EXPERTS = 32
TOP_K = 4
D_FF = 1024
SWIGLU_LIMIT = 7.0
SWIGLU_ALPHA = 1.702
NORM_EPS = 1e-6
NEG_INF = -1e30

NA_W = NA_HEADS * HEAD_DIM
HG_W = HG_HEADS * HG_DK
SWA_W = SWA_Q_HEADS * HEAD_DIM
SWA_KV_W = SWA_KV_HEADS * HEAD_DIM
HG_COLS = 5 * HG_W

LANES = 128
SUBLANES = 8
ROW_BLOCK = 256
NA_QROWS = 4
NA_KROWS = 12
NA_HEADS_PER_PASS = 1
SWA_HEADS_PER_PASS = 4
HG_STEP_ROWS = 256
HG_HEADS_PER_STEP = 4
MOE_ROWS = 256
RUN_BITS = ROW_BLOCK.bit_length()
VMEM_LIMIT = 56 * 1024 * 1024

F32 = jnp.float32
BF16 = jnp.bfloat16
HIGHEST = lax.Precision.HIGHEST


def _cparams(*sem):
    return pltpu.CompilerParams(dimension_semantics=sem, vmem_limit_bytes=VMEM_LIMIT)


def _dot(a, b, precision=None):
    return jnp.dot(a, b, preferred_element_type=F32, precision=precision)


def _dot_nt(a, b):
    return lax.dot_general(a, b, (((1,), (1,)), ((), ())), preferred_element_type=F32)


def _dot_tn(a, b):
    return lax.dot_general(a, b, (((0,), (0,)), ((), ())), preferred_element_type=F32)


def _store_token_tiles(ref, x):
    n = x.shape[0]
    for c in range(SUBLANES):
        ref[pl.ds(c, n, stride=SUBLANES), :] = x[:, c * LANES:(c + 1) * LANES]


def _load_token_tiles(ref, first_token, n):
    return jnp.concatenate([ref[pl.ds(first_token * SUBLANES + c, n, stride=SUBLANES), :] for c in range(SUBLANES)],
                           axis=1)


def _sigmoid(x):
    return 1.0 / (1.0 + jnp.exp(-x))


def _silu(x):
    return x * _sigmoid(x)


def _ada_kernel(c_ref, w_ref, b_ref, o_ref):
    o_ref[...] = _dot(_silu(c_ref[...]), w_ref[...], HIGHEST) + b_ref[...]


def _ada_mod(cc, w, b):
    m, d = cc.shape
    n = w.shape[1]
    tn = D_MODEL
    return pl.pallas_call(
        _ada_kernel,
        grid=(n // tn,),
        in_specs=[pl.BlockSpec((m, d), lambda j: (0, 0)),
                  pl.BlockSpec((d, tn), lambda j: (0, j)),
                  pl.BlockSpec((1, tn), lambda j: (0, j))],
        out_specs=pl.BlockSpec((m, tn), lambda j: (0, j)),
        out_shape=jax.ShapeDtypeStruct((m, n), F32),
        compiler_params=_cparams("parallel"),
        name="ada_mod",
    )(cc, w, b.reshape(1, n))


def _mod_rmsnorm(x, g, mod, shift_row, scale_row):
    y = x * lax.rsqrt(jnp.mean(x * x, axis=-1, keepdims=True) + NORM_EPS) * g
    return y * (1.0 + mod[scale_row:scale_row + 1]) + mod[shift_row:shift_row + 1]


def _split_bf16(x, terms):
    parts = []
    for _ in range(terms - 1):
        parts.append(x.astype(BF16))
        x = x - parts[-1].astype(F32)
    return parts + [x.astype(BF16)]


def _head_rmsnorm(x, w, group_ones):
    hi, lo = _split_bf16(x * x, 2)
    n = x.shape[0]
    ss = _dot(jnp.concatenate([hi, lo], axis=0), group_ones)
    return x * lax.rsqrt((ss[:n] + ss[n:]) * (1.0 / HEAD_DIM) + NORM_EPS) * w


def _rope(x, cos, sin_signed):
    n = x.shape[-1]
    lane = lax.broadcasted_iota(jnp.int32, x.shape, 1)
    quarter = HEAD_DIM // 4
    partner = jnp.where(lane % (2 * quarter) < quarter,
                        pltpu.roll(x, n - quarter, 1), pltpu.roll(x, quarter, 1))
    return x * cos + partner * sin_signed


def _stream_specs(stream, width, first_block=0):
    head, tail = stream
    tm = ROW_BLOCK
    if head is None:
        return [pl.BlockSpec((tm, width), lambda i: (i + first_block, 0))], [tail]
    n_head = head.shape[0] // tm
    return ([pl.BlockSpec((tm, width), lambda i: (jnp.minimum(i + first_block, n_head - 1), 0)),
             pl.BlockSpec((tm, width), lambda i: (jnp.maximum(i + first_block - n_head, 0), 0))], [head, tail])


def _stream_block(refs, n_head, first_block=0):
    if n_head is None:
        return refs[0][...]
    return jnp.where(pl.program_id(0) + first_block < n_head, refs[0][...], refs[1][...])


def _n_head(stream):
    return None if stream[0] is None else stream[0].shape[0] // ROW_BLOCK


def _inproj_kernel(*refs, n_head):
    n_x = 1 if n_head is None else 2
    (mod_ref, g_ref, w_ref, naq_w, nak_w, swq_w, swk_w, cos_ref, sin_ref, ones_ref, lb_ref,
     hg_ref, naq_ref, nak_ref, nav_ref, swq_ref, swk_ref, swv_ref) = refs[n_x:]
    h = _mod_rmsnorm(_stream_block(refs[:n_x], n_head), g_ref[...], mod_ref[...], 0, 1).astype(BF16)
    ones = ones_ref[...]

    def proj(lo, width):
        return _dot(h, w_ref[:, lo:lo + width])

    naq_ref[...] = (_head_rmsnorm(proj(0, NA_W), naq_w[...], ones) * ATTN_SCALE).astype(BF16)
    nak_ref[...] = _head_rmsnorm(proj(NA_W, NA_W), nak_w[...], ones).astype(BF16)
    nav_ref[...] = proj(2 * NA_W, NA_W).astype(BF16)
    base = 3 * NA_W
    lb = lb_ref[...]
    hg_ref[:, :HG_W] = _silu(proj(base, HG_W)) * (HG_DK ** -0.5)
    for j in (1, 2):
        hg_ref[:, j * HG_W:(j + 1) * HG_W] = jnp.log2(lb + (1.0 - lb) * _sigmoid(proj(base + j * HG_W, HG_W)))
    for j in (3, 4):
        hg_ref[:, j * HG_W:(j + 1) * HG_W] = proj(base + j * HG_W, HG_W)
    base += HG_COLS
    cos, sin = cos_ref[...], sin_ref[...]
    swq = _rope(_head_rmsnorm(proj(base, SWA_W), swq_w[...], ones), cos, sin)
    swq_ref[...] = (swq * ATTN_SCALE).astype(BF16)
    swk_ref[...] = _rope(_head_rmsnorm(proj(base + SWA_W, SWA_W), swk_w[...], ones), cos, sin).astype(BF16)
    swv_ref[...] = proj(base + 2 * SWA_W, SWA_W).astype(BF16)


def _mod_row(i, n_ctx_blocks, blocks_per_batch, n_batch):
    return jnp.where(i < n_ctx_blocks, n_batch, (i - n_ctx_blocks) // blocks_per_batch)


def _inproj(x_stream, mods, norm_g, w_ext, naq_w, nak_w, swq_w, swk_w, cos_t, sin_t, ones, lb, geom):
    n_batch, n_ctx_blocks, blocks_per_batch = geom
    tm = ROW_BLOCK
    t = (n_ctx_blocks + n_batch * blocks_per_batch) * tm
    row = lambda w: pl.BlockSpec((tm, w), lambda i: (i, 0))
    const = lambda a: pl.BlockSpec(a.shape, lambda i: (0,) * a.ndim)
    rope_blk = lambda i: (jnp.where(i < n_ctx_blocks, 0, 1 + (i - n_ctx_blocks) % blocks_per_batch), 0)
    bf = lambda w: jax.ShapeDtypeStruct((t, w), BF16)
    x_specs, x_arrays = _stream_specs(x_stream, D_MODEL)
    return pl.pallas_call(
        functools.partial(_inproj_kernel, n_head=_n_head(x_stream)),
        grid=(t // tm,),
        in_specs=x_specs + [
            pl.BlockSpec((None, 6, D_MODEL), lambda i: (_mod_row(i, n_ctx_blocks, blocks_per_batch, n_batch), 0, 0)),
            const(norm_g), const(w_ext), const(naq_w), const(nak_w), const(swq_w), const(swk_w),
            pl.BlockSpec((tm, SWA_W), rope_blk), pl.BlockSpec((tm, SWA_W), rope_blk), const(ones), const(lb)],
        out_specs=[row(HG_COLS), row(NA_W), row(NA_W), row(NA_W), row(SWA_W), row(SWA_W), row(SWA_W)],
        out_shape=[jax.ShapeDtypeStruct((t, HG_COLS), F32), bf(NA_W), bf(NA_W), bf(NA_W),
                   bf(SWA_W), bf(SWA_W), bf(SWA_W)],
        compiler_params=_cparams("parallel"),
        name="inproj",
    )(*x_arrays, mods, norm_g, w_ext, naq_w, nak_w, swq_w, swk_w, cos_t, sin_t, ones, lb)


def _attend(q, parts, sink_vals, heads_per_pass):
    m_rows, width = q.shape
    n_heads = width // HEAD_DIM
    lane = lax.broadcasted_iota(jnp.int32, (m_rows, width), 1)
    out = jnp.zeros((m_rows, width), F32)
    for h0 in range(0, n_heads, heads_per_pass):
        heads = range(h0, h0 + heads_per_pass)
        rows = slice(h0 * m_rows, (h0 + heads_per_pass) * m_rows)
        in_head = [(lane // HEAD_DIM) == h for h in heads]
        q_stack = jnp.concatenate([jnp.where(m, q, jnp.zeros_like(q)) for m in in_head], axis=0)
        scores = []
        for k, _, add in parts:
            s = _dot_nt(q_stack, k)
            scores.append(s if add is None else s + add[rows])
        mx = functools.reduce(jnp.maximum, [jnp.max(s, axis=-1, keepdims=True) for s in scores])
        if sink_vals is not None:
            head_row = lax.broadcasted_iota(jnp.int32, (heads_per_pass * m_rows, 1), 0) // m_rows + h0
            sink = functools.reduce(jnp.add, [jnp.where(head_row == h, sink_vals[h], 0.0) for h in heads])
            mx = jnp.maximum(mx, sink)
        ps = [jnp.exp(s - mx) for s in scores]
        den = functools.reduce(jnp.add, [jnp.sum(p, axis=-1, keepdims=True) for p in ps])
        if sink_vals is not None:
            den = den + jnp.exp(sink - mx)
        acc = functools.reduce(jnp.add, [_dot(p.astype(BF16), v) for p, (_, v, _) in zip(ps, parts)]) / den
        for i, m in enumerate(in_head):
            out = jnp.where(m, acc[i * m_rows:(i + 1) * m_rows], out)
    return out


def _na_kernel(q_ref, k0, k1, k2, kc, v0, v1, v2, vc, bias_ref, o_ref):
    k_lat = jnp.concatenate([k0[...], k1[...], k2[...]], axis=0)
    v_lat = jnp.concatenate([v0[...], v1[...], v2[...]], axis=0)
    bias = bias_ref[...]
    bias = bias.reshape(bias.shape[0] * bias.shape[1], bias.shape[2])
    o = _attend(q_ref[...], [(k_lat, v_lat, bias), (kc[...], vc[...], None)], None, NA_HEADS_PER_PASS)
    o_ref[...] = o.astype(o_ref.dtype)


def _na_attention(naq, nak, nav, bias, geom):
    n_batch, n_ctx_blocks, blocks_per_batch = geom
    tm = ROW_BLOCK
    n_rb = blocks_per_batch
    kb_max = n_rb - NA_KROWS // NA_QROWS
    lat = lambda rb, b: n_ctx_blocks + b * blocks_per_batch

    def band(j):
        return pl.BlockSpec((tm, NA_W), lambda rb, b: (lat(rb, b) + jnp.clip(rb - 1, 0, kb_max) + j, 0))

    ctx = pl.BlockSpec((tm, NA_W), lambda rb, b: (b, 0))
    return pl.pallas_call(
        _na_kernel,
        grid=(n_rb, n_batch),
        in_specs=[pl.BlockSpec((tm, NA_W), lambda rb, b: (lat(rb, b) + rb, 0)),
                  band(0), band(1), band(2), ctx, band(0), band(1), band(2), ctx,
                  pl.BlockSpec((None, NA_HEADS, tm, NA_KROWS * GRID_W), lambda rb, b: (
                      jnp.where(rb == 0, 0, jnp.where(rb == n_rb - 1, 2, 1)), 0, 0, 0))],
        out_specs=pl.BlockSpec((tm, NA_W), lambda rb, b: (b * blocks_per_batch + rb, 0)),
        out_shape=jax.ShapeDtypeStruct((n_batch * blocks_per_batch * tm, NA_W), BF16),
        compiler_params=_cparams("parallel", "parallel"),
        name="na_attn",
    )(naq, nak, nak, nak, nak, nav, nav, nav, nav, bias)


def _swa_kernel(sink_ref, q_ref, kp, kc_, kn, kx, vp, vc_, vn, vx, mask_ref, o_ref):
    k_lat = jnp.concatenate([kp[...], kc_[...], kn[...]], axis=0)
    v_lat = jnp.concatenate([vp[...], vc_[...], vn[...]], axis=0)
    sinks = [sink_ref[h] for h in range(SWA_Q_HEADS)]
    o = _attend(q_ref[...], [(k_lat, v_lat, mask_ref[...]), (kx[...], vx[...], None)], sinks, SWA_HEADS_PER_PASS)
    o_ref[...] = o.astype(o_ref.dtype)


def _swa_mask_table(tq, n_blocks):
    qi = np.arange(SWA_Q_HEADS * tq)[:, None] % tq
    mi = np.arange(3 * tq)[None, :]
    variants = []
    for n in (0, 1, n_blocks - 1):
        kpos = (n - 1) * tq + mi
        ok = (np.abs(qi + tq - mi) <= SWA_WINDOW) & (kpos >= 0) & (kpos < n_blocks * tq)
        variants.append(np.where(ok, 0.0, NEG_INF).astype(np.float32))
    return jnp.asarray(np.stack(variants, axis=0))


def _swa_attention(swq, swk, swv, sink, geom, seq, ctx_len):
    n_batch, _, _ = geom
    tq = SWA_BLOCK
    nb = seq // tq
    assert nb >= 3
    first = n_batch * ctx_len // tq
    blk = lambda f: pl.BlockSpec((tq, SWA_W), lambda b, n: (first + b * nb + f(n), 0))
    prev, cur, nxt = blk(lambda n: jnp.maximum(n - 1, 0)), blk(lambda n: n), blk(lambda n: jnp.minimum(n + 1, nb - 1))
    ctx = pl.BlockSpec((ctx_len, SWA_W), lambda b, n: (b, 0))
    mask = pl.BlockSpec((None, SWA_Q_HEADS * tq, 3 * tq),
                        lambda b, n: (jnp.where(n == 0, 0, jnp.where(n == nb - 1, 2, 1)), 0, 0))
    return pl.pallas_call(
        _swa_kernel,
        grid=(n_batch, nb),
        in_specs=[pl.BlockSpec(memory_space=pltpu.SMEM), cur, prev, cur, nxt, ctx, prev, cur, nxt, ctx, mask],
        out_specs=pl.BlockSpec((tq, SWA_W), lambda b, n: (b * nb + n, 0)),
        out_shape=jax.ShapeDtypeStruct((n_batch * seq, SWA_W), BF16),
        compiler_params=_cparams("parallel", "parallel"),
        name="swa_attn",
    )(sink, swq, swk, swk, swk, swk, swv, swv, swv, swv, _swa_mask_table(tq, nb))


def _ctx_attn_kernel(sink_ref, naq, nak, nav, swq, swk, swv, ona_ref, osw_ref):
    ona_ref[...] = _attend(naq[...], [(nak[...], nav[...], None)], None, NA_HEADS_PER_PASS).astype(ona_ref.dtype)
    sinks = [sink_ref[h] for h in range(SWA_Q_HEADS)]
    osw_ref[...] = _attend(swq[...], [(swk[...], swv[...], None)], sinks, NA_HEADS_PER_PASS).astype(osw_ref.dtype)


def _ctx_attention(naq, nak, nav, swq, swk, swv, sink, n_batch, ctx_len):
    blk = pl.BlockSpec((ctx_len, NA_W), lambda b: (b, 0))
    out = jax.ShapeDtypeStruct((n_batch * ctx_len, NA_W), BF16)
    return pl.pallas_call(
        _ctx_attn_kernel,
        grid=(n_batch,),
        in_specs=[pl.BlockSpec(memory_space=pltpu.SMEM)] + [blk] * 6,
        out_specs=[blk, blk],
        out_shape=[out, out],
        compiler_params=_cparams("parallel"),
        name="ctx_attn",
    )(sink, naq, nak, nav, swq, swk, swv)


def _dot_exact_lhs(m, x):
    out = _dot(m, jnp.concatenate(_split_bf16(x, 3), axis=1))
    n = x.shape[1]
    return out[:, :n] + out[:, n:2 * n] + out[:, 2 * n:]


def _hgrn_sum_table(n_rows, reverse):
    tri = np.tril(np.ones((n_rows, n_rows), np.float32))
    return tri.T if reverse else tri


def _tile_row(x, r):
    tiles = x.reshape(x.shape[0] // SUBLANES, SUBLANES, x.shape[1])
    return jnp.broadcast_to(tiles[:, r:r + 1, :], tiles.shape).reshape(x.shape)


def _hgrn_masks(n_rows):
    row = lax.broadcasted_iota(jnp.int32, (n_rows, LANES), 0)
    row_a = lax.broadcasted_iota(jnp.int32, (n_rows, n_rows), 0)
    col_a = lax.broadcasted_iota(jnp.int32, (n_rows, n_rows), 1)
    sizes = [1 << b for b in range(1, n_rows.bit_length())]
    return dict(
        diag=row_a == col_a,
        same={s: row_a // s == col_a // s for s in sizes if s < n_rows},
        upper_half={s: row % s >= s // 2 for s in sizes if s < 2 * SUBLANES},
        from_row={r: row % SUBLANES >= r for r in range(2, SUBLANES, 2)})


def _hgrn_block(q, k, g, v, st, sums, ones, masks, reverse):
    n_rows = q.shape[0]
    cum = _dot_exact_lhs(sums, g)
    tot = cum[0:1] if reverse else cum[n_rows - 1:n_rows]
    o_inter = _dot_nt((q * jnp.exp2(cum)).astype(BF16), st.astype(BF16))
    k_end = (k * jnp.exp2(tot - cum)).astype(BF16)
    st_new = st * jnp.exp2(tot) + _dot_tn(v.astype(BF16), k_end)

    a = None
    zero_tile = jnp.zeros((SUBLANES, LANES), F32)
    size = n_rows
    while size >= 2:
        half = size // 2
        if size < 2 * SUBLANES:
            ref_off = half if reverse else half - 1
            ref = _tile_row(cum, ref_off)
            for first in range(size, SUBLANES, size):
                ref = jnp.where(masks["from_row"][first], _tile_row(cum, first + ref_off), ref)
            upper = masks["upper_half"][size]
            d = cum - ref
            q_exp = jnp.where(upper, NEG_INF, d) if reverse else jnp.where(upper, d, NEG_INF)
            k_exp = jnp.where(upper, -d, NEG_INF) if reverse else jnp.where(upper, NEG_INF, -d)
            q_t = (q * jnp.exp2(q_exp)).astype(BF16)
            k_t = (k * jnp.exp2(k_exp)).astype(BF16)
        else:
            q_tiles, k_tiles = [], []
            for r0 in range(0, n_rows, SUBLANES):
                sl = slice(r0, r0 + SUBLANES)
                first = r0 // size * size
                ref_row = first + (half if reverse else half - 1)
                ref = cum[ref_row:ref_row + 1]
                if (r0 - first < half) if reverse else (r0 - first >= half):
                    q_tiles.append(q[sl] * jnp.exp2(cum[sl] - ref))
                    k_tiles.append(zero_tile)
                else:
                    q_tiles.append(zero_tile)
                    k_tiles.append(k[sl] * jnp.exp2(ref - cum[sl]))
            q_t = jnp.concatenate(q_tiles, axis=0).astype(BF16)
            k_t = jnp.concatenate(k_tiles, axis=0).astype(BF16)
        a_l = _dot_nt(q_t, k_t)
        a = a_l if a is None else jnp.where(masks["same"][size], a_l, a)
        size //= 2
    a = jnp.where(masks["diag"], _dot((q * k).astype(BF16), ones), a)
    return o_inter + _dot(a.astype(BF16), v.astype(BF16)), st_new


def _hgrn_kernel(q_f, g_f, v_f, q_b, g_b, v_b, trif_ref, trib_ref, ones_ref, of_ref, ob_ref, st_f, st_b):
    @pl.when(pl.program_id(2) == 0)
    def _():
        st_f[...] = jnp.zeros_like(st_f)
        st_b[...] = jnp.zeros_like(st_b)

    ones = ones_ref[...]
    masks = _hgrn_masks(q_f.shape[0])
    for h in range(HG_HEADS_PER_STEP):
        sl = slice(h * HG_DK, (h + 1) * HG_DK)
        g = g_f[:, sl]
        of_ref[:, sl], st_f[h] = _hgrn_block(q_f[:, sl], 1.0 - jnp.exp2(g), g, v_f[:, sl], st_f[h],
                                             trif_ref[...], ones, masks, False)
        g = g_b[:, sl]
        ob_ref[:, sl], st_b[h] = _hgrn_block(q_b[:, sl], 1.0 - jnp.exp2(g), g, v_b[:, sl], st_b[h],
                                             trib_ref[...], ones, masks, True)


def _hgrn(hg, tri_f, tri_b, ones, n_batch, seq, ctx_len):
    tr = HG_STEP_ROWS
    nc, nl = ctx_len // tr, seq // tr
    first = n_batch * nc

    def fwd(b, j):
        return jnp.where(j < nc, b * nc + j, first + b * nl + (j - nc))

    def bwd(b, j):
        return jnp.where(j < nc, b * nc + (nc - 1 - j), first + b * nl + (nl - 1 - (j - nc)))

    hps = HG_HEADS_PER_STEP
    groups = HG_HEADS // hps
    wide = hps * HG_DK
    col = lambda row_fn, group: pl.BlockSpec((tr, wide), lambda b, h, j: (row_fn(b, j), group * groups + h))
    const = lambda a: pl.BlockSpec(a.shape, lambda b, h, j: (0, 0))
    out = jax.ShapeDtypeStruct((hg.shape[0], HG_W), F32)
    return pl.pallas_call(
        _hgrn_kernel,
        grid=(n_batch, groups, nc + nl),
        in_specs=[col(fwd, 0), col(fwd, 1), col(fwd, 3), col(bwd, 0), col(bwd, 2), col(bwd, 3),
                  const(tri_f), const(tri_b), const(ones)],
        out_specs=[pl.BlockSpec((tr, wide), lambda b, h, j: (fwd(b, j), h)),
                   pl.BlockSpec((tr, wide), lambda b, h, j: (bwd(b, j), h))],
        out_shape=[out, out],
        scratch_shapes=[pltpu.VMEM((hps, HG_DK, HG_DK), F32), pltpu.VMEM((hps, HG_DK, HG_DK), F32)],
        compiler_params=_cparams("parallel", "parallel", "arbitrary"),
        name="hgrn",
    )(hg, hg, hg, hg, hg, hg, tri_f, tri_b, ones)


def _mixout_kernel(*refs, n_heads, stream_off):
    counts = [1 if n is None else 2 for n in n_heads]
    streams, pos = [], 0
    for n, c in zip(n_heads, counts):
        streams.append((refs[pos:pos + c], n))
        pos += c
    (of_ref, ob_ref, zg_ref, hgn_ref, w_ref, mod_ref, g2_ref, rw_ref, rb_ref,
     xo_ref, h2_ref, te_ref, tg_ref) = refs[pos:]
    x_in = _stream_block(*streams[0], first_block=stream_off)
    o = of_ref[...] + ob_ref[...]
    zg = zg_ref[...]
    parts = [_stream_block(*streams[1])]
    for h in range(HG_HEADS):
        sl = slice(h * HG_DK, (h + 1) * HG_DK)
        oh = o[:, sl]
        yh = oh * lax.rsqrt(jnp.mean(oh * oh, axis=-1, keepdims=True) + NORM_EPS) * hgn_ref[...]
        parts.append((yh * _silu(zg[:, sl])).astype(BF16))
    parts.append(_stream_block(*streams[2]))
    y = _dot(jnp.concatenate(parts, axis=1), w_ref[...])
    mod = mod_ref[...]
    x_new = x_in + mod[2:3] * y
    xo_ref[...] = x_new
    h2 = _mod_rmsnorm(x_new, g2_ref[...], mod, 3, 4)
    h2_ref[...] = h2.astype(BF16)

    h_hi, h_lo = _split_bf16(h2, 2)
    logits = _dot(jnp.concatenate([h_hi, h_lo, h_hi], axis=1), rw_ref[...]) + rb_ref[...]
    lane = lax.broadcasted_iota(jnp.int32, logits.shape, 1).astype(F32)
    top_e = jnp.zeros(logits.shape, F32)
    top_v = jnp.full(logits.shape, NEG_INF, F32)
    for j in range(TOP_K):
        best = jnp.max(logits, axis=-1, keepdims=True)
        arg = jnp.min(jnp.where(logits == best, lane, float(LANES)), axis=-1, keepdims=True)
        top_e = jnp.where(lane == j, arg, top_e)
        top_v = jnp.where(lane == j, best, top_v)
        logits = jnp.where(lane == arg, -jnp.inf, logits)
    ex = jnp.exp(top_v - jnp.max(top_v, axis=-1, keepdims=True))
    te_ref[...] = top_e.astype(jnp.int32)
    tg_ref[...] = ex / jnp.sum(ex, axis=-1, keepdims=True)


def _mixout(x_stream, yna_stream, o_f, o_b, hg, ysw_stream, hgn, w_out, mods, norm2_g, rw, rb, geom, stream_off):
    n_batch, n_ctx_blocks, blocks_per_batch = geom
    tm = ROW_BLOCK
    t = (n_ctx_blocks + n_batch * blocks_per_batch - stream_off) * tm
    row = lambda w: pl.BlockSpec((tm, w), lambda i: (i, 0))
    full = lambda w, cb: pl.BlockSpec((tm, w), lambda i: (i + stream_off, cb))
    const = lambda a: pl.BlockSpec(a.shape, lambda i: (0,) * a.ndim)
    x_specs, x_arrays = _stream_specs(x_stream, D_MODEL, stream_off)
    yna_specs, yna_arrays = _stream_specs(yna_stream, NA_W)
    ysw_specs, ysw_arrays = _stream_specs(ysw_stream, SWA_W)
    n_heads = (_n_head(x_stream), _n_head(yna_stream), _n_head(ysw_stream))
    return pl.pallas_call(
        functools.partial(_mixout_kernel, n_heads=n_heads, stream_off=stream_off),
        grid=(t // tm,),
        in_specs=x_specs + yna_specs + ysw_specs + [
            full(HG_W, 0), full(HG_W, 0), full(HG_W, 4), const(hgn), const(w_out),
            pl.BlockSpec((None, 6, D_MODEL), lambda i: (
                _mod_row(i + stream_off, n_ctx_blocks, blocks_per_batch, n_batch), 0, 0)),
            const(norm2_g), const(rw), const(rb)],
        out_specs=[row(D_MODEL), row(D_MODEL), row(LANES), row(LANES)],
        out_shape=[jax.ShapeDtypeStruct((t, D_MODEL), F32), jax.ShapeDtypeStruct((t, D_MODEL), BF16),
                   jax.ShapeDtypeStruct((t, LANES), jnp.int32), jax.ShapeDtypeStruct((t, LANES), F32)],
        compiler_params=_cparams("parallel"),
        name="mixout",
    )(*x_arrays, *yna_arrays, *ysw_arrays, o_f, o_b, hg, hgn, w_out, mods, norm2_g, rw, rb)


def _start_run_copies(run_ref, n_experts, make_copy):
    for e in range(n_experts):
        n = run_ref[0, 0, e]
        local0 = run_ref[0, 0, n_experts + e]
        slot0 = run_ref[0, 0, 2 * n_experts + e]
        for bit in reversed(range(RUN_BITS)):
            @pl.when((n & (1 << bit)) != 0)
            def _():
                done = (n >> (bit + 1)) << (bit + 1)
                make_copy(local0 + done, slot0 + done, 1 << bit).start(priority=e % 2)


def _tile_rows(first_token, n_tokens):
    return pl.ds(pl.multiple_of(first_token * SUBLANES, SUBLANES), n_tokens * SUBLANES)


def _local_positions(te_ref, off_ref, stri_ref):
    e_tok = te_ref[...]
    lane = lax.broadcasted_iota(jnp.int32, e_tok.shape, 1)
    member = [lane == e_tok[:, j:j + 1] for j in range(TOP_K)]
    count = functools.reduce(jnp.add, [jnp.where(m, 1.0, 0.0) for m in member])
    before = _dot(stri_ref[...], count.astype(BF16)) + off_ref[...]
    return [jnp.sum(jnp.where(m, before, 0.0), axis=1, keepdims=True) for m in member]


def _dispatch_kernel(pad_ref, nu_ref, run_ref, te_ref, off_ref, h_ref, stri_ref, xs_hbm, pos_ref,
                     zeros, xloc0, xloc1, sem, zero_sem, *, n_tok_blocks):
    n_tok = h_ref.shape[0]
    block_rows = zeros.shape[0]
    n_blocks = xs_hbm.shape[0] // block_rows

    def zero_slot(e, r):
        row = pl.multiple_of((pad_ref[0, e] + r) * SUBLANES, SUBLANES)
        return pltpu.make_async_copy(zeros.at[pl.ds(0, SUBLANES), :], xs_hbm.at[pl.ds(row, SUBLANES), :], zero_sem)

    def zero_block(b):
        row = pl.multiple_of(b * block_rows, block_rows)
        return pltpu.make_async_copy(zeros, xs_hbm.at[pl.ds(row, block_rows), :], zero_sem)

    def for_each_unused(slot_fn, block_fn):
        def per_expert(e, carry):
            lax.fori_loop(0, pad_ref[1, e], lambda r, c: (slot_fn(e, r), c)[1], 0)
            return carry

        lax.fori_loop(0, pad_ref.shape[1], per_expert, 0)
        lax.fori_loop(nu_ref[0], n_blocks, lambda b, c: (block_fn(b), c)[1], 0)

    @pl.when(pl.program_id(0) == 0)
    def _():
        zeros[...] = jnp.zeros_like(zeros)
        for_each_unused(lambda e, r: zero_slot(e, r).start(), lambda b: zero_block(b).start())
        for_each_unused(lambda e, r: zero_slot(e, r).wait(), lambda b: zero_block(b).wait())

    bufs = (xloc0, xloc1)

    def sort_into(b):
        pos = _local_positions(te_ref, off_ref, stri_ref)
        lane = lax.broadcasted_iota(jnp.int32, (n_tok, LANES), 1)
        pos_ref[...] = functools.reduce(jnp.add, [jnp.where(lane == j, p, 0.0) for j, p in enumerate(pos)])
        col = lax.broadcasted_iota(jnp.int32, (n_tok, TOP_K * n_tok), 1).astype(F32)
        chosen = functools.reduce(jnp.logical_or, [col == p for p in pos])
        sorted_rows = _dot_tn(jnp.where(chosen, 1.0, 0.0).astype(BF16), h_ref[...])
        _store_token_tiles(bufs[b], sorted_rows)

    def start_copies(b):
        _start_run_copies(run_ref, pad_ref.shape[1], lambda local, slot, size: pltpu.make_async_copy(
            bufs[b].at[_tile_rows(local, size), :], xs_hbm.at[_tile_rows(slot, size), :], sem.at[b]))

    def wait_copies(b):
        pltpu.make_async_copy(bufs[b], xs_hbm.at[pl.ds(0, bufs[b].shape[0]), :], sem.at[b]).wait()

    i = pl.program_id(0)

    @pl.when(i == 0)
    def _():
        sort_into(0)

    for parity in (0, 1):
        @pl.when((i % 2 == parity) & (i >= 1) & (i < n_tok_blocks))
        def _():
            @pl.when(i >= 2)
            def _():
                wait_copies(parity)

            start_copies(1 - parity)
            sort_into(parity)

    @pl.when(i == n_tok_blocks)
    def _():
        if n_tok_blocks >= 2:
            wait_copies(n_tok_blocks % 2)
        start_copies((n_tok_blocks - 1) % 2)
        wait_copies((n_tok_blocks - 1) % 2)


def _dispatch(h, top_e, runs, run_first, pad_slots, n_used, n_slots):
    tm = ROW_BLOCK
    t = h.shape[0]
    nb = t // tm
    assert n_slots % MOE_ROWS == 0
    stri = jnp.asarray(np.tril(np.ones((tm, tm), np.float32), -1), BF16)
    cur = lambda i: jnp.minimum(i, nb - 1)
    row = lambda w: pl.BlockSpec((tm, w), lambda i: (cur(i), 0))
    sort_buffer = pltpu.VMEM((TOP_K * tm * SUBLANES, LANES), F32)
    return pl.pallas_call(
        functools.partial(_dispatch_kernel, n_tok_blocks=nb),
        grid=(nb + 1,),
        in_specs=[pl.BlockSpec(memory_space=pltpu.SMEM), pl.BlockSpec(memory_space=pltpu.SMEM),
                  pl.BlockSpec((1, 1, runs.shape[2]), lambda i: (jnp.maximum(i - 1, 0), 0, 0),
                               memory_space=pltpu.SMEM),
                  row(LANES), pl.BlockSpec((None, 1, LANES), lambda i: (cur(i), 0, 0)), row(D_MODEL),
                  pl.BlockSpec((tm, tm), lambda i: (0, 0))],
        out_specs=[pl.BlockSpec(memory_space=pl.ANY), row(LANES)],
        out_shape=[jax.ShapeDtypeStruct((n_slots * SUBLANES, LANES), F32), jax.ShapeDtypeStruct((t, LANES), F32)],
        scratch_shapes=[pltpu.VMEM((MOE_ROWS * SUBLANES, LANES), F32), sort_buffer, sort_buffer,
                        pltpu.SemaphoreType.DMA((2,)), pltpu.SemaphoreType.DMA(())],
        compiler_params=_cparams("arbitrary"),
        name="moe_dispatch",
    )(pad_slots, n_used, runs, top_e, run_first, h, stri)


def _moe_kernel(be_ref, nu_ref, next_ref, run_ref, xs_ref, wgu_hbm, bgu_ref, wd_hbm, bd_ref, y_ref,
                wgu_f32, wd_f32, wgu_bf, wd_bf, sem, *, layer):
    i = pl.program_id(0)
    live = i < nu_ref[0]
    tm = y_ref.shape[0] // SUBLANES
    changed = (i == 0) | (be_ref[i] != be_ref[jnp.maximum(i - 1, 0)])

    def weight_copies(expert, buf):
        return (pltpu.make_async_copy(wgu_hbm.at[layer, expert], wgu_f32.at[buf], sem.at[buf]),
                pltpu.make_async_copy(wd_hbm.at[layer, expert], wd_f32.at[buf], sem.at[buf]))

    @pl.when(live & changed)
    def _():
        buf = run_ref[i] % 2

        @pl.when(i == 0)
        def _():
            for copy in weight_copies(be_ref[0], 0):
                copy.start()

        for copy in weight_copies(be_ref[i], buf):
            copy.wait()
        wgu_bf[...] = wgu_f32[buf].astype(BF16)
        wd_bf[...] = wd_f32[buf].astype(BF16)

        @pl.when(next_ref[i] >= 0)
        def _():
            for copy in weight_copies(next_ref[i], 1 - buf):
                copy.start()

    @pl.when(live)
    def _():
        x = _load_token_tiles(xs_ref, 0, tm)
        gu = _dot(x.astype(BF16), wgu_bf[...]) + bgu_ref[...]
        glu = jnp.minimum(gu[:, :D_FF], SWIGLU_LIMIT)
        lin = jnp.clip(gu[:, D_FF:], -SWIGLU_LIMIT, SWIGLU_LIMIT)
        act = glu * _sigmoid(SWIGLU_ALPHA * glu) * (lin + 1.0)
        _store_token_tiles(y_ref, _dot(act.astype(BF16), wd_bf[...]) + bd_ref[...])

    @pl.when(jnp.logical_not(live))
    def _():
        y_ref[...] = jnp.zeros_like(y_ref)


def _moe_ffn(xs_tiles, block_e, n_used, w_gu, b_gu, w_down, b_down, layer):
    tm = MOE_ROWS
    nb = xs_tiles.shape[0] // (tm * SUBLANES)
    n_l, n_e, d, f2 = w_gu.shape
    idx = jnp.arange(nb, dtype=jnp.int32)
    starts = jnp.concatenate([jnp.ones((1,), bool), block_e[1:] != block_e[:-1]]) & (idx < n_used[0])
    run_idx = jnp.cumsum(starts.astype(jnp.int32)) - 1
    run_expert = jnp.full((nb + 1,), -1, jnp.int32).at[jnp.where(starts, run_idx, nb)].set(block_e)[:nb]
    next_e = jnp.concatenate([run_expert[1:], jnp.full((1,), -1, jnp.int32)])[run_idx]
    bias = lambda c: pl.BlockSpec((None, None, 1, c), lambda i, be, nu, nx, ru: (layer, be[i], 0, 0))
    grid_spec = pltpu.PrefetchScalarGridSpec(
        num_scalar_prefetch=4,
        grid=(nb,),
        in_specs=[pl.BlockSpec((tm * SUBLANES, LANES), lambda i, be, nu, nx, ru: (jnp.minimum(i, nu[0] - 1), 0)),
                  pl.BlockSpec(memory_space=pl.ANY), bias(f2), pl.BlockSpec(memory_space=pl.ANY), bias(d)],
        out_specs=pl.BlockSpec((tm * SUBLANES, LANES), lambda i, be, nu, nx, ru: (i, 0)),
        scratch_shapes=[pltpu.VMEM((2, d, f2), F32), pltpu.VMEM((2, f2 // 2, d), F32),
                        pltpu.VMEM((d, f2), BF16), pltpu.VMEM((f2 // 2, d), BF16), pltpu.SemaphoreType.DMA((2,))],
    )
    return pl.pallas_call(
        functools.partial(_moe_kernel, layer=layer),
        grid_spec=grid_spec,
        out_shape=jax.ShapeDtypeStruct((nb * tm * SUBLANES, LANES), F32),
        compiler_params=_cparams("arbitrary"),
        name="moe_ffn",
    )(block_e, n_used, next_e, run_idx, xs_tiles, w_gu, b_gu.reshape(n_l, n_e, 1, f2), w_down,
      b_down.reshape(n_l, n_e, 1, d))


def _combine_kernel(run_ref, pos_ref, tg_ref, x_ref, y_hbm, mod_ref, o_ref, ybuf0, ybuf1, sem, *, n_tok_blocks):
    i = pl.program_id(0)
    tm = o_ref.shape[0]
    n_rows = TOP_K * tm
    bufs = (ybuf0, ybuf1)

    def fetch(b):
        _start_run_copies(run_ref, run_ref.shape[2] // 3, lambda local, slot, size: pltpu.make_async_copy(
            y_hbm.at[_tile_rows(slot, size), :], bufs[b].at[_tile_rows(local, size), :], sem.at[b]))

    def wait_fetch(b):
        pltpu.make_async_copy(y_hbm.at[pl.ds(0, n_rows * SUBLANES), :], bufs[b], sem.at[b]).wait()

    def combine(b):
        y = _load_token_tiles(bufs[b], 0, n_rows).astype(BF16)
        pos, gate = pos_ref[...], tg_ref[...]
        col = lax.broadcasted_iota(jnp.int32, (tm, n_rows), 1).astype(F32)
        g = functools.reduce(jnp.add, [jnp.where(col == pos[:, j:j + 1], gate[:, j:j + 1], 0.0)
                                       for j in range(TOP_K)])
        g_hi, g_lo = _split_bf16(g, 2)
        o_ref[...] = x_ref[...] + mod_ref[...][5:6] * (_dot(g_hi, y) + _dot(g_lo, y))

    @pl.when(i == 0)
    def _():
        fetch(0)

    for parity in (0, 1):
        @pl.when((i % 2 == parity) & (i >= 1) & (i < n_tok_blocks))
        def _():
            wait_fetch(1 - parity)
            fetch(parity)
            combine(1 - parity)

    @pl.when(i == n_tok_blocks)
    def _():
        wait_fetch((n_tok_blocks - 1) % 2)
        combine((n_tok_blocks - 1) % 2)


def _combine(x, y_tiles, runs, pos, gates, mods, geom, stream_off):
    n_batch, n_ctx_blocks, blocks_per_batch = geom
    t = x.shape[0]
    tm = ROW_BLOCK
    nb = t // tm
    prev = lambda i: jnp.maximum(i - 1, 0)
    row = lambda w: pl.BlockSpec((tm, w), lambda i: (prev(i), 0))
    sorted_buffer = pltpu.VMEM((TOP_K * tm * SUBLANES, LANES), F32)
    return pl.pallas_call(
        functools.partial(_combine_kernel, n_tok_blocks=nb),
        grid=(nb + 1,),
        in_specs=[pl.BlockSpec((1, 1, runs.shape[2]), lambda i: (jnp.minimum(i, nb - 1), 0, 0),
                               memory_space=pltpu.SMEM),
                  row(LANES), row(LANES), row(D_MODEL), pl.BlockSpec(memory_space=pl.ANY),
                  pl.BlockSpec((None, 6, D_MODEL), lambda i: (
                      _mod_row(prev(i) + stream_off, n_ctx_blocks, blocks_per_batch, n_batch), 0, 0))],
        out_specs=row(D_MODEL),
        out_shape=jax.ShapeDtypeStruct((t, D_MODEL), F32),
        scratch_shapes=[sorted_buffer, sorted_buffer, pltpu.SemaphoreType.DMA((2,))],
        compiler_params=_cparams("arbitrary"),
        name="moe_combine",
    )(runs, pos, gates, x, y_tiles, mods)


def _route(top_e, n_experts, tok_block, slot_block):
    t, k = top_e.shape
    nb = t // tok_block
    onehot = (top_e[:, :, None] == jnp.arange(n_experts, dtype=jnp.int32)).astype(jnp.int32)
    n = onehot.reshape(nb, tok_block * k, n_experts).sum(axis=1)
    counts = n.sum(axis=0)
    padded = (counts + slot_block - 1) // slot_block * slot_block
    pend = jnp.cumsum(padded)
    start = pend - padded
    local0 = jnp.cumsum(n, axis=1) - n
    slot0 = start[None, :] + jnp.cumsum(n, axis=0) - n
    runs = jnp.concatenate([n, local0, slot0], axis=1).astype(jnp.int32).reshape(nb, 1, 3 * n_experts)
    run_first = jnp.pad(local0.astype(F32), ((0, 0), (0, LANES - n_experts))).reshape(nb, 1, LANES)
    n_blocks = (t * k + n_experts * (slot_block - 1) + slot_block - 1) // slot_block
    block_start = jnp.arange(n_blocks, dtype=jnp.int32) * slot_block
    block_e = jnp.minimum((block_start[:, None] >= pend[None, :]).sum(axis=1), n_experts - 1).astype(jnp.int32)
    n_used = (pend[-1] // slot_block).astype(jnp.int32).reshape(1)
    pad_slots = jnp.stack([start + counts, padded - counts], axis=0).astype(jnp.int32)
    return runs, run_first, block_e, n_used, pad_slots, n_blocks * slot_block


def _na_row_pattern(rb, n_rows):
    wr = min(NA_WIN_H, n_rows)
    n_rb = n_rows // NA_QROWS
    q_row = rb * NA_QROWS + np.arange(NA_QROWS)[:, None]
    k_row = np.clip(rb - 1, 0, n_rb - NA_KROWS // NA_QROWS) * NA_QROWS + np.arange(NA_KROWS)[None, :]
    row_start = np.clip(q_row - wr // 2, 0, n_rows - wr)
    ok = (k_row >= row_start) & (k_row < row_start + wr)
    return np.where(ok, k_row - q_row + (NA_WIN_H - 1), -1)


def _na_bias_table(rpb, seq):
    n_rows = seq // GRID_W
    n_rb = n_rows // NA_QROWS
    patterns = [_na_row_pattern(rb, n_rows) for rb in range(n_rb)]
    assert all((p == patterns[1]).all() for p in patterns[1:-1])
    q_col = np.arange(GRID_W)[:, None]
    k_col = np.arange(GRID_W)[None, :]
    col_start = np.clip(q_col - NA_WIN_W // 2, 0, GRID_W - NA_WIN_W)
    col_ok = (k_col >= col_start) & (k_col < col_start + NA_WIN_W)
    dc = np.clip(k_col - q_col + (NA_WIN_W - 1), 0, 2 * NA_WIN_W - 2)
    onehot = ((dc[None] == np.arange(2 * NA_WIN_W - 1)[:, None, None]) & col_ok[None]).astype(np.float32)
    by_col = jnp.einsum('hab,bqk->haqk', rpb.astype(F32), jnp.asarray(onehot), precision=HIGHEST)
    by_col = by_col + jnp.asarray(np.where(col_ok, 0.0, NEG_INF).astype(np.float32))
    masked = jnp.full((rpb.shape[0], GRID_W, GRID_W), NEG_INF, F32)
    variants = []
    for pattern in (patterns[0], patterns[1], patterns[-1]):
        rows = [jnp.concatenate([by_col[:, a] if a >= 0 else masked for a in pattern[qr]], axis=2)
                for qr in range(NA_QROWS)]
        variants.append(jnp.concatenate(rows, axis=1))
    return jnp.stack(variants, axis=0)


def _rope_tables(seq, ctx_len):
    quarter = HEAD_DIM // 4
    lane = np.arange(SWA_W)
    inv = ROPE_BASE ** (-(lane % quarter).astype(np.float64) / quarter)
    t = np.arange(seq)
    pos = np.where((lane % HEAD_DIM < HEAD_DIM // 2)[None, :], (t // GRID_W)[:, None], (t % GRID_W)[:, None])
    ang = jnp.asarray(pos, F32) * jnp.asarray(inv, F32)[None, :]
    sign = np.where(lane % (2 * quarter) < quarter, -1.0, 1.0).astype(np.float32)
    cos = jnp.concatenate([jnp.ones((ctx_len, SWA_W), F32), jnp.cos(ang)], axis=0)
    sin = jnp.concatenate([jnp.zeros((ctx_len, SWA_W), F32), jnp.sin(ang) * sign[None, :]], axis=0)
    return cos, sin


def kernel(x, c, ctx, c_ctx, hg_lower_bounds, ada_w, ada_b, norm1_g, norm2_g, w_in, na_q_norm, na_k_norm, na_rpb,
           hg_norm_g, swa_q_norm, swa_k_norm, swa_sink, w_out, router_w, router_b, w_gu, b_gu, w_down, b_down):
    n_batch, seq, d = x.shape
    ctx_len = ctx.shape[1]
    depth = ada_w.shape[0]
    assert d == D_MODEL and seq % ROW_BLOCK == 0 and ctx_len == ROW_BLOCK
    n_ctx_rows = n_batch * ctx_len
    geom = (n_batch, n_ctx_rows // ROW_BLOCK, seq // ROW_BLOCK)

    p_lb = jax.nn.softmax(hg_lower_bounds.astype(F32), axis=0)
    lbs = jnp.cumsum(p_lb, axis=0) - p_lb[0]

    cos_t, sin_t = _rope_tables(seq, ctx_len)
    lane = np.arange(NA_W)
    group_ones = jnp.asarray((lane[:, None] // HEAD_DIM == lane[None, :] // HEAD_DIM).astype(np.float32), BF16)
    tri_f = jnp.asarray(_hgrn_sum_table(HG_STEP_ROWS, False), BF16)
    tri_b = jnp.asarray(_hgrn_sum_table(HG_STEP_ROWS, True), BF16)
    ones_bf = jnp.ones((LANES, HG_STEP_ROWS), BF16)
    n_mod_rows = -(-(n_batch + 1) // 8) * 8
    cc = jnp.zeros((n_mod_rows, d), F32).at[:n_batch].set(c).at[n_batch].set(c_ctx)
    dup = lambda w: jnp.concatenate([w[:, :HEAD_DIM], w[:, :HEAD_DIM], w[:, HEAD_DIM:], w[:, HEAD_DIM:]], axis=1)
    tile4 = lambda g: jnp.tile(g.astype(F32), 4).reshape(1, 4 * HEAD_DIM)
    pad_e = LANES - N_EXPERTS

    x_stream = (ctx.reshape(n_ctx_rows, d), x.reshape(n_batch * seq, d))
    for l in range(depth):
        last = l == depth - 1
        mods = _ada_mod(cc, ada_w[l], ada_b[l]).reshape(n_mod_rows, 6, d)
        w = w_in[l]
        kv0 = 3 * NA_W + HG_COLS + SWA_W
        w_ext = jnp.concatenate([w[:, :kv0], dup(w[:, kv0:kv0 + SWA_KV_W]), dup(w[:, kv0 + SWA_KV_W:])],
                                axis=1).astype(BF16)
        hg, naq, nak, nav, swq, swk, swv = _inproj(
            x_stream, mods, norm1_g[l].reshape(1, d), w_ext, tile4(na_q_norm[l]), tile4(na_k_norm[l]),
            tile4(swa_q_norm[l]), tile4(swa_k_norm[l]), cos_t, sin_t, group_ones, lbs[l].reshape(1, HG_W), geom)

        y_na = _na_attention(naq, nak, nav, _na_bias_table(na_rpb[l], seq), geom)
        y_sw = _swa_attention(swq, swk, swv, swa_sink[l].astype(F32), geom, seq, ctx_len)
        o_f, o_b = _hgrn(hg, tri_f, tri_b, ones_bf, n_batch, seq, ctx_len)

        if last:
            stream_off = geom[1]
            yna_stream, ysw_stream = (None, y_na), (None, y_sw)
        else:
            stream_off = 0
            yc_na, yc_sw = _ctx_attention(naq, nak, nav, swq, swk, swv, swa_sink[l].astype(F32), n_batch, ctx_len)
            yna_stream, ysw_stream = (yc_na, y_na), (yc_sw, y_sw)

        rw_hi, rw_lo = _split_bf16(jnp.pad(router_w[l].astype(F32), ((0, 0), (0, pad_e))), 2)
        rw = jnp.concatenate([rw_hi, rw_hi, rw_lo], axis=0)
        rb = jnp.pad(router_b[l].astype(F32), (0, pad_e), constant_values=NEG_INF).reshape(1, LANES)
        x_new, h2, top_e, top_g = _mixout(
            x_stream, yna_stream, o_f, o_b, hg, ysw_stream, hg_norm_g[l].reshape(1, HG_DK).astype(F32),
            w_out[l].astype(BF16), mods, norm2_g[l].reshape(1, d), rw, rb, geom, stream_off)

        runs, run_first, block_e, n_used, pad_slots, n_slots = _route(top_e[:, :TOP_K], N_EXPERTS, ROW_BLOCK, MOE_ROWS)
        xs, pos = _dispatch(h2, top_e, runs, run_first, pad_slots, n_used, n_slots)
        y_slots = _moe_ffn(xs, block_e, n_used, w_gu, b_gu, w_down, b_down, l)
        x_stream = (None, _combine(x_new, y_slots, runs, pos, top_g, mods, geom, stream_off))
    return x_stream[1].reshape(n_batch, seq, d)
```

```python
import functools

import numpy as np
import jax
import jax.numpy as jnp
from jax import lax
from jax.experimental import pallas as pl
from jax.experimental.pallas import tpu as pltpu

D_MODEL = 1024
GRID_W = 64
HEAD_DIM = 64
ATTN_SCALE = HEAD_DIM ** -0.5
NA_HEADS = 4
NA_WIN_H = 8
NA_WIN_W = 16
HG_HEADS = 4
HG_DK = 128
SWA_Q_HEADS = 4
SWA_KV_HEADS = 2
SWA_WINDOW = 128
SWA_BLOCK = 128
ROPE_BASE = 10000.0
N_EXPERTS = 32
TOP_K = 4
D_FF = 1024
SWIGLU_LIMIT = 7.0
SWIGLU_ALPHA = 1.702
NORM_EPS = 1e-6
NEG_INF = -1e30

NA_W = NA_HEADS * HEAD_DIM
HG_W = HG_HEADS * HG_DK
SWA_W = SWA_Q_HEADS * HEAD_DIM
SWA_KV_W = SWA_KV_HEADS * HEAD_DIM
HG_COLS = 5 * HG_W

LANES = 128
SUBLANES = 8
ROW_BLOCK = 256
NA_QROWS = 4
NA_KROWS = 12
NA_HEADS_PER_PASS = 1
SWA_HEADS_PER_PASS = 2
HG_STEP_ROWS = 256
HG_HEADS_PER_STEP = 4
MOE_ROWS = 256
RUN_BITS = ROW_BLOCK.bit_length()
VMEM_LIMIT = 56 * 1024 * 1024

F32 = jnp.float32
BF16 = jnp.bfloat16
HIGHEST = lax.Precision.HIGHEST


def _cparams(*sem):
    return pltpu.CompilerParams(dimension_semantics=sem, vmem_limit_bytes=VMEM_LIMIT)


def _dot(a, b, precision=None):
    return jnp.dot(a, b, preferred_element_type=F32, precision=precision)


def _dot_nt(a, b):
    return lax.dot_general(a, b, (((1,), (1,)), ((), ())), preferred_element_type=F32)


def _dot_tn(a, b):
    return lax.dot_general(a, b, (((0,), (0,)), ((), ())), preferred_element_type=F32)


def _store_token_tiles(ref, x):
    n = x.shape[0]
    for c in range(SUBLANES):
        ref[pl.ds(c, n, stride=SUBLANES), :] = x[:, c * LANES:(c + 1) * LANES]


def _load_token_tiles(ref, first_token, n):
    return jnp.concatenate([ref[pl.ds(first_token * SUBLANES + c, n, stride=SUBLANES), :] for c in range(SUBLANES)],
                           axis=1)


def _sigmoid(x):
    return 1.0 / (1.0 + jnp.exp(-x))


def _silu(x):
    return x * _sigmoid(x)


def _ada_kernel(c_ref, w_ref, b_ref, o_ref):
    o_ref[...] = _dot(_silu(c_ref[...]), w_ref[...], HIGHEST) + b_ref[...]


def _ada_mod(cc, w, b):
    m, d = cc.shape
    n = w.shape[1]
    tn = D_MODEL
    return pl.pallas_call(
        _ada_kernel,
        grid=(n // tn,),
        in_specs=[pl.BlockSpec((m, d), lambda j: (0, 0)),
                  pl.BlockSpec((d, tn), lambda j: (0, j)),
                  pl.BlockSpec((1, tn), lambda j: (0, j))],
        out_specs=pl.BlockSpec((m, tn), lambda j: (0, j)),
        out_shape=jax.ShapeDtypeStruct((m, n), F32),
        compiler_params=_cparams("parallel"),
        name="ada_mod",
    )(cc, w, b.reshape(1, n))


def _mod_rmsnorm(x, g, mod, shift_row, scale_row):
    y = x * lax.rsqrt(jnp.mean(x * x, axis=-1, keepdims=True) + NORM_EPS) * g
    return y * (1.0 + mod[scale_row:scale_row + 1]) + mod[shift_row:shift_row + 1]


def _split_bf16(x, terms):
    parts = []
    for _ in range(terms - 1):
        parts.append(x.astype(BF16))
        x = x - parts[-1].astype(F32)
    return parts + [x.astype(BF16)]


def _head_rmsnorm(x, w, group_ones):
    hi, lo = _split_bf16(x * x, 2)
    n = x.shape[0]
    ss = _dot(jnp.concatenate([hi, lo], axis=0), group_ones)
    return x * lax.rsqrt((ss[:n] + ss[n:]) * (1.0 / HEAD_DIM) + NORM_EPS) * w


def _rope(x, cos, sin_signed):
    n = x.shape[-1]
    lane = lax.broadcasted_iota(jnp.int32, x.shape, 1)
    quarter = HEAD_DIM // 4
    partner = jnp.where(lane % (2 * quarter) < quarter,
                        pltpu.roll(x, n - quarter, 1), pltpu.roll(x, quarter, 1))
    return x * cos + partner * sin_signed


def _stream_specs(stream, width, first_block=0):
    head, tail = stream
    tm = ROW_BLOCK
    if head is None:
        return [pl.BlockSpec((tm, width), lambda i: (i + first_block, 0))], [tail]
    n_head = head.shape[0] // tm
    return ([pl.BlockSpec((tm, width), lambda i: (jnp.minimum(i + first_block, n_head - 1), 0)),
             pl.BlockSpec((tm, width), lambda i: (jnp.maximum(i + first_block - n_head, 0), 0))], [head, tail])


def _stream_block(refs, n_head, first_block=0):
    if n_head is None:
        return refs[0][...]
    return jnp.where(pl.program_id(0) + first_block < n_head, refs[0][...], refs[1][...])


def _n_head(stream):
    return None if stream[0] is None else stream[0].shape[0] // ROW_BLOCK


def _inproj_kernel(*refs, n_head):
    n_x = 1 if n_head is None else 2
    (mod_ref, g_ref, w_ref, naq_w, nak_w, swq_w, swk_w, cos_ref, sin_ref, ones_ref, lb_ref,
     hg_ref, naq_ref, nak_ref, nav_ref, swq_ref, swk_ref, swv_ref) = refs[n_x:]
    h = _mod_rmsnorm(_stream_block(refs[:n_x], n_head), g_ref[...], mod_ref[...], 0, 1).astype(BF16)
    ones = ones_ref[...]

    def proj(lo, width):
        return _dot(h, w_ref[:, lo:lo + width])

    naq_ref[...] = (_head_rmsnorm(proj(0, NA_W), naq_w[...], ones) * ATTN_SCALE).astype(BF16)
    nak_ref[...] = _head_rmsnorm(proj(NA_W, NA_W), nak_w[...], ones).astype(BF16)
    nav_ref[...] = proj(2 * NA_W, NA_W).astype(BF16)
    base = 3 * NA_W
    lb = lb_ref[...]
    hg_ref[:, :HG_W] = _silu(proj(base, HG_W)) * (HG_DK ** -0.5)
    for j in (1, 2):
        hg_ref[:, j * HG_W:(j + 1) * HG_W] = jnp.log2(lb + (1.0 - lb) * _sigmoid(proj(base + j * HG_W, HG_W)))
    for j in (3, 4):
        hg_ref[:, j * HG_W:(j + 1) * HG_W] = proj(base + j * HG_W, HG_W)
    base += HG_COLS
    cos, sin = cos_ref[...], sin_ref[...]
    swq = _rope(_head_rmsnorm(proj(base, SWA_W), swq_w[...], ones), cos, sin)
    swq_ref[...] = (swq * ATTN_SCALE).astype(BF16)
    swk_ref[...] = _rope(_head_rmsnorm(proj(base + SWA_W, SWA_W), swk_w[...], ones), cos, sin).astype(BF16)
    swv_ref[...] = proj(base + 2 * SWA_W, SWA_W).astype(BF16)


def _mod_row(i, n_ctx_blocks, blocks_per_batch, n_batch):
    return jnp.where(i < n_ctx_blocks, n_batch, (i - n_ctx_blocks) // blocks_per_batch)


def _inproj(x_stream, mods, norm_g, w_ext, naq_w, nak_w, swq_w, swk_w, cos_t, sin_t, ones, lb, geom):
    n_batch, n_ctx_blocks, blocks_per_batch = geom
    tm = ROW_BLOCK
    t = (n_ctx_blocks + n_batch * blocks_per_batch) * tm
    row = lambda w: pl.BlockSpec((tm, w), lambda i: (i, 0))
    const = lambda a: pl.BlockSpec(a.shape, lambda i: (0,) * a.ndim)
    rope_blk = lambda i: (jnp.where(i < n_ctx_blocks, 0, 1 + (i - n_ctx_blocks) % blocks_per_batch), 0)
    bf = lambda w: jax.ShapeDtypeStruct((t, w), BF16)
    x_specs, x_arrays = _stream_specs(x_stream, D_MODEL)
    return pl.pallas_call(
        functools.partial(_inproj_kernel, n_head=_n_head(x_stream)),
        grid=(t // tm,),
        in_specs=x_specs + [
            pl.BlockSpec((None, 6, D_MODEL), lambda i: (_mod_row(i, n_ctx_blocks, blocks_per_batch, n_batch), 0, 0)),
            const(norm_g), const(w_ext), const(naq_w), const(nak_w), const(swq_w), const(swk_w),
            pl.BlockSpec((tm, SWA_W), rope_blk), pl.BlockSpec((tm, SWA_W), rope_blk), const(ones), const(lb)],
        out_specs=[row(HG_COLS), row(NA_W), row(NA_W), row(NA_W), row(SWA_W), row(SWA_W), row(SWA_W)],
        out_shape=[jax.ShapeDtypeStruct((t, HG_COLS), F32), bf(NA_W), bf(NA_W), bf(NA_W),
                   bf(SWA_W), bf(SWA_W), bf(SWA_W)],
        compiler_params=_cparams("parallel"),
        name="inproj",
    )(*x_arrays, mods, norm_g, w_ext, naq_w, nak_w, swq_w, swk_w, cos_t, sin_t, ones, lb)


def _attend(q, parts, sink_vals, heads_per_pass):
    m_rows, width = q.shape
    n_heads = width // HEAD_DIM
    lane = lax.broadcasted_iota(jnp.int32, (m_rows, width), 1)
    out = jnp.zeros((m_rows, width), F32)
    for h0 in range(0, n_heads, heads_per_pass):
        heads = range(h0, h0 + heads_per_pass)
        rows = slice(h0 * m_rows, (h0 + heads_per_pass) * m_rows)
        in_head = [(lane // HEAD_DIM) == h for h in heads]
        q_stack = jnp.concatenate([jnp.where(m, q, jnp.zeros_like(q)) for m in in_head], axis=0)
        scores = []
        for k, _, add in parts:
            s = _dot_nt(q_stack, k)
            scores.append(s if add is None else s + add[rows])
        mx = functools.reduce(jnp.maximum, [jnp.max(s, axis=-1, keepdims=True) for s in scores])
        if sink_vals is not None:
            head_row = lax.broadcasted_iota(jnp.int32, (heads_per_pass * m_rows, 1), 0) // m_rows + h0
            sink = functools.reduce(jnp.add, [jnp.where(head_row == h, sink_vals[h], 0.0) for h in heads])
            mx = jnp.maximum(mx, sink)
        ps = [jnp.exp(s - mx) for s in scores]
        den = functools.reduce(jnp.add, [jnp.sum(p, axis=-1, keepdims=True) for p in ps])
        if sink_vals is not None:
            den = den + jnp.exp(sink - mx)
        acc = functools.reduce(jnp.add, [_dot(p.astype(BF16), v) for p, (_, v, _) in zip(ps, parts)]) / den
        for i, m in enumerate(in_head):
            out = jnp.where(m, acc[i * m_rows:(i + 1) * m_rows], out)
    return out


def _na_kernel(q_ref, k0, k1, k2, kc, v0, v1, v2, vc, bias_ref, o_ref):
    k_lat = jnp.concatenate([k0[...], k1[...], k2[...]], axis=0)
    v_lat = jnp.concatenate([v0[...], v1[...], v2[...]], axis=0)
    bias = bias_ref[...]
    bias = bias.reshape(bias.shape[0] * bias.shape[1], bias.shape[2])
    o = _attend(q_ref[...], [(k_lat, v_lat, bias), (kc[...], vc[...], None)], None, NA_HEADS_PER_PASS)
    o_ref[...] = o.astype(o_ref.dtype)


def _na_attention(naq, nak, nav, bias, geom):
    n_batch, n_ctx_blocks, blocks_per_batch = geom
    tm = ROW_BLOCK
    n_rb = blocks_per_batch
    kb_max = n_rb - NA_KROWS // NA_QROWS
    lat = lambda rb, b: n_ctx_blocks + b * blocks_per_batch

    def band(j):
        return pl.BlockSpec((tm, NA_W), lambda rb, b: (lat(rb, b) + jnp.clip(rb - 1, 0, kb_max) + j, 0))

    ctx = pl.BlockSpec((tm, NA_W), lambda rb, b: (b, 0))
    return pl.pallas_call(
        _na_kernel,
        grid=(n_rb, n_batch),
        in_specs=[pl.BlockSpec((tm, NA_W), lambda rb, b: (lat(rb, b) + rb, 0)),
                  band(0), band(1), band(2), ctx, band(0), band(1), band(2), ctx,
                  pl.BlockSpec((None, NA_HEADS, tm, NA_KROWS * GRID_W), lambda rb, b: (
                      jnp.where(rb == 0, 0, jnp.where(rb == n_rb - 1, 2, 1)), 0, 0, 0))],
        out_specs=pl.BlockSpec((tm, NA_W), lambda rb, b: (b * blocks_per_batch + rb, 0)),
        out_shape=jax.ShapeDtypeStruct((n_batch * blocks_per_batch * tm, NA_W), BF16),
        compiler_params=_cparams("parallel", "parallel"),
        name="na_attn",
    )(naq, nak, nak, nak, nak, nav, nav, nav, nav, bias)


def _swa_kernel(sink_ref, q_ref, kp, kc_, kn, kx, vp, vc_, vn, vx, mask_ref, o_ref):
    k_lat = jnp.concatenate([kp[...], kc_[...], kn[...]], axis=0)
    v_lat = jnp.concatenate([vp[...], vc_[...], vn[...]], axis=0)
    sinks = [sink_ref[h] for h in range(SWA_Q_HEADS)]
    o = _attend(q_ref[...], [(k_lat, v_lat, mask_ref[...]), (kx[...], vx[...], None)], sinks, SWA_HEADS_PER_PASS)
    o_ref[...] = o.astype(o_ref.dtype)


def _swa_mask_table(tq, n_blocks):
    qi = np.arange(SWA_Q_HEADS * tq)[:, None] % tq
    mi = np.arange(3 * tq)[None, :]
    variants = []
    for n in (0, 1, n_blocks - 1):
        kpos = (n - 1) * tq + mi
        ok = (np.abs(qi + tq - mi) <= SWA_WINDOW) & (kpos >= 0) & (kpos < n_blocks * tq)
        variants.append(np.where(ok, 0.0, NEG_INF).astype(np.float32))
    return jnp.asarray(np.stack(variants, axis=0))


def _swa_attention(swq, swk, swv, sink, geom, seq, ctx_len):
    n_batch, _, _ = geom
    tq = SWA_BLOCK
    nb = seq // tq
    assert nb >= 3
    first = n_batch * ctx_len // tq
    blk = lambda f: pl.BlockSpec((tq, SWA_W), lambda b, n: (first + b * nb + f(n), 0))
    prev, cur, nxt = blk(lambda n: jnp.maximum(n - 1, 0)), blk(lambda n: n), blk(lambda n: jnp.minimum(n + 1, nb - 1))
    ctx = pl.BlockSpec((ctx_len, SWA_W), lambda b, n: (b, 0))
    mask = pl.BlockSpec((None, SWA_Q_HEADS * tq, 3 * tq),
                        lambda b, n: (jnp.where(n == 0, 0, jnp.where(n == nb - 1, 2, 1)), 0, 0))
    return pl.pallas_call(
        _swa_kernel,
        grid=(n_batch, nb),
        in_specs=[pl.BlockSpec(memory_space=pltpu.SMEM), cur, prev, cur, nxt, ctx, prev, cur, nxt, ctx, mask],
        out_specs=pl.BlockSpec((tq, SWA_W), lambda b, n: (b * nb + n, 0)),
        out_shape=jax.ShapeDtypeStruct((n_batch * seq, SWA_W), BF16),
        compiler_params=_cparams("parallel", "parallel"),
        name="swa_attn",
    )(sink, swq, swk, swk, swk, swk, swv, swv, swv, swv, _swa_mask_table(tq, nb))


def _ctx_attn_kernel(sink_ref, naq, nak, nav, swq, swk, swv, ona_ref, osw_ref):
    ona_ref[...] = _attend(naq[...], [(nak[...], nav[...], None)], None, NA_HEADS_PER_PASS).astype(ona_ref.dtype)
    sinks = [sink_ref[h] for h in range(SWA_Q_HEADS)]
    osw_ref[...] = _attend(swq[...], [(swk[...], swv[...], None)], sinks, NA_HEADS_PER_PASS).astype(osw_ref.dtype)


def _ctx_attention(naq, nak, nav, swq, swk, swv, sink, n_batch, ctx_len):
    blk = pl.BlockSpec((ctx_len, NA_W), lambda b: (b, 0))
    out = jax.ShapeDtypeStruct((n_batch * ctx_len, NA_W), BF16)
    return pl.pallas_call(
        _ctx_attn_kernel,
        grid=(n_batch,),
        in_specs=[pl.BlockSpec(memory_space=pltpu.SMEM)] + [blk] * 6,
        out_specs=[blk, blk],
        out_shape=[out, out],
        compiler_params=_cparams("parallel"),
        name="ctx_attn",
    )(sink, naq, nak, nav, swq, swk, swv)


def _dot_exact_lhs(m, x):
    out = _dot(m, jnp.concatenate(_split_bf16(x, 3), axis=1))
    n = x.shape[1]
    return out[:, :n] + out[:, n:2 * n] + out[:, 2 * n:]


def _hgrn_sum_table(n_rows, reverse):
    tri = np.tril(np.ones((n_rows, n_rows), np.float32))
    return tri.T if reverse else tri


def _tile_row(x, r):
    tiles = x.reshape(x.shape[0] // SUBLANES, SUBLANES, x.shape[1])
    return jnp.broadcast_to(tiles[:, r:r + 1, :], tiles.shape).reshape(x.shape)


def _hgrn_masks(n_rows):
    row = lax.broadcasted_iota(jnp.int32, (n_rows, LANES), 0)
    row_a = lax.broadcasted_iota(jnp.int32, (n_rows, n_rows), 0)
    col_a = lax.broadcasted_iota(jnp.int32, (n_rows, n_rows), 1)
    sizes = [1 << b for b in range(1, n_rows.bit_length())]
    return dict(
        diag=row_a == col_a,
        same={s: row_a // s == col_a // s for s in sizes if s < n_rows},
        upper_half={s: row % s >= s // 2 for s in sizes if s < 2 * SUBLANES},
        from_row={r: row % SUBLANES >= r for r in range(2, SUBLANES, 2)})


def _hgrn_block(q, k, g, v, st, sums, ones, masks, reverse):
    n_rows = q.shape[0]
    cum = _dot_exact_lhs(sums, g)
    tot = cum[0:1] if reverse else cum[n_rows - 1:n_rows]
    o_inter = _dot_nt((q * jnp.exp2(cum)).astype(BF16), st.astype(BF16))
    k_end = (k * jnp.exp2(tot - cum)).astype(BF16)
    st_new = st * jnp.exp2(tot) + _dot_tn(v.astype(BF16), k_end)

    a = None
    zero_tile = jnp.zeros((SUBLANES, LANES), F32)
    size = n_rows
    while size >= 2:
        half = size // 2
        if size < 2 * SUBLANES:
            ref_off = half if reverse else half - 1
            ref = _tile_row(cum, ref_off)
            for first in range(size, SUBLANES, size):
                ref = jnp.where(masks["from_row"][first], _tile_row(cum, first + ref_off), ref)
            upper = masks["upper_half"][size]
            d = cum - ref
            q_exp = jnp.where(upper, NEG_INF, d) if reverse else jnp.where(upper, d, NEG_INF)
            k_exp = jnp.where(upper, -d, NEG_INF) if reverse else jnp.where(upper, NEG_INF, -d)
            q_t = (q * jnp.exp2(q_exp)).astype(BF16)
            k_t = (k * jnp.exp2(k_exp)).astype(BF16)
        else:
            q_tiles, k_tiles = [], []
            for r0 in range(0, n_rows, SUBLANES):
                sl = slice(r0, r0 + SUBLANES)
                first = r0 // size * size
                ref_row = first + (half if reverse else half - 1)
                ref = cum[ref_row:ref_row + 1]
                if (r0 - first < half) if reverse else (r0 - first >= half):
                    q_tiles.append(q[sl] * jnp.exp2(cum[sl] - ref))
                    k_tiles.append(zero_tile)
                else:
                    q_tiles.append(zero_tile)
                    k_tiles.append(k[sl] * jnp.exp2(ref - cum[sl]))
            q_t = jnp.concatenate(q_tiles, axis=0).astype(BF16)
            k_t = jnp.concatenate(k_tiles, axis=0).astype(BF16)
        a_l = _dot_nt(q_t, k_t)
        a = a_l if a is None else jnp.where(masks["same"][size], a_l, a)
        size //= 2
    a = jnp.where(masks["diag"], _dot((q * k).astype(BF16), ones), a)
    return o_inter + _dot(a.astype(BF16), v.astype(BF16)), st_new


def _hgrn_kernel(q_f, g_f, v_f, q_b, g_b, v_b, trif_ref, trib_ref, ones_ref, of_ref, ob_ref, st_f, st_b):
    @pl.when(pl.program_id(2) == 0)
    def _():
        st_f[...] = jnp.zeros_like(st_f)
        st_b[...] = jnp.zeros_like(st_b)

    ones = ones_ref[...]
    masks = _hgrn_masks(q_f.shape[0])
    for h in range(HG_HEADS_PER_STEP):
        sl = slice(h * HG_DK, (h + 1) * HG_DK)
        g = g_f[:, sl]
        of_ref[:, sl], st_f[h] = _hgrn_block(q_f[:, sl], 1.0 - jnp.exp2(g), g, v_f[:, sl], st_f[h],
                                             trif_ref[...], ones, masks, False)
        g = g_b[:, sl]
        ob_ref[:, sl], st_b[h] = _hgrn_block(q_b[:, sl], 1.0 - jnp.exp2(g), g, v_b[:, sl], st_b[h],
                                             trib_ref[...], ones, masks, True)


def _hgrn(hg, tri_f, tri_b, ones, n_batch, seq, ctx_len):
    tr = HG_STEP_ROWS
    nc, nl = ctx_len // tr, seq // tr
    first = n_batch * nc

    def fwd(b, j):
        return jnp.where(j < nc, b * nc + j, first + b * nl + (j - nc))

    def bwd(b, j):
        return jnp.where(j < nc, b * nc + (nc - 1 - j), first + b * nl + (nl - 1 - (j - nc)))

    hps = HG_HEADS_PER_STEP
    groups = HG_HEADS // hps
    wide = hps * HG_DK
    col = lambda row_fn, group: pl.BlockSpec((tr, wide), lambda b, h, j: (row_fn(b, j), group * groups + h))
    const = lambda a: pl.BlockSpec(a.shape, lambda b, h, j: (0, 0))
    out = jax.ShapeDtypeStruct((hg.shape[0], HG_W), F32)
    return pl.pallas_call(
        _hgrn_kernel,
        grid=(n_batch, groups, nc + nl),
        in_specs=[col(fwd, 0), col(fwd, 1), col(fwd, 3), col(bwd, 0), col(bwd, 2), col(bwd, 3),
                  const(tri_f), const(tri_b), const(ones)],
        out_specs=[pl.BlockSpec((tr, wide), lambda b, h, j: (fwd(b, j), h)),
                   pl.BlockSpec((tr, wide), lambda b, h, j: (bwd(b, j), h))],
        out_shape=[out, out],
        scratch_shapes=[pltpu.VMEM((hps, HG_DK, HG_DK), F32), pltpu.VMEM((hps, HG_DK, HG_DK), F32)],
        compiler_params=_cparams("parallel", "parallel", "arbitrary"),
        name="hgrn",
    )(hg, hg, hg, hg, hg, hg, tri_f, tri_b, ones)


def _mixout_kernel(*refs, n_heads, stream_off):
    counts = [1 if n is None else 2 for n in n_heads]
    streams, pos = [], 0
    for n, c in zip(n_heads, counts):
        streams.append((refs[pos:pos + c], n))
        pos += c
    (of_ref, ob_ref, zg_ref, hgn_ref, w_ref, mod_ref, g2_ref, rw_ref, rb_ref,
     xo_ref, h2_ref, te_ref, tg_ref) = refs[pos:]
    x_in = _stream_block(*streams[0], first_block=stream_off)
    o = of_ref[...] + ob_ref[...]
    zg = zg_ref[...]
    parts = [_stream_block(*streams[1])]
    for h in range(HG_HEADS):
        sl = slice(h * HG_DK, (h + 1) * HG_DK)
        oh = o[:, sl]
        yh = oh * lax.rsqrt(jnp.mean(oh * oh, axis=-1, keepdims=True) + NORM_EPS) * hgn_ref[...]
        parts.append((yh * _silu(zg[:, sl])).astype(BF16))
    parts.append(_stream_block(*streams[2]))
    y = _dot(jnp.concatenate(parts, axis=1), w_ref[...])
    mod = mod_ref[...]
    x_new = x_in + mod[2:3] * y
    xo_ref[...] = x_new
    h2 = _mod_rmsnorm(x_new, g2_ref[...], mod, 3, 4)
    h2_ref[...] = h2.astype(BF16)

    h_hi, h_lo = _split_bf16(h2, 2)
    logits = _dot(jnp.concatenate([h_hi, h_lo, h_hi], axis=1), rw_ref[...]) + rb_ref[...]
    lane = lax.broadcasted_iota(jnp.int32, logits.shape, 1).astype(F32)
    top_e = jnp.zeros(logits.shape, F32)
    top_v = jnp.full(logits.shape, NEG_INF, F32)
    for j in range(TOP_K):
        best = jnp.max(logits, axis=-1, keepdims=True)
        arg = jnp.min(jnp.where(logits == best, lane, float(LANES)), axis=-1, keepdims=True)
        top_e = jnp.where(lane == j, arg, top_e)
        top_v = jnp.where(lane == j, best, top_v)
        logits = jnp.where(lane == arg, -jnp.inf, logits)
    ex = jnp.exp(top_v - jnp.max(top_v, axis=-1, keepdims=True))
    te_ref[...] = top_e.astype(jnp.int32)
    tg_ref[...] = ex / jnp.sum(ex, axis=-1, keepdims=True)


def _mixout(x_stream, yna_stream, o_f, o_b, hg, ysw_stream, hgn, w_out, mods, norm2_g, rw, rb, geom, stream_off):
    n_batch, n_ctx_blocks, blocks_per_batch = geom
    tm = ROW_BLOCK
    t = (n_ctx_blocks + n_batch * blocks_per_batch - stream_off) * tm
    row = lambda w: pl.BlockSpec((tm, w), lambda i: (i, 0))
    full = lambda w, cb: pl.BlockSpec((tm, w), lambda i: (i + stream_off, cb))
    const = lambda a: pl.BlockSpec(a.shape, lambda i: (0,) * a.ndim)
    x_specs, x_arrays = _stream_specs(x_stream, D_MODEL, stream_off)
    yna_specs, yna_arrays = _stream_specs(yna_stream, NA_W)
    ysw_specs, ysw_arrays = _stream_specs(ysw_stream, SWA_W)
    n_heads = (_n_head(x_stream), _n_head(yna_stream), _n_head(ysw_stream))
    return pl.pallas_call(
        functools.partial(_mixout_kernel, n_heads=n_heads, stream_off=stream_off),
        grid=(t // tm,),
        in_specs=x_specs + yna_specs + ysw_specs + [
            full(HG_W, 0), full(HG_W, 0), full(HG_W, 4), const(hgn), const(w_out),
            pl.BlockSpec((None, 6, D_MODEL), lambda i: (
                _mod_row(i + stream_off, n_ctx_blocks, blocks_per_batch, n_batch), 0, 0)),
            const(norm2_g), const(rw), const(rb)],
        out_specs=[row(D_MODEL), row(D_MODEL), row(LANES), row(LANES)],
        out_shape=[jax.ShapeDtypeStruct((t, D_MODEL), F32), jax.ShapeDtypeStruct((t, D_MODEL), BF16),
                   jax.ShapeDtypeStruct((t, LANES), jnp.int32), jax.ShapeDtypeStruct((t, LANES), F32)],
        compiler_params=_cparams("parallel"),
        name="mixout",
    )(*x_arrays, *yna_arrays, *ysw_arrays, o_f, o_b, hg, hgn, w_out, mods, norm2_g, rw, rb)


def _start_run_copies(run_ref, n_experts, make_copy):
    for e in range(n_experts):
        n = run_ref[0, 0, e]
        local0 = run_ref[0, 0, n_experts + e]
        slot0 = run_ref[0, 0, 2 * n_experts + e]
        for bit in reversed(range(RUN_BITS)):
            @pl.when((n & (1 << bit)) != 0)
            def _():
                done = (n >> (bit + 1)) << (bit + 1)
                make_copy(local0 + done, slot0 + done, 1 << bit).start()


def _tile_rows(first_token, n_tokens):
    return pl.ds(pl.multiple_of(first_token * SUBLANES, SUBLANES), n_tokens * SUBLANES)


def _local_positions(te_ref, off_ref, stri_ref):
    e_tok = te_ref[...]
    lane = lax.broadcasted_iota(jnp.int32, e_tok.shape, 1)
    member = [lane == e_tok[:, j:j + 1] for j in range(TOP_K)]
    count = functools.reduce(jnp.add, [jnp.where(m, 1.0, 0.0) for m in member])
    before = _dot(stri_ref[...], count.astype(BF16)) + off_ref[...]
    return [jnp.sum(jnp.where(m, before, 0.0), axis=1, keepdims=True) for m in member]


def _dispatch_kernel(pad_ref, nu_ref, run_ref, te_ref, off_ref, h_ref, stri_ref, xs_hbm, pos_ref,
                     zeros, xloc0, xloc1, sem, zero_sem, *, n_tok_blocks):
    n_tok = h_ref.shape[0]
    block_rows = zeros.shape[0]
    n_blocks = xs_hbm.shape[0] // block_rows

    def zero_slot(e, r):
        row = pl.multiple_of((pad_ref[0, e] + r) * SUBLANES, SUBLANES)
        return pltpu.make_async_copy(zeros.at[pl.ds(0, SUBLANES), :], xs_hbm.at[pl.ds(row, SUBLANES), :], zero_sem)

    def zero_block(b):
        row = pl.multiple_of(b * block_rows, block_rows)
        return pltpu.make_async_copy(zeros, xs_hbm.at[pl.ds(row, block_rows), :], zero_sem)

    def for_each_unused(slot_fn, block_fn):
        def per_expert(e, carry):
            lax.fori_loop(0, pad_ref[1, e], lambda r, c: (slot_fn(e, r), c)[1], 0)
            return carry

        lax.fori_loop(0, pad_ref.shape[1], per_expert, 0)
        lax.fori_loop(nu_ref[0], n_blocks, lambda b, c: (block_fn(b), c)[1], 0)

    @pl.when(pl.program_id(0) == 0)
    def _():
        zeros[...] = jnp.zeros_like(zeros)
        for_each_unused(lambda e, r: zero_slot(e, r).start(), lambda b: zero_block(b).start())
        for_each_unused(lambda e, r: zero_slot(e, r).wait(), lambda b: zero_block(b).wait())

    bufs = (xloc0, xloc1)

    def sort_into(b):
        pos = _local_positions(te_ref, off_ref, stri_ref)
        lane = lax.broadcasted_iota(jnp.int32, (n_tok, LANES), 1)
        pos_ref[...] = functools.reduce(jnp.add, [jnp.where(lane == j, p, 0.0) for j, p in enumerate(pos)])
        col = lax.broadcasted_iota(jnp.int32, (n_tok, TOP_K * n_tok), 1).astype(F32)
        chosen = functools.reduce(jnp.logical_or, [col == p for p in pos])
        sorted_rows = _dot_tn(jnp.where(chosen, 1.0, 0.0).astype(BF16), h_ref[...])
        _store_token_tiles(bufs[b], sorted_rows)

    def start_copies(b):
        _start_run_copies(run_ref, pad_ref.shape[1], lambda local, slot, size: pltpu.make_async_copy(
            bufs[b].at[_tile_rows(local, size), :], xs_hbm.at[_tile_rows(slot, size), :], sem.at[b]))

    def wait_copies(b):
        pltpu.make_async_copy(bufs[b], xs_hbm.at[pl.ds(0, bufs[b].shape[0]), :], sem.at[b]).wait()

    i = pl.program_id(0)

    @pl.when(i == 0)
    def _():
        sort_into(0)

    for parity in (0, 1):
        @pl.when((i % 2 == parity) & (i >= 1) & (i < n_tok_blocks))
        def _():
            @pl.when(i >= 2)
            def _():
                wait_copies(parity)

            start_copies(1 - parity)
            sort_into(parity)

    @pl.when(i == n_tok_blocks)
    def _():
        if n_tok_blocks >= 2:
            wait_copies(n_tok_blocks % 2)
        start_copies((n_tok_blocks - 1) % 2)
        wait_copies((n_tok_blocks - 1) % 2)


def _dispatch(h, top_e, runs, run_first, pad_slots, n_used, n_slots):
    tm = ROW_BLOCK
    t = h.shape[0]
    nb = t // tm
    assert n_slots % MOE_ROWS == 0
    stri = jnp.asarray(np.tril(np.ones((tm, tm), np.float32), -1), BF16)
    cur = lambda i: jnp.minimum(i, nb - 1)
    row = lambda w: pl.BlockSpec((tm, w), lambda i: (cur(i), 0))
    sort_buffer = pltpu.VMEM((TOP_K * tm * SUBLANES, LANES), F32)
    return pl.pallas_call(
        functools.partial(_dispatch_kernel, n_tok_blocks=nb),
        grid=(nb + 1,),
        in_specs=[pl.BlockSpec(memory_space=pltpu.SMEM), pl.BlockSpec(memory_space=pltpu.SMEM),
                  pl.BlockSpec((1, 1, runs.shape[2]), lambda i: (jnp.maximum(i - 1, 0), 0, 0),
                               memory_space=pltpu.SMEM),
                  row(LANES), pl.BlockSpec((None, 1, LANES), lambda i: (cur(i), 0, 0)), row(D_MODEL),
                  pl.BlockSpec((tm, tm), lambda i: (0, 0))],
        out_specs=[pl.BlockSpec(memory_space=pl.ANY), row(LANES)],
        out_shape=[jax.ShapeDtypeStruct((n_slots * SUBLANES, LANES), F32), jax.ShapeDtypeStruct((t, LANES), F32)],
        scratch_shapes=[pltpu.VMEM((MOE_ROWS * SUBLANES, LANES), F32), sort_buffer, sort_buffer,
                        pltpu.SemaphoreType.DMA((2,)), pltpu.SemaphoreType.DMA(())],
        compiler_params=_cparams("arbitrary"),
        name="moe_dispatch",
    )(pad_slots, n_used, runs, top_e, run_first, h, stri)


def _moe_kernel(be_ref, nu_ref, next_ref, run_ref, xs_ref, wgu_hbm, bgu_ref, wd_hbm, bd_ref, y_ref,
                wgu_f32, wd_f32, wgu_bf, wd_bf, sem, *, layer):
    i = pl.program_id(0)
    live = i < nu_ref[0]
    tm = y_ref.shape[0] // SUBLANES
    changed = (i == 0) | (be_ref[i] != be_ref[jnp.maximum(i - 1, 0)])

    def weight_copies(expert, buf):
        return (pltpu.make_async_copy(wgu_hbm.at[layer, expert], wgu_f32.at[buf], sem.at[buf]),
                pltpu.make_async_copy(wd_hbm.at[layer, expert], wd_f32.at[buf], sem.at[buf]))

    @pl.when(live & changed)
    def _():
        buf = run_ref[i] % 2

        @pl.when(i == 0)
        def _():
            for copy in weight_copies(be_ref[0], 0):
                copy.start()

        for copy in weight_copies(be_ref[i], buf):
            copy.wait()
        wgu_bf[...] = wgu_f32[buf].astype(BF16)
        wd_bf[...] = wd_f32[buf].astype(BF16)

        @pl.when(next_ref[i] >= 0)
        def _():
            for copy in weight_copies(next_ref[i], 1 - buf):
                copy.start()

    @pl.when(live)
    def _():
        x = _load_token_tiles(xs_ref, 0, tm)
        gu = _dot(x.astype(BF16), wgu_bf[...]) + bgu_ref[...]
        glu = jnp.minimum(gu[:, :D_FF], SWIGLU_LIMIT)
        lin = jnp.clip(gu[:, D_FF:], -SWIGLU_LIMIT, SWIGLU_LIMIT)
        act = glu * _sigmoid(SWIGLU_ALPHA * glu) * (lin + 1.0)
        _store_token_tiles(y_ref, _dot(act.astype(BF16), wd_bf[...]) + bd_ref[...])

    @pl.when(jnp.logical_not(live))
    def _():
        y_ref[...] = jnp.zeros_like(y_ref)


def _moe_ffn(xs_tiles, block_e, n_used, w_gu, b_gu, w_down, b_down, layer):
    tm = MOE_ROWS
    nb = xs_tiles.shape[0] // (tm * SUBLANES)
    n_l, n_e, d, f2 = w_gu.shape
    idx = jnp.arange(nb, dtype=jnp.int32)
    starts = jnp.concatenate([jnp.ones((1,), bool), block_e[1:] != block_e[:-1]]) & (idx < n_used[0])
    run_idx = jnp.cumsum(starts.astype(jnp.int32)) - 1
    run_expert = jnp.full((nb + 1,), -1, jnp.int32).at[jnp.where(starts, run_idx, nb)].set(block_e)[:nb]
    next_e = jnp.concatenate([run_expert[1:], jnp.full((1,), -1, jnp.int32)])[run_idx]
    bias = lambda c: pl.BlockSpec((None, None, 1, c), lambda i, be, nu, nx, ru: (layer, be[i], 0, 0))
    grid_spec = pltpu.PrefetchScalarGridSpec(
        num_scalar_prefetch=4,
        grid=(nb,),
        in_specs=[pl.BlockSpec((tm * SUBLANES, LANES), lambda i, be, nu, nx, ru: (jnp.minimum(i, nu[0] - 1), 0)),
                  pl.BlockSpec(memory_space=pl.ANY), bias(f2), pl.BlockSpec(memory_space=pl.ANY), bias(d)],
        out_specs=pl.BlockSpec((tm * SUBLANES, LANES), lambda i, be, nu, nx, ru: (i, 0)),
        scratch_shapes=[pltpu.VMEM((2, d, f2), F32), pltpu.VMEM((2, f2 // 2, d), F32),
                        pltpu.VMEM((d, f2), BF16), pltpu.VMEM((f2 // 2, d), BF16), pltpu.SemaphoreType.DMA((2,))],
    )
    return pl.pallas_call(
        functools.partial(_moe_kernel, layer=layer),
        grid_spec=grid_spec,
        out_shape=jax.ShapeDtypeStruct((nb * tm * SUBLANES, LANES), F32),
        compiler_params=_cparams("arbitrary"),
        name="moe_ffn",
    )(block_e, n_used, next_e, run_idx, xs_tiles, w_gu, b_gu.reshape(n_l, n_e, 1, f2), w_down,
      b_down.reshape(n_l, n_e, 1, d))


def _combine_kernel(run_ref, pos_ref, tg_ref, x_ref, y_hbm, mod_ref, o_ref, ybuf0, ybuf1, sem, *, n_tok_blocks):
    i = pl.program_id(0)
    tm = o_ref.shape[0]
    n_rows = TOP_K * tm
    bufs = (ybuf0, ybuf1)

    def fetch(b):
        _start_run_copies(run_ref, run_ref.shape[2] // 3, lambda local, slot, size: pltpu.make_async_copy(
            y_hbm.at[_tile_rows(slot, size), :], bufs[b].at[_tile_rows(local, size), :], sem.at[b]))

    def wait_fetch(b):
        pltpu.make_async_copy(y_hbm.at[pl.ds(0, n_rows * SUBLANES), :], bufs[b], sem.at[b]).wait()

    def combine(b):
        y = _load_token_tiles(bufs[b], 0, n_rows).astype(BF16)
        pos, gate = pos_ref[...], tg_ref[...]
        col = lax.broadcasted_iota(jnp.int32, (tm, n_rows), 1).astype(F32)
        g = functools.reduce(jnp.add, [jnp.where(col == pos[:, j:j + 1], gate[:, j:j + 1], 0.0)
                                       for j in range(TOP_K)])
        g_hi, g_lo = _split_bf16(g, 2)
        o_ref[...] = x_ref[...] + mod_ref[...][5:6] * (_dot(g_hi, y) + _dot(g_lo, y))

    @pl.when(i == 0)
    def _():
        fetch(0)

    for parity in (0, 1):
        @pl.when((i % 2 == parity) & (i >= 1) & (i < n_tok_blocks))
        def _():
            wait_fetch(1 - parity)
            fetch(parity)
            combine(1 - parity)

    @pl.when(i == n_tok_blocks)
    def _():
        wait_fetch((n_tok_blocks - 1) % 2)
        combine((n_tok_blocks - 1) % 2)


def _combine(x, y_tiles, runs, pos, gates, mods, geom, stream_off):
    n_batch, n_ctx_blocks, blocks_per_batch = geom
    t = x.shape[0]
    tm = ROW_BLOCK
    nb = t // tm
    prev = lambda i: jnp.maximum(i - 1, 0)
    row = lambda w: pl.BlockSpec((tm, w), lambda i: (prev(i), 0))
    sorted_buffer = pltpu.VMEM((TOP_K * tm * SUBLANES, LANES), F32)
    return pl.pallas_call(
        functools.partial(_combine_kernel, n_tok_blocks=nb),
        grid=(nb + 1,),
        in_specs=[pl.BlockSpec((1, 1, runs.shape[2]), lambda i: (jnp.minimum(i, nb - 1), 0, 0),
                               memory_space=pltpu.SMEM),
                  row(LANES), row(LANES), row(D_MODEL), pl.BlockSpec(memory_space=pl.ANY),
                  pl.BlockSpec((None, 6, D_MODEL), lambda i: (
                      _mod_row(prev(i) + stream_off, n_ctx_blocks, blocks_per_batch, n_batch), 0, 0))],
        out_specs=row(D_MODEL),
        out_shape=jax.ShapeDtypeStruct((t, D_MODEL), F32),
        scratch_shapes=[sorted_buffer, sorted_buffer, pltpu.SemaphoreType.DMA((2,))],
        compiler_params=_cparams("arbitrary"),
        name="moe_combine",
    )(runs, pos, gates, x, y_tiles, mods)


def _route(top_e, n_experts, tok_block, slot_block):
    t, k = top_e.shape
    nb = t // tok_block
    onehot = (top_e[:, :, None] == jnp.arange(n_experts, dtype=jnp.int32)).astype(jnp.int32)
    n = onehot.reshape(nb, tok_block * k, n_experts).sum(axis=1)
    counts = n.sum(axis=0)
    padded = (counts + slot_block - 1) // slot_block * slot_block
    pend = jnp.cumsum(padded)
    start = pend - padded
    local0 = jnp.cumsum(n, axis=1) - n
    slot0 = start[None, :] + jnp.cumsum(n, axis=0) - n
    runs = jnp.concatenate([n, local0, slot0], axis=1).astype(jnp.int32).reshape(nb, 1, 3 * n_experts)
    run_first = jnp.pad(local0.astype(F32), ((0, 0), (0, LANES - n_experts))).reshape(nb, 1, LANES)
    n_blocks = (t * k + n_experts * (slot_block - 1) + slot_block - 1) // slot_block
    block_start = jnp.arange(n_blocks, dtype=jnp.int32) * slot_block
    block_e = jnp.minimum((block_start[:, None] >= pend[None, :]).sum(axis=1), n_experts - 1).astype(jnp.int32)
    n_used = (pend[-1] // slot_block).astype(jnp.int32).reshape(1)
    pad_slots = jnp.stack([start + counts, padded - counts], axis=0).astype(jnp.int32)
    return runs, run_first, block_e, n_used, pad_slots, n_blocks * slot_block


def _na_row_pattern(rb, n_rows):
    wr = min(NA_WIN_H, n_rows)
    n_rb = n_rows // NA_QROWS
    q_row = rb * NA_QROWS + np.arange(NA_QROWS)[:, None]
    k_row = np.clip(rb - 1, 0, n_rb - NA_KROWS // NA_QROWS) * NA_QROWS + np.arange(NA_KROWS)[None, :]
    row_start = np.clip(q_row - wr // 2, 0, n_rows - wr)
    ok = (k_row >= row_start) & (k_row < row_start + wr)
    return np.where(ok, k_row - q_row + (NA_WIN_H - 1), -1)


def _na_bias_table(rpb, seq):
    n_rows = seq // GRID_W
    n_rb = n_rows // NA_QROWS
    patterns = [_na_row_pattern(rb, n_rows) for rb in range(n_rb)]
    assert all((p == patterns[1]).all() for p in patterns[1:-1])
    q_col = np.arange(GRID_W)[:, None]
    k_col = np.arange(GRID_W)[None, :]
    col_start = np.clip(q_col - NA_WIN_W // 2, 0, GRID_W - NA_WIN_W)
    col_ok = (k_col >= col_start) & (k_col < col_start + NA_WIN_W)
    dc = np.clip(k_col - q_col + (NA_WIN_W - 1), 0, 2 * NA_WIN_W - 2)
    onehot = ((dc[None] == np.arange(2 * NA_WIN_W - 1)[:, None, None]) & col_ok[None]).astype(np.float32)
    by_col = jnp.einsum('hab,bqk->haqk', rpb.astype(F32), jnp.asarray(onehot), precision=HIGHEST)
    by_col = by_col + jnp.asarray(np.where(col_ok, 0.0, NEG_INF).astype(np.float32))
    masked = jnp.full((rpb.shape[0], GRID_W, GRID_W), NEG_INF, F32)
    variants = []
    for pattern in (patterns[0], patterns[1], patterns[-1]):
        rows = [jnp.concatenate([by_col[:, a] if a >= 0 else masked for a in pattern[qr]], axis=2)
                for qr in range(NA_QROWS)]
        variants.append(jnp.concatenate(rows, axis=1))
    return jnp.stack(variants, axis=0)


def _rope_tables(seq, ctx_len):
    quarter = HEAD_DIM // 4
    lane = np.arange(SWA_W)
    inv = ROPE_BASE ** (-(lane % quarter).astype(np.float64) / quarter)
    t = np.arange(seq)
    pos = np.where((lane % HEAD_DIM < HEAD_DIM // 2)[None, :], (t // GRID_W)[:, None], (t % GRID_W)[:, None])
    ang = jnp.asarray(pos, F32) * jnp.asarray(inv, F32)[None, :]
    sign = np.where(lane % (2 * quarter) < quarter, -1.0, 1.0).astype(np.float32)
    cos = jnp.concatenate([jnp.ones((ctx_len, SWA_W), F32), jnp.cos(ang)], axis=0)
    sin = jnp.concatenate([jnp.zeros((ctx_len, SWA_W), F32), jnp.sin(ang) * sign[None, :]], axis=0)
    return cos, sin


def kernel(x, c, ctx, c_ctx, hg_lower_bounds, ada_w, ada_b, norm1_g, norm2_g, w_in, na_q_norm, na_k_norm, na_rpb,
           hg_norm_g, swa_q_norm, swa_k_norm, swa_sink, w_out, router_w, router_b, w_gu, b_gu, w_down, b_down):
    n_batch, seq, d = x.shape
    ctx_len = ctx.shape[1]
    depth = ada_w.shape[0]
    assert d == D_MODEL and seq % ROW_BLOCK == 0 and ctx_len == ROW_BLOCK
    n_ctx_rows = n_batch * ctx_len
    geom = (n_batch, n_ctx_rows // ROW_BLOCK, seq // ROW_BLOCK)

    p_lb = jax.nn.softmax(hg_lower_bounds.astype(F32), axis=0)
    lbs = jnp.cumsum(p_lb, axis=0) - p_lb[0]

    cos_t, sin_t = _rope_tables(seq, ctx_len)
    lane = np.arange(NA_W)
    group_ones = jnp.asarray((lane[:, None] // HEAD_DIM == lane[None, :] // HEAD_DIM).astype(np.float32), BF16)
    tri_f = jnp.asarray(_hgrn_sum_table(HG_STEP_ROWS, False), BF16)
    tri_b = jnp.asarray(_hgrn_sum_table(HG_STEP_ROWS, True), BF16)
    ones_bf = jnp.ones((LANES, HG_STEP_ROWS), BF16)
    n_mod_rows = -(-(n_batch + 1) // 8) * 8
    cc = jnp.zeros((n_mod_rows, d), F32).at[:n_batch].set(c).at[n_batch].set(c_ctx)
    dup = lambda w: jnp.concatenate([w[:, :HEAD_DIM], w[:, :HEAD_DIM], w[:, HEAD_DIM:], w[:, HEAD_DIM:]], axis=1)
    tile4 = lambda g: jnp.tile(g.astype(F32), 4).reshape(1, 4 * HEAD_DIM)
    pad_e = LANES - N_EXPERTS

    x_stream = (ctx.reshape(n_ctx_rows, d), x.reshape(n_batch * seq, d))
    for l in range(depth):
        last = l == depth - 1
        mods = _ada_mod(cc, ada_w[l], ada_b[l]).reshape(n_mod_rows, 6, d)
        w = w_in[l]
        kv0 = 3 * NA_W + HG_COLS + SWA_W
        w_ext = jnp.concatenate([w[:, :kv0], dup(w[:, kv0:kv0 + SWA_KV_W]), dup(w[:, kv0 + SWA_KV_W:])],
                                axis=1).astype(BF16)
        hg, naq, nak, nav, swq, swk, swv = _inproj(
            x_stream, mods, norm1_g[l].reshape(1, d), w_ext, tile4(na_q_norm[l]), tile4(na_k_norm[l]),
            tile4(swa_q_norm[l]), tile4(swa_k_norm[l]), cos_t, sin_t, group_ones, lbs[l].reshape(1, HG_W), geom)

        y_na = _na_attention(naq, nak, nav, _na_bias_table(na_rpb[l], seq), geom)
        y_sw = _swa_attention(swq, swk, swv, swa_sink[l].astype(F32), geom, seq, ctx_len)
        o_f, o_b = _hgrn(hg, tri_f, tri_b, ones_bf, n_batch, seq, ctx_len)

        if last:
            stream_off = geom[1]
            yna_stream, ysw_stream = (None, y_na), (None, y_sw)
        else:
            stream_off = 0
            yc_na, yc_sw = _ctx_attention(naq, nak, nav, swq, swk, swv, swa_sink[l].astype(F32), n_batch, ctx_len)
            yna_stream, ysw_stream = (yc_na, y_na), (yc_sw, y_sw)

        rw_hi, rw_lo = _split_bf16(jnp.pad(router_w[l].astype(F32), ((0, 0), (0, pad_e))), 2)
        rw = jnp.concatenate([rw_hi, rw_hi, rw_lo], axis=0)
        rb = jnp.pad(router_b[l].astype(F32), (0, pad_e), constant_values=NEG_INF).reshape(1, LANES)
        x_new, h2, top_e, top_g = _mixout(
            x_stream, yna_stream, o_f, o_b, hg, ysw_stream, hg_norm_g[l].reshape(1, HG_DK).astype(F32),
            w_out[l].astype(BF16), mods, norm2_g[l].reshape(1, d), rw, rb, geom, stream_off)

        runs, run_first, block_e, n_used, pad_slots, n_slots = _route(top_e[:, :TOP_K], N_EXPERTS, ROW_BLOCK, MOE_ROWS)
        xs, pos = _dispatch(h2, top_e, runs, run_first, pad_slots, n_used, n_slots)
        y_slots = _moe_ffn(xs, block_e, n_used, w_gu, b_gu, w_down, b_down, l)
        x_stream = (None, _combine(x_new, y_slots, runs, pos, top_g, mods, geom, stream_off))
    return x_stream[1].reshape(n_batch, seq, d)
```
